```python
import functools
import jax, jax.numpy as jnp
from jax import lax
import numpy as np

D_MODEL = 2048
BATCH = 2
SEQ = 4096
DEPTH = 4

CTX_LEN = 256
GRID_W = 64
EPS = 1e-6
N_BRANCH = 4
BRANCH_W = D_MODEL // N_BRANCH

MLA_NOPE = 128
MLA_ROPE = 64
MLA_V = 128
MLA_HEADS = BRANCH_W // MLA_V
MLA_Q_RANK = (3 * D_MODEL) // 16
MLA_KV_RANK = D_MODEL // 8
MLA_SCALE = (MLA_NOPE + MLA_ROPE) ** -0.5
ROPE_THETA = 10000.0
Q_BLOCK = 128

HG_DK = 128
HG_DV = 128
HG_W = BRANCH_W
HG_HEADS = HG_W // HG_DK
HG_CHUNK = 64

CONV_CH = BRANCH_W
CONV_WIDTH = 31

POOL_CH = BRANCH_W
POOL_WINDOWS = (2, 4, 8, 16)
POOL_GROUP = POOL_CH // len(POOL_WINDOWS)

D_FF = 5632
N_EXPERTS = 8
TOP_K = 2
D_FF_EXPERT = 5632

MIX_SPLITS = (MLA_Q_RANK, MLA_KV_RANK, MLA_ROPE, HG_W, HG_W, HG_W, HG_W, HG_W, 2 * CONV_CH, POOL_CH)
N_MIX_IN = sum(MIX_SPLITS)
N_IN = N_MIX_IN + N_BRANCH * D_MODEL

kernel_name = 'hybrid_mla_hgrn2_conformer_pool_moe_dit'


def _rmsnorm(x, g):
    xf = x.astype(jnp.float32)
    y = xf * lax.rsqrt(jnp.mean(xf * xf, axis=-1, keepdims=True) + EPS)
    return (y * g.astype(jnp.float32)).astype(x.dtype)


def _layernorm(x, g, b):
    xf = x.astype(jnp.float32)
    mu = jnp.mean(xf, axis=-1, keepdims=True)
    var = jnp.mean(jnp.square(xf - mu), axis=-1, keepdims=True)
    y = (xf - mu) * lax.rsqrt(var + EPS)
    return (y * g.astype(jnp.float32) + b.astype(jnp.float32)).astype(x.dtype)


def _modulate(x, g, shift, scale):
    return _rmsnorm(x, g) * (1 + scale) + shift


def _split_mix(z):
    offs = np.cumsum(MIX_SPLITS)[:-1]
    return jnp.split(z, [int(o) for o in offs], axis=-1)


def _axial_rope_angles(n_tokens):
    rows = n_tokens // GRID_W
    t = jnp.arange(rows * GRID_W)
    row = (t // GRID_W).astype(jnp.float32)
    col = (t % GRID_W).astype(jnp.float32)
    axis_dim = MLA_ROPE // 2
    inv_freq = 1.0 / (ROPE_THETA ** (jnp.arange(0, axis_dim, 2, dtype=jnp.float32) / axis_dim))
    return row[:, None] * inv_freq, col[:, None] * inv_freq


def _rotate_half(x, ang):
    x1, x2 = jnp.split(x, 2, axis=-1)
    cos, sin = jnp.cos(ang), jnp.sin(ang)
    return jnp.concatenate([x1 * cos - x2 * sin, x2 * cos + x1 * sin], axis=-1)


def _axial_rope(x, ang_row, ang_col):
    xf = x.astype(jnp.float32)
    xr, xc = jnp.split(xf, 2, axis=-1)
    return jnp.concatenate([_rotate_half(xr, ang_row), _rotate_half(xc, ang_col)], axis=-1).astype(x.dtype)


def _mla_queries(zq, g_q, w_uq):
    B, L, _ = zq.shape
    return (_rmsnorm(zq, g_q) @ w_uq).reshape(B, L, MLA_HEADS, MLA_NOPE + MLA_ROPE)


def _mla_keys_values(zkv, k_rope, g_kv, w_ukv):
    B, L, _ = zkv.shape
    kv = (_rmsnorm(zkv, g_kv) @ w_ukv).reshape(B, L, MLA_HEADS, MLA_NOPE + MLA_V)
    k_r = jnp.broadcast_to(k_rope[:, :, None, :], (B, L, MLA_HEADS, MLA_ROPE))
    return jnp.concatenate([kv[..., :MLA_NOPE], k_r], axis=-1), kv[..., MLA_NOPE:]


def _attend(q, k, v):
    s = jnp.einsum('bqhd,bkhd->bhqk', q, k, preferred_element_type=jnp.float32) * MLA_SCALE
    p = jax.nn.softmax(s, axis=-1).astype(v.dtype)
    return jnp.einsum('bhqk,bkhd->bqhd', p, v)


def mla_mixer(zq, zkv, zr, zq_c, zkv_c, zr_c, g_q, g_kv, w_uq, w_ukv, ctx_out):
    B, L, _ = zq.shape
    ang_r, ang_c = _axial_rope_angles(L)
    q = _mla_queries(zq, g_q, w_uq)
    q = jnp.concatenate([q[..., :MLA_NOPE],
                         _axial_rope(q[..., MLA_NOPE:], ang_r[None, :, None], ang_c[None, :, None])], axis=-1)
    k, v = _mla_keys_values(zkv, _axial_rope(zr, ang_r[None], ang_c[None]), g_kv, w_ukv)
    k_c, v_c = _mla_keys_values(zkv_c, zr_c, g_kv, w_ukv)
    k_all = jnp.concatenate([k, k_c], axis=1)
    v_all = jnp.concatenate([v, v_c], axis=1)
    n_blk = L // Q_BLOCK
    q_blk = q.reshape(B, n_blk, Q_BLOCK, MLA_HEADS, MLA_NOPE + MLA_ROPE).swapaxes(0, 1)
    o = lax.map(lambda qb: _attend(qb, k_all, v_all), q_blk)
    y = o.swapaxes(0, 1).reshape(B, L, MLA_HEADS * MLA_V)
    yc = None
    if ctx_out:
        yc = _attend(_mla_queries(zq_c, g_q, w_uq), k_c, v_c).reshape(B, -1, MLA_HEADS * MLA_V)
    return y, yc


def _gla_chunked(q, k, v, logf, s0):
    B, L, H, DK = q.shape
    DV = v.shape[-1]
    n = L // HG_CHUNK

    def chunks(t):
        return t.reshape(B, n, HG_CHUNK, H, t.shape[-1]).transpose(1, 0, 2, 3, 4)

    lower = jnp.tril(jnp.ones((HG_CHUNK, HG_CHUNK), dtype=bool))[None, :, :, None, None]

    def step(S, inp):
        qc, kc, vc, lf = inp
        b = jnp.cumsum(lf, axis=1)
        rel = jnp.exp(jnp.where(lower, b[:, :, None] - b[:, None, :], -jnp.inf))
        a = jnp.einsum('bthd,bshd,btshd->bhts', qc, kc, rel)
        o = jnp.einsum('bhts,bshv->bthv', a, vc) + jnp.einsum('bthd,bhdv->bthv', qc * jnp.exp(b), S)
        b_end = b[:, -1]
        S = jnp.exp(b_end)[..., None] * S + jnp.einsum('bshd,bshv->bhdv', kc * jnp.exp(b_end[:, None] - b), vc)
        return S, o

    s_fin, o = lax.scan(step, s0, (chunks(q), chunks(k), chunks(v), chunks(logf)))
    return o.transpose(1, 0, 2, 3, 4).reshape(B, L, H, DV), s_fin


def _hgrn_gate(zf, lb):
    B, L, _ = zf.shape
    f = lb + (1.0 - lb) * jax.nn.sigmoid(zf.astype(jnp.float32))
    shp = (B, L, HG_HEADS, HG_DK)
    return (1.0 - f).reshape(shp), jnp.log(f).reshape(shp)


def _maybe_flip(t, rev):
    return jnp.flip(t, axis=1) if rev else t


def _hgrn_readout(o, zg, g_norm):
    B, L = o.shape[:2]
    o = _rmsnorm(o, g_norm.reshape(HG_HEADS, HG_DV))
    return (o.reshape(B, L, HG_HEADS * HG_DV) * jax.nn.silu(zg.astype(jnp.float32))).astype(zg.dtype)


def hgrn_mixer(zq, zi, zff, zfb, zg, zq_c, zi_c, zff_c, zfb_c, zg_c, lb_f, lb_b, g_norm, ctx_out):
    B, L, _ = zq.shape
    Lc = zq_c.shape[1]
    q = jax.nn.silu(zq.astype(jnp.float32)).reshape(B, L, HG_HEADS, HG_DK)
    v = zi.astype(jnp.float32).reshape(B, L, HG_HEADS, HG_DV)
    q_c = jax.nn.silu(zq_c.astype(jnp.float32)).reshape(B, Lc, HG_HEADS, HG_DK)
    v_c = zi_c.astype(jnp.float32).reshape(B, Lc, HG_HEADS, HG_DV)
    lb_f = lb_f.astype(jnp.float32)
    lb_b = lb_b.astype(jnp.float32)
    outs, outs_c = [], []
    for zf, zf_c, lb, rev in ((zff, zff_c, lb_f, False), (zfb, zfb_c, lb_b, True)):
        k, logf = _hgrn_gate(zf, lb)
        k_c, logf_c = _hgrn_gate(zf_c, lb)
        s0 = jnp.zeros((B, HG_HEADS, HG_DK, HG_DV), jnp.float32)
        o_c, s_ctx = _gla_chunked(_maybe_flip(q_c, rev), _maybe_flip(k_c, rev), _maybe_flip(v_c, rev),
                                  _maybe_flip(logf_c, rev), s0)
        o_l, _ = _gla_chunked(_maybe_flip(q, rev), _maybe_flip(k, rev), _maybe_flip(v, rev),
                              _maybe_flip(logf, rev), s_ctx)
        outs.append(_maybe_flip(o_l, rev))
        outs_c.append(_maybe_flip(o_c, rev))
    y = _hgrn_readout(outs[0] + outs[1], zg, g_norm)
    yc = _hgrn_readout(outs_c[0] + outs_c[1], zg_c, g_norm) if ctx_out else None
    return y, yc


def conv_mixer(z, w_dw, b_dw, ln_g, ln_b):
    a, gt = jnp.split(z, 2, axis=-1)
    u = a * jax.nn.sigmoid(gt)
    u = lax.conv_general_dilated(u, w_dw[:, None, :].astype(u.dtype), window_strides=(1,),
                                 padding=[(CONV_WIDTH // 2, CONV_WIDTH // 2)],
                                 dimension_numbers=('NWC', 'WIO', 'NWC'),
                                 feature_group_count=CONV_CH) + b_dw
    return jax.nn.silu(_layernorm(u, ln_g, ln_b))


def pool_mixer(z, w_pool, scale):
    B, L, C = z.shape
    zf = z.astype(jnp.float32)
    cs = jnp.concatenate([jnp.zeros((B, 1, C), jnp.float32), jnp.cumsum(zf, axis=1)], axis=1)
    t = jnp.arange(L)
    groups = []
    for gi, w in enumerate(POOL_WINDOWS):
        lo = jnp.clip(t - w // 2, 0, L)
        hi = jnp.clip(t + w // 2, 0, L)
        s = cs[:, hi, gi * POOL_GROUP:(gi + 1) * POOL_GROUP] - cs[:, lo, gi * POOL_GROUP:(gi + 1) * POOL_GROUP]
        groups.append(s / (hi - lo).astype(jnp.float32)[None, :, None])
    pooled = jnp.concatenate(groups, axis=-1) - zf
    y = jnp.einsum('blgc,gcd->blgd', pooled.reshape(B, L, len(POOL_WINDOWS), POOL_GROUP),
                   w_pool.astype(jnp.float32)).reshape(B, L, C)
    return (y * scale.astype(jnp.float32)).astype(z.dtype)


def _merge(branches, zgate, w_branch, w_out):
    B, L = zgate.shape[:2]
    ys = jnp.stack(branches, axis=2)
    proj = jnp.einsum('blnc,ncd->blnd', ys, w_branch)
    g = jax.nn.sigmoid(zgate.reshape(B, L, N_BRANCH, D_MODEL))
    return jnp.einsum('blnd,blnd->bld', g, proj) @ w_out


def mixer_sublayer(x, xc, shift, scale, gate, shift_c, scale_c, gate_c, g_pre, g_post, w_in,
                   mla_g_q, mla_g_kv, mla_w_uq, mla_w_ukv, lb_f, lb_b, hg_g_norm,
                   conv_w, conv_b, conv_ln_g, conv_ln_b, pool_w, pool_scale, w_branch, w_out, ctx_out):
    h = _modulate(x, g_pre, shift, scale)
    hc = _modulate(xc, g_pre, shift_c, scale_c)
    z = h @ w_in
    zc = hc @ w_in if ctx_out else hc @ w_in[:, :N_MIX_IN]
    zq, zkv, zr, hq, hi, hff, hfb, hg, zconv, zpool = _split_mix(z[..., :N_MIX_IN])
    zq_c, zkv_c, zr_c, hq_c, hi_c, hff_c, hfb_c, hg_c, zconv_c, zpool_c = _split_mix(zc[..., :N_MIX_IN])
    y_mla, yc_mla = mla_mixer(zq, zkv, zr, zq_c, zkv_c, zr_c, mla_g_q, mla_g_kv, mla_w_uq, mla_w_ukv, ctx_out)
    y_hg, yc_hg = hgrn_mixer(hq, hi, hff, hfb, hg, hq_c, hi_c, hff_c, hfb_c, hg_c, lb_f, lb_b, hg_g_norm, ctx_out)
    y_conv = conv_mixer(zconv, conv_w, conv_b, conv_ln_g, conv_ln_b)
    y_pool = pool_mixer(zpool, pool_w, pool_scale)
    y = _merge([y_mla, y_hg, y_conv, y_pool], z[..., N_MIX_IN:], w_branch, w_out)
    x_new = x + gate * _rmsnorm(y, g_post)
    xc_new = None
    if ctx_out:
        yc_conv = conv_mixer(zconv_c, conv_w, conv_b, conv_ln_g, conv_ln_b)
        yc_pool = pool_mixer(zpool_c, pool_w, pool_scale)
        yc = _merge([yc_mla, yc_hg, yc_conv, yc_pool], zc[..., N_MIX_IN:], w_branch, w_out)
        xc_new = xc + gate_c * _rmsnorm(yc, g_post)
    return x_new, xc_new


def _swiglu(h, w1, w3, w2):
    return (jax.nn.silu(h @ w1) * (h @ w3)) @ w2


def _moe_swiglu(h, w_router, w1, w3, w2):
    logits = jnp.einsum('bld,de->ble', h, w_router, preferred_element_type=jnp.float32)
    top_v, top_i = lax.top_k(logits, TOP_K)
    top_w = jax.nn.softmax(top_v, axis=-1)
    gates = jnp.einsum('blk,blke->ble', top_w,
                       jax.nn.one_hot(top_i, N_EXPERTS, dtype=jnp.float32)).astype(h.dtype)
    out = gates[..., 0:1] * _swiglu(h, w1[0], w3[0], w2[0])
    for e in range(1, N_EXPERTS):
        out = out + gates[..., e:e + 1] * _swiglu(h, w1[e], w3[e], w2[e])
    return out


def setup_inputs(seed: int = 0) -> dict:
    key = jax.random.key(seed)
    keys = list(jax.random.split(key, 40))
    f32 = jnp.float32

    def nrm(shape, scale=1.0):
        return scale * jax.random.normal(keys.pop(), shape, f32)

    def gain(shape):
        return 1.0 + nrm(shape, 0.05)

    D = D_MODEL
    n_dense = (DEPTH + 1) // 2
    n_moe = DEPTH // 2
    return {
        'x': nrm((BATCH, SEQ, D)),
        'c': nrm((BATCH, D)),
        'ctx': nrm((BATCH, CTX_LEN, D)),
        'c_ctx': nrm((D,)),
        'w_mod': nrm((DEPTH, D, 6 * D), 0.5 * D ** -0.5),
        'b_mod': nrm((DEPTH, 6 * D), 0.02),
        'g_mix_pre': gain((DEPTH, D)),
        'g_mix_post': gain((DEPTH, D)),
        'g_ffn_pre': gain((DEPTH, D)),
        'g_ffn_post': gain((DEPTH, D)),
        'w_in': nrm((DEPTH, D, N_IN), D ** -0.5),
        'mla_g_q': gain((DEPTH, MLA_Q_RANK)),
        'mla_g_kv': gain((DEPTH, MLA_KV_RANK)),
        'mla_w_uq': nrm((DEPTH, MLA_Q_RANK, MLA_HEADS * (MLA_NOPE + MLA_ROPE)), MLA_Q_RANK ** -0.5),
        'mla_w_ukv': nrm((DEPTH, MLA_KV_RANK, MLA_HEADS * (MLA_NOPE + MLA_V)), MLA_KV_RANK ** -0.5),
        'hg_lb_logits': nrm((2, DEPTH, HG_W)),
        'hg_g_norm': gain((DEPTH, HG_W)),
        'conv_w': nrm((DEPTH, CONV_WIDTH, CONV_CH), CONV_WIDTH ** -0.5),
        'conv_b': nrm((DEPTH, CONV_CH), 0.02),
        'conv_ln_g': gain((DEPTH, CONV_CH)),
        'conv_ln_b': nrm((DEPTH, CONV_CH), 0.02),
        'pool_w': nrm((DEPTH, len(POOL_WINDOWS), POOL_GROUP, POOL_GROUP), POOL_GROUP ** -0.5),
        'pool_scale': gain((DEPTH, POOL_CH)),
        'w_branch': nrm((DEPTH, N_BRANCH, BRANCH_W, D), BRANCH_W ** -0.5),
        'w_out': nrm((DEPTH, D, D), D ** -0.5),
        'ffn_w1': nrm((n_dense, D, D_FF), D ** -0.5),
        'ffn_w3': nrm((n_dense, D, D_FF), D ** -0.5),
        'ffn_w2': nrm((n_dense, D_FF, D), D_FF ** -0.5),
        'moe_router': nrm((n_moe, D, N_EXPERTS), D ** -0.5),
        'moe_w1': nrm((n_moe, N_EXPERTS, D, D_FF_EXPERT), D ** -0.5),
        'moe_w3': nrm((n_moe, N_EXPERTS, D, D_FF_EXPERT), D ** -0.5),
        'moe_w2': nrm((n_moe, N_EXPERTS, D_FF_EXPERT, D), D_FF_EXPERT ** -0.5),
    }


def reference(x, c, ctx, c_ctx, w_mod, b_mod, g_mix_pre, g_mix_post, g_ffn_pre, g_ffn_post, w_in,
              mla_g_q, mla_g_kv, mla_w_uq, mla_w_ukv, hg_lb_logits, hg_g_norm,
              conv_w, conv_b, conv_ln_g, conv_ln_b, pool_w, pool_scale, w_branch, w_out,
              ffn_w1, ffn_w3, ffn_w2, moe_router, moe_w1, moe_w3, moe_w2):
    lb = jnp.cumsum(jax.nn.softmax(hg_lb_logits.astype(jnp.float32), axis=1), axis=1)
    lb = lb - lb[:, :1]
    sc = jax.nn.silu(c)
    sc_ctx = jax.nn.silu(c_ctx)
    xc = ctx
    for l in range(DEPTH):
        ctx_out = l < DEPTH - 1
        mod = jnp.split((sc @ w_mod[l] + b_mod[l])[:, None, :], 6, axis=-1)
        mod_c = jnp.split((sc_ctx @ w_mod[l] + b_mod[l])[None, None, :], 6, axis=-1)
        x, xc_new = mixer_sublayer(x, xc, mod[0], mod[1], mod[2], mod_c[0], mod_c[1], mod_c[2],
                                   g_mix_pre[l], g_mix_post[l], w_in[l],
                                   mla_g_q[l], mla_g_kv[l], mla_w_uq[l], mla_w_ukv[l],
                                   lb[0, l], lb[1, l], hg_g_norm[l],
                                   conv_w[l], conv_b[l], conv_ln_g[l], conv_ln_b[l],
                                   pool_w[l], pool_scale[l], w_branch[l], w_out[l], ctx_out)
        j = l // 2
        if l % 2 == 0:
            ffn = functools.partial(_swiglu, w1=ffn_w1[j], w3=ffn_w3[j], w2=ffn_w2[j])
        else:
            ffn = functools.partial(_moe_swiglu, w_router=moe_router[j], w1=moe_w1[j], w3=moe_w3[j], w2=moe_w2[j])
        x = x + mod[5] * _rmsnorm(ffn(_modulate(x, g_ffn_pre[l], mod[3], mod[4])), g_ffn_post[l])
        if ctx_out:
            xc = xc_new + mod_c[5] * _rmsnorm(ffn(_modulate(xc_new, g_ffn_pre[l], mod_c[3], mod_c[4])), g_ffn_post[l])
    return x
```

```python
import functools

import numpy as np
import jax
import jax.numpy as jnp
from jax import lax
from jax.experimental import pallas as pl
from jax.experimental.pallas import tpu as pltpu

F32 = jnp.float32
BF = jnp.bfloat16

EPS = 1e-6
GRID_W = 64
ROPE_THETA = 10000.0
N_BRANCH = 4
MLA_NOPE, MLA_ROPE, MLA_V = 128, 64, 128
MLA_QK = MLA_NOPE + MLA_ROPE
HEAD_PAD = 256
HG_DK = 128
CONV_WIDTH = 31
POOL_WINDOWS = (2, 4, 8, 16)
POOL_GROUP = 128
HALO = 16
HG_CHUNK = 64
HG_EXP_CLAMP = 80.0
NEG_BIG = -1e30

VMEM_LIMIT_V7X = 56 * 1024 * 1024

Z_CONV, Z_HQ, Z_HI, Z_HFF, Z_HFB, Z_HG, Z_POOL, Z_MLA, Z_GATE = 0, 1024, 1536, 2048, 2560, 3072, 3584, 4096, 5120
MLA_ZKV_OFF, MLA_ZR_OFF = 384, 640


def _cparams(sem, vmem=VMEM_LIMIT_V7X):
    return pltpu.CompilerParams(dimension_semantics=sem, vmem_limit_bytes=vmem)


def _divisor_tile(n, target, mult=16):
    best = None
    for t in range(mult, min(n, target) + 1, mult):
        if n % t == 0:
            best = t
    assert best is not None, (n, target, mult)
    return best


def _pick_row(m, r):
    rows = lax.broadcasted_iota(jnp.int32, m.shape, 0)
    return jnp.sum(jnp.where(rows == r, m, 0.0), axis=0, keepdims=True)


def _row_vec(m_ref, b, n_batch, ctx_mask):
    m = m_ref[0, 0]
    return jnp.where(ctx_mask, _pick_row(m, n_batch), _pick_row(m, b))


def _tile_rows(i, tm, p_rows, lc):
    per_batch = p_rows // tm
    b = i // per_batch
    pos = (i % per_batch) * tm + lax.broadcasted_iota(jnp.int32, (tm, 1), 0)
    return b, pos < lc


def _rms(x, g):
    return x * lax.rsqrt(jnp.mean(x * x, axis=-1, keepdims=True) + EPS) * g


def _sigmoid(x):
    return 1.0 / (1.0 + jnp.exp(-x))


def _silu(x):
    return x * _sigmoid(x)


def _modvec_kernel(c_ref, w_ref, b_ref, o_ref):
    s = _silu(c_ref[...])
    s_hi = s.astype(BF)
    s_lo = (s - s_hi.astype(F32)).astype(BF)
    w = w_ref[0].astype(BF)
    acc = jnp.dot(s_hi, w, preferred_element_type=F32) + jnp.dot(s_lo, w, preferred_element_type=F32)
    o_ref[0, 0] = acc + b_ref[0, 0]


def _modvecs(cvec, w_mod, b_mod):
    depth, d, n6 = w_mod.shape
    nj = n6 // d
    tn = _divisor_tile(d, 1024, 128)
    npj = d // tn
    return pl.pallas_call(
        _modvec_kernel,
        grid=(depth, nj, npj),
        in_specs=[
            pl.BlockSpec((8, d), lambda l, j, n: (0, 0)),
            pl.BlockSpec((1, d, tn), lambda l, j, n: (l, 0, j * npj + n)),
            pl.BlockSpec((1, 1, 1, tn), lambda l, j, n: (l, j, 0, n)),
        ],
        out_specs=pl.BlockSpec((1, 1, 8, tn), lambda l, j, n: (l, j, 0, n)),
        out_shape=jax.ShapeDtypeStruct((depth, nj, 8, d), F32),
        compiler_params=_cparams(("parallel", "parallel", "parallel")),
        name="modvec",
    )(cvec, w_mod, b_mod.reshape(depth, nj, 1, d))


def _modulate_kernel(x_ref, g_ref, sh_ref, sc_ref, h_ref, *, tm, p_rows, lc, n_batch):
    b, cm = _tile_rows(pl.program_id(0), tm, p_rows, lc)
    shift = _row_vec(sh_ref, b, n_batch, cm)
    scale = _row_vec(sc_ref, b, n_batch, cm)
    h_ref[...] = (_rms(x_ref[...], g_ref[...]) * (1.0 + scale) + shift).astype(h_ref.dtype)


def _mod_spec(l, j, d):
    return pl.BlockSpec((1, 1, 8, d), lambda *_: (l, j, 0, 0))


def _modulate(xs, g, mods, l, j_shift, j_scale, dims):
    t, d = xs.shape
    p_rows, lc, n_batch = dims
    tm = _divisor_tile(p_rows, 544)
    return pl.pallas_call(
        functools.partial(_modulate_kernel, tm=tm, p_rows=p_rows, lc=lc, n_batch=n_batch),
        grid=(t // tm,),
        in_specs=[
            pl.BlockSpec((tm, d), lambda i: (i, 0)),
            pl.BlockSpec((1, d), lambda i: (0, 0)),
            _mod_spec(l, j_shift, d),
            _mod_spec(l, j_scale, d),
        ],
        out_specs=pl.BlockSpec((tm, d), lambda i: (i, 0)),
        out_shape=jax.ShapeDtypeStruct((t, d), BF),
        compiler_params=_cparams(("parallel",)),
        name="modulate",
    )(xs, g.reshape(1, d), mods, mods)


def _mm_kernel(x_ref, w_ref, o_ref):
    o_ref[...] = jnp.dot(x_ref[...], w_ref[...], preferred_element_type=F32).astype(o_ref.dtype)


def _matmul(x, w, out_dtype, tm_target=1088, tn_target=1024):
    t, k = x.shape
    n = w.shape[1]
    tm = _divisor_tile(t, tm_target)
    tn = _divisor_tile(n, tn_target, 128)
    return pl.pallas_call(
        _mm_kernel,
        grid=(n // tn, t // tm),
        in_specs=[
            pl.BlockSpec((tm, k), lambda j, i: (i, 0)),
            pl.BlockSpec((k, tn), lambda j, i: (0, j)),
        ],
        out_specs=pl.BlockSpec((tm, tn), lambda j, i: (i, j)),
        out_shape=jax.ShapeDtypeStruct((t, n), out_dtype),
        compiler_params=_cparams(("parallel", "arbitrary")),
        name="in_proj",
    )(x, w)


def _rope128(t, cos, sin, pm):
    partner = jnp.dot(t.astype(BF), pm, preferred_element_type=F32)
    return t * cos + partner * sin


def _mla_proj_kernel(z_ref, gq_ref, gkv_ref, wq_ref, wkv_ref, cos_ref, sin_ref, pm_ref,
                     q_ref, k_ref, v_ref, wq_s, wkv_s, *, n_heads, q_rank, kv_rank, scale):
    @pl.when(pl.program_id(0) == 0)
    def _():
        wq_s[...] = wq_ref[...].astype(BF)
        wkv_s[...] = wkv_ref[...].astype(BF)

    z = z_ref[...]
    zq = z[:, :q_rank].astype(F32)
    zkv = z[:, MLA_ZKV_OFF:MLA_ZKV_OFF + kv_rank].astype(F32)
    zr = z[:, MLA_ZR_OFF:MLA_ZR_OFF + 128].astype(F32)
    cos, sin, pm = cos_ref[...], sin_ref[...], pm_ref[...]
    qn = _rms(zq, gq_ref[...]).astype(BF)
    kvn = _rms(zkv, gkv_ref[...]).astype(BF)
    k_rope = _rope128(zr, cos, sin, pm)[:, :MLA_ROPE]
    for h in range(n_heads):
        qh = jnp.dot(qn, wq_s[:, h * HEAD_PAD:(h + 1) * HEAD_PAD], preferred_element_type=F32)
        q_rope = _rope128(qh[:, MLA_NOPE:], cos, sin, pm)[:, :MLA_ROPE]
        q_ref[h] = (jnp.concatenate([qh[:, :MLA_NOPE], q_rope], axis=-1) * scale).astype(q_ref.dtype)
        kvh = jnp.dot(kvn, wkv_s[:, h * HEAD_PAD:(h + 1) * HEAD_PAD], preferred_element_type=F32)
        k_ref[h] = jnp.concatenate([kvh[:, :MLA_NOPE], k_rope], axis=-1).astype(k_ref.dtype)
        v_ref[h] = kvh[:, MLA_NOPE:].astype(v_ref.dtype)


def _mla_proj(z, g_q, g_kv, wq_pad, wkv, cos, sin, pm, dims):
    t = z.shape[0]
    p_rows = dims[0]
    q_rank, kv_rank = g_q.shape[0], g_kv.shape[0]
    n_heads = wkv.shape[1] // HEAD_PAD
    tm = _divisor_tile(p_rows, 544)
    npb = p_rows // tm
    scale = float(MLA_QK) ** -0.5
    const = lambda i: (0, 0)
    return pl.pallas_call(
        functools.partial(_mla_proj_kernel, n_heads=n_heads, q_rank=q_rank, kv_rank=kv_rank, scale=scale),
        grid=(t // tm,),
        in_specs=[
            pl.BlockSpec((tm, 1024), lambda i: (i, Z_MLA // 1024)),
            pl.BlockSpec((1, q_rank), const),
            pl.BlockSpec((1, kv_rank), const),
            pl.BlockSpec(wq_pad.shape, const),
            pl.BlockSpec(wkv.shape, const),
            pl.BlockSpec((tm, 128), lambda i: (i % npb, 0)),
            pl.BlockSpec((tm, 128), lambda i: (i % npb, 0)),
            pl.BlockSpec((128, 128), const),
        ],
        out_specs=[
            pl.BlockSpec((n_heads, tm, MLA_QK), lambda i: (0, i, 0)),
            pl.BlockSpec((n_heads, tm, MLA_QK), lambda i: (0, i, 0)),
            pl.BlockSpec((n_heads, tm, MLA_V), lambda i: (0, i, 0)),
        ],
        out_shape=[
            jax.ShapeDtypeStruct((n_heads, t, MLA_QK), BF),
            jax.ShapeDtypeStruct((n_heads, t, MLA_QK), BF),
            jax.ShapeDtypeStruct((n_heads, t, MLA_V), BF),
        ],
        scratch_shapes=[pltpu.VMEM(wq_pad.shape, BF), pltpu.VMEM(wkv.shape, BF)],
        compiler_params=_cparams(("arbitrary",)),
        name="mla_proj",
    )(z, g_q.reshape(1, -1), g_kv.reshape(1, -1), wq_pad, wkv, cos, sin, pm)


def _attn_kernel(q_ref, k_ref, v_ref, o_ref, *, tq, lc):
    s = lax.dot_general(q_ref[0], k_ref[0], (((1,), (1,)), ((), ())), preferred_element_type=F32)
    col = lax.broadcasted_iota(jnp.int32, s.shape, 1)
    ctx_tile = pl.program_id(2) * tq < lc
    s = jnp.where(jnp.logical_and(ctx_tile, col >= lc), NEG_BIG, s)
    p = jnp.exp(s - jnp.max(s, axis=-1, keepdims=True))
    denom = jnp.sum(p, axis=-1, keepdims=True)
    o = jnp.dot(p.astype(BF), v_ref[0], preferred_element_type=F32)
    o_ref[...] = (o / denom).astype(o_ref.dtype)


def _attention(q3, k3, v3, dims):
    n_heads, t, _ = q3.shape
    p_rows, lc, n_batch = dims
    tq = _divisor_tile(lc, 256)
    nq = p_rows // tq
    return pl.pallas_call(
        functools.partial(_attn_kernel, tq=tq, lc=lc),
        grid=(n_batch, n_heads, nq),
        in_specs=[
            pl.BlockSpec((1, tq, MLA_QK), lambda b, h, i: (h, b * nq + i, 0)),
            pl.BlockSpec((1, p_rows, MLA_QK), lambda b, h, i: (h, b, 0)),
            pl.BlockSpec((1, p_rows, MLA_V), lambda b, h, i: (h, b, 0)),
        ],
        out_specs=pl.BlockSpec((tq, MLA_V), lambda b, h, i: (b * nq + i, h)),
        out_shape=jax.ShapeDtypeStruct((t, n_heads * MLA_V), BF),
        compiler_params=_cparams(("parallel", "parallel", "arbitrary")),
        name="mla_attention",
    )(q3, k3, v3)


def _hgrn_kernel(zq_ref, zi_ref, zf_ref, lb_ref, tri_ref, o_ref, st_ref, *, n_heads):
    @pl.when(pl.program_id(1) == 0)
    def _():
        st_ref[...] = jnp.zeros_like(st_ref)

    tri = tri_ref[0]
    feeds = tri > 0.5
    lb = lb_ref[0]
    f = lb + (1.0 - lb) * _sigmoid(zf_ref[...].astype(F32))
    lf = jnp.log(f)
    lf1 = lf.astype(BF)
    r1 = lf - lf1.astype(F32)
    lf2 = r1.astype(BF)
    lf3 = (r1 - lf2.astype(F32)).astype(BF)
    b_all = (jnp.dot(tri, lf1, preferred_element_type=F32) + jnp.dot(tri, lf2, preferred_element_type=F32)
             + jnp.dot(tri, lf3, preferred_element_type=F32))
    tot_all = jnp.sum(lf, axis=0, keepdims=True)
    q_all = _silu(zq_ref[...].astype(F32))
    nt = (((1,), (1,)), ((), ()))
    for h in range(n_heads):
        sl = slice(h * HG_DK, (h + 1) * HG_DK)
        b, tot, q, k = b_all[:, sl], tot_all[:, sl], q_all[:, sl], 1.0 - f[:, sl]
        v = zi_ref[:, sl]
        ref = 0.5 * tot
        q_loc = (q * jnp.exp(jnp.minimum(b - ref, HG_EXP_CLAMP))).astype(BF)
        k_loc = (k * jnp.exp(jnp.minimum(ref - b, HG_EXP_CLAMP))).astype(BF)
        a = lax.dot_general(q_loc, k_loc, nt, preferred_element_type=F32)
        a = jnp.where(feeds, a, 0.0).astype(BF)
        state_t = st_ref[h]
        q_in = (q * jnp.exp(b)).astype(BF)
        o = jnp.dot(a, v, preferred_element_type=F32)
        o = o + lax.dot_general(q_in, state_t.astype(BF), nt, preferred_element_type=F32)
        o_ref[0, :, sl] = o
        k_out = (k * jnp.exp(tot - b)).astype(BF)
        upd = lax.dot_general(v, k_out, (((0,), (0,)), ((), ())), preferred_element_type=F32)
        st_ref[h] = state_t * jnp.exp(tot) + upd


def _hgrn(z, lb2, dims):
    t = z.shape[0]
    p_rows, lc, n_batch = dims
    c = HG_CHUNK
    w = lb2.shape[-1]
    n_heads = w // HG_DK
    nch, nctx = p_rows // c, lc // c
    idx = np.arange(c)
    tri = jnp.asarray(np.stack([idx[None, :] <= idx[:, None], idx[None, :] >= idx[:, None]]), BF)

    def chunk(g, j):
        rev = g % 2
        back = jnp.where(j < nctx, nctx - 1 - j, nch - 1 - (j - nctx))
        return (g // 2) * nch + jnp.where(rev == 1, back, j)

    return pl.pallas_call(
        functools.partial(_hgrn_kernel, n_heads=n_heads),
        grid=(2 * n_batch, nch),
        in_specs=[
            pl.BlockSpec((c, w), lambda g, j: (chunk(g, j), Z_HQ // w)),
            pl.BlockSpec((c, w), lambda g, j: (chunk(g, j), Z_HI // w)),
            pl.BlockSpec((c, w), lambda g, j: (chunk(g, j), Z_HFF // w + g % 2)),
            pl.BlockSpec((1, 1, w), lambda g, j: (g % 2, 0, 0)),
            pl.BlockSpec((1, c, c), lambda g, j: (g % 2, 0, 0)),
        ],
        out_specs=pl.BlockSpec((1, c, w), lambda g, j: (g % 2, chunk(g, j), 0)),
        out_shape=jax.ShapeDtypeStruct((2, t, w), F32),
        scratch_shapes=[pltpu.VMEM((n_heads, HG_DK, HG_DK), F32)],
        compiler_params=_cparams(("parallel", "arbitrary")),
        name="hgrn_scan",
    )(z, z, z, lb2.reshape(2, 1, w), tri)


def _hgrn_readout_kernel(of_ref, ob_ref, zg_ref, g_ref, y_ref, *, n_heads):
    o = of_ref[0] + ob_ref[0]
    g = g_ref[...]
    gate = _silu(zg_ref[...].astype(F32))
    for h in range(n_heads):
        sl = slice(h * HG_DK, (h + 1) * HG_DK)
        y_ref[:, sl] = (_rms(o[:, sl], g[:, sl]) * gate[:, sl]).astype(y_ref.dtype)


def _hgrn_readout(o2, z, g_norm):
    _, t, w = o2.shape
    tm = _divisor_tile(t, 1088)
    return pl.pallas_call(
        functools.partial(_hgrn_readout_kernel, n_heads=w // HG_DK),
        grid=(t // tm,),
        in_specs=[
            pl.BlockSpec((1, tm, w), lambda i: (0, i, 0)),
            pl.BlockSpec((1, tm, w), lambda i: (1, i, 0)),
            pl.BlockSpec((tm, w), lambda i: (i, Z_HG // w)),
            pl.BlockSpec((1, w), lambda i: (0, 0)),
        ],
        out_specs=pl.BlockSpec((tm, w), lambda i: (i, 0)),
        out_shape=jax.ShapeDtypeStruct((t, w), BF),
        compiler_params=_cparams(("parallel",)),
        name="hgrn_readout",
    )(o2, o2, z, g_norm.reshape(1, w))


def _segment_tile(i, tc, p_rows, lc):
    per_batch = p_rows // tc
    j = i % per_batch
    nctx = lc // tc
    first = jnp.logical_or(j == 0, j == nctx)
    last = jnp.logical_or(j == nctx - 1, j == per_batch - 1)
    in_ctx = j < nctx
    seg_pos0 = jnp.where(in_ctx, j, j - nctx) * tc
    seg_len = jnp.where(in_ctx, lc, p_rows - lc)
    return first, last, seg_pos0, seg_len


def _halo_specs(tc, width, col_block, n_rows):
    per = tc // HALO
    last_blk = n_rows // HALO - 1
    return [
        pl.BlockSpec((HALO, width), lambda i: (jnp.maximum(i * per - 1, 0), col_block)),
        pl.BlockSpec((tc, width), lambda i: (i, col_block)),
        pl.BlockSpec((HALO, width), lambda i: (jnp.minimum((i + 1) * per, last_blk), col_block)),
    ]


def _conv_kernel(prev_ref, cur_ref, next_ref, w_ref, b_ref, lg_ref, lb_ref, y_ref, buf, *, tc, p_rows, lc, ch):
    first, last, _, _ = _segment_tile(pl.program_id(0), tc, p_rows, lc)

    def glu(zz):
        zz = zz.astype(F32)
        return zz[:, :ch] * _sigmoid(zz[:, ch:])

    buf[0:HALO, :] = jnp.where(first, 0.0, glu(prev_ref[...]))
    buf[HALO:HALO + tc, :] = glu(cur_ref[...])
    buf[HALO + tc:, :] = jnp.where(last, 0.0, glu(next_ref[...]))
    w = w_ref[...]
    acc = jnp.zeros((tc, ch), F32) + b_ref[...]
    for kk in range(CONV_WIDTH):
        off = HALO - CONV_WIDTH // 2 + kk
        acc = acc + buf[off:off + tc, :] * w[kk:kk + 1, :]
    mu = jnp.mean(acc, axis=-1, keepdims=True)
    cen = acc - mu
    var = jnp.mean(cen * cen, axis=-1, keepdims=True)
    y = cen * lax.rsqrt(var + EPS) * lg_ref[...] + lb_ref[...]
    y_ref[...] = _silu(y).astype(y_ref.dtype)


def _conv_mixer(z, w_dw, b_dw, ln_g, ln_b, dims):
    t = z.shape[0]
    p_rows, lc, _ = dims
    ch = w_dw.shape[1]
    tc = _divisor_tile(lc, 256)
    row = lambda a: a.reshape(1, ch)
    const = lambda i: (0, 0)
    return pl.pallas_call(
        functools.partial(_conv_kernel, tc=tc, p_rows=p_rows, lc=lc, ch=ch),
        grid=(t // tc,),
        in_specs=_halo_specs(tc, 2 * ch, Z_CONV // (2 * ch), t) + [
            pl.BlockSpec((CONV_WIDTH, ch), const),
            pl.BlockSpec((1, ch), const), pl.BlockSpec((1, ch), const), pl.BlockSpec((1, ch), const),
        ],
        out_specs=pl.BlockSpec((tc, ch), lambda i: (i, 0)),
        out_shape=jax.ShapeDtypeStruct((t, ch), BF),
        scratch_shapes=[pltpu.VMEM((tc + 2 * HALO, ch), F32)],
        compiler_params=_cparams(("parallel",)),
        name="conv_mixer",
    )(z, z, z, w_dw, row(b_dw), row(ln_g), row(ln_b))


def _pool_kernel(prev_ref, cur_ref, next_ref, w_ref, s_ref, y_ref, buf, *, tc, p_rows, lc):
    first, last, seg_pos0, seg_len = _segment_tile(pl.program_id(0), tc, p_rows, lc)
    buf[0:HALO, :] = jnp.where(first, 0.0, prev_ref[...].astype(F32))
    buf[HALO:HALO + tc, :] = cur_ref[...].astype(F32)
    buf[HALO + tc:, :] = jnp.where(last, 0.0, next_ref[...].astype(F32))
    pos = seg_pos0 + lax.broadcasted_iota(jnp.int32, (tc, 1), 0)
    scale = s_ref[...]
    for gi, win in enumerate(POOL_WINDOWS):
        sl = slice(gi * POOL_GROUP, (gi + 1) * POOL_GROUP)
        half = win // 2
        tot = jnp.zeros((tc, POOL_GROUP), F32)
        for u in range(-half, half):
            tot = tot + buf[HALO + u:HALO + u + tc, sl]
        cnt = (jnp.minimum(pos + half, seg_len) - jnp.maximum(pos - half, 0)).astype(F32)
        pooled = tot / cnt - buf[HALO:HALO + tc, sl]
        y = jnp.dot(pooled.astype(BF), w_ref[gi].astype(BF), preferred_element_type=F32)
        y_ref[:, sl] = (y * scale[:, sl]).astype(y_ref.dtype)


def _pool_mixer(z, w_pool, scale, dims):
    t = z.shape[0]
    p_rows, lc, _ = dims
    ch = scale.shape[0]
    tc = _divisor_tile(lc, 256)
    return pl.pallas_call(
        functools.partial(_pool_kernel, tc=tc, p_rows=p_rows, lc=lc),
        grid=(t // tc,),
        in_specs=_halo_specs(tc, ch, Z_POOL // ch, t) + [
            pl.BlockSpec(w_pool.shape, lambda i: (0, 0, 0)),
            pl.BlockSpec((1, ch), lambda i: (0, 0)),
        ],
        out_specs=pl.BlockSpec((tc, ch), lambda i: (i, 0)),
        out_shape=jax.ShapeDtypeStruct((t, ch), BF),
        scratch_shapes=[pltpu.VMEM((tc + 2 * HALO, ch), F32)],
        compiler_params=_cparams(("parallel",)),
        name="pool_mixer",
    )(z, z, z, w_pool, scale.reshape(1, ch))


def _merge_kernel(y0, y1, y2, y3, g0, g1, g2, g3, w_ref, o_ref, w_s):
    @pl.when(pl.program_id(1) == 0)
    def _():
        w_s[...] = w_ref[...].astype(BF)

    acc = None
    for n, (y_ref, g_ref) in enumerate(((y0, g0), (y1, g1), (y2, g2), (y3, g3))):
        proj = jnp.dot(y_ref[...], w_s[n], preferred_element_type=F32)
        term = _sigmoid(g_ref[...].astype(F32)) * proj
        acc = term if acc is None else acc + term
    o_ref[...] = acc.astype(o_ref.dtype)


def _merge(ys, z, w_branch):
    t, bw = ys[0].shape
    nb, _, d = w_branch.shape
    tm = _divisor_tile(t, 544)
    tn = _divisor_tile(d, 1024, 128)
    npn = d // tn

    def gate_spec(n):
        return pl.BlockSpec((tm, tn), lambda j, i: (i, Z_GATE // tn + n * npn + j))

    return pl.pallas_call(
        _merge_kernel,
        grid=(npn, t // tm),
        in_specs=[pl.BlockSpec((tm, bw), lambda j, i: (i, 0))] * nb + [gate_spec(n) for n in range(nb)] + [
            pl.BlockSpec((nb, bw, tn), lambda j, i: (0, 0, j))],
        out_specs=pl.BlockSpec((tm, tn), lambda j, i: (i, j)),
        out_shape=jax.ShapeDtypeStruct((t, d), BF),
        scratch_shapes=[pltpu.VMEM((nb, bw, tn), BF)],
        compiler_params=_cparams(("parallel", "arbitrary")),
        name="branch_merge",
    )(*ys, z, z, z, z, w_branch)


def _residual_epilogue(y, x_ref, gpost_ref, gate_ref, gnext_ref, sh_ref, sc_ref, xo_ref, ho_ref, b, n_batch, cm):
    gate = _row_vec(gate_ref, b, n_batch, cm)
    x_new = x_ref[...] + gate * _rms(y, gpost_ref[...])
    xo_ref[...] = x_new
    if ho_ref is None:
        return None
    shift = _row_vec(sh_ref, b, n_batch, cm)
    scale = _row_vec(sc_ref, b, n_batch, cm)
    h = _rms(x_new, gnext_ref[...]) * (1.0 + scale) + shift
    ho_ref[...] = h.astype(ho_ref.dtype)
    return h


def _route_top2(logits, n_experts):
    lane = lax.broadcasted_iota(jnp.int32, logits.shape, 1)
    lg = jnp.where(lane < n_experts, logits, NEG_BIG)
    m1 = jnp.max(lg, axis=-1, keepdims=True)
    i1 = jnp.min(jnp.where(lg == m1, lane, 1 << 20), axis=-1, keepdims=True)
    lg2 = jnp.where(lane == i1, NEG_BIG, lg)
    m2 = jnp.max(lg2, axis=-1, keepdims=True)
    i2 = jnp.min(jnp.where(lg2 == m2, lane, 1 << 20), axis=-1, keepdims=True)
    e2 = jnp.exp(m2 - m1)
    w1 = 1.0 / (1.0 + e2)
    w2 = e2 / (1.0 + e2)
    out = jnp.where(lane == 0, i1.astype(F32), 0.0)
    out = jnp.where(lane == 1, i2.astype(F32), out)
    out = jnp.where(lane == 2, w1, out)
    return jnp.where(lane == 3, w2, out)


def _router_logits(h, wr_ref):
    h_hi = h.astype(BF)
    h_lo = (h - h_hi.astype(F32)).astype(BF)
    return (jnp.dot(h_hi, wr_ref[0], preferred_element_type=F32) + jnp.dot(h_hi, wr_ref[1], preferred_element_type=F32)
            + jnp.dot(h_lo, wr_ref[0], preferred_element_type=F32))


def _out_proj_kernel(*refs, tm, p_rows, lc, n_batch, emit_h, n_experts):
    m_ref, w_ref, x_ref, gpost_ref, gate_ref = refs[:5]
    pos = 5
    gnext_ref = sh_ref = sc_ref = wr_ref = None
    if emit_h:
        gnext_ref, sh_ref, sc_ref = refs[pos:pos + 3]
        pos += 3
    if n_experts:
        wr_ref = refs[pos]
        pos += 1
    xo_ref = refs[pos]
    pos += 1
    ho_ref = route_ref = None
    if emit_h:
        ho_ref = refs[pos]
        pos += 1
    if n_experts:
        route_ref = refs[pos]
        pos += 1
    b, cm = _tile_rows(pl.program_id(0), tm, p_rows, lc)
    y = jnp.dot(m_ref[...], w_ref[...], preferred_element_type=F32)
    h = _residual_epilogue(y, x_ref, gpost_ref, gate_ref, gnext_ref, sh_ref, sc_ref, xo_ref, ho_ref, b, n_batch, cm)
    if n_experts:
        route_ref[...] = _route_top2(_router_logits(h, wr_ref), n_experts)


def _out_proj(m, w_out, xs, g_post, mods, l, j_gate, nxt, dims, router=None):
    t, d = xs.shape
    p_rows, lc, n_batch = dims
    tm = _divisor_tile(p_rows, 544)
    const = lambda i: (0, 0)
    rowtile = pl.BlockSpec((tm, d), lambda i: (i, 0))
    args = [m, w_out.astype(BF), xs, g_post.reshape(1, d), mods]
    in_specs = [rowtile, pl.BlockSpec((d, d), const), rowtile, pl.BlockSpec((1, d), const), _mod_spec(l, j_gate, d)]
    out_shape = [jax.ShapeDtypeStruct((t, d), F32)]
    out_specs = [rowtile]
    n_experts = 0
    if nxt is not None:
        g_next, l_next, j_shift, j_scale = nxt
        args += [g_next.reshape(1, d), mods, mods]
        in_specs += [pl.BlockSpec((1, d), const), _mod_spec(l_next, j_shift, d), _mod_spec(l_next, j_scale, d)]
        out_shape.append(jax.ShapeDtypeStruct((t, d), BF))
        out_specs.append(rowtile)
    if router is not None:
        wr, n_experts = router
        args.append(wr)
        in_specs.append(pl.BlockSpec(wr.shape, lambda i: (0, 0, 0)))
        out_shape.append(jax.ShapeDtypeStruct((t, 128), F32))
        out_specs.append(pl.BlockSpec((tm, 128), lambda i: (i, 0)))
    return pl.pallas_call(
        functools.partial(_out_proj_kernel, tm=tm, p_rows=p_rows, lc=lc, n_batch=n_batch,
                          emit_h=nxt is not None, n_experts=n_experts),
        grid=(t // tm,),
        in_specs=in_specs,
        out_specs=out_specs,
        out_shape=out_shape,
        compiler_params=_cparams(("parallel",)),
        name="out_proj",
    )(*args)


def _swiglu_kernel(x_ref, w1_ref, w3_ref, o_ref, w1_s, w3_s):
    @pl.when(pl.program_id(1) == 0)
    def _():
        w1_s[...] = w1_ref[...].astype(BF)
        w3_s[...] = w3_ref[...].astype(BF)

    x = x_ref[...]
    a = jnp.dot(x, w1_s[...], preferred_element_type=F32)
    g = jnp.dot(x, w3_s[...], preferred_element_type=F32)
    o_ref[...] = (_silu(a) * g).astype(o_ref.dtype)


def _swiglu_up(h, w1, w3):
    t, d = h.shape
    dff = w1.shape[1]
    tm = _divisor_tile(t, 1088)
    tn = _divisor_tile(dff, 512, 128)
    return pl.pallas_call(
        _swiglu_kernel,
        grid=(dff // tn, t // tm),
        in_specs=[
            pl.BlockSpec((tm, d), lambda j, i: (i, 0)),
            pl.BlockSpec((d, tn), lambda j, i: (0, j)),
            pl.BlockSpec((d, tn), lambda j, i: (0, j)),
        ],
        out_specs=pl.BlockSpec((tm, tn), lambda j, i: (i, j)),
        out_shape=jax.ShapeDtypeStruct((t, dff), BF),
        scratch_shapes=[pltpu.VMEM((d, tn), BF), pltpu.VMEM((d, tn), BF)],
        compiler_params=_cparams(("parallel", "arbitrary")),
        name="swiglu_up",
    )(h, w1, w3)


def _ffn_down_kernel(*refs, tm, p_rows, lc, n_batch, emit_h):
    a_ref, w_ref, x_ref, gpost_ref, gate_ref = refs[:5]
    gnext_ref = sh_ref = sc_ref = ho_ref = None
    if emit_h:
        gnext_ref, sh_ref, sc_ref, xo_ref, ho_ref, acc = refs[5:]
    else:
        xo_ref, acc = refs[5:]
    kk = pl.program_id(1)

    @pl.when(kk == 0)
    def _():
        acc[...] = jnp.zeros_like(acc)

    acc[...] += jnp.dot(a_ref[...], w_ref[...].astype(BF), preferred_element_type=F32)

    @pl.when(kk == pl.num_programs(1) - 1)
    def _():
        b, cm = _tile_rows(pl.program_id(0), tm, p_rows, lc)
        _residual_epilogue(acc[...], x_ref, gpost_ref, gate_ref, gnext_ref, sh_ref, sc_ref, xo_ref, ho_ref,
                           b, n_batch, cm)


def _ffn_down(act, w2, xs, g_post, mods, l, j_gate, nxt, dims):
    t, d = xs.shape
    dff = act.shape[1]
    p_rows, lc, n_batch = dims
    tm = _divisor_tile(p_rows, 544)
    tk = _divisor_tile(dff, 512, 128)
    const = lambda i, k: (0, 0)
    rowtile = pl.BlockSpec((tm, d), lambda i, k: (i, 0))

    def mod_spec(ll, j):
        return pl.BlockSpec((1, 1, 8, d), lambda i, k: (ll, j, 0, 0))

    args = [act, w2, xs, g_post.reshape(1, d), mods]
    in_specs = [pl.BlockSpec((tm, tk), lambda i, k: (i, k)), pl.BlockSpec((tk, d), lambda i, k: (k, 0)), rowtile,
                pl.BlockSpec((1, d), const), mod_spec(l, j_gate)]
    out_shape = [jax.ShapeDtypeStruct((t, d), F32)]
    out_specs = [rowtile]
    if nxt is not None:
        g_next, l_next, j_shift, j_scale = nxt
        args += [g_next.reshape(1, d), mods, mods]
        in_specs += [pl.BlockSpec((1, d), const), mod_spec(l_next, j_shift), mod_spec(l_next, j_scale)]
        out_shape.append(jax.ShapeDtypeStruct((t, d), BF))
        out_specs.append(rowtile)
    return pl.pallas_call(
        functools.partial(_ffn_down_kernel, tm=tm, p_rows=p_rows, lc=lc, n_batch=n_batch, emit_h=nxt is not None),
        grid=(t // tm, dff // tk),
        in_specs=in_specs,
        out_specs=out_specs,
        out_shape=out_shape,
        scratch_shapes=[pltpu.VMEM((tm, d), F32)],
        compiler_params=_cparams(("parallel", "arbitrary")),
        name="ffn_down",
    )(*args)


def _gather_rows_kernel(idx_ref, src_ref, o_ref, sem, *, tr):
    base = pl.program_id(0) * tr

    def copy(r):
        return pltpu.make_async_copy(src_ref.at[idx_ref[base + r]], o_ref.at[r], sem)

    def issue(r, carry):
        copy(r).start()
        return carry

    lax.fori_loop(0, tr, issue, 0)

    def drain(r, carry):
        copy(r).wait()
        return carry

    lax.fori_loop(0, tr, drain, 0)


def _gather_rows(src, idx):
    n, d = src.shape
    rows = idx.shape[0]
    tr = _divisor_tile(rows, 256, 8)
    lanes = 128
    src3 = src.reshape(n, d // lanes, lanes)
    out = pl.pallas_call(
        functools.partial(_gather_rows_kernel, tr=tr),
        grid_spec=pltpu.PrefetchScalarGridSpec(
            num_scalar_prefetch=1,
            grid=(rows // tr,),
            in_specs=[pl.BlockSpec(memory_space=pl.ANY)],
            out_specs=pl.BlockSpec((tr, d // lanes, lanes), lambda i, idx_ref: (i, 0, 0)),
            scratch_shapes=[pltpu.SemaphoreType.DMA(())],
        ),
        out_shape=jax.ShapeDtypeStruct((rows, d // lanes, lanes), src.dtype),
        compiler_params=_cparams(("arbitrary",)),
        name="moe_gather",
    )(idx, src3)
    return out.reshape(rows, d)


def _moe_up_kernel(te_ref, nv_ref, x_ref, w1_ref, w3_ref, o_ref, w1_s, w3_s):
    i = pl.program_id(1)
    fresh = jnp.logical_or(i == 0, te_ref[i] != te_ref[jnp.maximum(i - 1, 0)])

    @pl.when(fresh)
    def _():
        w1_s[...] = w1_ref[0].astype(BF)
        w3_s[...] = w3_ref[0].astype(BF)

    @pl.when(i < nv_ref[0])
    def _():
        x = x_ref[...]
        a = jnp.dot(x, w1_s[...], preferred_element_type=F32)
        g = jnp.dot(x, w3_s[...], preferred_element_type=F32)
        o_ref[...] = (_silu(a) * g).astype(o_ref.dtype)

    @pl.when(i >= nv_ref[0])
    def _():
        o_ref[...] = jnp.zeros_like(o_ref)


def _moe_up(xs, w1, w3, tile_expert, n_valid, tm):
    rows, d = xs.shape
    dff = w1.shape[2]
    tn = _divisor_tile(dff, 512, 128)
    return pl.pallas_call(
        _moe_up_kernel,
        grid_spec=pltpu.PrefetchScalarGridSpec(
            num_scalar_prefetch=2,
            grid=(dff // tn, rows // tm),
            in_specs=[
                pl.BlockSpec((tm, d), lambda j, i, te, nv: (i, 0)),
                pl.BlockSpec((1, d, tn), lambda j, i, te, nv: (te[i], 0, j)),
                pl.BlockSpec((1, d, tn), lambda j, i, te, nv: (te[i], 0, j)),
            ],
            out_specs=pl.BlockSpec((tm, tn), lambda j, i, te, nv: (i, j)),
            scratch_shapes=[pltpu.VMEM((d, tn), BF), pltpu.VMEM((d, tn), BF)],
        ),
        out_shape=jax.ShapeDtypeStruct((rows, dff), BF),
        compiler_params=_cparams(("arbitrary", "arbitrary")),
        name="moe_up",
    )(tile_expert, n_valid, xs, w1, w3)


def _moe_down_kernel(te_ref, nv_ref, a_ref, w_ref, rw_ref, o_ref, w_s):
    i = pl.program_id(1)
    fresh = jnp.logical_or(i == 0, te_ref[i] != te_ref[jnp.maximum(i - 1, 0)])

    @pl.when(fresh)
    def _():
        w_s[...] = w_ref[0].astype(BF)

    @pl.when(i < nv_ref[0])
    def _():
        y = jnp.dot(a_ref[...], w_s[...], preferred_element_type=F32)
        o_ref[...] = (y * rw_ref[...]).astype(o_ref.dtype)

    @pl.when(i >= nv_ref[0])
    def _():
        o_ref[...] = jnp.zeros_like(o_ref)


def _moe_down(act, w2, row_w, tile_expert, n_valid, tm):
    rows, dff = act.shape
    d = w2.shape[2]
    tn = _divisor_tile(d, 512, 128)
    return pl.pallas_call(
        _moe_down_kernel,
        grid_spec=pltpu.PrefetchScalarGridSpec(
            num_scalar_prefetch=2,
            grid=(d // tn, rows // tm),
            in_specs=[
                pl.BlockSpec((tm, dff), lambda j, i, te, nv: (i, 0)),
                pl.BlockSpec((1, dff, tn), lambda j, i, te, nv: (te[i], 0, j)),
                pl.BlockSpec((tm, 1), lambda j, i, te, nv: (i, 0)),
            ],
            out_specs=pl.BlockSpec((tm, tn), lambda j, i, te, nv: (i, j)),
            scratch_shapes=[pltpu.VMEM((dff, tn), BF)],
        ),
        out_shape=jax.ShapeDtypeStruct((rows, d), F32),
        compiler_params=_cparams(("arbitrary", "arbitrary")),
        name="moe_down",
    )(tile_expert, n_valid, act, w2, row_w)


def _combine_kernel(*refs, tm, p_rows, lc, n_batch, emit_h):
    ya_ref, yb_ref, x_ref, gpost_ref, gate_ref = refs[:5]
    gnext_ref = sh_ref = sc_ref = ho_ref = None
    if emit_h:
        gnext_ref, sh_ref, sc_ref, xo_ref, ho_ref = refs[5:]
    else:
        (xo_ref,) = refs[5:]
    b, cm = _tile_rows(pl.program_id(0), tm, p_rows, lc)
    y = ya_ref[...] + yb_ref[...]
    _residual_epilogue(y, x_ref, gpost_ref, gate_ref, gnext_ref, sh_ref, sc_ref, xo_ref, ho_ref, b, n_batch, cm)


def _moe_combine(y2, xs, g_post, mods, l, j_gate, nxt, dims):
    t, d = xs.shape
    p_rows, lc, n_batch = dims
    tm = _divisor_tile(p_rows, 544)
    nt = t // tm
    const = lambda i: (0, 0)
    rowtile = pl.BlockSpec((tm, d), lambda i: (i, 0))
    args = [y2, y2, xs, g_post.reshape(1, d), mods]
    in_specs = [rowtile, pl.BlockSpec((tm, d), lambda i: (i + nt, 0)), rowtile, pl.BlockSpec((1, d), const),
                _mod_spec(l, j_gate, d)]
    out_shape = [jax.ShapeDtypeStruct((t, d), F32)]
    out_specs = [rowtile]
    if nxt is not None:
        g_next, l_next, j_shift, j_scale = nxt
        args += [g_next.reshape(1, d), mods, mods]
        in_specs += [pl.BlockSpec((1, d), const), _mod_spec(l_next, j_shift, d), _mod_spec(l_next, j_scale, d)]
        out_shape.append(jax.ShapeDtypeStruct((t, d), BF))
        out_specs.append(rowtile)
    return pl.pallas_call(
        functools.partial(_combine_kernel, tm=tm, p_rows=p_rows, lc=lc, n_batch=n_batch, emit_h=nxt is not None),
        grid=(nt,),
        in_specs=in_specs,
        out_specs=out_specs,
        out_shape=out_shape,
        compiler_params=_cparams(("parallel",)),
        name="moe_combine",
    )(*args)


def _moe_ffn(h, route, w1, w3, w2, xs, g_post, mods, l, nxt, dims, tm=512):
    t, d = h.shape
    n_exp = w1.shape[0]
    n_assign = 2 * t
    expert = jnp.concatenate([route[:, 0], route[:, 1]]).astype(jnp.int32)
    weight = jnp.concatenate([route[:, 2], route[:, 3]])
    token = jnp.concatenate([jnp.arange(t, dtype=jnp.int32)] * 2)
    onehot = (expert[:, None] == jnp.arange(n_exp, dtype=jnp.int32)[None, :]).astype(jnp.int32)
    rank = jnp.sum((jnp.cumsum(onehot, axis=0) - onehot) * onehot, axis=1)
    counts = jnp.sum(onehot, axis=0)
    padded = ((counts + tm - 1) // tm) * tm
    ends = jnp.cumsum(padded)
    starts = ends - padded
    dest = starts[expert] + rank
    rows = n_assign + n_exp * tm
    n_tiles = rows // tm
    row_token = jnp.zeros((rows,), jnp.int32).at[dest].set(token)
    row_w = jnp.zeros((rows, 1), F32).at[dest, 0].set(weight)
    tile_start = jnp.arange(n_tiles, dtype=jnp.int32) * tm
    tile_expert = jnp.minimum(jnp.sum((tile_start[:, None] >= ends[None, :]).astype(jnp.int32), axis=1), n_exp - 1)
    n_valid = (ends[-1] // tm).astype(jnp.int32).reshape(1)

    xs_sorted = _gather_rows(h, row_token)
    act = _moe_up(xs_sorted, w1, w3, tile_expert, n_valid, tm)
    y_sorted = _moe_down(act, w2, row_w, tile_expert, n_valid, tm)
    y2 = _gather_rows(y_sorted, dest)
    return _moe_combine(y2, xs, g_post, mods, l, 5, nxt, dims)


def _permute_w_in(w_in_l, d):
    q_rank, kv_rank = (3 * d) // 16, d // 8
    bw = d // N_BRANCH
    o = 0
    pieces = {}
    for name, width in (("zq", q_rank), ("zkv", kv_rank), ("zr", MLA_ROPE), ("hq", bw), ("hi", bw), ("hff", bw),
                        ("hfb", bw), ("hg", bw), ("conv", 2 * bw), ("pool", bw), ("gate", N_BRANCH * d)):
        pieces[name] = w_in_l[:, o:o + width]
        o += width
    pad = jnp.zeros((d, Z_GATE - Z_MLA - q_rank - kv_rank - MLA_ROPE), w_in_l.dtype)
    cols = [pieces["conv"], pieces["hq"], pieces["hi"], pieces["hff"], pieces["hfb"], pieces["hg"], pieces["pool"],
            pieces["zq"], pieces["zkv"], pieces["zr"], pad, pieces["gate"]]
    return jnp.concatenate(cols, axis=1).astype(BF)


def _pad_heads(w, n_heads, width):
    r = w.shape[0]
    w3 = w.reshape(r, n_heads, width)
    return jnp.pad(w3, ((0, 0), (0, 0), (0, HEAD_PAD - width))).reshape(r, n_heads * HEAD_PAD)


def _rope_tables(l_lat, lc):
    tt = jnp.arange(l_lat)
    row = (tt // GRID_W).astype(F32)
    col = (tt % GRID_W).astype(F32)
    axis_dim = MLA_ROPE // 2
    inv_freq = 1.0 / (ROPE_THETA ** (jnp.arange(0, axis_dim, 2, dtype=F32) / axis_dim))
    ar, ac = row[:, None] * inv_freq, col[:, None] * inv_freq
    ang = jnp.concatenate([ar, ar, ac, ac], axis=-1)
    cos = jnp.concatenate([jnp.ones((lc, MLA_ROPE), F32), jnp.cos(ang)], axis=0)
    sin = jnp.concatenate([jnp.zeros((lc, MLA_ROPE), F32), jnp.sin(ang)], axis=0)
    padc = jnp.zeros((lc + l_lat, 128 - MLA_ROPE), F32)
    pm = np.zeros((128, 128), np.float32)
    q4 = MLA_ROPE // 4
    for base in (0, 2 * q4):
        for i in range(q4):
            pm[base + q4 + i, base + i] = -1.0
            pm[base + i, base + q4 + i] = 1.0
    return jnp.concatenate([cos, padc], axis=1), jnp.concatenate([sin, padc], axis=1), jnp.asarray(pm, BF)


def kernel(x, c, ctx, c_ctx, w_mod, b_mod, g_mix_pre, g_mix_post, g_ffn_pre, g_ffn_post, w_in, mla_g_q, mla_g_kv, mla_w_uq, mla_w_ukv, hg_lb_logits, hg_g_norm, conv_w, conv_b, conv_ln_g, conv_ln_b, pool_w, pool_scale, w_branch, w_out, ffn_w1, ffn_w3, ffn_w2, moe_router, moe_w1, moe_w3, moe_w2):
    n_batch, l_lat, d = x.shape
    lc = ctx.shape[1]
    depth = w_in.shape[0]
    p_rows = lc + l_lat
    t = n_batch * p_rows
    dims = (p_rows, lc, n_batch)
    assert n_batch + 1 <= 8 and lc % HG_CHUNK == 0 and l_lat % HG_CHUNK == 0 and l_lat % GRID_W == 0
    n_heads = mla_w_uq.shape[2] // MLA_QK

    xs = jnp.concatenate([ctx, x], axis=1).reshape(t, d)
    cvec = jnp.zeros((8, d), F32).at[:n_batch].set(c).at[n_batch].set(c_ctx)
    mods = _modvecs(cvec, w_mod, b_mod)

    lb = jnp.cumsum(jax.nn.softmax(hg_lb_logits.astype(F32), axis=1), axis=1)
    lb = lb - lb[:, :1]
    cos, sin, pm = _rope_tables(l_lat, lc)

    h = _modulate(xs, g_mix_pre[0], mods, 0, 0, 1, dims)
    for l in range(depth):
        last = l == depth - 1
        z = _matmul(h, _permute_w_in(w_in[l], d), BF)
        q3, k3, v3 = _mla_proj(z, mla_g_q[l], mla_g_kv[l], _pad_heads(mla_w_uq[l], n_heads, MLA_QK),
                               mla_w_ukv[l], cos, sin, pm, dims)
        y_mla = _attention(q3, k3, v3, dims)
        y_hg = _hgrn_readout(_hgrn(z, lb[:, l], dims), z, hg_g_norm[l])
        y_conv = _conv_mixer(z, conv_w[l], conv_b[l], conv_ln_g[l], conv_ln_b[l], dims)
        y_pool = _pool_mixer(z, pool_w[l], pool_scale[l], dims)
        m = _merge([y_mla, y_hg, y_conv, y_pool], z, w_branch[l])
        j = l // 2
        nxt_ffn = (g_ffn_pre[l], l, 3, 4)
        nxt_mix = None if last else (g_mix_pre[l + 1], l + 1, 0, 1)
        if l % 2 == 0:
            xs, h2 = _out_proj(m, w_out[l], xs, g_mix_post[l], mods, l, 2, nxt_ffn, dims)
            act = _swiglu_up(h2, ffn_w1[j], ffn_w3[j])
            res = _ffn_down(act, ffn_w2[j], xs, g_ffn_post[l], mods, l, 5, nxt_mix, dims)
        else:
            n_exp = moe_router.shape[2]
            wr = jnp.pad(moe_router[j], ((0, 0), (0, 128 - n_exp)))
            wr_hi = wr.astype(BF)
            wr2 = jnp.stack([wr_hi, (wr - wr_hi.astype(F32)).astype(BF)])
            xs, h2, route = _out_proj(m, w_out[l], xs, g_mix_post[l], mods, l, 2, nxt_ffn, dims, router=(wr2, n_exp))
            res = _moe_ffn(h2, route, moe_w1[j], moe_w3[j], moe_w2[j], xs, g_ffn_post[l], mods, l, nxt_mix, dims)
        if last:
            (xs,) = res
        else:
            xs, h = res
    return xs.reshape(n_batch, p_rows, d)[:, lc:, :]
```

```python
import functools

import numpy as np
import jax
import jax.numpy as jnp
from jax import lax
from jax.experimental import pallas as pl
from jax.experimental.pallas import tpu as pltpu

F32 = jnp.float32
BF = jnp.bfloat16

EPS = 1e-6
GRID_W = 64
ROPE_THETA = 10000.0
N_BRANCH = 4
MLA_NOPE, MLA_ROPE, MLA_V = 128, 64, 128
MLA_QK = MLA_NOPE + MLA_ROPE
HEAD_PAD = 256
HG_DK = 128
CONV_WIDTH = 31
POOL_WINDOWS = (2, 4, 8, 16)
POOL_GROUP = 128
HALO = 16
HG_CHUNK = 64
HG_EXP_CLAMP = 80.0
NEG_BIG = -1e30
ATTN_Q_TILE = 512
MOE_ROW_TILE = 256
MOE_ROW_ALIGN = 16
MOE_GATHER_ROWS = 1024

VMEM_LIMIT_V7X = 56 * 1024 * 1024

Z_CONV, Z_HQ, Z_HI, Z_HFF, Z_HFB, Z_HG, Z_POOL, Z_MLA, Z_GATE = 0, 1024, 1536, 2048, 2560, 3072, 3584, 4096, 5120
MLA_ZKV_OFF, MLA_ZR_OFF = 384, 640


def _cparams(sem, vmem=VMEM_LIMIT_V7X):
    return pltpu.CompilerParams(dimension_semantics=sem, vmem_limit_bytes=vmem)


def _divisor_tile(n, target, mult=16):
    best = None
    for t in range(mult, min(n, target) + 1, mult):
        if n % t == 0:
            best = t
    assert best is not None, (n, target, mult)
    return best


def _pick_row(m, r):
    rows = lax.broadcasted_iota(jnp.int32, m.shape, 0)
    return jnp.sum(jnp.where(rows == r, m, 0.0), axis=0, keepdims=True)


def _row_vec(m_ref, b, n_batch, ctx_mask):
    m = m_ref[0, 0]
    return jnp.where(ctx_mask, _pick_row(m, n_batch), _pick_row(m, b))


def _tile_rows(i, tm, p_rows, lc):
    per_batch = p_rows // tm
    b = i // per_batch
    pos = (i % per_batch) * tm + lax.broadcasted_iota(jnp.int32, (tm, 1), 0)
    return b, pos < lc


def _rms(x, g):
    return x * lax.rsqrt(jnp.mean(x * x, axis=-1, keepdims=True) + EPS) * g


def _sigmoid(x):
    return 1.0 / (1.0 + jnp.exp(-x))


def _silu(x):
    return x * _sigmoid(x)


def _modvec_kernel(c_ref, w_ref, b_ref, o_ref):
    s = _silu(c_ref[...])
    s_hi = s.astype(BF)
    s_lo = (s - s_hi.astype(F32)).astype(BF)
    w = w_ref[0].astype(BF)
    acc = jnp.dot(s_hi, w, preferred_element_type=F32) + jnp.dot(s_lo, w, preferred_element_type=F32)
    o_ref[0, 0] = acc + b_ref[0, 0]


def _modvecs(cvec, w_mod, b_mod):
    depth, d, n6 = w_mod.shape
    nj = n6 // d
    tn = _divisor_tile(d, 1024, 128)
    npj = d // tn
    return pl.pallas_call(
        _modvec_kernel,
        grid=(depth, nj, npj),
        in_specs=[
            pl.BlockSpec((8, d), lambda l, j, n: (0, 0)),
            pl.BlockSpec((1, d, tn), lambda l, j, n: (l, 0, j * npj + n)),
            pl.BlockSpec((1, 1, 1, tn), lambda l, j, n: (l, j, 0, n)),
        ],
        out_specs=pl.BlockSpec((1, 1, 8, tn), lambda l, j, n: (l, j, 0, n)),
        out_shape=jax.ShapeDtypeStruct((depth, nj, 8, d), F32),
        compiler_params=_cparams(("parallel", "parallel", "parallel")),
        name="modvec",
    )(cvec, w_mod, b_mod.reshape(depth, nj, 1, d))


def _modulate_kernel(x_ref, g_ref, sh_ref, sc_ref, h_ref, *, tm, p_rows, lc, n_batch):
    b, cm = _tile_rows(pl.program_id(0), tm, p_rows, lc)
    shift = _row_vec(sh_ref, b, n_batch, cm)
    scale = _row_vec(sc_ref, b, n_batch, cm)
    h_ref[...] = (_rms(x_ref[...], g_ref[...]) * (1.0 + scale) + shift).astype(h_ref.dtype)


def _mod_spec(l, j, d):
    return pl.BlockSpec((1, 1, 8, d), lambda *_: (l, j, 0, 0))


def _modulate(xs, g, mods, l, j_shift, j_scale, dims):
    t, d = xs.shape
    p_rows, lc, n_batch = dims
    tm = _divisor_tile(p_rows, 544)
    return pl.pallas_call(
        functools.partial(_modulate_kernel, tm=tm, p_rows=p_rows, lc=lc, n_batch=n_batch),
        grid=(t // tm,),
        in_specs=[
            pl.BlockSpec((tm, d), lambda i: (i, 0)),
            pl.BlockSpec((1, d), lambda i: (0, 0)),
            _mod_spec(l, j_shift, d),
            _mod_spec(l, j_scale, d),
        ],
        out_specs=pl.BlockSpec((tm, d), lambda i: (i, 0)),
        out_shape=jax.ShapeDtypeStruct((t, d), BF),
        compiler_params=_cparams(("parallel",)),
        name="modulate",
    )(xs, g.reshape(1, d), mods, mods)


def _mm_kernel(x_ref, w_ref, o_ref):
    o_ref[...] = jnp.dot(x_ref[...], w_ref[0], preferred_element_type=F32).astype(o_ref.dtype)


def _matmul(x, w, l, out_dtype, tm_target=1088, tn_target=1024):
    t, k = x.shape
    n = w.shape[2]
    tm = _divisor_tile(t, tm_target)
    tn = _divisor_tile(n, tn_target, 128)
    return pl.pallas_call(
        _mm_kernel,
        grid=(n // tn, t // tm),
        in_specs=[
            pl.BlockSpec((tm, k), lambda j, i: (i, 0)),
            pl.BlockSpec((1, k, tn), lambda j, i: (l, 0, j)),
        ],
        out_specs=pl.BlockSpec((tm, tn), lambda j, i: (i, j)),
        out_shape=jax.ShapeDtypeStruct((t, n), out_dtype),
        compiler_params=_cparams(("parallel", "arbitrary")),
        name="in_proj",
    )(x, w)


def _rope128(t, cos, sin, pm):
    partner = jnp.dot(t.astype(BF), pm, preferred_element_type=F32)
    return t * cos + partner * sin


def _mla_proj_kernel(z_ref, gq_ref, gkv_ref, wq_ref, wkv_ref, cos_ref, sin_ref, pm_ref,
                     q_ref, k_ref, v_ref, wq_s, wkv_s, *, n_heads, q_rank, kv_rank, scale):
    @pl.when(pl.program_id(0) == 0)
    def _():
        wq_s[...] = wq_ref[...].astype(BF)
        wkv_s[...] = wkv_ref[...].astype(BF)

    z = z_ref[...]
    zq = z[:, :q_rank].astype(F32)
    zkv = z[:, MLA_ZKV_OFF:MLA_ZKV_OFF + kv_rank].astype(F32)
    zr = z[:, MLA_ZR_OFF:MLA_ZR_OFF + 128].astype(F32)
    cos, sin, pm = cos_ref[...], sin_ref[...], pm_ref[...]
    qn = _rms(zq, gq_ref[...]).astype(BF)
    kvn = _rms(zkv, gkv_ref[...]).astype(BF)
    k_rope = _rope128(zr, cos, sin, pm)[:, :MLA_ROPE]
    for h in range(n_heads):
        qh = jnp.dot(qn, wq_s[:, h * HEAD_PAD:(h + 1) * HEAD_PAD], preferred_element_type=F32)
        q_rope = _rope128(qh[:, MLA_NOPE:], cos, sin, pm)[:, :MLA_ROPE]
        q_ref[h] = (jnp.concatenate([qh[:, :MLA_NOPE], q_rope], axis=-1) * scale).astype(q_ref.dtype)
        kvh = jnp.dot(kvn, wkv_s[:, h * HEAD_PAD:(h + 1) * HEAD_PAD], preferred_element_type=F32)
        k_ref[h] = jnp.concatenate([kvh[:, :MLA_NOPE], k_rope], axis=-1).astype(k_ref.dtype)
        v_ref[h] = jnp.concatenate([kvh[:, MLA_NOPE:], jnp.ones_like(kvh[:, MLA_NOPE:])], axis=-1).astype(v_ref.dtype)


def _mla_proj(z, g_q, g_kv, wq_pad, wkv, cos, sin, pm, dims):
    t = z.shape[0]
    p_rows = dims[0]
    q_rank, kv_rank = g_q.shape[0], g_kv.shape[0]
    n_heads = wkv.shape[1] // HEAD_PAD
    tm = _divisor_tile(p_rows, 544)
    npb = p_rows // tm
    scale = float(MLA_QK) ** -0.5 * float(np.log2(np.e))
    const = lambda i: (0, 0)
    return pl.pallas_call(
        functools.partial(_mla_proj_kernel, n_heads=n_heads, q_rank=q_rank, kv_rank=kv_rank, scale=scale),
        grid=(t // tm,),
        in_specs=[
            pl.BlockSpec((tm, 1024), lambda i: (i, Z_MLA // 1024)),
            pl.BlockSpec((1, q_rank), const),
            pl.BlockSpec((1, kv_rank), const),
            pl.BlockSpec(wq_pad.shape, const),
            pl.BlockSpec(wkv.shape, const),
            pl.BlockSpec((tm, 128), lambda i: (i % npb, 0)),
            pl.BlockSpec((tm, 128), lambda i: (i % npb, 0)),
            pl.BlockSpec((128, 128), const),
        ],
        out_specs=[
            pl.BlockSpec((n_heads, tm, MLA_QK), lambda i: (0, i, 0)),
            pl.BlockSpec((n_heads, tm, MLA_QK), lambda i: (0, i, 0)),
            pl.BlockSpec((n_heads, tm, 2 * MLA_V), lambda i: (0, i, 0)),
        ],
        out_shape=[
            jax.ShapeDtypeStruct((n_heads, t, MLA_QK), BF),
            jax.ShapeDtypeStruct((n_heads, t, MLA_QK), BF),
            jax.ShapeDtypeStruct((n_heads, t, 2 * MLA_V), BF),
        ],
        scratch_shapes=[pltpu.VMEM(wq_pad.shape, BF), pltpu.VMEM(wkv.shape, BF)],
        compiler_params=_cparams(("arbitrary",)),
        name="mla_proj",
    )(z, g_q.reshape(1, -1), g_kv.reshape(1, -1), wq_pad, wkv, cos, sin, pm)


def _attn_kernel(q_ref, k_ref, v_ref, o_ref, p_s, *, tq, lc):
    p_rows = k_ref.shape[1]
    n_tiles = (p_rows - lc) // tq
    nt = (((1,), (1,)), ((), ()))

    def weights(q0, n_q, n_keys):
        s = lax.dot_general(q_ref[0, pl.ds(q0, n_q), :], k_ref[0, :n_keys, :], nt, preferred_element_type=F32)
        return jnp.exp2((s - jnp.max(s, axis=-1, keepdims=True)).astype(BF))

    def emit(p, q0, n_q, n_keys):
        o = jnp.dot(p, v_ref[0, :n_keys, :], preferred_element_type=F32)
        o_ref[pl.ds(q0, n_q), :] = (o[:, :MLA_V] / o[:, MLA_V:MLA_V + 1]).astype(o_ref.dtype)

    def tile_start(i):
        return pl.multiple_of(lc + i * tq, lc)

    emit(weights(0, lc, lc), 0, lc, lc)
    p_s[...] = weights(tile_start(0), tq, p_rows)

    def body(i, carry):
        p_prev = p_s[...]
        p_next = weights(tile_start(i), tq, p_rows)
        emit(p_prev, tile_start(i - 1), tq, p_rows)
        p_s[...] = p_next
        return carry

    lax.fori_loop(1, n_tiles, body, 0)
    emit(p_s[...], tile_start(n_tiles - 1), tq, p_rows)


def _attention(q3, k3, v3, dims):
    n_heads, t, _ = q3.shape
    p_rows, lc, n_batch = dims
    tq = _divisor_tile(p_rows - lc, ATTN_Q_TILE, lc)
    return pl.pallas_call(
        functools.partial(_attn_kernel, tq=tq, lc=lc),
        grid=(n_batch, n_heads),
        in_specs=[
            pl.BlockSpec((1, p_rows, MLA_QK), lambda b, h: (h, b, 0)),
            pl.BlockSpec((1, p_rows, MLA_QK), lambda b, h: (h, b, 0)),
            pl.BlockSpec((1, p_rows, 2 * MLA_V), lambda b, h: (h, b, 0)),
        ],
        out_specs=pl.BlockSpec((p_rows, MLA_V), lambda b, h: (b, h)),
        out_shape=jax.ShapeDtypeStruct((t, n_heads * MLA_V), BF),
        scratch_shapes=[pltpu.VMEM((tq, p_rows), BF)],
        compiler_params=_cparams(("parallel", "parallel")),
        name="mla_attention",
    )(q3, k3, v3)


def _hgrn_chunk(zq, zi, zf, lb, tri, o_ref, st_ref, stream, bi, n_heads):
    feeds = tri > 0.5
    f = lb + (1.0 - lb) * _sigmoid(zf.astype(F32))
    lf = jnp.log(f)
    lf1 = lf.astype(BF)
    r1 = lf - lf1.astype(F32)
    lf2 = r1.astype(BF)
    lf3 = (r1 - lf2.astype(F32)).astype(BF)
    b_all = (jnp.dot(tri, lf1, preferred_element_type=F32) + jnp.dot(tri, lf2, preferred_element_type=F32)
             + jnp.dot(tri, lf3, preferred_element_type=F32))
    tot_all = jnp.sum(lf, axis=0, keepdims=True)
    q_all = _silu(zq.astype(F32))
    nt = (((1,), (1,)), ((), ()))
    for h in range(n_heads):
        sl = slice(h * HG_DK, (h + 1) * HG_DK)
        b, tot, q, k = b_all[:, sl], tot_all[:, sl], q_all[:, sl], 1.0 - f[:, sl]
        v = zi[:, sl]
        ref = 0.5 * tot
        q_loc = (q * jnp.exp(jnp.minimum(b - ref, HG_EXP_CLAMP))).astype(BF)
        k_loc = (k * jnp.exp(jnp.minimum(ref - b, HG_EXP_CLAMP))).astype(BF)
        a = lax.dot_general(q_loc, k_loc, nt, preferred_element_type=F32)
        a = jnp.where(feeds, a, 0.0).astype(BF)
        state_t = st_ref[stream, h]
        q_in = (q * jnp.exp(b)).astype(BF)
        o = jnp.dot(a, v, preferred_element_type=F32)
        o = o + lax.dot_general(q_in, state_t.astype(BF), nt, preferred_element_type=F32)
        o_ref[bi, :, sl] = o
        k_out = (k * jnp.exp(tot - b)).astype(BF)
        upd = lax.dot_general(v, k_out, (((0,), (0,)), ((), ())), preferred_element_type=F32)
        st_ref[stream, h] = state_t * jnp.exp(tot) + upd


def _hgrn_kernel(fq_ref, fi_ref, ff_ref, bq_ref, bi_ref, bf_ref, lb_ref, tri_ref, of_ref, ob_ref, st_ref,
                 *, n_heads, n_batch):
    @pl.when(pl.program_id(0) == 0)
    def _():
        st_ref[...] = jnp.zeros_like(st_ref)

    for rev, (q_ref, i_ref, f_ref, o_ref) in enumerate(((fq_ref, fi_ref, ff_ref, of_ref),
                                                       (bq_ref, bi_ref, bf_ref, ob_ref))):
        for bi in range(n_batch):
            _hgrn_chunk(q_ref[bi], i_ref[bi], f_ref[bi], lb_ref[rev], tri_ref[rev], o_ref, st_ref,
                        rev * n_batch + bi, bi, n_heads)


def _hgrn(z, lb2, dims):
    t, nz = z.shape
    p_rows, lc, n_batch = dims
    c = HG_CHUNK
    w = lb2.shape[-1]
    n_heads = w // HG_DK
    nch, nctx = p_rows // c, lc // c
    idx = np.arange(c)
    tri = jnp.asarray(np.stack([idx[None, :] <= idx[:, None], idx[None, :] >= idx[:, None]]), BF)
    z3 = z.reshape(n_batch, p_rows, nz)

    def back(j):
        return jnp.where(j < nctx, nctx - 1 - j, nch - 1 - (j - nctx))

    def spec(col, rev):
        if rev:
            return pl.BlockSpec((n_batch, c, w), lambda j: (0, back(j), col // w))
        return pl.BlockSpec((n_batch, c, w), lambda j: (0, j, col // w))

    out_f, out_b = pl.pallas_call(
        functools.partial(_hgrn_kernel, n_heads=n_heads, n_batch=n_batch),
        grid=(nch,),
        in_specs=[
            spec(Z_HQ, False), spec(Z_HI, False), spec(Z_HFF, False),
            spec(Z_HQ, True), spec(Z_HI, True), spec(Z_HFB, True),
            pl.BlockSpec((2, 1, w), lambda j: (0, 0, 0)),
            pl.BlockSpec((2, c, c), lambda j: (0, 0, 0)),
        ],
        out_specs=[pl.BlockSpec((n_batch, c, w), lambda j: (0, j, 0)),
                   pl.BlockSpec((n_batch, c, w), lambda j: (0, back(j), 0))],
        out_shape=[jax.ShapeDtypeStruct((n_batch, p_rows, w), F32)] * 2,
        scratch_shapes=[pltpu.VMEM((2 * n_batch, n_heads, HG_DK, HG_DK), F32)],
        compiler_params=_cparams(("arbitrary",)),
        name="hgrn_scan",
    )(z3, z3, z3, z3, z3, z3, lb2.reshape(2, 1, w), tri)
    return out_f.reshape(t, w), out_b.reshape(t, w)


def _hgrn_readout_kernel(of_ref, ob_ref, zg_ref, g_ref, y_ref, *, n_heads):
    o = of_ref[...] + ob_ref[...]
    g = g_ref[...]
    gate = _silu(zg_ref[...].astype(F32))
    for h in range(n_heads):
        sl = slice(h * HG_DK, (h + 1) * HG_DK)
        y_ref[:, sl] = (_rms(o[:, sl], g[:, sl]) * gate[:, sl]).astype(y_ref.dtype)


def _hgrn_readout(o_fwd, o_bwd, z, g_norm):
    t, w = o_fwd.shape
    tm = _divisor_tile(t, 1088)
    return pl.pallas_call(
        functools.partial(_hgrn_readout_kernel, n_heads=w // HG_DK),
        grid=(t // tm,),
        in_specs=[
            pl.BlockSpec((tm, w), lambda i: (i, 0)),
            pl.BlockSpec((tm, w), lambda i: (i, 0)),
            pl.BlockSpec((tm, w), lambda i: (i, Z_HG // w)),
            pl.BlockSpec((1, w), lambda i: (0, 0)),
        ],
        out_specs=pl.BlockSpec((tm, w), lambda i: (i, 0)),
        out_shape=jax.ShapeDtypeStruct((t, w), BF),
        compiler_params=_cparams(("parallel",)),
        name="hgrn_readout",
    )(o_fwd, o_bwd, z, g_norm.reshape(1, w))


def _segment_tile(i, tc, p_rows, lc):
    per_batch = p_rows // tc
    j = i % per_batch
    nctx = lc // tc
    first = jnp.logical_or(j == 0, j == nctx)
    last = jnp.logical_or(j == nctx - 1, j == per_batch - 1)
    in_ctx = j < nctx
    seg_pos0 = jnp.where(in_ctx, j, j - nctx) * tc
    seg_len = jnp.where(in_ctx, lc, p_rows - lc)
    return first, last, seg_pos0, seg_len


def _halo_specs(tc, width, col_block, n_rows):
    per = tc // HALO
    last_blk = n_rows // HALO - 1
    return [
        pl.BlockSpec((HALO, width), lambda i: (jnp.maximum(i * per - 1, 0), col_block)),
        pl.BlockSpec((tc, width), lambda i: (i, col_block)),
        pl.BlockSpec((HALO, width), lambda i: (jnp.minimum((i + 1) * per, last_blk), col_block)),
    ]


def _conv_kernel(prev_ref, cur_ref, next_ref, w_ref, b_ref, lg_ref, lb_ref, y_ref, buf, *, tc, p_rows, lc, ch):
    first, last, _, _ = _segment_tile(pl.program_id(0), tc, p_rows, lc)

    def glu(zz):
        zz = zz.astype(F32)
        return zz[:, :ch] * _sigmoid(zz[:, ch:])

    buf[0:HALO, :] = jnp.where(first, 0.0, glu(prev_ref[...]))
    buf[HALO:HALO + tc, :] = glu(cur_ref[...])
    buf[HALO + tc:, :] = jnp.where(last, 0.0, glu(next_ref[...]))
    w = w_ref[...]
    acc = jnp.zeros((tc, ch), F32) + b_ref[...]
    for kk in range(CONV_WIDTH):
        off = HALO - CONV_WIDTH // 2 + kk
        acc = acc + buf[off:off + tc, :] * w[kk:kk + 1, :]
    mu = jnp.mean(acc, axis=-1, keepdims=True)
    cen = acc - mu
    var = jnp.mean(cen * cen, axis=-1, keepdims=True)
    y = cen * lax.rsqrt(var + EPS) * lg_ref[...] + lb_ref[...]
    y_ref[...] = _silu(y).astype(y_ref.dtype)


def _conv_mixer(z, w_dw, b_dw, ln_g, ln_b, dims):
    t = z.shape[0]
    p_rows, lc, _ = dims
    ch = w_dw.shape[1]
    tc = _divisor_tile(lc, 256)
    row = lambda a: a.reshape(1, ch)
    const = lambda i: (0, 0)
    return pl.pallas_call(
        functools.partial(_conv_kernel, tc=tc, p_rows=p_rows, lc=lc, ch=ch),
        grid=(t // tc,),
        in_specs=_halo_specs(tc, 2 * ch, Z_CONV // (2 * ch), t) + [
            pl.BlockSpec((CONV_WIDTH, ch), const),
            pl.BlockSpec((1, ch), const), pl.BlockSpec((1, ch), const), pl.BlockSpec((1, ch), const),
        ],
        out_specs=pl.BlockSpec((tc, ch), lambda i: (i, 0)),
        out_shape=jax.ShapeDtypeStruct((t, ch), BF),
        scratch_shapes=[pltpu.VMEM((tc + 2 * HALO, ch), F32)],
        compiler_params=_cparams(("parallel",)),
        name="conv_mixer",
    )(z, z, z, w_dw, row(b_dw), row(ln_g), row(ln_b))


def _pool_kernel(prev_ref, cur_ref, next_ref, w_ref, s_ref, y_ref, buf, *, tc, p_rows, lc):
    first, last, seg_pos0, seg_len = _segment_tile(pl.program_id(0), tc, p_rows, lc)
    buf[0:HALO, :] = jnp.where(first, 0.0, prev_ref[...].astype(F32))
    buf[HALO:HALO + tc, :] = cur_ref[...].astype(F32)
    buf[HALO + tc:, :] = jnp.where(last, 0.0, next_ref[...].astype(F32))
    pos = seg_pos0 + lax.broadcasted_iota(jnp.int32, (tc, 1), 0)
    scale = s_ref[...]
    for gi, win in enumerate(POOL_WINDOWS):
        sl = slice(gi * POOL_GROUP, (gi + 1) * POOL_GROUP)
        half = win // 2
        tot = jnp.zeros((tc, POOL_GROUP), F32)
        for u in range(-half, half):
            tot = tot + buf[HALO + u:HALO + u + tc, sl]
        cnt = (jnp.minimum(pos + half, seg_len) - jnp.maximum(pos - half, 0)).astype(F32)
        pooled = tot / cnt - buf[HALO:HALO + tc, sl]
        y = jnp.dot(pooled.astype(BF), w_ref[gi].astype(BF), preferred_element_type=F32)
        y_ref[:, sl] = (y * scale[:, sl]).astype(y_ref.dtype)


def _pool_mixer(z, w_pool, scale, dims):
    t = z.shape[0]
    p_rows, lc, _ = dims
    ch = scale.shape[0]
    tc = _divisor_tile(lc, 256)
    return pl.pallas_call(
        functools.partial(_pool_kernel, tc=tc, p_rows=p_rows, lc=lc),
        grid=(t // tc,),
        in_specs=_halo_specs(tc, ch, Z_POOL // ch, t) + [
            pl.BlockSpec(w_pool.shape, lambda i: (0, 0, 0)),
            pl.BlockSpec((1, ch), lambda i: (0, 0)),
        ],
        out_specs=pl.BlockSpec((tc, ch), lambda i: (i, 0)),
        out_shape=jax.ShapeDtypeStruct((t, ch), BF),
        scratch_shapes=[pltpu.VMEM((tc + 2 * HALO, ch), F32)],
        compiler_params=_cparams(("parallel",)),
        name="pool_mixer",
    )(z, z, z, w_pool, scale.reshape(1, ch))


def _merge_kernel(y0, y1, y2, y3, g0, g1, g2, g3, w_ref, o_ref, w_s):
    @pl.when(pl.program_id(1) == 0)
    def _():
        w_s[...] = w_ref[0].astype(BF)

    acc = None
    for n, (y_ref, g_ref) in enumerate(((y0, g0), (y1, g1), (y2, g2), (y3, g3))):
        proj = jnp.dot(y_ref[...], w_s[n], preferred_element_type=F32)
        term = _sigmoid(g_ref[...].astype(F32)) * proj
        acc = term if acc is None else acc + term
    o_ref[...] = acc.astype(o_ref.dtype)


def _merge(ys, z, w_branch, l):
    t, bw = ys[0].shape
    _, nb, _, d = w_branch.shape
    tm = _divisor_tile(t, 544)
    tn = _divisor_tile(d, 1024, 128)
    npn = d // tn

    def gate_spec(n):
        return pl.BlockSpec((tm, tn), lambda j, i: (i, Z_GATE // tn + n * npn + j))

    return pl.pallas_call(
        _merge_kernel,
        grid=(npn, t // tm),
        in_specs=[pl.BlockSpec((tm, bw), lambda j, i: (i, 0))] * nb + [gate_spec(n) for n in range(nb)] + [
            pl.BlockSpec((1, nb, bw, tn), lambda j, i: (l, 0, 0, j))],
        out_specs=pl.BlockSpec((tm, tn), lambda j, i: (i, j)),
        out_shape=jax.ShapeDtypeStruct((t, d), BF),
        scratch_shapes=[pltpu.VMEM((nb, bw, tn), BF)],
        compiler_params=_cparams(("parallel", "arbitrary")),
        name="branch_merge",
    )(*ys, z, z, z, z, w_branch)


def _residual_epilogue(y, x_ref, gpost_ref, gate_ref, gnext_ref, sh_ref, sc_ref, xo_ref, ho_ref, b, n_batch, cm):
    gate = _row_vec(gate_ref, b, n_batch, cm)
    x_new = x_ref[...] + gate * _rms(y, gpost_ref[...])
    xo_ref[...] = x_new
    if ho_ref is None:
        return None
    shift = _row_vec(sh_ref, b, n_batch, cm)
    scale = _row_vec(sc_ref, b, n_batch, cm)
    h = _rms(x_new, gnext_ref[...]) * (1.0 + scale) + shift
    ho_ref[...] = h.astype(ho_ref.dtype)
    return h


def _route_top2(logits, n_experts):
    lane = lax.broadcasted_iota(jnp.int32, logits.shape, 1)
    lg = jnp.where(lane < n_experts, logits, NEG_BIG)
    m1 = jnp.max(lg, axis=-1, keepdims=True)
    i1 = jnp.min(jnp.where(lg == m1, lane, 1 << 20), axis=-1, keepdims=True)
    lg2 = jnp.where(lane == i1, NEG_BIG, lg)
    m2 = jnp.max(lg2, axis=-1, keepdims=True)
    i2 = jnp.min(jnp.where(lg2 == m2, lane, 1 << 20), axis=-1, keepdims=True)
    e2 = jnp.exp(m2 - m1)
    w1 = 1.0 / (1.0 + e2)
    w2 = e2 / (1.0 + e2)
    out = jnp.where(lane == 0, i1.astype(F32), 0.0)
    out = jnp.where(lane == 1, i2.astype(F32), out)
    out = jnp.where(lane == 2, w1, out)
    return jnp.where(lane == 3, w2, out)


def _router_logits(h, wr_ref):
    h_hi = h.astype(BF)
    h_lo = (h - h_hi.astype(F32)).astype(BF)
    return (jnp.dot(h_hi, wr_ref[0], preferred_element_type=F32) + jnp.dot(h_hi, wr_ref[1], preferred_element_type=F32)
            + jnp.dot(h_lo, wr_ref[0], preferred_element_type=F32))


def _out_proj_kernel(*refs, tm, p_rows, lc, n_batch, emit_h, n_experts):
    m_ref, w_ref, x_ref, gpost_ref, gate_ref = refs[:5]
    pos = 5
    gnext_ref = sh_ref = sc_ref = wr_ref = None
    if emit_h:
        gnext_ref, sh_ref, sc_ref = refs[pos:pos + 3]
        pos += 3
    if n_experts:
        wr_ref = refs[pos]
        pos += 1
    xo_ref = refs[pos]
    pos += 1
    ho_ref = route_ref = None
    if emit_h:
        ho_ref = refs[pos]
        pos += 1
    if n_experts:
        route_ref = refs[pos]
        pos += 1
    b, cm = _tile_rows(pl.program_id(0), tm, p_rows, lc)
    y = jnp.dot(m_ref[...], w_ref[0], preferred_element_type=F32)
    h = _residual_epilogue(y, x_ref, gpost_ref, gate_ref, gnext_ref, sh_ref, sc_ref, xo_ref, ho_ref, b, n_batch, cm)
    if n_experts:
        route_ref[...] = _route_top2(_router_logits(h, wr_ref), n_experts)


def _out_proj(m, w_stack, lw, xs, g_post, mods, l, j_gate, nxt, dims, router=None, name="out_proj"):
    t, d = xs.shape
    k = m.shape[1]
    p_rows, lc, n_batch = dims
    tm = _divisor_tile(p_rows, 544 if k <= d else 272)
    const = lambda i: (0, 0)
    rowtile = pl.BlockSpec((tm, d), lambda i: (i, 0))
    args = [m, w_stack, xs, g_post.reshape(1, d), mods]
    in_specs = [pl.BlockSpec((tm, k), lambda i: (i, 0)),
                pl.BlockSpec((1, k, d), lambda i: (lw, 0, 0), pipeline_mode=pl.Buffered(1)),
                rowtile, pl.BlockSpec((1, d), const), _mod_spec(l, j_gate, d)]
    out_shape = [jax.ShapeDtypeStruct((t, d), F32)]
    out_specs = [rowtile]
    n_experts = 0
    if nxt is not None:
        g_next, l_next, j_shift, j_scale = nxt
        args += [g_next.reshape(1, d), mods, mods]
        in_specs += [pl.BlockSpec((1, d), const), _mod_spec(l_next, j_shift, d), _mod_spec(l_next, j_scale, d)]
        out_shape.append(jax.ShapeDtypeStruct((t, d), BF))
        out_specs.append(rowtile)
    if router is not None:
        wr, n_experts = router
        args.append(wr)
        in_specs.append(pl.BlockSpec(wr.shape, lambda i: (0, 0, 0)))
        out_shape.append(jax.ShapeDtypeStruct((t, 128), F32))
        out_specs.append(pl.BlockSpec((tm, 128), lambda i: (i, 0)))
    return pl.pallas_call(
        functools.partial(_out_proj_kernel, tm=tm, p_rows=p_rows, lc=lc, n_batch=n_batch,
                          emit_h=nxt is not None, n_experts=n_experts),
        grid=(t // tm,),
        in_specs=in_specs,
        out_specs=out_specs,
        out_shape=out_shape,
        compiler_params=_cparams(("parallel",)),
        name=name,
    )(*args)


def _swiglu_kernel(x_ref, w1_ref, w3_ref, o_ref, w1_s, w3_s):
    @pl.when(pl.program_id(1) == 0)
    def _():
        w1_s[...] = w1_ref[0].astype(BF)
        w3_s[...] = w3_ref[0].astype(BF)

    x = x_ref[...]
    a = jnp.dot(x, w1_s[...], preferred_element_type=F32)
    g = jnp.dot(x, w3_s[...], preferred_element_type=F32)
    o_ref[...] = (_silu(a) * g).astype(o_ref.dtype)


def _swiglu_up(h, w1, w3, jl):
    t, d = h.shape
    dff = w1.shape[2]
    tm = _divisor_tile(t, 1088)
    tn = _divisor_tile(dff, 512, 128)
    return pl.pallas_call(
        _swiglu_kernel,
        grid=(dff // tn, t // tm),
        in_specs=[
            pl.BlockSpec((tm, d), lambda j, i: (i, 0)),
            pl.BlockSpec((1, d, tn), lambda j, i: (jl, 0, j)),
            pl.BlockSpec((1, d, tn), lambda j, i: (jl, 0, j)),
        ],
        out_specs=pl.BlockSpec((tm, tn), lambda j, i: (i, j)),
        out_shape=jax.ShapeDtypeStruct((t, dff), BF),
        scratch_shapes=[pltpu.VMEM((d, tn), BF), pltpu.VMEM((d, tn), BF)],
        compiler_params=_cparams(("parallel", "arbitrary")),
        name="swiglu_up",
    )(h, w1, w3)


def _gather_rows_kernel(idx_ref, src_ref, o_ref, sem, *, tr):
    base = pl.program_id(0) * tr

    def copy(r):
        return pltpu.make_async_copy(src_ref.at[idx_ref[base + r]], o_ref.at[r], sem)

    def issue(r, carry):
        copy(r).start()
        return carry

    lax.fori_loop(0, tr, issue, 0, unroll=8)

    def drain(r, carry):
        copy(r).wait()
        return carry

    lax.fori_loop(0, tr, drain, 0, unroll=8)


def _gather_rows(src, idx):
    n, d = src.shape
    rows = idx.shape[0]
    tr = _divisor_tile(rows, MOE_GATHER_ROWS, 8)
    lanes = 128
    src3 = src.reshape(n, d // lanes, lanes)
    out = pl.pallas_call(
        functools.partial(_gather_rows_kernel, tr=tr),
        grid_spec=pltpu.PrefetchScalarGridSpec(
            num_scalar_prefetch=1,
            grid=(rows // tr,),
            in_specs=[pl.BlockSpec(memory_space=pl.ANY)],
            out_specs=pl.BlockSpec((tr, d // lanes, lanes), lambda i, idx_ref: (i, 0, 0)),
            scratch_shapes=[pltpu.SemaphoreType.DMA(())],
        ),
        out_shape=jax.ShapeDtypeStruct((rows, d // lanes, lanes), src.dtype),
        compiler_params=_cparams(("arbitrary",)),
        name="moe_gather",
    )(idx, src3)
    return out.reshape(rows, d)


def _expert_rows_loop(n_tiles, load_copy, store_copy, compute):
    @pl.when(n_tiles > 0)
    def _():
        load_copy(0, 0).start()

    def body(i, carry):
        slot = i % 2
        load_copy(i, slot).wait()

        @pl.when(i + 1 < n_tiles)
        def _():
            load_copy(i + 1, 1 - slot).start()

        @pl.when(i >= 2)
        def _():
            store_copy(i - 2, slot).wait()

        compute(slot)
        store_copy(i, slot).start()
        return carry

    lax.fori_loop(0, n_tiles, body, 0)

    @pl.when(n_tiles >= 2)
    def _():
        store_copy(n_tiles - 2, n_tiles % 2).wait()

    @pl.when(n_tiles >= 1)
    def _():
        store_copy(n_tiles - 1, (n_tiles - 1) % 2).wait()


def _zero_tail_rows(out_hbm, zbuf, sem, first_row, col0, tn):
    zbuf[...] = jnp.zeros_like(zbuf)
    n_chunks = (out_hbm.shape[0] - first_row) // MOE_ROW_ALIGN

    def copy(c):
        r0 = pl.multiple_of(first_row + c * MOE_ROW_ALIGN, MOE_ROW_ALIGN)
        return pltpu.make_async_copy(zbuf, out_hbm.at[pl.ds(r0, MOE_ROW_ALIGN), pl.ds(col0, tn)], sem)

    def issue(c, carry):
        copy(c).start()
        return carry

    def drain(c, carry):
        copy(c).wait()
        return carry

    lax.fori_loop(0, n_chunks, issue, 0)
    lax.fori_loop(0, n_chunks, drain, 0)


def _moe_up_kernel(st_ref, nt_ref, x_hbm, w1_ref, w3_ref, a_hbm, w1_s, w3_s, xbuf, obuf, zbuf, sem_in, sem_out, sem_z,
                   *, tr, tn):
    e, n = pl.program_id(0), pl.program_id(1)
    w1_s[...] = w1_ref[0, 0].astype(BF)
    w3_s[...] = w3_ref[0, 0].astype(BF)
    start = st_ref[e]
    col0 = pl.multiple_of(n * tn, tn)

    def rows(i):
        return pl.ds(pl.multiple_of(start + i * tr, MOE_ROW_ALIGN), tr)

    def load_copy(i, slot):
        return pltpu.make_async_copy(x_hbm.at[rows(i)], xbuf.at[slot], sem_in.at[slot])

    def store_copy(i, slot):
        return pltpu.make_async_copy(obuf.at[slot], a_hbm.at[rows(i), pl.ds(col0, tn)], sem_out.at[slot])

    def compute(slot):
        x = xbuf[slot]
        a = jnp.dot(x, w1_s[...], preferred_element_type=F32)
        g = jnp.dot(x, w3_s[...], preferred_element_type=F32)
        obuf[slot] = (_silu(a) * g).astype(obuf.dtype)

    _expert_rows_loop(nt_ref[e], load_copy, store_copy, compute)

    @pl.when(e == pl.num_programs(0) - 1)
    def _():
        _zero_tail_rows(a_hbm, zbuf, sem_z, start + nt_ref[e] * tr, col0, tn)


def _moe_up(xs, w1, w3, jl, starts, n_tiles, tr):
    rows, d = xs.shape
    n_exp, dff = w1.shape[1], w1.shape[3]
    tn = _divisor_tile(dff, 512, 128)
    return pl.pallas_call(
        functools.partial(_moe_up_kernel, tr=tr, tn=tn),
        grid_spec=pltpu.PrefetchScalarGridSpec(
            num_scalar_prefetch=2,
            grid=(n_exp, dff // tn),
            in_specs=[
                pl.BlockSpec(memory_space=pl.ANY),
                pl.BlockSpec((1, 1, d, tn), lambda e, n, st, nt: (jl, e, 0, n)),
                pl.BlockSpec((1, 1, d, tn), lambda e, n, st, nt: (jl, e, 0, n)),
            ],
            out_specs=pl.BlockSpec(memory_space=pl.ANY),
            scratch_shapes=[pltpu.VMEM((d, tn), BF), pltpu.VMEM((d, tn), BF),
                            pltpu.VMEM((2, tr, d), BF), pltpu.VMEM((2, tr, tn), BF),
                            pltpu.VMEM((MOE_ROW_ALIGN, tn), BF),
                            pltpu.SemaphoreType.DMA((2,)), pltpu.SemaphoreType.DMA((2,)), pltpu.SemaphoreType.DMA(())],
        ),
        out_shape=jax.ShapeDtypeStruct((rows, dff), BF),
        compiler_params=_cparams(("arbitrary", "arbitrary")),
        name="moe_up",
    )(starts, n_tiles, xs, w1, w3)


def _moe_down_kernel(st_ref, nt_ref, a_hbm, w_ref, y_hbm, w_s, abuf, obuf, zbuf, sem_in, sem_out, sem_z, *, tr, tn):
    e, n = pl.program_id(0), pl.program_id(1)
    w_s[...] = w_ref[0, 0].astype(BF)
    start = st_ref[e]
    col0 = pl.multiple_of(n * tn, tn)

    def rows(i):
        return pl.ds(pl.multiple_of(start + i * tr, MOE_ROW_ALIGN), tr)

    def load_copy(i, slot):
        return pltpu.make_async_copy(a_hbm.at[rows(i)], abuf.at[slot], sem_in.at[slot])

    def store_copy(i, slot):
        return pltpu.make_async_copy(obuf.at[slot], y_hbm.at[rows(i), pl.ds(col0, tn)], sem_out.at[slot])

    def compute(slot):
        obuf[slot] = jnp.dot(abuf[slot], w_s[...], preferred_element_type=F32)

    _expert_rows_loop(nt_ref[e], load_copy, store_copy, compute)

    @pl.when(e == pl.num_programs(0) - 1)
    def _():
        _zero_tail_rows(y_hbm, zbuf, sem_z, start + nt_ref[e] * tr, col0, tn)


def _moe_down(act, w2, jl, starts, n_tiles, tr):
    rows, dff = act.shape
    n_exp, d = w2.shape[1], w2.shape[3]
    tn = _divisor_tile(d, 512, 128)
    return pl.pallas_call(
        functools.partial(_moe_down_kernel, tr=tr, tn=tn),
        grid_spec=pltpu.PrefetchScalarGridSpec(
            num_scalar_prefetch=2,
            grid=(n_exp, d // tn),
            in_specs=[
                pl.BlockSpec(memory_space=pl.ANY),
                pl.BlockSpec((1, 1, dff, tn), lambda e, n, st, nt: (jl, e, 0, n)),
            ],
            out_specs=pl.BlockSpec(memory_space=pl.ANY),
            scratch_shapes=[pltpu.VMEM((dff, tn), BF), pltpu.VMEM((2, tr, dff), BF), pltpu.VMEM((2, tr, tn), F32),
                            pltpu.VMEM((MOE_ROW_ALIGN, tn), F32),
                            pltpu.SemaphoreType.DMA((2,)), pltpu.SemaphoreType.DMA((2,)), pltpu.SemaphoreType.DMA(())],
        ),
        out_shape=jax.ShapeDtypeStruct((rows, d), F32),
        compiler_params=_cparams(("arbitrary", "arbitrary")),
        name="moe_down",
    )(starts, n_tiles, act, w2)


def _combine_kernel(*refs, tm, p_rows, lc, n_batch, emit_h):
    ya_ref, yb_ref, rt_ref, x_ref, gpost_ref, gate_ref = refs[:6]
    gnext_ref = sh_ref = sc_ref = ho_ref = None
    if emit_h:
        gnext_ref, sh_ref, sc_ref, xo_ref, ho_ref = refs[6:]
    else:
        (xo_ref,) = refs[6:]
    b, cm = _tile_rows(pl.program_id(0), tm, p_rows, lc)
    rt = rt_ref[...]
    y = rt[:, 2:3] * ya_ref[...] + rt[:, 3:4] * yb_ref[...]
    _residual_epilogue(y, x_ref, gpost_ref, gate_ref, gnext_ref, sh_ref, sc_ref, xo_ref, ho_ref, b, n_batch, cm)


def _moe_combine(y2, route, xs, g_post, mods, l, j_gate, nxt, dims):
    t, d = xs.shape
    p_rows, lc, n_batch = dims
    tm = _divisor_tile(p_rows, 544)
    nt = t // tm
    const = lambda i: (0, 0)
    rowtile = pl.BlockSpec((tm, d), lambda i: (i, 0))
    args = [y2, y2, route, xs, g_post.reshape(1, d), mods]
    in_specs = [rowtile, pl.BlockSpec((tm, d), lambda i: (i + nt, 0)), pl.BlockSpec((tm, 128), lambda i: (i, 0)),
                rowtile, pl.BlockSpec((1, d), const), _mod_spec(l, j_gate, d)]
    out_shape = [jax.ShapeDtypeStruct((t, d), F32)]
    out_specs = [rowtile]
    if nxt is not None:
        g_next, l_next, j_shift, j_scale = nxt
        args += [g_next.reshape(1, d), mods, mods]
        in_specs += [pl.BlockSpec((1, d), const), _mod_spec(l_next, j_shift, d), _mod_spec(l_next, j_scale, d)]
        out_shape.append(jax.ShapeDtypeStruct((t, d), BF))
        out_specs.append(rowtile)
    return pl.pallas_call(
        functools.partial(_combine_kernel, tm=tm, p_rows=p_rows, lc=lc, n_batch=n_batch, emit_h=nxt is not None),
        grid=(nt,),
        in_specs=in_specs,
        out_specs=out_specs,
        out_shape=out_shape,
        compiler_params=_cparams(("parallel",)),
        name="moe_combine",
    )(*args)


def _moe_ffn(h, route, w1, w3, w2, jl, xs, g_post, mods, l, nxt, dims):
    t, d = h.shape
    n_exp = w1.shape[1]
    tr = MOE_ROW_TILE
    expert = jnp.concatenate([route[:, 0], route[:, 1]]).astype(jnp.int32)
    token = jnp.concatenate([jnp.arange(t, dtype=jnp.int32)] * 2)
    onehot = (expert[:, None] == jnp.arange(n_exp, dtype=jnp.int32)[None, :]).astype(jnp.int32)
    rank = jnp.sum((jnp.cumsum(onehot, axis=0) - onehot) * onehot, axis=1)
    counts = jnp.sum(onehot, axis=0)
    padded = ((counts + MOE_ROW_ALIGN - 1) // MOE_ROW_ALIGN) * MOE_ROW_ALIGN
    ends = jnp.cumsum(padded)
    starts = (ends - padded).astype(jnp.int32)
    n_tiles = ((counts + tr - 1) // tr).astype(jnp.int32)
    dest = starts[expert] + rank
    rows = 2 * t + n_exp * MOE_ROW_ALIGN + tr
    rows = ((rows + MOE_GATHER_ROWS - 1) // MOE_GATHER_ROWS) * MOE_GATHER_ROWS
    row_token = jnp.zeros((rows,), jnp.int32).at[dest].set(token)

    xs_sorted = _gather_rows(h, row_token)
    act = _moe_up(xs_sorted, w1, w3, jl, starts, n_tiles, tr)
    y_sorted = _moe_down(act, w2, jl, starts, n_tiles, tr)
    y2 = _gather_rows(y_sorted, dest)
    return _moe_combine(y2, route, xs, g_post, mods, l, 5, nxt, dims)


def _permute_w_in(w_in, d):
    q_rank, kv_rank = (3 * d) // 16, d // 8
    bw = d // N_BRANCH
    o = 0
    pieces = {}
    for name, width in (("zq", q_rank), ("zkv", kv_rank), ("zr", MLA_ROPE), ("hq", bw), ("hi", bw), ("hff", bw),
                        ("hfb", bw), ("hg", bw), ("conv", 2 * bw), ("pool", bw), ("gate", N_BRANCH * d)):
        pieces[name] = w_in[:, :, o:o + width].astype(BF)
        o += width
    pad = jnp.zeros((w_in.shape[0], d, Z_GATE - Z_MLA - q_rank - kv_rank - MLA_ROPE), BF)
    cols = [pieces["conv"], pieces["hq"], pieces["hi"], pieces["hff"], pieces["hfb"], pieces["hg"], pieces["pool"],
            pieces["zq"], pieces["zkv"], pieces["zr"], pad, pieces["gate"]]
    return jnp.concatenate(cols, axis=2)


def _pad_heads(w, n_heads, width):
    r = w.shape[0]
    w3 = w.reshape(r, n_heads, width)
    return jnp.pad(w3, ((0, 0), (0, 0), (0, HEAD_PAD - width))).reshape(r, n_heads * HEAD_PAD)


def _rope_tables(l_lat, lc):
    tt = jnp.arange(l_lat)
    row = (tt // GRID_W).astype(F32)
    col = (tt % GRID_W).astype(F32)
    axis_dim = MLA_ROPE // 2
    inv_freq = 1.0 / (ROPE_THETA ** (jnp.arange(0, axis_dim, 2, dtype=F32) / axis_dim))
    ar, ac = row[:, None] * inv_freq, col[:, None] * inv_freq
    ang = jnp.concatenate([ar, ar, ac, ac], axis=-1)
    cos = jnp.concatenate([jnp.ones((lc, MLA_ROPE), F32), jnp.cos(ang)], axis=0)
    sin = jnp.concatenate([jnp.zeros((lc, MLA_ROPE), F32), jnp.sin(ang)], axis=0)
    padc = jnp.zeros((lc + l_lat, 128 - MLA_ROPE), F32)
    pm = np.zeros((128, 128), np.float32)
    q4 = MLA_ROPE // 4
    for base in (0, 2 * q4):
        for i in range(q4):
            pm[base + q4 + i, base + i] = -1.0
            pm[base + i, base + q4 + i] = 1.0
    return jnp.concatenate([cos, padc], axis=1), jnp.concatenate([sin, padc], axis=1), jnp.asarray(pm, BF)


def kernel(x, c, ctx, c_ctx, w_mod, b_mod, g_mix_pre, g_mix_post, g_ffn_pre, g_ffn_post, w_in, mla_g_q, mla_g_kv, mla_w_uq, mla_w_ukv, hg_lb_logits, hg_g_norm, conv_w, conv_b, conv_ln_g, conv_ln_b, pool_w, pool_scale, w_branch, w_out, ffn_w1, ffn_w3, ffn_w2, moe_router, moe_w1, moe_w3, moe_w2):
    n_batch, l_lat, d = x.shape
    lc = ctx.shape[1]
    depth = w_in.shape[0]
    p_rows = lc + l_lat
    t = n_batch * p_rows
    dims = (p_rows, lc, n_batch)
    assert n_batch + 1 <= 8 and lc % HG_CHUNK == 0 and l_lat % HG_CHUNK == 0 and l_lat % GRID_W == 0
    n_heads = mla_w_uq.shape[2] // MLA_QK

    xs = jnp.concatenate([ctx, x], axis=1).reshape(t, d)
    cvec = jnp.zeros((8, d), F32).at[:n_batch].set(c).at[n_batch].set(c_ctx)
    mods = _modvecs(cvec, w_mod, b_mod)

    lb = jnp.cumsum(jax.nn.softmax(hg_lb_logits.astype(F32), axis=1), axis=1)
    lb = lb - lb[:, :1]
    cos, sin, pm = _rope_tables(l_lat, lc)

    w_in_bf = _permute_w_in(w_in, d)
    w_out_bf = w_out.astype(BF)
    ffn_w2_bf = ffn_w2.astype(BF)

    h = _modulate(xs, g_mix_pre[0], mods, 0, 0, 1, dims)
    for l in range(depth):
        last = l == depth - 1
        z = _matmul(h, w_in_bf, l, BF)
        q3, k3, v3 = _mla_proj(z, mla_g_q[l], mla_g_kv[l], _pad_heads(mla_w_uq[l], n_heads, MLA_QK),
                               mla_w_ukv[l], cos, sin, pm, dims)
        y_mla = _attention(q3, k3, v3, dims)
        y_hg = _hgrn_readout(*_hgrn(z, lb[:, l], dims), z, hg_g_norm[l])
        y_conv = _conv_mixer(z, conv_w[l], conv_b[l], conv_ln_g[l], conv_ln_b[l], dims)
        y_pool = _pool_mixer(z, pool_w[l], pool_scale[l], dims)
        m = _merge([y_mla, y_hg, y_conv, y_pool], z, w_branch, l)
        j = l // 2
        nxt_ffn = (g_ffn_pre[l], l, 3, 4)
        nxt_mix = None if last else (g_mix_pre[l + 1], l + 1, 0, 1)
        if l % 2 == 0:
            xs, h2 = _out_proj(m, w_out_bf, l, xs, g_mix_post[l], mods, l, 2, nxt_ffn, dims)
            act = _swiglu_up(h2, ffn_w1, ffn_w3, j)
            res = _out_proj(act, ffn_w2_bf, j, xs, g_ffn_post[l], mods, l, 5, nxt_mix, dims, name="ffn_down")
        else:
            n_exp = moe_router.shape[2]
            wr = jnp.pad(moe_router[j], ((0, 0), (0, 128 - n_exp)))
            wr_hi = wr.astype(BF)
            wr2 = jnp.stack([wr_hi, (wr - wr_hi.astype(F32)).astype(BF)])
            xs, h2, route = _out_proj(m, w_out_bf, l, xs, g_mix_post[l], mods, l, 2, nxt_ffn, dims,
                                      router=(wr2, n_exp))
            res = _moe_ffn(h2, route, moe_w1, moe_w3, moe_w2, j, xs, g_ffn_post[l], mods, l, nxt_mix, dims)
        if last:
            (xs,) = res
        else:
            xs, h = res
    return xs.reshape(n_batch, p_rows, d)[:, lc:, :]
```

```python
import functools

import numpy as np
import jax
import jax.numpy as jnp
from jax import lax
from jax.experimental import pallas as pl
from jax.experimental.pallas import tpu as pltpu

F32 = jnp.float32
BF = jnp.bfloat16

EPS = 1e-6
GRID_W = 64
ROPE_THETA = 10000.0
N_BRANCH = 4
MLA_NOPE, MLA_ROPE, MLA_V = 128, 64, 128
MLA_QK = MLA_NOPE + MLA_ROPE
HEAD_PAD = 256
HG_DK = 128
CONV_WIDTH = 31
POOL_WINDOWS = (2, 4, 8, 16)
POOL_GROUP = 128
HALO = 16
HG_CHUNK = 64
HG_EXP_CLAMP = 80.0
NEG_BIG = -1e30
ATTN_Q_TILE = 512
MOE_ROW_TILE = 256
MOE_LOAD_SLOTS = 3
MOE_ROW_ALIGN = 16
MOE_GATHER_ROWS = 1024

VMEM_LIMIT_V7X = 56 * 1024 * 1024

Z_CONV, Z_HQ, Z_HI, Z_HFF, Z_HFB, Z_HG, Z_POOL, Z_MLA, Z_GATE = 0, 1024, 1536, 2048, 2560, 3072, 3584, 4096, 5120
MLA_ZKV_OFF, MLA_ZR_OFF = 384, 640


def _cparams(sem, vmem=VMEM_LIMIT_V7X):
    return pltpu.CompilerParams(dimension_semantics=sem, vmem_limit_bytes=vmem)


def _divisor_tile(n, target, mult=16):
    best = None
    for t in range(mult, min(n, target) + 1, mult):
        if n % t == 0:
            best = t
    assert best is not None, (n, target, mult)
    return best


def _pick_row(m, r):
    rows = lax.broadcasted_iota(jnp.int32, m.shape, 0)
    return jnp.sum(jnp.where(rows == r, m, 0.0), axis=0, keepdims=True)


def _row_vec(m_ref, b, n_batch, ctx_mask):
    m = m_ref[0, 0]
    return jnp.where(ctx_mask, _pick_row(m, n_batch), _pick_row(m, b))


def _tile_rows(i, tm, p_rows, lc):
    per_batch = p_rows // tm
    b = i // per_batch
    pos = (i % per_batch) * tm + lax.broadcasted_iota(jnp.int32, (tm, 1), 0)
    return b, pos < lc


def _rms(x, g):
    return x * lax.rsqrt(jnp.mean(x * x, axis=-1, keepdims=True) + EPS) * g


def _sigmoid(x):
    return 1.0 / (1.0 + jnp.exp(-x))


def _silu(x):
    return x * _sigmoid(x)


def _modvec_kernel(c_ref, w_ref, b_ref, o_ref):
    s = _silu(c_ref[...])
    s_hi = s.astype(BF)
    s_lo = (s - s_hi.astype(F32)).astype(BF)
    w = w_ref[0].astype(BF)
    acc = jnp.dot(s_hi, w, preferred_element_type=F32) + jnp.dot(s_lo, w, preferred_element_type=F32)
    o_ref[0, 0] = acc + b_ref[0, 0]


def _modvecs(cvec, w_mod, b_mod):
    depth, d, n6 = w_mod.shape
    nj = n6 // d
    tn = _divisor_tile(d, 1024, 128)
    npj = d // tn
    return pl.pallas_call(
        _modvec_kernel,
        grid=(depth, nj, npj),
        in_specs=[
            pl.BlockSpec((8, d), lambda l, j, n: (0, 0)),
            pl.BlockSpec((1, d, tn), lambda l, j, n: (l, 0, j * npj + n)),
            pl.BlockSpec((1, 1, 1, tn), lambda l, j, n: (l, j, 0, n)),
        ],
        out_specs=pl.BlockSpec((1, 1, 8, tn), lambda l, j, n: (l, j, 0, n)),
        out_shape=jax.ShapeDtypeStruct((depth, nj, 8, d), F32),
        compiler_params=_cparams(("parallel", "parallel", "parallel")),
        name="modvec",
    )(cvec, w_mod, b_mod.reshape(depth, nj, 1, d))


def _modulate_kernel(x_ref, g_ref, sh_ref, sc_ref, h_ref, *, tm, p_rows, lc, n_batch):
    b, cm = _tile_rows(pl.program_id(0), tm, p_rows, lc)
    shift = _row_vec(sh_ref, b, n_batch, cm)
    scale = _row_vec(sc_ref, b, n_batch, cm)
    h_ref[...] = (_rms(x_ref[...], g_ref[...]) * (1.0 + scale) + shift).astype(h_ref.dtype)


def _mod_spec(l, j, d):
    return pl.BlockSpec((1, 1, 8, d), lambda *_: (l, j, 0, 0))


def _modulate(xs, g, mods, l, j_shift, j_scale, dims):
    t, d = xs.shape
    p_rows, lc, n_batch = dims
    tm = _divisor_tile(p_rows, 544)
    return pl.pallas_call(
        functools.partial(_modulate_kernel, tm=tm, p_rows=p_rows, lc=lc, n_batch=n_batch),
        grid=(t // tm,),
        in_specs=[
            pl.BlockSpec((tm, d), lambda i: (i, 0)),
            pl.BlockSpec((1, d), lambda i: (0, 0)),
            _mod_spec(l, j_shift, d),
            _mod_spec(l, j_scale, d),
        ],
        out_specs=pl.BlockSpec((tm, d), lambda i: (i, 0)),
        out_shape=jax.ShapeDtypeStruct((t, d), BF),
        compiler_params=_cparams(("parallel",)),
        name="modulate",
    )(xs, g.reshape(1, d), mods, mods)


def _mm_kernel(x_ref, w_ref, o_ref):
    o_ref[...] = jnp.dot(x_ref[...], w_ref[0], preferred_element_type=F32).astype(o_ref.dtype)


def _matmul(x, w, l, out_dtype, tm_target=2176, tn_target=1024):
    t, k = x.shape
    n = w.shape[2]
    tm = _divisor_tile(t, tm_target)
    tn = _divisor_tile(n, tn_target, 128)
    return pl.pallas_call(
        _mm_kernel,
        grid=(n // tn, t // tm),
        in_specs=[
            pl.BlockSpec((tm, k), lambda j, i: (i, 0)),
            pl.BlockSpec((1, k, tn), lambda j, i: (l, 0, j)),
        ],
        out_specs=pl.BlockSpec((tm, tn), lambda j, i: (i, j)),
        out_shape=jax.ShapeDtypeStruct((t, n), out_dtype),
        compiler_params=_cparams(("parallel", "arbitrary")),
        name="in_proj",
    )(x, w)


def _rope128(t, cos, sin, pm):
    partner = jnp.dot(t.astype(BF), pm, preferred_element_type=F32)
    return t * cos + partner * sin


def _mla_proj_kernel(z_ref, gq_ref, gkv_ref, wq_ref, wkv_ref, cos_ref, sin_ref, pm_ref,
                     q_ref, k_ref, v_ref, wq_s, wkv_s, *, n_heads, q_rank, kv_rank, scale):
    @pl.when(pl.program_id(0) == 0)
    def _():
        wq_s[...] = wq_ref[...].astype(BF)
        wkv_s[...] = wkv_ref[...].astype(BF)

    z = z_ref[...]
    zq = z[:, :q_rank].astype(F32)
    zkv = z[:, MLA_ZKV_OFF:MLA_ZKV_OFF + kv_rank].astype(F32)
    zr = z[:, MLA_ZR_OFF:MLA_ZR_OFF + 128].astype(F32)
    cos, sin, pm = cos_ref[...], sin_ref[...], pm_ref[...]
    qn = _rms(zq, gq_ref[...]).astype(BF)
    kvn = _rms(zkv, gkv_ref[...]).astype(BF)
    k_rope = _rope128(zr, cos, sin, pm)[:, :MLA_ROPE]
    for h in range(n_heads):
        qh = jnp.dot(qn, wq_s[:, h * HEAD_PAD:(h + 1) * HEAD_PAD], preferred_element_type=F32)
        q_rope = _rope128(qh[:, MLA_NOPE:], cos, sin, pm)[:, :MLA_ROPE]
        q_ref[h] = (jnp.concatenate([qh[:, :MLA_NOPE], q_rope], axis=-1) * scale).astype(q_ref.dtype)
        kvh = jnp.dot(kvn, wkv_s[:, h * HEAD_PAD:(h + 1) * HEAD_PAD], preferred_element_type=F32)
        k_ref[h] = jnp.concatenate([kvh[:, :MLA_NOPE], k_rope], axis=-1).astype(k_ref.dtype)
        v_ref[h] = jnp.concatenate([kvh[:, MLA_NOPE:], jnp.ones_like(kvh[:, MLA_NOPE:])], axis=-1).astype(v_ref.dtype)


def _mla_proj(z, g_q, g_kv, wq_pad, wkv, cos, sin, pm, dims):
    t = z.shape[0]
    p_rows = dims[0]
    q_rank, kv_rank = g_q.shape[0], g_kv.shape[0]
    n_heads = wkv.shape[1] // HEAD_PAD
    tm = _divisor_tile(p_rows, 544)
    npb = p_rows // tm
    scale = float(MLA_QK) ** -0.5 * float(np.log2(np.e))
    const = lambda i: (0, 0)
    return pl.pallas_call(
        functools.partial(_mla_proj_kernel, n_heads=n_heads, q_rank=q_rank, kv_rank=kv_rank, scale=scale),
        grid=(t // tm,),
        in_specs=[
            pl.BlockSpec((tm, 1024), lambda i: (i, Z_MLA // 1024)),
            pl.BlockSpec((1, q_rank), const),
            pl.BlockSpec((1, kv_rank), const),
            pl.BlockSpec(wq_pad.shape, const),
            pl.BlockSpec(wkv.shape, const),
            pl.BlockSpec((tm, 128), lambda i: (i % npb, 0)),
            pl.BlockSpec((tm, 128), lambda i: (i % npb, 0)),
            pl.BlockSpec((128, 128), const),
        ],
        out_specs=[
            pl.BlockSpec((n_heads, tm, MLA_QK), lambda i: (0, i, 0)),
            pl.BlockSpec((n_heads, tm, MLA_QK), lambda i: (0, i, 0)),
            pl.BlockSpec((n_heads, tm, 2 * MLA_V), lambda i: (0, i, 0)),
        ],
        out_shape=[
            jax.ShapeDtypeStruct((n_heads, t, MLA_QK), BF),
            jax.ShapeDtypeStruct((n_heads, t, MLA_QK), BF),
            jax.ShapeDtypeStruct((n_heads, t, 2 * MLA_V), BF),
        ],
        scratch_shapes=[pltpu.VMEM(wq_pad.shape, BF), pltpu.VMEM(wkv.shape, BF)],
        compiler_params=_cparams(("arbitrary",)),
        name="mla_proj",
    )(z, g_q.reshape(1, -1), g_kv.reshape(1, -1), wq_pad, wkv, cos, sin, pm)


def _attn_kernel(q_ref, k_ref, v_ref, o_ref, p_s, *, tq, lc):
    p_rows = k_ref.shape[1]
    n_tiles = (p_rows - lc) // tq
    nt = (((1,), (1,)), ((), ()))

    def weights(q0, n_q, n_keys):
        s = lax.dot_general(q_ref[0, pl.ds(q0, n_q), :], k_ref[0, :n_keys, :], nt, preferred_element_type=F32)
        return jnp.exp2((s - jnp.max(s, axis=-1, keepdims=True)).astype(BF))

    def emit(p, q0, n_q, n_keys):
        o = jnp.dot(p, v_ref[0, :n_keys, :], preferred_element_type=F32)
        o_ref[pl.ds(q0, n_q), :] = (o[:, :MLA_V] / o[:, MLA_V:MLA_V + 1]).astype(o_ref.dtype)

    def tile_start(i):
        return pl.multiple_of(lc + i * tq, lc)

    emit(weights(0, lc, lc), 0, lc, lc)
    p_s[...] = weights(tile_start(0), tq, p_rows)

    def body(i, carry):
        p_prev = p_s[...]
        p_next = weights(tile_start(i), tq, p_rows)
        emit(p_prev, tile_start(i - 1), tq, p_rows)
        p_s[...] = p_next
        return carry

    lax.fori_loop(1, n_tiles, body, 0)
    emit(p_s[...], tile_start(n_tiles - 1), tq, p_rows)


def _attention(q3, k3, v3, dims):
    n_heads, t, _ = q3.shape
    p_rows, lc, n_batch = dims
    tq = _divisor_tile(p_rows - lc, ATTN_Q_TILE, lc)
    return pl.pallas_call(
        functools.partial(_attn_kernel, tq=tq, lc=lc),
        grid=(n_batch, n_heads),
        in_specs=[
            pl.BlockSpec((1, p_rows, MLA_QK), lambda b, h: (h, b, 0)),
            pl.BlockSpec((1, p_rows, MLA_QK), lambda b, h: (h, b, 0)),
            pl.BlockSpec((1, p_rows, 2 * MLA_V), lambda b, h: (h, b, 0)),
        ],
        out_specs=pl.BlockSpec((p_rows, MLA_V), lambda b, h: (b, h)),
        out_shape=jax.ShapeDtypeStruct((t, n_heads * MLA_V), BF),
        scratch_shapes=[pltpu.VMEM((tq, p_rows), BF)],
        compiler_params=_cparams(("parallel", "parallel")),
        name="mla_attention",
    )(q3, k3, v3)


def _hgrn_chunk(zq, zi, zf, lb, tri, o_ref, st_ref, stream, bi, n_heads):
    feeds = tri > 0.5
    f = lb + (1.0 - lb) * _sigmoid(zf.astype(F32))
    lf = jnp.log(f)
    lf1 = lf.astype(BF)
    r1 = lf - lf1.astype(F32)
    lf2 = r1.astype(BF)
    lf3 = (r1 - lf2.astype(F32)).astype(BF)
    b_all = (jnp.dot(tri, lf1, preferred_element_type=F32) + jnp.dot(tri, lf2, preferred_element_type=F32)
             + jnp.dot(tri, lf3, preferred_element_type=F32))
    tot_all = jnp.sum(lf, axis=0, keepdims=True)
    q_all = _silu(zq.astype(F32))
    nt = (((1,), (1,)), ((), ()))
    for h in range(n_heads):
        sl = slice(h * HG_DK, (h + 1) * HG_DK)
        b, tot, q, k = b_all[:, sl], tot_all[:, sl], q_all[:, sl], 1.0 - f[:, sl]
        v = zi[:, sl]
        ref = 0.5 * tot
        q_loc = (q * jnp.exp(jnp.minimum(b - ref, HG_EXP_CLAMP))).astype(BF)
        k_loc = (k * jnp.exp(jnp.minimum(ref - b, HG_EXP_CLAMP))).astype(BF)
        a = lax.dot_general(q_loc, k_loc, nt, preferred_element_type=F32)
        a = jnp.where(feeds, a, 0.0).astype(BF)
        state_t = st_ref[stream, h]
        q_in = (q * jnp.exp(b)).astype(BF)
        o = jnp.dot(a, v, preferred_element_type=F32)
        o = o + lax.dot_general(q_in, state_t.astype(BF), nt, preferred_element_type=F32)
        o_ref[bi, :, sl] = o
        k_out = (k * jnp.exp(tot - b)).astype(BF)
        upd = lax.dot_general(v, k_out, (((0,), (0,)), ((), ())), preferred_element_type=F32)
        st_ref[stream, h] = state_t * jnp.exp(tot) + upd


def _hgrn_kernel(fq_ref, fi_ref, ff_ref, bq_ref, bi_ref, bf_ref, lb_ref, tri_ref, of_ref, ob_ref, st_ref,
                 *, n_heads, n_batch):
    @pl.when(pl.program_id(0) == 0)
    def _():
        st_ref[...] = jnp.zeros_like(st_ref)

    for rev, (q_ref, i_ref, f_ref, o_ref) in enumerate(((fq_ref, fi_ref, ff_ref, of_ref),
                                                       (bq_ref, bi_ref, bf_ref, ob_ref))):
        for bi in range(n_batch):
            _hgrn_chunk(q_ref[bi], i_ref[bi], f_ref[bi], lb_ref[rev], tri_ref[rev], o_ref, st_ref,
                        rev * n_batch + bi, bi, n_heads)


def _hgrn(z, lb2, dims):
    t, nz = z.shape
    p_rows, lc, n_batch = dims
    c = HG_CHUNK
    w = lb2.shape[-1]
    n_heads = w // HG_DK
    nch, nctx = p_rows // c, lc // c
    idx = np.arange(c)
    tri = jnp.asarray(np.stack([idx[None, :] <= idx[:, None], idx[None, :] >= idx[:, None]]), BF)
    z3 = z.reshape(n_batch, p_rows, nz)

    def back(j):
        return jnp.where(j < nctx, nctx - 1 - j, nch - 1 - (j - nctx))

    def spec(col, rev):
        if rev:
            return pl.BlockSpec((n_batch, c, w), lambda j: (0, back(j), col // w))
        return pl.BlockSpec((n_batch, c, w), lambda j: (0, j, col // w))

    out_f, out_b = pl.pallas_call(
        functools.partial(_hgrn_kernel, n_heads=n_heads, n_batch=n_batch),
        grid=(nch,),
        in_specs=[
            spec(Z_HQ, False), spec(Z_HI, False), spec(Z_HFF, False),
            spec(Z_HQ, True), spec(Z_HI, True), spec(Z_HFB, True),
            pl.BlockSpec((2, 1, w), lambda j: (0, 0, 0)),
            pl.BlockSpec((2, c, c), lambda j: (0, 0, 0)),
        ],
        out_specs=[pl.BlockSpec((n_batch, c, w), lambda j: (0, j, 0)),
                   pl.BlockSpec((n_batch, c, w), lambda j: (0, back(j), 0))],
        out_shape=[jax.ShapeDtypeStruct((n_batch, p_rows, w), F32)] * 2,
        scratch_shapes=[pltpu.VMEM((2 * n_batch, n_heads, HG_DK, HG_DK), F32)],
        compiler_params=_cparams(("arbitrary",)),
        name="hgrn_scan",
    )(z3, z3, z3, z3, z3, z3, lb2.reshape(2, 1, w), tri)
    return out_f.reshape(t, w), out_b.reshape(t, w)


def _hgrn_readout_kernel(of_ref, ob_ref, zg_ref, g_ref, y_ref, *, n_heads):
    o = of_ref[...] + ob_ref[...]
    g = g_ref[...]
    gate = _silu(zg_ref[...].astype(F32))
    for h in range(n_heads):
        sl = slice(h * HG_DK, (h + 1) * HG_DK)
        y_ref[:, sl] = (_rms(o[:, sl], g[:, sl]) * gate[:, sl]).astype(y_ref.dtype)


def _hgrn_readout(o_fwd, o_bwd, z, g_norm):
    t, w = o_fwd.shape
    tm = _divisor_tile(t, 1088)
    return pl.pallas_call(
        functools.partial(_hgrn_readout_kernel, n_heads=w // HG_DK),
        grid=(t // tm,),
        in_specs=[
            pl.BlockSpec((tm, w), lambda i: (i, 0)),
            pl.BlockSpec((tm, w), lambda i: (i, 0)),
            pl.BlockSpec((tm, w), lambda i: (i, Z_HG // w)),
            pl.BlockSpec((1, w), lambda i: (0, 0)),
        ],
        out_specs=pl.BlockSpec((tm, w), lambda i: (i, 0)),
        out_shape=jax.ShapeDtypeStruct((t, w), BF),
        compiler_params=_cparams(("parallel",)),
        name="hgrn_readout",
    )(o_fwd, o_bwd, z, g_norm.reshape(1, w))


def _segment_tile(i, tc, p_rows, lc):
    per_batch = p_rows // tc
    j = i % per_batch
    nctx = lc // tc
    first = jnp.logical_or(j == 0, j == nctx)
    last = jnp.logical_or(j == nctx - 1, j == per_batch - 1)
    in_ctx = j < nctx
    seg_pos0 = jnp.where(in_ctx, j, j - nctx) * tc
    seg_len = jnp.where(in_ctx, lc, p_rows - lc)
    return first, last, seg_pos0, seg_len


def _halo_specs(tc, width, col_block, n_rows):
    per = tc // HALO
    last_blk = n_rows // HALO - 1
    return [
        pl.BlockSpec((HALO, width), lambda i: (jnp.maximum(i * per - 1, 0), col_block)),
        pl.BlockSpec((tc, width), lambda i: (i, col_block)),
        pl.BlockSpec((HALO, width), lambda i: (jnp.minimum((i + 1) * per, last_blk), col_block)),
    ]


def _conv_kernel(prev_ref, cur_ref, next_ref, w_ref, b_ref, lg_ref, lb_ref, y_ref, buf, *, tc, p_rows, lc, ch):
    first, last, _, _ = _segment_tile(pl.program_id(0), tc, p_rows, lc)

    def glu(zz):
        zz = zz.astype(F32)
        return zz[:, :ch] * _sigmoid(zz[:, ch:])

    buf[0:HALO, :] = jnp.where(first, 0.0, glu(prev_ref[...]))
    buf[HALO:HALO + tc, :] = glu(cur_ref[...])
    buf[HALO + tc:, :] = jnp.where(last, 0.0, glu(next_ref[...]))
    w = w_ref[...]
    acc = jnp.zeros((tc, ch), F32) + b_ref[...]
    for kk in range(CONV_WIDTH):
        off = HALO - CONV_WIDTH // 2 + kk
        acc = acc + buf[off:off + tc, :] * w[kk:kk + 1, :]
    mu = jnp.mean(acc, axis=-1, keepdims=True)
    cen = acc - mu
    var = jnp.mean(cen * cen, axis=-1, keepdims=True)
    y = cen * lax.rsqrt(var + EPS) * lg_ref[...] + lb_ref[...]
    y_ref[...] = _silu(y).astype(y_ref.dtype)


def _conv_mixer(z, w_dw, b_dw, ln_g, ln_b, dims):
    t = z.shape[0]
    p_rows, lc, _ = dims
    ch = w_dw.shape[1]
    tc = _divisor_tile(lc, 256)
    row = lambda a: a.reshape(1, ch)
    const = lambda i: (0, 0)
    return pl.pallas_call(
        functools.partial(_conv_kernel, tc=tc, p_rows=p_rows, lc=lc, ch=ch),
        grid=(t // tc,),
        in_specs=_halo_specs(tc, 2 * ch, Z_CONV // (2 * ch), t) + [
            pl.BlockSpec((CONV_WIDTH, ch), const),
            pl.BlockSpec((1, ch), const), pl.BlockSpec((1, ch), const), pl.BlockSpec((1, ch), const),
        ],
        out_specs=pl.BlockSpec((tc, ch), lambda i: (i, 0)),
        out_shape=jax.ShapeDtypeStruct((t, ch), BF),
        scratch_shapes=[pltpu.VMEM((tc + 2 * HALO, ch), F32)],
        compiler_params=_cparams(("parallel",)),
        name="conv_mixer",
    )(z, z, z, w_dw, row(b_dw), row(ln_g), row(ln_b))


def _pool_kernel(prev_ref, cur_ref, next_ref, w_ref, s_ref, y_ref, buf, *, tc, p_rows, lc):
    first, last, seg_pos0, seg_len = _segment_tile(pl.program_id(0), tc, p_rows, lc)
    buf[0:HALO, :] = jnp.where(first, 0.0, prev_ref[...].astype(F32))
    buf[HALO:HALO + tc, :] = cur_ref[...].astype(F32)
    buf[HALO + tc:, :] = jnp.where(last, 0.0, next_ref[...].astype(F32))
    pos = seg_pos0 + lax.broadcasted_iota(jnp.int32, (tc, 1), 0)
    scale = s_ref[...]
    for gi, win in enumerate(POOL_WINDOWS):
        sl = slice(gi * POOL_GROUP, (gi + 1) * POOL_GROUP)
        half = win // 2
        tot = jnp.zeros((tc, POOL_GROUP), F32)
        for u in range(-half, half):
            tot = tot + buf[HALO + u:HALO + u + tc, sl]
        cnt = (jnp.minimum(pos + half, seg_len) - jnp.maximum(pos - half, 0)).astype(F32)
        pooled = tot / cnt - buf[HALO:HALO + tc, sl]
        y = jnp.dot(pooled.astype(BF), w_ref[gi].astype(BF), preferred_element_type=F32)
        y_ref[:, sl] = (y * scale[:, sl]).astype(y_ref.dtype)


def _pool_mixer(z, w_pool, scale, dims):
    t = z.shape[0]
    p_rows, lc, _ = dims
    ch = scale.shape[0]
    tc = _divisor_tile(lc, 256)
    return pl.pallas_call(
        functools.partial(_pool_kernel, tc=tc, p_rows=p_rows, lc=lc),
        grid=(t // tc,),
        in_specs=_halo_specs(tc, ch, Z_POOL // ch, t) + [
            pl.BlockSpec(w_pool.shape, lambda i: (0, 0, 0)),
            pl.BlockSpec((1, ch), lambda i: (0, 0)),
        ],
        out_specs=pl.BlockSpec((tc, ch), lambda i: (i, 0)),
        out_shape=jax.ShapeDtypeStruct((t, ch), BF),
        scratch_shapes=[pltpu.VMEM((tc + 2 * HALO, ch), F32)],
        compiler_params=_cparams(("parallel",)),
        name="pool_mixer",
    )(z, z, z, w_pool, scale.reshape(1, ch))


def _merge_kernel(y0, y1, y2, y3, g0, g1, g2, g3, w_ref, o_ref, w_s):
    @pl.when(pl.program_id(1) == 0)
    def _():
        w_s[...] = w_ref[0].astype(BF)

    acc = None
    for n, (y_ref, g_ref) in enumerate(((y0, g0), (y1, g1), (y2, g2), (y3, g3))):
        proj = jnp.dot(y_ref[...], w_s[n], preferred_element_type=F32)
        gate = 0.5 * jnp.tanh(0.5 * g_ref[...].astype(F32)) + 0.5
        term = gate * proj
        acc = term if acc is None else acc + term
    o_ref[...] = acc.astype(o_ref.dtype)


def _merge(ys, z, w_branch, l):
    t, bw = ys[0].shape
    _, nb, _, d = w_branch.shape
    tm = _divisor_tile(t, 544)
    tn = _divisor_tile(d, 1024, 128)
    npn = d // tn

    def gate_spec(n):
        return pl.BlockSpec((tm, tn), lambda j, i: (i, Z_GATE // tn + n * npn + j))

    return pl.pallas_call(
        _merge_kernel,
        grid=(npn, t // tm),
        in_specs=[pl.BlockSpec((tm, bw), lambda j, i: (i, 0))] * nb + [gate_spec(n) for n in range(nb)] + [
            pl.BlockSpec((1, nb, bw, tn), lambda j, i: (l, 0, 0, j))],
        out_specs=pl.BlockSpec((tm, tn), lambda j, i: (i, j)),
        out_shape=jax.ShapeDtypeStruct((t, d), BF),
        scratch_shapes=[pltpu.VMEM((nb, bw, tn), BF)],
        compiler_params=_cparams(("parallel", "arbitrary")),
        name="branch_merge",
    )(*ys, z, z, z, z, w_branch)


def _residual_epilogue(y, x_ref, gpost_ref, gate_ref, gnext_ref, sh_ref, sc_ref, xo_ref, ho_ref, b, n_batch, cm):
    gate = _row_vec(gate_ref, b, n_batch, cm)
    x_new = x_ref[...] + gate * _rms(y, gpost_ref[...])
    xo_ref[...] = x_new
    if ho_ref is None:
        return None
    shift = _row_vec(sh_ref, b, n_batch, cm)
    scale = _row_vec(sc_ref, b, n_batch, cm)
    h = _rms(x_new, gnext_ref[...]) * (1.0 + scale) + shift
    ho_ref[...] = h.astype(ho_ref.dtype)
    return h


def _route_top2(logits, n_experts):
    lane = lax.broadcasted_iota(jnp.int32, logits.shape, 1)
    lg = jnp.where(lane < n_experts, logits, NEG_BIG)
    m1 = jnp.max(lg, axis=-1, keepdims=True)
    i1 = jnp.min(jnp.where(lg == m1, lane, 1 << 20), axis=-1, keepdims=True)
    lg2 = jnp.where(lane == i1, NEG_BIG, lg)
    m2 = jnp.max(lg2, axis=-1, keepdims=True)
    i2 = jnp.min(jnp.where(lg2 == m2, lane, 1 << 20), axis=-1, keepdims=True)
    e2 = jnp.exp(m2 - m1)
    w1 = 1.0 / (1.0 + e2)
    w2 = e2 / (1.0 + e2)
    out = jnp.where(lane == 0, i1.astype(F32), 0.0)
    out = jnp.where(lane == 1, i2.astype(F32), out)
    out = jnp.where(lane == 2, w1, out)
    return jnp.where(lane == 3, w2, out)


def _router_logits(h, wr_ref):
    h_hi = h.astype(BF)
    h_lo = (h - h_hi.astype(F32)).astype(BF)
    return (jnp.dot(h_hi, wr_ref[0], preferred_element_type=F32) + jnp.dot(h_hi, wr_ref[1], preferred_element_type=F32)
            + jnp.dot(h_lo, wr_ref[0], preferred_element_type=F32))


def _out_proj_kernel(*refs, tm, p_rows, lc, n_batch, emit_h, n_experts):
    m_ref, w_ref, x_ref, gpost_ref, gate_ref = refs[:5]
    pos = 5
    gnext_ref = sh_ref = sc_ref = wr_ref = None
    if emit_h:
        gnext_ref, sh_ref, sc_ref = refs[pos:pos + 3]
        pos += 3
    if n_experts:
        wr_ref = refs[pos]
        pos += 1
    xo_ref = refs[pos]
    pos += 1
    ho_ref = route_ref = None
    if emit_h:
        ho_ref = refs[pos]
        pos += 1
    if n_experts:
        route_ref = refs[pos]
        pos += 1
    b, cm = _tile_rows(pl.program_id(0), tm, p_rows, lc)
    y = jnp.dot(m_ref[...], w_ref[0], preferred_element_type=F32)
    h = _residual_epilogue(y, x_ref, gpost_ref, gate_ref, gnext_ref, sh_ref, sc_ref, xo_ref, ho_ref, b, n_batch, cm)
    if n_experts:
        route_ref[...] = _route_top2(_router_logits(h, wr_ref), n_experts)


def _out_proj(m, w_stack, lw, xs, g_post, mods, l, j_gate, nxt, dims, router=None, name="out_proj"):
    t, d = xs.shape
    k = m.shape[1]
    p_rows, lc, n_batch = dims
    tm = _divisor_tile(p_rows, 544 if k <= d else 272)
    const = lambda i: (0, 0)
    rowtile = pl.BlockSpec((tm, d), lambda i: (i, 0))
    args = [m, w_stack, xs, g_post.reshape(1, d), mods]
    in_specs = [pl.BlockSpec((tm, k), lambda i: (i, 0)),
                pl.BlockSpec((1, k, d), lambda i: (lw, 0, 0), pipeline_mode=pl.Buffered(1)),
                rowtile, pl.BlockSpec((1, d), const), _mod_spec(l, j_gate, d)]
    out_shape = [jax.ShapeDtypeStruct((t, d), F32)]
    out_specs = [rowtile]
    n_experts = 0
    if nxt is not None:
        g_next, l_next, j_shift, j_scale = nxt
        args += [g_next.reshape(1, d), mods, mods]
        in_specs += [pl.BlockSpec((1, d), const), _mod_spec(l_next, j_shift, d), _mod_spec(l_next, j_scale, d)]
        out_shape.append(jax.ShapeDtypeStruct((t, d), BF if router is None else F32))
        out_specs.append(rowtile)
    if router is not None:
        wr, n_experts = router
        args.append(wr)
        in_specs.append(pl.BlockSpec(wr.shape, lambda i: (0, 0, 0)))
        out_shape.append(jax.ShapeDtypeStruct((t, 128), F32))
        out_specs.append(pl.BlockSpec((tm, 128), lambda i: (i, 0)))
    return pl.pallas_call(
        functools.partial(_out_proj_kernel, tm=tm, p_rows=p_rows, lc=lc, n_batch=n_batch,
                          emit_h=nxt is not None, n_experts=n_experts),
        grid=(t // tm,),
        in_specs=in_specs,
        out_specs=out_specs,
        out_shape=out_shape,
        compiler_params=_cparams(("parallel",)),
        name=name,
    )(*args)


def _swiglu_kernel(x_ref, w1_ref, w3_ref, o_ref, w1_s, w3_s):
    @pl.when(pl.program_id(1) == 0)
    def _():
        w1_s[...] = w1_ref[0].astype(BF)
        w3_s[...] = w3_ref[0].astype(BF)

    x = x_ref[...]
    a = jnp.dot(x, w1_s[...], preferred_element_type=F32)
    g = jnp.dot(x, w3_s[...], preferred_element_type=F32)
    o_ref[...] = (_silu(a) * g).astype(o_ref.dtype)


def _swiglu_up(h, w1, w3, jl):
    t, d = h.shape
    dff = w1.shape[2]
    tm = _divisor_tile(t, 1088)
    tn = _divisor_tile(dff, 512, 128)
    return pl.pallas_call(
        _swiglu_kernel,
        grid=(dff // tn, t // tm),
        in_specs=[
            pl.BlockSpec((tm, d), lambda j, i: (i, 0)),
            pl.BlockSpec((1, d, tn), lambda j, i: (jl, 0, j)),
            pl.BlockSpec((1, d, tn), lambda j, i: (jl, 0, j)),
        ],
        out_specs=pl.BlockSpec((tm, tn), lambda j, i: (i, j)),
        out_shape=jax.ShapeDtypeStruct((t, dff), BF),
        scratch_shapes=[pltpu.VMEM((d, tn), BF), pltpu.VMEM((d, tn), BF)],
        compiler_params=_cparams(("parallel", "arbitrary")),
        name="swiglu_up",
    )(h, w1, w3)


def _gather_rows_kernel(idx_ref, src_ref, o_ref, *scratch, tr):
    base = pl.program_id(0) * tr
    dst, sem = (o_ref, scratch[0]) if len(scratch) == 1 else scratch

    def copy(r):
        return pltpu.make_async_copy(src_ref.at[pl.ds(idx_ref[base + r], 1), :], dst.at[pl.ds(r, 1), :], sem)

    def issue(r, carry):
        copy(r).start()
        return carry

    lax.fori_loop(0, tr, issue, 0, unroll=8)

    def drain(r, carry):
        copy(r).wait()
        return carry

    lax.fori_loop(0, tr, drain, 0, unroll=8)
    if dst is not o_ref:
        o_ref[...] = dst[...].astype(o_ref.dtype)


def _gather_rows(src, idx, out_dtype):
    n, d = src.shape
    assert src.dtype == F32
    rows = idx.shape[0]
    tr = _divisor_tile(rows, MOE_GATHER_ROWS, 8)
    staging = [] if out_dtype == src.dtype else [pltpu.VMEM((tr, d), src.dtype)]
    return pl.pallas_call(
        functools.partial(_gather_rows_kernel, tr=tr),
        grid_spec=pltpu.PrefetchScalarGridSpec(
            num_scalar_prefetch=1,
            grid=(rows // tr,),
            in_specs=[pl.BlockSpec(memory_space=pl.ANY)],
            out_specs=pl.BlockSpec((tr, d), lambda i, idx_ref: (i, 0)),
            scratch_shapes=staging + [pltpu.SemaphoreType.DMA(())],
        ),
        out_shape=jax.ShapeDtypeStruct((rows, d), out_dtype),
        compiler_params=_cparams(("arbitrary",)),
        name="moe_gather",
    )(idx, src)


def _expert_rows_loop(st_ref, nt_ref, tr, load_copy, store_copy, compute):
    e, n = pl.program_id(0), pl.program_id(1)
    n_exp, n_col = pl.num_programs(0), pl.num_programs(1)
    start, n_tiles = st_ref[e], nt_ref[e]
    first_step = jnp.logical_and(e == 0, n == 0)
    last_step = jnp.logical_and(e == n_exp - 1, n == n_col - 1)
    e_next = jnp.minimum(jnp.where(n == n_col - 1, e + 1, e), n_exp - 1)

    @pl.when(jnp.logical_and(first_step, n_tiles > 0))
    def _():
        load_copy(start, 0, 0).start(priority=1)

    @pl.when(n_tiles > 1)
    def _():
        load_copy(start, 1, 1).start(priority=1)

    def body(i, carry):
        load_copy(start, i, i % MOE_LOAD_SLOTS).wait()

        @pl.when(i + 2 < n_tiles)
        def _():
            load_copy(start, i + 2, (i + 2) % MOE_LOAD_SLOTS).start(priority=1)

        @pl.when(i >= 2)
        def _():
            store_copy(start, i - 2, i % 2).wait()

        compute(i % MOE_LOAD_SLOTS, i % 2)
        store_copy(start, i, i % 2).start(priority=1)
        return carry

    lax.fori_loop(0, n_tiles, body, 0)

    @pl.when(jnp.logical_and(jnp.logical_not(last_step), nt_ref[e_next] > 0))
    def _():
        load_copy(st_ref[e_next], 0, 0).start(priority=1)

    @pl.when(n_tiles >= 2)
    def _():
        store_copy(start, n_tiles - 2, n_tiles % 2).wait()

    @pl.when(n_tiles >= 1)
    def _():
        store_copy(start, n_tiles - 1, (n_tiles - 1) % 2).wait()


def _zero_tail_rows(out_hbm, zbuf, sem, first_row, col0, tn):
    zbuf[...] = jnp.zeros_like(zbuf)
    n_chunks = (out_hbm.shape[0] - first_row) // MOE_ROW_ALIGN

    def copy(c):
        r0 = pl.multiple_of(first_row + c * MOE_ROW_ALIGN, MOE_ROW_ALIGN)
        return pltpu.make_async_copy(zbuf, out_hbm.at[pl.ds(r0, MOE_ROW_ALIGN), pl.ds(col0, tn)], sem)

    def issue(c, carry):
        copy(c).start()
        return carry

    def drain(c, carry):
        copy(c).wait()
        return carry

    lax.fori_loop(0, n_chunks, issue, 0)
    lax.fori_loop(0, n_chunks, drain, 0)


def _moe_up_kernel(st_ref, nt_ref, x_hbm, w1_ref, w3_ref, a_hbm, w1_s, w3_s, xbuf, obuf, zbuf, sem_in, sem_out, sem_z,
                   *, tr, tn):
    e, n = pl.program_id(0), pl.program_id(1)
    w1_s[...] = w1_ref[0, 0].astype(BF)
    w3_s[...] = w3_ref[0, 0].astype(BF)
    col0 = pl.multiple_of(n * tn, tn)

    def rows(start, i):
        return pl.ds(pl.multiple_of(start + i * tr, MOE_ROW_ALIGN), tr)

    def load_copy(start, i, slot):
        return pltpu.make_async_copy(x_hbm.at[rows(start, i)], xbuf.at[slot], sem_in.at[slot])

    def store_copy(start, i, slot):
        return pltpu.make_async_copy(obuf.at[slot], a_hbm.at[rows(start, i), pl.ds(col0, tn)], sem_out.at[slot])

    def compute(in_slot, out_slot):
        x = xbuf[in_slot]
        a = jnp.dot(x, w1_s[...], preferred_element_type=F32)
        g = jnp.dot(x, w3_s[...], preferred_element_type=F32)
        obuf[out_slot] = (_silu(a) * g).astype(obuf.dtype)

    _expert_rows_loop(st_ref, nt_ref, tr, load_copy, store_copy, compute)

    @pl.when(e == pl.num_programs(0) - 1)
    def _():
        _zero_tail_rows(a_hbm, zbuf, sem_z, st_ref[e] + nt_ref[e] * tr, col0, tn)


def _moe_up(xs, w1, w3, jl, starts, n_tiles, tr):
    rows, d = xs.shape
    n_exp, dff = w1.shape[1], w1.shape[3]
    tn = _divisor_tile(dff, 512, 128)
    return pl.pallas_call(
        functools.partial(_moe_up_kernel, tr=tr, tn=tn),
        grid_spec=pltpu.PrefetchScalarGridSpec(
            num_scalar_prefetch=2,
            grid=(n_exp, dff // tn),
            in_specs=[
                pl.BlockSpec(memory_space=pl.ANY),
                pl.BlockSpec((1, 1, d, tn), lambda e, n, st, nt: (jl, e, 0, n)),
                pl.BlockSpec((1, 1, d, tn), lambda e, n, st, nt: (jl, e, 0, n)),
            ],
            out_specs=pl.BlockSpec(memory_space=pl.ANY),
            scratch_shapes=[pltpu.VMEM((d, tn), BF), pltpu.VMEM((d, tn), BF),
                            pltpu.VMEM((MOE_LOAD_SLOTS, tr, d), BF), pltpu.VMEM((2, tr, tn), BF),
                            pltpu.VMEM((MOE_ROW_ALIGN, tn), BF),
                            pltpu.SemaphoreType.DMA((MOE_LOAD_SLOTS,)), pltpu.SemaphoreType.DMA((2,)),
                            pltpu.SemaphoreType.DMA(())],
        ),
        out_shape=jax.ShapeDtypeStruct((rows, dff), BF),
        compiler_params=_cparams(("arbitrary", "arbitrary")),
        name="moe_up",
    )(starts, n_tiles, xs, w1, w3)


def _moe_down_kernel(st_ref, nt_ref, a_hbm, w_ref, y_hbm, w_s, abuf, obuf, zbuf, sem_in, sem_out, sem_z, *, tr, tn):
    e, n = pl.program_id(0), pl.program_id(1)
    w_s[...] = w_ref[0, 0].astype(BF)
    col0 = pl.multiple_of(n * tn, tn)

    def rows(start, i):
        return pl.ds(pl.multiple_of(start + i * tr, MOE_ROW_ALIGN), tr)

    def load_copy(start, i, slot):
        return pltpu.make_async_copy(a_hbm.at[rows(start, i)], abuf.at[slot], sem_in.at[slot])

    def store_copy(start, i, slot):
        return pltpu.make_async_copy(obuf.at[slot], y_hbm.at[rows(start, i), pl.ds(col0, tn)], sem_out.at[slot])

    def compute(in_slot, out_slot):
        obuf[out_slot] = jnp.dot(abuf[in_slot], w_s[...], preferred_element_type=F32)

    _expert_rows_loop(st_ref, nt_ref, tr, load_copy, store_copy, compute)

    @pl.when(e == pl.num_programs(0) - 1)
    def _():
        _zero_tail_rows(y_hbm, zbuf, sem_z, st_ref[e] + nt_ref[e] * tr, col0, tn)


def _moe_down(act, w2, jl, starts, n_tiles, tr):
    rows, dff = act.shape
    n_exp, d = w2.shape[1], w2.shape[3]
    tn = _divisor_tile(d, 512, 128)
    return pl.pallas_call(
        functools.partial(_moe_down_kernel, tr=tr, tn=tn),
        grid_spec=pltpu.PrefetchScalarGridSpec(
            num_scalar_prefetch=2,
            grid=(n_exp, d // tn),
            in_specs=[
                pl.BlockSpec(memory_space=pl.ANY),
                pl.BlockSpec((1, 1, dff, tn), lambda e, n, st, nt: (jl, e, 0, n)),
            ],
            out_specs=pl.BlockSpec(memory_space=pl.ANY),
            scratch_shapes=[pltpu.VMEM((dff, tn), BF), pltpu.VMEM((MOE_LOAD_SLOTS, tr, dff), BF),
                            pltpu.VMEM((2, tr, tn), F32), pltpu.VMEM((MOE_ROW_ALIGN, tn), F32),
                            pltpu.SemaphoreType.DMA((MOE_LOAD_SLOTS,)), pltpu.SemaphoreType.DMA((2,)),
                            pltpu.SemaphoreType.DMA(())],
        ),
        out_shape=jax.ShapeDtypeStruct((rows, d), F32),
        compiler_params=_cparams(("arbitrary", "arbitrary")),
        name="moe_down",
    )(starts, n_tiles, act, w2)


def _combine_kernel(*refs, tm, p_rows, lc, n_batch, emit_h):
    ya_ref, yb_ref, rt_ref, x_ref, gpost_ref, gate_ref = refs[:6]
    gnext_ref = sh_ref = sc_ref = ho_ref = None
    if emit_h:
        gnext_ref, sh_ref, sc_ref, xo_ref, ho_ref = refs[6:]
    else:
        (xo_ref,) = refs[6:]
    b, cm = _tile_rows(pl.program_id(0), tm, p_rows, lc)
    rt = rt_ref[...]
    y = rt[:, 2:3] * ya_ref[...] + rt[:, 3:4] * yb_ref[...]
    _residual_epilogue(y, x_ref, gpost_ref, gate_ref, gnext_ref, sh_ref, sc_ref, xo_ref, ho_ref, b, n_batch, cm)


def _moe_combine(y2, route, xs, g_post, mods, l, j_gate, nxt, dims):
    t, d = xs.shape
    p_rows, lc, n_batch = dims
    tm = _divisor_tile(p_rows, 544)
    nt = t // tm
    const = lambda i: (0, 0)
    rowtile = pl.BlockSpec((tm, d), lambda i: (i, 0))
    args = [y2, y2, route, xs, g_post.reshape(1, d), mods]
    in_specs = [rowtile, pl.BlockSpec((tm, d), lambda i: (i + nt, 0)), pl.BlockSpec((tm, 128), lambda i: (i, 0)),
                rowtile, pl.BlockSpec((1, d), const), _mod_spec(l, j_gate, d)]
    out_shape = [jax.ShapeDtypeStruct((t, d), F32)]
    out_specs = [rowtile]
    if nxt is not None:
        g_next, l_next, j_shift, j_scale = nxt
        args += [g_next.reshape(1, d), mods, mods]
        in_specs += [pl.BlockSpec((1, d), const), _mod_spec(l_next, j_shift, d), _mod_spec(l_next, j_scale, d)]
        out_shape.append(jax.ShapeDtypeStruct((t, d), BF))
        out_specs.append(rowtile)
    return pl.pallas_call(
        functools.partial(_combine_kernel, tm=tm, p_rows=p_rows, lc=lc, n_batch=n_batch, emit_h=nxt is not None),
        grid=(nt,),
        in_specs=in_specs,
        out_specs=out_specs,
        out_shape=out_shape,
        compiler_params=_cparams(("parallel",)),
        name="moe_combine",
    )(*args)


def _moe_ffn(h, route, w1, w3, w2, jl, xs, g_post, mods, l, nxt, dims):
    t, d = h.shape
    n_exp = w1.shape[1]
    tr = MOE_ROW_TILE
    expert = jnp.concatenate([route[:, 0], route[:, 1]]).astype(jnp.int32)
    token = jnp.concatenate([jnp.arange(t, dtype=jnp.int32)] * 2)
    onehot = (expert[:, None] == jnp.arange(n_exp, dtype=jnp.int32)[None, :]).astype(jnp.int32)
    rank = jnp.sum((jnp.cumsum(onehot, axis=0) - onehot) * onehot, axis=1)
    counts = jnp.sum(onehot, axis=0)
    padded = ((counts + MOE_ROW_ALIGN - 1) // MOE_ROW_ALIGN) * MOE_ROW_ALIGN
    ends = jnp.cumsum(padded)
    starts = (ends - padded).astype(jnp.int32)
    n_tiles = ((counts + tr - 1) // tr).astype(jnp.int32)
    dest = starts[expert] + rank
    rows = 2 * t + n_exp * MOE_ROW_ALIGN + tr
    rows = ((rows + MOE_GATHER_ROWS - 1) // MOE_GATHER_ROWS) * MOE_GATHER_ROWS
    row_token = jnp.zeros((rows,), jnp.int32).at[dest].set(token)

    xs_sorted = _gather_rows(h, row_token, BF)
    act = _moe_up(xs_sorted, w1, w3, jl, starts, n_tiles, tr)
    y_sorted = _moe_down(act, w2, jl, starts, n_tiles, tr)
    y2 = _gather_rows(y_sorted, dest, F32)
    return _moe_combine(y2, route, xs, g_post, mods, l, 5, nxt, dims)


def _permute_w_in(w_in, d):
    q_rank, kv_rank = (3 * d) // 16, d // 8
    bw = d // N_BRANCH
    o = 0
    pieces = {}
    for name, width in (("zq", q_rank), ("zkv", kv_rank), ("zr", MLA_ROPE), ("hq", bw), ("hi", bw), ("hff", bw),
                        ("hfb", bw), ("hg", bw), ("conv", 2 * bw), ("pool", bw), ("gate", N_BRANCH * d)):
        pieces[name] = w_in[:, :, o:o + width]
        o += width
    pad = jnp.zeros((w_in.shape[0], d, Z_GATE - Z_MLA - q_rank - kv_rank - MLA_ROPE), w_in.dtype)
    cols = [pieces["conv"], pieces["hq"], pieces["hi"], pieces["hff"], pieces["hfb"], pieces["hg"], pieces["pool"],
            pieces["zq"], pieces["zkv"], pieces["zr"], pad, pieces["gate"]]
    return jnp.concatenate(cols, axis=2).astype(BF)


def _pad_heads(w, n_heads, width):
    r = w.shape[0]
    w3 = w.reshape(r, n_heads, width)
    return jnp.pad(w3, ((0, 0), (0, 0), (0, HEAD_PAD - width))).reshape(r, n_heads * HEAD_PAD)


def _rope_tables(l_lat, lc):
    tt = jnp.arange(l_lat)
    row = (tt // GRID_W).astype(F32)
    col = (tt % GRID_W).astype(F32)
    axis_dim = MLA_ROPE // 2
    inv_freq = 1.0 / (ROPE_THETA ** (jnp.arange(0, axis_dim, 2, dtype=F32) / axis_dim))
    ar, ac = row[:, None] * inv_freq, col[:, None] * inv_freq
    ang = jnp.concatenate([ar, ar, ac, ac], axis=-1)
    cos = jnp.concatenate([jnp.ones((lc, MLA_ROPE), F32), jnp.cos(ang)], axis=0)
    sin = jnp.concatenate([jnp.zeros((lc, MLA_ROPE), F32), jnp.sin(ang)], axis=0)
    padc = jnp.zeros((lc + l_lat, 128 - MLA_ROPE), F32)
    pm = np.zeros((128, 128), np.float32)
    q4 = MLA_ROPE // 4
    for base in (0, 2 * q4):
        for i in range(q4):
            pm[base + q4 + i, base + i] = -1.0
            pm[base + i, base + q4 + i] = 1.0
    return jnp.concatenate([cos, padc], axis=1), jnp.concatenate([sin, padc], axis=1), jnp.asarray(pm, BF)


def kernel(x, c, ctx, c_ctx, w_mod, b_mod, g_mix_pre, g_mix_post, g_ffn_pre, g_ffn_post, w_in, mla_g_q, mla_g_kv, mla_w_uq, mla_w_ukv, hg_lb_logits, hg_g_norm, conv_w, conv_b, conv_ln_g, conv_ln_b, pool_w, pool_scale, w_branch, w_out, ffn_w1, ffn_w3, ffn_w2, moe_router, moe_w1, moe_w3, moe_w2):
    n_batch, l_lat, d = x.shape
    lc = ctx.shape[1]
    depth = w_in.shape[0]
    p_rows = lc + l_lat
    t = n_batch * p_rows
    dims = (p_rows, lc, n_batch)
    assert n_batch + 1 <= 8 and lc % HG_CHUNK == 0 and l_lat % HG_CHUNK == 0 and l_lat % GRID_W == 0
    n_heads = mla_w_uq.shape[2] // MLA_QK

    xs = jnp.concatenate([ctx, x], axis=1).reshape(t, d)
    cvec = jnp.zeros((8, d), F32).at[:n_batch].set(c).at[n_batch].set(c_ctx)
    mods = _modvecs(cvec, w_mod, b_mod)

    lb = jnp.cumsum(jax.nn.softmax(hg_lb_logits.astype(F32), axis=1), axis=1)
    lb = lb - lb[:, :1]
    cos, sin, pm = _rope_tables(l_lat, lc)

    w_in_bf = _permute_w_in(w_in, d)
    w_out_bf = w_out.astype(BF)
    ffn_w2_bf = ffn_w2.astype(BF)

    h = _modulate(xs, g_mix_pre[0], mods, 0, 0, 1, dims)
    for l in range(depth):
        last = l == depth - 1
        z = _matmul(h, w_in_bf, l, BF)
        q3, k3, v3 = _mla_proj(z, mla_g_q[l], mla_g_kv[l], _pad_heads(mla_w_uq[l], n_heads, MLA_QK),
                               mla_w_ukv[l], cos, sin, pm, dims)
        y_mla = _attention(q3, k3, v3, dims)
        y_hg = _hgrn_readout(*_hgrn(z, lb[:, l], dims), z, hg_g_norm[l])
        y_conv = _conv_mixer(z, conv_w[l], conv_b[l], conv_ln_g[l], conv_ln_b[l], dims)
        y_pool = _pool_mixer(z, pool_w[l], pool_scale[l], dims)
        m = _merge([y_mla, y_hg, y_conv, y_pool], z, w_branch, l)
        j = l // 2
        nxt_ffn = (g_ffn_pre[l], l, 3, 4)
        nxt_mix = None if last else (g_mix_pre[l + 1], l + 1, 0, 1)
        if l % 2 == 0:
            xs, h2 = _out_proj(m, w_out_bf, l, xs, g_mix_post[l], mods, l, 2, nxt_ffn, dims)
            act = _swiglu_up(h2, ffn_w1, ffn_w3, j)
            res = _out_proj(act, ffn_w2_bf, j, xs, g_ffn_post[l], mods, l, 5, nxt_mix, dims, name="ffn_down")
        else:
            n_exp = moe_router.shape[2]
            wr = jnp.pad(moe_router[j], ((0, 0), (0, 128 - n_exp)))
            wr_hi = wr.astype(BF)
            wr2 = jnp.stack([wr_hi, (wr - wr_hi.astype(F32)).astype(BF)])
            xs, h2, route = _out_proj(m, w_out_bf, l, xs, g_mix_post[l], mods, l, 2, nxt_ffn, dims,
                                      router=(wr2, n_exp))
            res = _moe_ffn(h2, route, moe_w1, moe_w3, moe_w2, j, xs, g_ffn_post[l], mods, l, nxt_mix, dims)
        if last:
            (xs,) = res
        else:
            xs, h = res
    return xs.reshape(n_batch, p_rows, d)[:, lc:, :]
```

```python
import functools

import numpy as np
import jax
import jax.numpy as jnp
from jax import lax
from jax.experimental import pallas as pl
from jax.experimental.pallas import tpu as pltpu

F32 = jnp.float32
BF = jnp.bfloat16

EPS = 1e-6
GRID_W = 64
ROPE_THETA = 10000.0
N_BRANCH = 4
MLA_NOPE, MLA_ROPE, MLA_V = 128, 64, 128
MLA_QK = MLA_NOPE + MLA_ROPE
HEAD_PAD = 256
HG_DK = 128
CONV_WIDTH = 31
POOL_WINDOWS = (2, 4, 8, 16)
POOL_GROUP = 128
SUBLANES = 8
HALO = 16
HG_CHUNK = 64
HG_EXP_CLAMP = 80.0
NEG_BIG = -1e30
ATTN_Q_TILE = 512
MOE_ROW_TILE = 256
GATHER_UNROLL = 8
EPILOGUE_SPLIT = 2
MOE_LOAD_SLOTS = 3
MOE_ROW_ALIGN = 16
MOE_GATHER_ROWS = 1024

VMEM_LIMIT_V7X = 56 * 1024 * 1024

Z_CONV, Z_HQ, Z_HI, Z_HFF, Z_HFB, Z_HG, Z_POOL, Z_MLA, Z_GATE = 0, 1024, 1536, 2048, 2560, 3072, 3584, 4096, 5120
MLA_ZKV_OFF, MLA_ZR_OFF = 384, 640


def _cparams(sem, vmem=VMEM_LIMIT_V7X):
    return pltpu.CompilerParams(dimension_semantics=sem, vmem_limit_bytes=vmem)


def _divisor_tile(n, target, mult=16):
    best = None
    for t in range(mult, min(n, target) + 1, mult):
        if n % t == 0:
            best = t
    assert best is not None, (n, target, mult)
    return best


def _pick_row(m, r):
    rows = lax.broadcasted_iota(jnp.int32, m.shape, 0)
    return jnp.sum(jnp.where(rows == r, m, 0.0), axis=0, keepdims=True)


def _row_vec(m_ref, b, n_batch, ctx_mask):
    m = m_ref[0, 0]
    return jnp.where(ctx_mask, _pick_row(m, n_batch), _pick_row(m, b))


def _tile_rows(i, tm, p_rows, lc):
    per_batch = p_rows // tm
    b = i // per_batch
    pos = (i % per_batch) * tm + lax.broadcasted_iota(jnp.int32, (tm, 1), 0)
    return b, pos < lc


def _rms(x, g):
    return x * lax.rsqrt(jnp.mean(x * x, axis=-1, keepdims=True) + EPS) * g


def _sigmoid(x):
    return 1.0 / (1.0 + jnp.exp(-x))


def _silu(x):
    return x * _sigmoid(x)


def _modvec_kernel(c_ref, w_ref, b_ref, o_ref):
    s = _silu(c_ref[...])
    s_hi = s.astype(BF)
    s_lo = (s - s_hi.astype(F32)).astype(BF)
    w = w_ref[0].astype(BF)
    acc = jnp.dot(s_hi, w, preferred_element_type=F32) + jnp.dot(s_lo, w, preferred_element_type=F32)
    o_ref[0, 0] = acc + b_ref[0, 0]


def _modvecs(cvec, w_mod, b_mod):
    depth, d, n6 = w_mod.shape
    nj = n6 // d
    tn = _divisor_tile(d, 1024, 128)
    npj = d // tn
    return pl.pallas_call(
        _modvec_kernel,
        grid=(depth, nj, npj),
        in_specs=[
            pl.BlockSpec((8, d), lambda l, j, n: (0, 0)),
            pl.BlockSpec((1, d, tn), lambda l, j, n: (l, 0, j * npj + n)),
            pl.BlockSpec((1, 1, 1, tn), lambda l, j, n: (l, j, 0, n)),
        ],
        out_specs=pl.BlockSpec((1, 1, 8, tn), lambda l, j, n: (l, j, 0, n)),
        out_shape=jax.ShapeDtypeStruct((depth, nj, 8, d), F32),
        compiler_params=_cparams(("parallel", "parallel", "parallel")),
        name="modvec",
    )(cvec, w_mod, b_mod.reshape(depth, nj, 1, d))


def _modulate_kernel(x_ref, g_ref, sh_ref, sc_ref, h_ref, *, tm, p_rows, lc, n_batch):
    b, cm = _tile_rows(pl.program_id(0), tm, p_rows, lc)
    shift = _row_vec(sh_ref, b, n_batch, cm)
    scale = _row_vec(sc_ref, b, n_batch, cm)
    h_ref[...] = (_rms(x_ref[...], g_ref[...]) * (1.0 + scale) + shift).astype(h_ref.dtype)


def _mod_spec(l, j, d):
    return pl.BlockSpec((1, 1, 8, d), lambda *_: (l, j, 0, 0))


def _modulate(xs, g, mods, l, j_shift, j_scale, dims):
    t, d = xs.shape
    p_rows, lc, n_batch = dims
    tm = _divisor_tile(p_rows, 544)
    return pl.pallas_call(
        functools.partial(_modulate_kernel, tm=tm, p_rows=p_rows, lc=lc, n_batch=n_batch),
        grid=(t // tm,),
        in_specs=[
            pl.BlockSpec((tm, d), lambda i: (i, 0)),
            pl.BlockSpec((1, d), lambda i: (0, 0)),
            _mod_spec(l, j_shift, d),
            _mod_spec(l, j_scale, d),
        ],
        out_specs=pl.BlockSpec((tm, d), lambda i: (i, 0)),
        out_shape=jax.ShapeDtypeStruct((t, d), BF),
        compiler_params=_cparams(("parallel",)),
        name="modulate",
    )(xs, g.reshape(1, d), mods, mods)


def _mm_kernel(x_ref, w_ref, o_ref):
    o_ref[...] = jnp.dot(x_ref[...], w_ref[0], preferred_element_type=F32).astype(o_ref.dtype)


def _matmul(x, w, l, out_dtype, tm_target=2176, tn_target=1024):
    t, k = x.shape
    n = w.shape[2]
    tm = _divisor_tile(t, tm_target)
    tn = _divisor_tile(n, tn_target, 128)
    return pl.pallas_call(
        _mm_kernel,
        grid=(n // tn, t // tm),
        in_specs=[
            pl.BlockSpec((tm, k), lambda j, i: (i, 0)),
            pl.BlockSpec((1, k, tn), lambda j, i: (l, 0, j)),
        ],
        out_specs=pl.BlockSpec((tm, tn), lambda j, i: (i, j)),
        out_shape=jax.ShapeDtypeStruct((t, n), out_dtype),
        compiler_params=_cparams(("parallel", "arbitrary")),
        name="in_proj",
    )(x, w)


def _rope128(t, cos, sin, pm):
    partner = jnp.dot(t.astype(BF), pm, preferred_element_type=F32)
    return t * cos + partner * sin


def _mla_proj_kernel(z_ref, gq_ref, gkv_ref, wq_ref, wkv_ref, cos_ref, sin_ref, pm_ref,
                     q_ref, k_ref, v_ref, wq_s, wkv_s, *, n_heads, q_rank, kv_rank, scale):
    @pl.when(pl.program_id(0) == 0)
    def _():
        wq_s[...] = wq_ref[...].astype(BF)
        wkv_s[...] = wkv_ref[...].astype(BF)

    z = z_ref[...]
    zq = z[:, :q_rank].astype(F32)
    zkv = z[:, MLA_ZKV_OFF:MLA_ZKV_OFF + kv_rank].astype(F32)
    zr = z[:, MLA_ZR_OFF:MLA_ZR_OFF + 128].astype(F32)
    cos, sin, pm = cos_ref[...], sin_ref[...], pm_ref[...]
    qn = _rms(zq, gq_ref[...]).astype(BF)
    kvn = _rms(zkv, gkv_ref[...]).astype(BF)
    k_rope = _rope128(zr, cos, sin, pm)[:, :MLA_ROPE]
    for h in range(n_heads):
        qh = jnp.dot(qn, wq_s[:, h * HEAD_PAD:(h + 1) * HEAD_PAD], preferred_element_type=F32)
        q_rope = _rope128(qh[:, MLA_NOPE:], cos, sin, pm)[:, :MLA_ROPE]
        q_ref[h] = (jnp.concatenate([qh[:, :MLA_NOPE], q_rope], axis=-1) * scale).astype(q_ref.dtype)
        kvh = jnp.dot(kvn, wkv_s[:, h * HEAD_PAD:(h + 1) * HEAD_PAD], preferred_element_type=F32)
        k_ref[h] = jnp.concatenate([kvh[:, :MLA_NOPE], k_rope], axis=-1).astype(k_ref.dtype)
        v_ref[h] = jnp.concatenate([kvh[:, MLA_NOPE:], jnp.ones_like(kvh[:, MLA_NOPE:])], axis=-1).astype(v_ref.dtype)


def _mla_proj(z, g_q, g_kv, wq_pad, wkv, cos, sin, pm, dims):
    t = z.shape[0]
    p_rows = dims[0]
    q_rank, kv_rank = g_q.shape[0], g_kv.shape[0]
    n_heads = wkv.shape[1] // HEAD_PAD
    tm = _divisor_tile(p_rows, 544)
    npb = p_rows // tm
    scale = float(MLA_QK) ** -0.5 * float(np.log2(np.e))
    const = lambda i: (0, 0)
    return pl.pallas_call(
        functools.partial(_mla_proj_kernel, n_heads=n_heads, q_rank=q_rank, kv_rank=kv_rank, scale=scale),
        grid=(t // tm,),
        in_specs=[
            pl.BlockSpec((tm, 1024), lambda i: (i, Z_MLA // 1024)),
            pl.BlockSpec((1, q_rank), const),
            pl.BlockSpec((1, kv_rank), const),
            pl.BlockSpec(wq_pad.shape, const),
            pl.BlockSpec(wkv.shape, const),
            pl.BlockSpec((tm, 128), lambda i: (i % npb, 0)),
            pl.BlockSpec((tm, 128), lambda i: (i % npb, 0)),
            pl.BlockSpec((128, 128), const),
        ],
        out_specs=[
            pl.BlockSpec((n_heads, tm, MLA_QK), lambda i: (0, i, 0)),
            pl.BlockSpec((n_heads, tm, MLA_QK), lambda i: (0, i, 0)),
            pl.BlockSpec((n_heads, tm, 2 * MLA_V), lambda i: (0, i, 0)),
        ],
        out_shape=[
            jax.ShapeDtypeStruct((n_heads, t, MLA_QK), BF),
            jax.ShapeDtypeStruct((n_heads, t, MLA_QK), BF),
            jax.ShapeDtypeStruct((n_heads, t, 2 * MLA_V), BF),
        ],
        scratch_shapes=[pltpu.VMEM(wq_pad.shape, BF), pltpu.VMEM(wkv.shape, BF)],
        compiler_params=_cparams(("arbitrary",)),
        name="mla_proj",
    )(z, g_q.reshape(1, -1), g_kv.reshape(1, -1), wq_pad, wkv, cos, sin, pm)


def _attn_kernel(q_ref, k_ref, v_ref, o_ref, s0_ref, s1_ref, p0_ref, p1_ref, *, tq, lc):
    p_rows = k_ref.shape[1]
    n_tiles = (p_rows - lc) // tq
    nt = (((1,), (1,)), ((), ()))

    def tile_start(t):
        return pl.multiple_of(lc + t * tq, lc)

    def scores(t, s_ref):
        s_ref[...] = lax.dot_general(q_ref[0, pl.ds(tile_start(t), tq), :], k_ref[0], nt, preferred_element_type=F32)

    def softmax(s_ref, p_ref):
        s = s_ref[...]
        p_ref[...] = jnp.exp2((s - jnp.max(s, axis=-1, keepdims=True)).astype(BF))

    def emit(p, v, q0, n_q):
        o = jnp.dot(p, v, preferred_element_type=F32)
        o_ref[pl.ds(q0, n_q), :] = (o[:, :MLA_V] / o[:, MLA_V:MLA_V + 1]).astype(o_ref.dtype)

    def values(t, p_ref):
        emit(p_ref[...], v_ref[0], tile_start(t), tq)

    s_ctx = lax.dot_general(q_ref[0, :lc, :], k_ref[0, :lc, :], nt, preferred_element_type=F32)
    emit(jnp.exp2((s_ctx - jnp.max(s_ctx, axis=-1, keepdims=True)).astype(BF)), v_ref[0, :lc, :], 0, lc)

    scores(0, s0_ref)
    scores(1, s1_ref)
    softmax(s0_ref, p0_ref)

    def body(kk, carry):
        t = 2 * kk
        scores(t, s0_ref)
        softmax(s1_ref, p1_ref)
        values(t - 2, p0_ref)
        scores(t + 1, s1_ref)
        softmax(s0_ref, p0_ref)
        values(t - 1, p1_ref)
        return carry

    lax.fori_loop(1, n_tiles // 2, body, 0)
    softmax(s1_ref, p1_ref)
    values(n_tiles - 2, p0_ref)
    values(n_tiles - 1, p1_ref)


def _attention(q3, k3, v3, dims):
    n_heads, t, _ = q3.shape
    p_rows, lc, n_batch = dims
    l_lat = p_rows - lc
    tq = max(c for c in range(lc, ATTN_Q_TILE + 1, lc) if l_lat % (2 * c) == 0)
    return pl.pallas_call(
        functools.partial(_attn_kernel, tq=tq, lc=lc),
        grid=(n_batch, n_heads),
        in_specs=[
            pl.BlockSpec((1, p_rows, MLA_QK), lambda b, h: (h, b, 0)),
            pl.BlockSpec((1, p_rows, MLA_QK), lambda b, h: (h, b, 0)),
            pl.BlockSpec((1, p_rows, 2 * MLA_V), lambda b, h: (h, b, 0)),
        ],
        out_specs=pl.BlockSpec((p_rows, MLA_V), lambda b, h: (b, h)),
        out_shape=jax.ShapeDtypeStruct((t, n_heads * MLA_V), BF),
        scratch_shapes=[pltpu.VMEM((tq, p_rows), F32), pltpu.VMEM((tq, p_rows), F32),
                        pltpu.VMEM((tq, p_rows), BF), pltpu.VMEM((tq, p_rows), BF)],
        compiler_params=_cparams(("parallel", "parallel")),
        name="mla_attention",
    )(q3, k3, v3)


def _hgrn_chunk(zq, zi, zf, lb, tri, o_ref, st_ref, stream, bi, n_heads):
    feeds = tri > 0.5
    f = lb + (1.0 - lb) * _sigmoid(zf.astype(F32))
    lf = jnp.log(f)
    lf1 = lf.astype(BF)
    r1 = lf - lf1.astype(F32)
    lf2 = r1.astype(BF)
    lf3 = (r1 - lf2.astype(F32)).astype(BF)
    b_all = (jnp.dot(tri, lf1, preferred_element_type=F32) + jnp.dot(tri, lf2, preferred_element_type=F32)
             + jnp.dot(tri, lf3, preferred_element_type=F32))
    tot_all = jnp.sum(lf, axis=0, keepdims=True)
    q_all = _silu(zq.astype(F32))
    nt = (((1,), (1,)), ((), ()))
    for h in range(n_heads):
        sl = slice(h * HG_DK, (h + 1) * HG_DK)
        b, tot, q, k = b_all[:, sl], tot_all[:, sl], q_all[:, sl], 1.0 - f[:, sl]
        v = zi[:, sl]
        ref = 0.5 * tot
        q_loc = (q * jnp.exp(jnp.minimum(b - ref, HG_EXP_CLAMP))).astype(BF)
        k_loc = (k * jnp.exp(jnp.minimum(ref - b, HG_EXP_CLAMP))).astype(BF)
        a = lax.dot_general(q_loc, k_loc, nt, preferred_element_type=F32)
        a = jnp.where(feeds, a, 0.0).astype(BF)
        state_t = st_ref[stream, h]
        q_in = (q * jnp.exp(b)).astype(BF)
        o = jnp.dot(a, v, preferred_element_type=F32)
        o = o + lax.dot_general(q_in, state_t.astype(BF), nt, preferred_element_type=F32)
        o_ref[bi, :, sl] = o
        k_out = (k * jnp.exp(tot - b)).astype(BF)
        upd = lax.dot_general(v, k_out, (((0,), (0,)), ((), ())), preferred_element_type=F32)
        st_ref[stream, h] = state_t * jnp.exp(tot) + upd


def _hgrn_kernel(fq_ref, fi_ref, ff_ref, bq_ref, bi_ref, bf_ref, lb_ref, tri_ref, of_ref, ob_ref, st_ref,
                 *, n_heads, n_batch):
    @pl.when(pl.program_id(0) == 0)
    def _():
        st_ref[...] = jnp.zeros_like(st_ref)

    for rev, (q_ref, i_ref, f_ref, o_ref) in enumerate(((fq_ref, fi_ref, ff_ref, of_ref),
                                                       (bq_ref, bi_ref, bf_ref, ob_ref))):
        for bi in range(n_batch):
            _hgrn_chunk(q_ref[bi], i_ref[bi], f_ref[bi], lb_ref[rev], tri_ref[rev], o_ref, st_ref,
                        rev * n_batch + bi, bi, n_heads)


def _hgrn(z, lb2, dims):
    t, nz = z.shape
    p_rows, lc, n_batch = dims
    c = HG_CHUNK
    w = lb2.shape[-1]
    n_heads = w // HG_DK
    nch, nctx = p_rows // c, lc // c
    idx = np.arange(c)
    tri = jnp.asarray(np.stack([idx[None, :] <= idx[:, None], idx[None, :] >= idx[:, None]]), BF)
    z3 = z.reshape(n_batch, p_rows, nz)

    def back(j):
        return jnp.where(j < nctx, nctx - 1 - j, nch - 1 - (j - nctx))

    def spec(col, rev):
        if rev:
            return pl.BlockSpec((n_batch, c, w), lambda j: (0, back(j), col // w))
        return pl.BlockSpec((n_batch, c, w), lambda j: (0, j, col // w))

    out_f, out_b = pl.pallas_call(
        functools.partial(_hgrn_kernel, n_heads=n_heads, n_batch=n_batch),
        grid=(nch,),
        in_specs=[
            spec(Z_HQ, False), spec(Z_HI, False), spec(Z_HFF, False),
            spec(Z_HQ, True), spec(Z_HI, True), spec(Z_HFB, True),
            pl.BlockSpec((2, 1, w), lambda j: (0, 0, 0)),
            pl.BlockSpec((2, c, c), lambda j: (0, 0, 0)),
        ],
        out_specs=[pl.BlockSpec((n_batch, c, w), lambda j: (0, j, 0)),
                   pl.BlockSpec((n_batch, c, w), lambda j: (0, back(j), 0))],
        out_shape=[jax.ShapeDtypeStruct((n_batch, p_rows, w), F32)] * 2,
        scratch_shapes=[pltpu.VMEM((2 * n_batch, n_heads, HG_DK, HG_DK), F32)],
        compiler_params=_cparams(("arbitrary",)),
        name="hgrn_scan",
    )(z3, z3, z3, z3, z3, z3, lb2.reshape(2, 1, w), tri)
    return out_f.reshape(t, w), out_b.reshape(t, w)


def _hgrn_readout_kernel(of_ref, ob_ref, zg_ref, g_ref, y_ref, *, n_heads):
    o = of_ref[...] + ob_ref[...]
    g = g_ref[...]
    gate = _silu(zg_ref[...].astype(F32))
    for h in range(n_heads):
        sl = slice(h * HG_DK, (h + 1) * HG_DK)
        y_ref[:, sl] = (_rms(o[:, sl], g[:, sl]) * gate[:, sl]).astype(y_ref.dtype)


def _hgrn_readout(o_fwd, o_bwd, z, g_norm):
    t, w = o_fwd.shape
    tm = _divisor_tile(t, 1088)
    return pl.pallas_call(
        functools.partial(_hgrn_readout_kernel, n_heads=w // HG_DK),
        grid=(t // tm,),
        in_specs=[
            pl.BlockSpec((tm, w), lambda i: (i, 0)),
            pl.BlockSpec((tm, w), lambda i: (i, 0)),
            pl.BlockSpec((tm, w), lambda i: (i, Z_HG // w)),
            pl.BlockSpec((1, w), lambda i: (0, 0)),
        ],
        out_specs=pl.BlockSpec((tm, w), lambda i: (i, 0)),
        out_shape=jax.ShapeDtypeStruct((t, w), BF),
        compiler_params=_cparams(("parallel",)),
        name="hgrn_readout",
    )(o_fwd, o_bwd, z, g_norm.reshape(1, w))


def _segment_tile(i, tc, p_rows, lc):
    per_batch = p_rows // tc
    j = i % per_batch
    nctx = lc // tc
    first = jnp.logical_or(j == 0, j == nctx)
    last = jnp.logical_or(j == nctx - 1, j == per_batch - 1)
    in_ctx = j < nctx
    seg_pos0 = jnp.where(in_ctx, j, j - nctx) * tc
    seg_len = jnp.where(in_ctx, lc, p_rows - lc)
    return first, last, seg_pos0, seg_len


def _halo_specs(tc, width, col_block, n_rows):
    per = tc // HALO
    last_blk = n_rows // HALO - 1
    return [
        pl.BlockSpec((HALO, width), lambda i: (jnp.maximum(i * per - 1, 0), col_block)),
        pl.BlockSpec((tc, width), lambda i: (i, col_block)),
        pl.BlockSpec((HALO, width), lambda i: (jnp.minimum((i + 1) * per, last_blk), col_block)),
    ]


def _conv_kernel(prev_ref, cur_ref, next_ref, w_ref, b_ref, lg_ref, lb_ref, y_ref, buf, shifted, *, tc, p_rows, lc, ch):
    first, last, _, _ = _segment_tile(pl.program_id(0), tc, p_rows, lc)

    def glu(zz):
        zz = zz.astype(F32)
        return zz[:, :ch] * _sigmoid(zz[:, ch:])

    buf[0:HALO, :] = jnp.where(first, 0.0, glu(prev_ref[...]))
    buf[HALO:HALO + tc, :] = glu(cur_ref[...])
    buf[HALO + tc:, :] = jnp.where(last, 0.0, glu(next_ref[...]))
    span = shifted.shape[1]
    for s in range(1, SUBLANES):
        shifted[s - 1] = buf[s:s + span, :]
    w = w_ref[...]
    acc = jnp.zeros((tc, ch), F32) + b_ref[...]
    for kk in range(CONV_WIDTH):
        blk, s = divmod(HALO - CONV_WIDTH // 2 + kk, SUBLANES)
        r0 = blk * SUBLANES
        win = buf[r0:r0 + tc, :] if s == 0 else shifted[s - 1, r0:r0 + tc, :]
        acc = acc + win * w[kk:kk + 1, :]
    mu = jnp.mean(acc, axis=-1, keepdims=True)
    cen = acc - mu
    var = jnp.mean(cen * cen, axis=-1, keepdims=True)
    y = cen * lax.rsqrt(var + EPS) * lg_ref[...] + lb_ref[...]
    y_ref[...] = _silu(y).astype(y_ref.dtype)


def _conv_mixer(z, w_dw, b_dw, ln_g, ln_b, dims):
    t = z.shape[0]
    p_rows, lc, _ = dims
    ch = w_dw.shape[1]
    tc = _divisor_tile(lc, 256)
    row = lambda a: a.reshape(1, ch)
    const = lambda i: (0, 0)
    return pl.pallas_call(
        functools.partial(_conv_kernel, tc=tc, p_rows=p_rows, lc=lc, ch=ch),
        grid=(t // tc,),
        in_specs=_halo_specs(tc, 2 * ch, Z_CONV // (2 * ch), t) + [
            pl.BlockSpec((CONV_WIDTH, ch), const),
            pl.BlockSpec((1, ch), const), pl.BlockSpec((1, ch), const), pl.BlockSpec((1, ch), const),
        ],
        out_specs=pl.BlockSpec((tc, ch), lambda i: (i, 0)),
        out_shape=jax.ShapeDtypeStruct((t, ch), BF),
        scratch_shapes=[pltpu.VMEM((tc + 2 * HALO, ch), F32),
                        pltpu.VMEM((SUBLANES - 1, tc + 2 * HALO - SUBLANES, ch), F32)],
        compiler_params=_cparams(("parallel",)),
        name="conv_mixer",
    )(z, z, z, w_dw, row(b_dw), row(ln_g), row(ln_b))


def _pool_kernel(prev_ref, cur_ref, next_ref, w_ref, s_ref, y_ref, buf, *, tc, p_rows, lc):
    first, last, seg_pos0, seg_len = _segment_tile(pl.program_id(0), tc, p_rows, lc)
    buf[0:HALO, :] = jnp.where(first, 0.0, prev_ref[...].astype(F32))
    buf[HALO:HALO + tc, :] = cur_ref[...].astype(F32)
    buf[HALO + tc:, :] = jnp.where(last, 0.0, next_ref[...].astype(F32))
    pos = seg_pos0 + lax.broadcasted_iota(jnp.int32, (tc, 1), 0)
    scale = s_ref[...]
    for gi, win in enumerate(POOL_WINDOWS):
        sl = slice(gi * POOL_GROUP, (gi + 1) * POOL_GROUP)
        half = win // 2
        tot = jnp.zeros((tc, POOL_GROUP), F32)
        for u in range(-half, half):
            tot = tot + buf[HALO + u:HALO + u + tc, sl]
        cnt = (jnp.minimum(pos + half, seg_len) - jnp.maximum(pos - half, 0)).astype(F32)
        pooled = tot / cnt - buf[HALO:HALO + tc, sl]
        y = jnp.dot(pooled.astype(BF), w_ref[gi].astype(BF), preferred_element_type=F32)
        y_ref[:, sl] = (y * scale[:, sl]).astype(y_ref.dtype)


def _pool_mixer(z, w_pool, scale, dims):
    t = z.shape[0]
    p_rows, lc, _ = dims
    ch = scale.shape[0]
    tc = _divisor_tile(lc, 256)
    return pl.pallas_call(
        functools.partial(_pool_kernel, tc=tc, p_rows=p_rows, lc=lc),
        grid=(t // tc,),
        in_specs=_halo_specs(tc, ch, Z_POOL // ch, t) + [
            pl.BlockSpec(w_pool.shape, lambda i: (0, 0, 0)),
            pl.BlockSpec((1, ch), lambda i: (0, 0)),
        ],
        out_specs=pl.BlockSpec((tc, ch), lambda i: (i, 0)),
        out_shape=jax.ShapeDtypeStruct((t, ch), BF),
        scratch_shapes=[pltpu.VMEM((tc + 2 * HALO, ch), F32)],
        compiler_params=_cparams(("parallel",)),
        name="pool_mixer",
    )(z, z, z, w_pool, scale.reshape(1, ch))


def _merge_kernel(y0, y1, y2, y3, g0, g1, g2, g3, w_ref, o_ref, w_s):
    @pl.when(pl.program_id(1) == 0)
    def _():
        w_s[...] = w_ref[0].astype(BF)

    acc = None
    for n, (y_ref, g_ref) in enumerate(((y0, g0), (y1, g1), (y2, g2), (y3, g3))):
        proj = jnp.dot(y_ref[...], w_s[n], preferred_element_type=F32)
        gate = 0.5 * jnp.tanh(0.5 * g_ref[...].astype(F32)) + 0.5
        term = gate * proj
        acc = term if acc is None else acc + term
    o_ref[...] = acc.astype(o_ref.dtype)


def _merge(ys, z, w_branch, l):
    t, bw = ys[0].shape
    _, nb, _, d = w_branch.shape
    tm = _divisor_tile(t, 544)
    tn = _divisor_tile(d, 1024, 128)
    npn = d // tn

    def gate_spec(n):
        return pl.BlockSpec((tm, tn), lambda j, i: (i, Z_GATE // tn + n * npn + j))

    return pl.pallas_call(
        _merge_kernel,
        grid=(npn, t // tm),
        in_specs=[pl.BlockSpec((tm, bw), lambda j, i: (i, 0))] * nb + [gate_spec(n) for n in range(nb)] + [
            pl.BlockSpec((1, nb, bw, tn), lambda j, i: (l, 0, 0, j))],
        out_specs=pl.BlockSpec((tm, tn), lambda j, i: (i, j)),
        out_shape=jax.ShapeDtypeStruct((t, d), BF),
        scratch_shapes=[pltpu.VMEM((nb, bw, tn), BF)],
        compiler_params=_cparams(("parallel", "arbitrary")),
        name="branch_merge",
    )(*ys, z, z, z, z, w_branch)


def _residual_epilogue(y, x_ref, gpost_ref, gate_ref, gnext_ref, sh_ref, sc_ref, xo_ref, ho_ref, b, n_batch, cm,
                       rows=slice(None)):
    cm = cm[rows]
    gate = _row_vec(gate_ref, b, n_batch, cm)
    x_new = x_ref[rows, :] + gate * _rms(y, gpost_ref[...])
    xo_ref[rows, :] = x_new
    if ho_ref is None:
        return None
    shift = _row_vec(sh_ref, b, n_batch, cm)
    scale = _row_vec(sc_ref, b, n_batch, cm)
    h = _rms(x_new, gnext_ref[...]) * (1.0 + scale) + shift
    ho_ref[rows, :] = h.astype(ho_ref.dtype)
    return h


def _route_top2(logits, n_experts):
    lane = lax.broadcasted_iota(jnp.int32, logits.shape, 1)
    lg = jnp.where(lane < n_experts, logits, NEG_BIG)
    m1 = jnp.max(lg, axis=-1, keepdims=True)
    i1 = jnp.min(jnp.where(lg == m1, lane, 1 << 20), axis=-1, keepdims=True)
    lg2 = jnp.where(lane == i1, NEG_BIG, lg)
    m2 = jnp.max(lg2, axis=-1, keepdims=True)
    i2 = jnp.min(jnp.where(lg2 == m2, lane, 1 << 20), axis=-1, keepdims=True)
    e2 = jnp.exp(m2 - m1)
    w1 = 1.0 / (1.0 + e2)
    w2 = e2 / (1.0 + e2)
    out = jnp.where(lane == 0, i1.astype(F32), 0.0)
    out = jnp.where(lane == 1, i2.astype(F32), out)
    out = jnp.where(lane == 2, w1, out)
    return jnp.where(lane == 3, w2, out)


def _router_logits(h, wr_ref):
    h_hi = h.astype(BF)
    h_lo = (h - h_hi.astype(F32)).astype(BF)
    return (jnp.dot(h_hi, wr_ref[0], preferred_element_type=F32) + jnp.dot(h_hi, wr_ref[1], preferred_element_type=F32)
            + jnp.dot(h_lo, wr_ref[0], preferred_element_type=F32))


def _out_proj_kernel(*refs, tm, p_rows, lc, n_batch, emit_h, n_experts):
    m_ref, w_ref, x_ref, gpost_ref, gate_ref = refs[:5]
    pos = 5
    gnext_ref = sh_ref = sc_ref = wr_ref = None
    if emit_h:
        gnext_ref, sh_ref, sc_ref = refs[pos:pos + 3]
        pos += 3
    if n_experts:
        wr_ref = refs[pos]
        pos += 1
    xo_ref = refs[pos]
    pos += 1
    ho_ref = route_ref = None
    if emit_h:
        ho_ref = refs[pos]
        pos += 1
    if n_experts:
        route_ref = refs[pos]
        pos += 1
    b, cm = _tile_rows(pl.program_id(0), tm, p_rows, lc)
    n_split = EPILOGUE_SPLIT if tm % (EPILOGUE_SPLIT * 16) == 0 else 1
    rs = tm // n_split
    groups = [slice(s * rs, (s + 1) * rs) for s in range(n_split)]
    ys = [jnp.dot(m_ref[rows, :], w_ref[0], preferred_element_type=F32) for rows in groups]
    for rows, y in zip(groups, ys):
        h = _residual_epilogue(y, x_ref, gpost_ref, gate_ref, gnext_ref, sh_ref, sc_ref, xo_ref, ho_ref, b, n_batch,
                               cm, rows)
        if n_experts:
            route_ref[rows, :] = _route_top2(_router_logits(h, wr_ref), n_experts)


def _out_proj(m, w_stack, lw, xs, g_post, mods, l, j_gate, nxt, dims, router=None, name="out_proj"):
    t, d = xs.shape
    k = m.shape[1]
    p_rows, lc, n_batch = dims
    tm = _divisor_tile(p_rows, 544 if k <= d else 272)
    const = lambda i: (0, 0)
    rowtile = pl.BlockSpec((tm, d), lambda i: (i, 0))
    args = [m, w_stack, xs, g_post.reshape(1, d), mods]
    in_specs = [pl.BlockSpec((tm, k), lambda i: (i, 0)),
                pl.BlockSpec((1, k, d), lambda i: (lw, 0, 0), pipeline_mode=pl.Buffered(1)),
                rowtile, pl.BlockSpec((1, d), const), _mod_spec(l, j_gate, d)]
    out_shape = [jax.ShapeDtypeStruct((t, d), F32)]
    out_specs = [rowtile]
    n_experts = 0
    if nxt is not None:
        g_next, l_next, j_shift, j_scale = nxt
        args += [g_next.reshape(1, d), mods, mods]
        in_specs += [pl.BlockSpec((1, d), const), _mod_spec(l_next, j_shift, d), _mod_spec(l_next, j_scale, d)]
        out_shape.append(jax.ShapeDtypeStruct((t, d), BF if router is None else F32))
        out_specs.append(rowtile)
    if router is not None:
        wr, n_experts = router
        args.append(wr)
        in_specs.append(pl.BlockSpec(wr.shape, lambda i: (0, 0, 0)))
        out_shape.append(jax.ShapeDtypeStruct((t, 128), F32))
        out_specs.append(pl.BlockSpec((tm, 128), lambda i: (i, 0)))
    return pl.pallas_call(
        functools.partial(_out_proj_kernel, tm=tm, p_rows=p_rows, lc=lc, n_batch=n_batch,
                          emit_h=nxt is not None, n_experts=n_experts),
        grid=(t // tm,),
        in_specs=in_specs,
        out_specs=out_specs,
        out_shape=out_shape,
        compiler_params=_cparams(("parallel",)),
        name=name,
    )(*args)


def _swiglu_kernel(x_ref, w1_ref, w3_ref, o_ref, w1_s, w3_s):
    @pl.when(pl.program_id(1) == 0)
    def _():
        w1_s[...] = w1_ref[0].astype(BF)
        w3_s[...] = w3_ref[0].astype(BF)

    x = x_ref[...]
    a = jnp.dot(x, w1_s[...], preferred_element_type=F32)
    g = jnp.dot(x, w3_s[...], preferred_element_type=F32)
    o_ref[...] = (_silu(a) * g).astype(o_ref.dtype)


def _swiglu_up(h, w1, w3, jl):
    t, d = h.shape
    dff = w1.shape[2]
    tm = _divisor_tile(t, 1088)
    tn = _divisor_tile(dff, 512, 128)
    return pl.pallas_call(
        _swiglu_kernel,
        grid=(dff // tn, t // tm),
        in_specs=[
            pl.BlockSpec((tm, d), lambda j, i: (i, 0)),
            pl.BlockSpec((1, d, tn), lambda j, i: (jl, 0, j)),
            pl.BlockSpec((1, d, tn), lambda j, i: (jl, 0, j)),
        ],
        out_specs=pl.BlockSpec((tm, tn), lambda j, i: (i, j)),
        out_shape=jax.ShapeDtypeStruct((t, dff), BF),
        scratch_shapes=[pltpu.VMEM((d, tn), BF), pltpu.VMEM((d, tn), BF)],
        compiler_params=_cparams(("parallel", "arbitrary")),
        name="swiglu_up",
    )(h, w1, w3)


def _gather_rows_kernel(idx_ref, src_ref, o_ref, *scratch, tr):
    base = pl.program_id(0) * tr
    dst, sem = (o_ref, scratch[0]) if len(scratch) == 1 else scratch

    def copy(r):
        return pltpu.make_async_copy(src_ref.at[pl.ds(idx_ref[base + r], 1), :], dst.at[pl.ds(r, 1), :], sem)

    def issue(g, carry):
        for u in range(GATHER_UNROLL):
            copy(g * GATHER_UNROLL + u).start(priority=u % 2)
        return carry

    lax.fori_loop(0, tr // GATHER_UNROLL, issue, 0)

    def drain(r, carry):
        copy(r).wait()
        return carry

    lax.fori_loop(0, tr, drain, 0, unroll=8)
    if dst is not o_ref:
        o_ref[...] = dst[...].astype(o_ref.dtype)


def _gather_rows(src, idx, out_dtype):
    n, d = src.shape
    assert src.dtype == F32
    rows = idx.shape[0]
    tr = _divisor_tile(rows, MOE_GATHER_ROWS, 8)
    staging = [] if out_dtype == src.dtype else [pltpu.VMEM((tr, d), src.dtype)]
    return pl.pallas_call(
        functools.partial(_gather_rows_kernel, tr=tr),
        grid_spec=pltpu.PrefetchScalarGridSpec(
            num_scalar_prefetch=1,
            grid=(rows // tr,),
            in_specs=[pl.BlockSpec(memory_space=pl.ANY)],
            out_specs=pl.BlockSpec((tr, d), lambda i, idx_ref: (i, 0)),
            scratch_shapes=staging + [pltpu.SemaphoreType.DMA(())],
        ),
        out_shape=jax.ShapeDtypeStruct((rows, d), out_dtype),
        compiler_params=_cparams(("arbitrary",)),
        name="moe_gather",
    )(idx, src)


def _expert_rows_loop(st_ref, nt_ref, tr, load_copy, store_copy, compute):
    e, n = pl.program_id(0), pl.program_id(1)
    n_exp, n_col = pl.num_programs(0), pl.num_programs(1)
    start, n_tiles = st_ref[e], nt_ref[e]
    first_step = jnp.logical_and(e == 0, n == 0)
    last_step = jnp.logical_and(e == n_exp - 1, n == n_col - 1)
    e_next = jnp.minimum(jnp.where(n == n_col - 1, e + 1, e), n_exp - 1)

    @pl.when(jnp.logical_and(first_step, n_tiles > 0))
    def _():
        load_copy(start, 0, 0).start(priority=1)

    @pl.when(n_tiles > 1)
    def _():
        load_copy(start, 1, 1).start(priority=1)

    def body(i, carry):
        load_copy(start, i, i % MOE_LOAD_SLOTS).wait()

        @pl.when(i + 2 < n_tiles)
        def _():
            load_copy(start, i + 2, (i + 2) % MOE_LOAD_SLOTS).start(priority=1)

        @pl.when(i >= 2)
        def _():
            store_copy(start, i - 2, i % 2).wait()

        compute(i % MOE_LOAD_SLOTS, i % 2)
        store_copy(start, i, i % 2).start(priority=1)
        return carry

    lax.fori_loop(0, n_tiles, body, 0)

    @pl.when(jnp.logical_and(jnp.logical_not(last_step), nt_ref[e_next] > 0))
    def _():
        load_copy(st_ref[e_next], 0, 0).start(priority=1)

    @pl.when(n_tiles >= 2)
    def _():
        store_copy(start, n_tiles - 2, n_tiles % 2).wait()

    @pl.when(n_tiles >= 1)
    def _():
        store_copy(start, n_tiles - 1, (n_tiles - 1) % 2).wait()


def _zero_tail_rows(out_hbm, zbuf, sem, first_row, col0, tn):
    zbuf[...] = jnp.zeros_like(zbuf)
    n_chunks = (out_hbm.shape[0] - first_row) // MOE_ROW_ALIGN

    def copy(c):
        r0 = pl.multiple_of(first_row + c * MOE_ROW_ALIGN, MOE_ROW_ALIGN)
        return pltpu.make_async_copy(zbuf, out_hbm.at[pl.ds(r0, MOE_ROW_ALIGN), pl.ds(col0, tn)], sem)

    def issue(c, carry):
        copy(c).start()
        return carry

    def drain(c, carry):
        copy(c).wait()
        return carry

    lax.fori_loop(0, n_chunks, issue, 0)
    lax.fori_loop(0, n_chunks, drain, 0)


def _moe_up_kernel(st_ref, nt_ref, x_hbm, w1_ref, w3_ref, a_hbm, w1_s, w3_s, xbuf, obuf, zbuf, sem_in, sem_out, sem_z,
                   *, tr, tn):
    e, n = pl.program_id(0), pl.program_id(1)
    w1_s[...] = w1_ref[0, 0].astype(BF)
    w3_s[...] = w3_ref[0, 0].astype(BF)
    col0 = pl.multiple_of(n * tn, tn)

    def rows(start, i):
        return pl.ds(pl.multiple_of(start + i * tr, MOE_ROW_ALIGN), tr)

    def load_copy(start, i, slot):
        return pltpu.make_async_copy(x_hbm.at[rows(start, i)], xbuf.at[slot], sem_in.at[slot])

    def store_copy(start, i, slot):
        return pltpu.make_async_copy(obuf.at[slot], a_hbm.at[rows(start, i), pl.ds(col0, tn)], sem_out.at[slot])

    def compute(in_slot, out_slot):
        x = xbuf[in_slot]
        a = jnp.dot(x, w1_s[...], preferred_element_type=F32)
        g = jnp.dot(x, w3_s[...], preferred_element_type=F32)
        obuf[out_slot] = (_silu(a) * g).astype(obuf.dtype)

    _expert_rows_loop(st_ref, nt_ref, tr, load_copy, store_copy, compute)

    @pl.when(e == pl.num_programs(0) - 1)
    def _():
        _zero_tail_rows(a_hbm, zbuf, sem_z, st_ref[e] + nt_ref[e] * tr, col0, tn)


def _moe_up(xs, w1, w3, jl, starts, n_tiles, tr):
    rows, d = xs.shape
    n_exp, dff = w1.shape[1], w1.shape[3]
    tn = _divisor_tile(dff, 512, 128)
    return pl.pallas_call(
        functools.partial(_moe_up_kernel, tr=tr, tn=tn),
        grid_spec=pltpu.PrefetchScalarGridSpec(
            num_scalar_prefetch=2,
            grid=(n_exp, dff // tn),
            in_specs=[
                pl.BlockSpec(memory_space=pl.ANY),
                pl.BlockSpec((1, 1, d, tn), lambda e, n, st, nt: (jl, e, 0, n)),
                pl.BlockSpec((1, 1, d, tn), lambda e, n, st, nt: (jl, e, 0, n)),
            ],
            out_specs=pl.BlockSpec(memory_space=pl.ANY),
            scratch_shapes=[pltpu.VMEM((d, tn), BF), pltpu.VMEM((d, tn), BF),
                            pltpu.VMEM((MOE_LOAD_SLOTS, tr, d), BF), pltpu.VMEM((2, tr, tn), BF),
                            pltpu.VMEM((MOE_ROW_ALIGN, tn), BF),
                            pltpu.SemaphoreType.DMA((MOE_LOAD_SLOTS,)), pltpu.SemaphoreType.DMA((2,)),
                            pltpu.SemaphoreType.DMA(())],
        ),
        out_shape=jax.ShapeDtypeStruct((rows, dff), BF),
        compiler_params=_cparams(("arbitrary", "arbitrary")),
        name="moe_up",
    )(starts, n_tiles, xs, w1, w3)


def _moe_down_kernel(st_ref, nt_ref, a_hbm, w_ref, y_hbm, w_s, abuf, obuf, zbuf, sem_in, sem_out, sem_z, *, tr, tn):
    e, n = pl.program_id(0), pl.program_id(1)
    w_s[...] = w_ref[0, 0].astype(BF)
    col0 = pl.multiple_of(n * tn, tn)

    def rows(start, i):
        return pl.ds(pl.multiple_of(start + i * tr, MOE_ROW_ALIGN), tr)

    def load_copy(start, i, slot):
        return pltpu.make_async_copy(a_hbm.at[rows(start, i)], abuf.at[slot], sem_in.at[slot])

    def store_copy(start, i, slot):
        return pltpu.make_async_copy(obuf.at[slot], y_hbm.at[rows(start, i), pl.ds(col0, tn)], sem_out.at[slot])

    def compute(in_slot, out_slot):
        obuf[out_slot] = jnp.dot(abuf[in_slot], w_s[...], preferred_element_type=F32)

    _expert_rows_loop(st_ref, nt_ref, tr, load_copy, store_copy, compute)

    @pl.when(e == pl.num_programs(0) - 1)
    def _():
        _zero_tail_rows(y_hbm, zbuf, sem_z, st_ref[e] + nt_ref[e] * tr, col0, tn)


def _moe_down(act, w2, jl, starts, n_tiles, tr):
    rows, dff = act.shape
    n_exp, d = w2.shape[1], w2.shape[3]
    tn = _divisor_tile(d, 512, 128)
    return pl.pallas_call(
        functools.partial(_moe_down_kernel, tr=tr, tn=tn),
        grid_spec=pltpu.PrefetchScalarGridSpec(
            num_scalar_prefetch=2,
            grid=(n_exp, d // tn),
            in_specs=[
                pl.BlockSpec(memory_space=pl.ANY),
                pl.BlockSpec((1, 1, dff, tn), lambda e, n, st, nt: (jl, e, 0, n)),
            ],
            out_specs=pl.BlockSpec(memory_space=pl.ANY),
            scratch_shapes=[pltpu.VMEM((dff, tn), BF), pltpu.VMEM((MOE_LOAD_SLOTS, tr, dff), BF),
                            pltpu.VMEM((2, tr, tn), F32), pltpu.VMEM((MOE_ROW_ALIGN, tn), F32),
                            pltpu.SemaphoreType.DMA((MOE_LOAD_SLOTS,)), pltpu.SemaphoreType.DMA((2,)),
                            pltpu.SemaphoreType.DMA(())],
        ),
        out_shape=jax.ShapeDtypeStruct((rows, d), F32),
        compiler_params=_cparams(("arbitrary", "arbitrary")),
        name="moe_down",
    )(starts, n_tiles, act, w2)


def _combine_kernel(*refs, tm, p_rows, lc, n_batch, emit_h):
    ya_ref, yb_ref, rt_ref, x_ref, gpost_ref, gate_ref = refs[:6]
    gnext_ref = sh_ref = sc_ref = ho_ref = None
    if emit_h:
        gnext_ref, sh_ref, sc_ref, xo_ref, ho_ref = refs[6:]
    else:
        (xo_ref,) = refs[6:]
    b, cm = _tile_rows(pl.program_id(0), tm, p_rows, lc)
    rt = rt_ref[...]
    y = rt[:, 2:3] * ya_ref[...] + rt[:, 3:4] * yb_ref[...]
    _residual_epilogue(y, x_ref, gpost_ref, gate_ref, gnext_ref, sh_ref, sc_ref, xo_ref, ho_ref, b, n_batch, cm)


def _moe_combine(y2, route, xs, g_post, mods, l, j_gate, nxt, dims):
    t, d = xs.shape
    p_rows, lc, n_batch = dims
    tm = _divisor_tile(p_rows, 544)
    nt = t // tm
    const = lambda i: (0, 0)
    rowtile = pl.BlockSpec((tm, d), lambda i: (i, 0))
    args = [y2, y2, route, xs, g_post.reshape(1, d), mods]
    in_specs = [rowtile, pl.BlockSpec((tm, d), lambda i: (i + nt, 0)), pl.BlockSpec((tm, 128), lambda i: (i, 0)),
                rowtile, pl.BlockSpec((1, d), const), _mod_spec(l, j_gate, d)]
    out_shape = [jax.ShapeDtypeStruct((t, d), F32)]
    out_specs = [rowtile]
    if nxt is not None:
        g_next, l_next, j_shift, j_scale = nxt
        args += [g_next.reshape(1, d), mods, mods]
        in_specs += [pl.BlockSpec((1, d), const), _mod_spec(l_next, j_shift, d), _mod_spec(l_next, j_scale, d)]
        out_shape.append(jax.ShapeDtypeStruct((t, d), BF))
        out_specs.append(rowtile)
    return pl.pallas_call(
        functools.partial(_combine_kernel, tm=tm, p_rows=p_rows, lc=lc, n_batch=n_batch, emit_h=nxt is not None),
        grid=(nt,),
        in_specs=in_specs,
        out_specs=out_specs,
        out_shape=out_shape,
        compiler_params=_cparams(("parallel",)),
        name="moe_combine",
    )(*args)


def _moe_ffn(h, route, w1, w3, w2, jl, xs, g_post, mods, l, nxt, dims):
    t, d = h.shape
    n_exp = w1.shape[1]
    tr = MOE_ROW_TILE
    expert = jnp.concatenate([route[:, 0], route[:, 1]]).astype(jnp.int32)
    token = jnp.concatenate([jnp.arange(t, dtype=jnp.int32)] * 2)
    onehot = (expert[:, None] == jnp.arange(n_exp, dtype=jnp.int32)[None, :]).astype(jnp.int32)
    rank = jnp.sum((jnp.cumsum(onehot, axis=0) - onehot) * onehot, axis=1)
    counts = jnp.sum(onehot, axis=0)
    padded = ((counts + MOE_ROW_ALIGN - 1) // MOE_ROW_ALIGN) * MOE_ROW_ALIGN
    ends = jnp.cumsum(padded)
    starts = (ends - padded).astype(jnp.int32)
    n_tiles = ((counts + tr - 1) // tr).astype(jnp.int32)
    dest = starts[expert] + rank
    rows = 2 * t + n_exp * MOE_ROW_ALIGN + tr
    rows = ((rows + MOE_GATHER_ROWS - 1) // MOE_GATHER_ROWS) * MOE_GATHER_ROWS
    row_token = jnp.zeros((rows,), jnp.int32).at[dest].set(token)

    xs_sorted = _gather_rows(h, row_token, BF)
    act = _moe_up(xs_sorted, w1, w3, jl, starts, n_tiles, tr)
    y_sorted = _moe_down(act, w2, jl, starts, n_tiles, tr)
    y2 = _gather_rows(y_sorted, dest, F32)
    return _moe_combine(y2, route, xs, g_post, mods, l, 5, nxt, dims)


def _permute_w_in(w_in, d):
    q_rank, kv_rank = (3 * d) // 16, d // 8
    bw = d // N_BRANCH
    o = 0
    pieces = {}
    for name, width in (("zq", q_rank), ("zkv", kv_rank), ("zr", MLA_ROPE), ("hq", bw), ("hi", bw), ("hff", bw),
                        ("hfb", bw), ("hg", bw), ("conv", 2 * bw), ("pool", bw), ("gate", N_BRANCH * d)):
        pieces[name] = (o, width)
        o += width
    runs = [(Z_CONV, "conv", "conv"), (Z_HQ, "hq", "hg"), (Z_POOL, "pool", "pool"), (Z_MLA, "zq", "zr"),
            (Z_GATE, "gate", "gate")]
    out = jnp.zeros((w_in.shape[0], d, Z_GATE + N_BRANCH * d), BF)
    for dst, first, last in runs:
        lo = pieces[first][0]
        hi = pieces[last][0] + pieces[last][1]
        out = lax.dynamic_update_slice(out, w_in[:, :, lo:hi].astype(BF), (0, 0, dst))
    return out


def _pad_heads(w, n_heads, width):
    r = w.shape[0]
    w3 = w.reshape(r, n_heads, width)
    return jnp.pad(w3, ((0, 0), (0, 0), (0, HEAD_PAD - width))).reshape(r, n_heads * HEAD_PAD)


def _rope_tables(l_lat, lc):
    tt = jnp.arange(l_lat)
    row = (tt // GRID_W).astype(F32)
    col = (tt % GRID_W).astype(F32)
    axis_dim = MLA_ROPE // 2
    inv_freq = 1.0 / (ROPE_THETA ** (jnp.arange(0, axis_dim, 2, dtype=F32) / axis_dim))
    ar, ac = row[:, None] * inv_freq, col[:, None] * inv_freq
    ang = jnp.concatenate([ar, ar, ac, ac], axis=-1)
    cos = jnp.concatenate([jnp.ones((lc, MLA_ROPE), F32), jnp.cos(ang)], axis=0)
    sin = jnp.concatenate([jnp.zeros((lc, MLA_ROPE), F32), jnp.sin(ang)], axis=0)
    padc = jnp.zeros((lc + l_lat, 128 - MLA_ROPE), F32)
    pm = np.zeros((128, 128), np.float32)
    q4 = MLA_ROPE // 4
    for base in (0, 2 * q4):
        for i in range(q4):
            pm[base + q4 + i, base + i] = -1.0
            pm[base + i, base + q4 + i] = 1.0
    return jnp.concatenate([cos, padc], axis=1), jnp.concatenate([sin, padc], axis=1), jnp.asarray(pm, BF)


def kernel(x, c, ctx, c_ctx, w_mod, b_mod, g_mix_pre, g_mix_post, g_ffn_pre, g_ffn_post, w_in, mla_g_q, mla_g_kv, mla_w_uq, mla_w_ukv, hg_lb_logits, hg_g_norm, conv_w, conv_b, conv_ln_g, conv_ln_b, pool_w, pool_scale, w_branch, w_out, ffn_w1, ffn_w3, ffn_w2, moe_router, moe_w1, moe_w3, moe_w2):
    n_batch, l_lat, d = x.shape
    lc = ctx.shape[1]
    depth = w_in.shape[0]
    p_rows = lc + l_lat
    t = n_batch * p_rows
    dims = (p_rows, lc, n_batch)
    assert n_batch + 1 <= 8 and lc % HG_CHUNK == 0 and l_lat % HG_CHUNK == 0 and l_lat % GRID_W == 0
    n_heads = mla_w_uq.shape[2] // MLA_QK

    xs = jnp.concatenate([ctx, x], axis=1).reshape(t, d)
    cvec = jnp.zeros((8, d), F32).at[:n_batch].set(c).at[n_batch].set(c_ctx)
    mods = _modvecs(cvec, w_mod, b_mod)

    lb = jnp.cumsum(jax.nn.softmax(hg_lb_logits.astype(F32), axis=1), axis=1)
    lb = lb - lb[:, :1]
    cos, sin, pm = _rope_tables(l_lat, lc)

    w_in_bf = _permute_w_in(w_in, d)
    w_out_bf = w_out.astype(BF)
    ffn_w2_bf = ffn_w2.astype(BF)

    h = _modulate(xs, g_mix_pre[0], mods, 0, 0, 1, dims)
    for l in range(depth):
        last = l == depth - 1
        z = _matmul(h, w_in_bf, l, BF)
        q3, k3, v3 = _mla_proj(z, mla_g_q[l], mla_g_kv[l], _pad_heads(mla_w_uq[l], n_heads, MLA_QK),
                               mla_w_ukv[l], cos, sin, pm, dims)
        y_mla = _attention(q3, k3, v3, dims)
        y_hg = _hgrn_readout(*_hgrn(z, lb[:, l], dims), z, hg_g_norm[l])
        y_conv = _conv_mixer(z, conv_w[l], conv_b[l], conv_ln_g[l], conv_ln_b[l], dims)
        y_pool = _pool_mixer(z, pool_w[l], pool_scale[l], dims)
        m = _merge([y_mla, y_hg, y_conv, y_pool], z, w_branch, l)
        j = l // 2
        nxt_ffn = (g_ffn_pre[l], l, 3, 4)
        nxt_mix = None if last else (g_mix_pre[l + 1], l + 1, 0, 1)
        if l % 2 == 0:
            xs, h2 = _out_proj(m, w_out_bf, l, xs, g_mix_post[l], mods, l, 2, nxt_ffn, dims)
            act = _swiglu_up(h2, ffn_w1, ffn_w3, j)
            res = _out_proj(act, ffn_w2_bf, j, xs, g_ffn_post[l], mods, l, 5, nxt_mix, dims, name="ffn_down")
        else:
            n_exp = moe_router.shape[2]
            wr = jnp.pad(moe_router[j], ((0, 0), (0, 128 - n_exp)))
            wr_hi = wr.astype(BF)
            wr2 = jnp.stack([wr_hi, (wr - wr_hi.astype(F32)).astype(BF)])
            xs, h2, route = _out_proj(m, w_out_bf, l, xs, g_mix_post[l], mods, l, 2, nxt_ffn, dims,
                                      router=(wr2, n_exp))
            res = _moe_ffn(h2, route, moe_w1, moe_w3, moe_w2, j, xs, g_ffn_post[l], mods, l, nxt_mix, dims)
        if last:
            (xs,) = res
        else:
            xs, h = res
    return xs.reshape(n_batch, p_rows, d)[:, lc:, :]
```

```python
import functools

import numpy as np
import jax
import jax.numpy as jnp
from jax import lax
from jax.experimental import pallas as pl
from jax.experimental.pallas import tpu as pltpu

F32 = jnp.float32
BF = jnp.bfloat16

EPS = 1e-6
GRID_W = 64
ROPE_THETA = 10000.0
N_BRANCH = 4
MLA_NOPE, MLA_ROPE, MLA_V = 128, 64, 128
MLA_QK = MLA_NOPE + MLA_ROPE
HEAD_PAD = 256
HG_DK = 128
CONV_WIDTH = 31
POOL_WINDOWS = (2, 4, 8, 16)
POOL_GROUP = 128
LANES = 128
IN_HALF = 512
IN_WIN_TILES = 5
SUBLANES = 8
HALO = 16
HG_CHUNK = 64
HG_EXP_CLAMP = 80.0
NEG_BIG = -1e30
ATTN_Q_TILE = 512
MOE_ROW_TILE = 256
GATHER_UNROLL = 8
EPILOGUE_SPLIT = 2
MOE_LOAD_SLOTS = 3
MOE_ROW_ALIGN = 16
MOE_GATHER_ROWS = 1024

VMEM_LIMIT_V7X = 56 * 1024 * 1024

Z_CONV, Z_HQ, Z_HI, Z_HFF, Z_HFB, Z_HG, Z_POOL, Z_MLA, Z_GATE = 0, 1024, 1536, 2048, 2560, 3072, 3584, 4096, 5120
MLA_ZKV_OFF, MLA_ZR_OFF = 384, 640


def _cparams(sem, vmem=VMEM_LIMIT_V7X):
    return pltpu.CompilerParams(dimension_semantics=sem, vmem_limit_bytes=vmem)


def _divisor_tile(n, target, mult=16):
    best = None
    for t in range(mult, min(n, target) + 1, mult):
        if n % t == 0:
            best = t
    assert best is not None, (n, target, mult)
    return best


def _pick_row(m, r):
    rows = lax.broadcasted_iota(jnp.int32, m.shape, 0)
    return jnp.sum(jnp.where(rows == r, m, 0.0), axis=0, keepdims=True)


def _row_vec(m_ref, b, n_batch, ctx_mask):
    m = m_ref[0, 0]
    return jnp.where(ctx_mask, _pick_row(m, n_batch), _pick_row(m, b))


def _tile_rows(i, tm, p_rows, lc):
    per_batch = p_rows // tm
    b = i // per_batch
    pos = (i % per_batch) * tm + lax.broadcasted_iota(jnp.int32, (tm, 1), 0)
    return b, pos < lc


def _rms(x, g):
    return x * lax.rsqrt(jnp.mean(x * x, axis=-1, keepdims=True) + EPS) * g


def _sigmoid(x):
    return 1.0 / (1.0 + jnp.exp(-x))


def _silu(x):
    return x * _sigmoid(x)


def _modvec_kernel(c_ref, w_ref, b_ref, o_ref):
    s = _silu(c_ref[...])
    s_hi = s.astype(BF)
    s_lo = (s - s_hi.astype(F32)).astype(BF)
    w = w_ref[0].astype(BF)
    acc = jnp.dot(s_hi, w, preferred_element_type=F32) + jnp.dot(s_lo, w, preferred_element_type=F32)
    o_ref[0, 0] = acc + b_ref[0, 0]


def _modvecs(cvec, w_mod, b_mod):
    depth, d, n6 = w_mod.shape
    nj = n6 // d
    tn = _divisor_tile(d, 1024, 128)
    npj = d // tn
    return pl.pallas_call(
        _modvec_kernel,
        grid=(depth, nj, npj),
        in_specs=[
            pl.BlockSpec((8, d), lambda l, j, n: (0, 0)),
            pl.BlockSpec((1, d, tn), lambda l, j, n: (l, 0, j * npj + n)),
            pl.BlockSpec((1, 1, 1, tn), lambda l, j, n: (l, j, 0, n)),
        ],
        out_specs=pl.BlockSpec((1, 1, 8, tn), lambda l, j, n: (l, j, 0, n)),
        out_shape=jax.ShapeDtypeStruct((depth, nj, 8, d), F32),
        compiler_params=_cparams(("parallel", "parallel", "parallel")),
        name="modvec",
    )(cvec, w_mod, b_mod.reshape(depth, nj, 1, d))


def _modulate_kernel(x_ref, g_ref, sh_ref, sc_ref, h_ref, *, tm, p_rows, lc, n_batch):
    b, cm = _tile_rows(pl.program_id(0), tm, p_rows, lc)
    shift = _row_vec(sh_ref, b, n_batch, cm)
    scale = _row_vec(sc_ref, b, n_batch, cm)
    h_ref[...] = (_rms(x_ref[...], g_ref[...]) * (1.0 + scale) + shift).astype(h_ref.dtype)


def _mod_spec(l, j, d):
    return pl.BlockSpec((1, 1, 8, d), lambda *_: (l, j, 0, 0))


def _modulate(xs, g, mods, l, j_shift, j_scale, dims):
    t, d = xs.shape
    p_rows, lc, n_batch = dims
    tm = _divisor_tile(p_rows, 544)
    return pl.pallas_call(
        functools.partial(_modulate_kernel, tm=tm, p_rows=p_rows, lc=lc, n_batch=n_batch),
        grid=(t // tm,),
        in_specs=[
            pl.BlockSpec((tm, d), lambda i: (i, 0)),
            pl.BlockSpec((1, d), lambda i: (0, 0)),
            _mod_spec(l, j_shift, d),
            _mod_spec(l, j_scale, d),
        ],
        out_specs=pl.BlockSpec((tm, d), lambda i: (i, 0)),
        out_shape=jax.ShapeDtypeStruct((t, d), BF),
        compiler_params=_cparams(("parallel",)),
        name="modulate",
    )(xs, g.reshape(1, d), mods, mods)


def _in_proj_kernel(base_ref, shift_ref, valid_ref, x_ref, *refs):
    w_refs, o_ref, w_s = refs[:2 * IN_WIN_TILES], refs[2 * IN_WIN_TILES], refs[2 * IN_WIN_TILES + 1]
    j = pl.program_id(0)
    win_w = IN_WIN_TILES * LANES

    @pl.when(pl.program_id(1) == 0)
    def _():
        for u in range(2):
            shift, valid = shift_ref[2 * j + u], valid_ref[2 * j + u]
            win = jnp.concatenate([w_refs[IN_WIN_TILES * u + i][0] for i in range(IN_WIN_TILES)], axis=1)
            lane = lax.broadcasted_iota(jnp.int32, win.shape, 1)
            win = jnp.where(lane < shift + valid, win, 0.0).astype(BF)
            r = lax.broadcasted_iota(jnp.int32, (win_w, IN_HALF), 0)
            c = lax.broadcasted_iota(jnp.int32, (win_w, IN_HALF), 1)
            sel = jnp.where(jnp.logical_and(r == c + shift, c < valid), 1.0, 0.0).astype(BF)
            w_s[:, u * IN_HALF:(u + 1) * IN_HALF] = jnp.dot(win, sel, preferred_element_type=F32).astype(BF)

    o_ref[...] = jnp.dot(x_ref[...], w_s[...], preferred_element_type=F32).astype(o_ref.dtype)


def _in_proj_tables(d):
    q_rank, kv_rank = (3 * d) // 16, d // 8
    bw = d // N_BRANCH
    src, o = {}, 0
    for name, width in (("mla", q_rank + kv_rank + MLA_ROPE), ("hq", bw), ("hi", bw), ("hff", bw), ("hfb", bw),
                        ("hg", bw), ("conv", 2 * bw), ("pool", bw), ("gate", N_BRANCH * d)):
        src[name] = (o, width)
        o += width
    dst = (("conv", Z_CONV), ("hq", Z_HQ), ("hi", Z_HI), ("hff", Z_HFF), ("hfb", Z_HFB), ("hg", Z_HG),
           ("pool", Z_POOL), ("mla", Z_MLA), ("gate", Z_GATE))
    n_half = (Z_GATE + N_BRANCH * d) // IN_HALF
    base, shift, valid = [0] * n_half, [0] * n_half, [0] * n_half
    for name, z0 in dst:
        s0, width = src[name]
        assert z0 % IN_HALF == 0
        for k in range(-(-width // IN_HALF)):
            s = s0 + k * IN_HALF
            base[z0 // IN_HALF + k] = s // LANES
            shift[z0 // IN_HALF + k] = s % LANES
            valid[z0 // IN_HALF + k] = min(IN_HALF, width - k * IN_HALF)
    as_i32 = lambda v: jnp.asarray(np.array(v, np.int32))
    return as_i32(base), as_i32(shift), as_i32(valid)


def _in_proj(x, w_in, l, out_dtype, tm_target=1088):
    t, k = x.shape
    d = k
    n = Z_GATE + N_BRANCH * d
    tn = 2 * IN_HALF
    tm = _divisor_tile(t, tm_target)
    tables = _in_proj_tables(d)
    last_tile = (w_in.shape[2] - 1) // LANES

    def w_spec(u, i):
        return pl.BlockSpec((1, k, LANES),
                            lambda j, m, base, sh, va: (l, 0, jnp.minimum(base[2 * j + u] + i, last_tile)))

    return pl.pallas_call(
        _in_proj_kernel,
        grid_spec=pltpu.PrefetchScalarGridSpec(
            num_scalar_prefetch=3,
            grid=(n // tn, t // tm),
            in_specs=[pl.BlockSpec((tm, k), lambda j, m, *_: (m, 0))]
            + [w_spec(u, i) for u in range(2) for i in range(IN_WIN_TILES)],
            out_specs=pl.BlockSpec((tm, tn), lambda j, m, *_: (m, j)),
            scratch_shapes=[pltpu.VMEM((k, tn), BF)],
        ),
        out_shape=jax.ShapeDtypeStruct((t, n), out_dtype),
        compiler_params=_cparams(("parallel", "arbitrary")),
        name="in_proj",
    )(*tables, x, *([w_in] * (2 * IN_WIN_TILES)))


def _rope128(t, cos, sin, pm):
    partner = jnp.dot(t.astype(BF), pm, preferred_element_type=F32)
    return t * cos + partner * sin


def _mla_proj_kernel(z_ref, gq_ref, gkv_ref, wq_ref, wkv_ref, cos_ref, sin_ref, pm_ref,
                     q_ref, k_ref, v_ref, wq_s, wkv_s, *, n_heads, q_rank, kv_rank, scale):
    @pl.when(pl.program_id(0) == 0)
    def _():
        wq_s[...] = wq_ref[...].astype(BF)
        wkv_s[...] = wkv_ref[...].astype(BF)

    z = z_ref[...]
    zq = z[:, :q_rank].astype(F32)
    zkv = z[:, MLA_ZKV_OFF:MLA_ZKV_OFF + kv_rank].astype(F32)
    zr = z[:, MLA_ZR_OFF:MLA_ZR_OFF + 128].astype(F32)
    cos, sin, pm = cos_ref[...], sin_ref[...], pm_ref[...]
    qn = _rms(zq, gq_ref[...]).astype(BF)
    kvn = _rms(zkv, gkv_ref[...]).astype(BF)
    k_rope = _rope128(zr, cos, sin, pm)[:, :MLA_ROPE]
    for h in range(n_heads):
        qh = jnp.dot(qn, wq_s[:, h * HEAD_PAD:(h + 1) * HEAD_PAD], preferred_element_type=F32)
        q_rope = _rope128(qh[:, MLA_NOPE:], cos, sin, pm)[:, :MLA_ROPE]
        q_ref[h] = (jnp.concatenate([qh[:, :MLA_NOPE], q_rope], axis=-1) * scale).astype(q_ref.dtype)
        kvh = jnp.dot(kvn, wkv_s[:, h * HEAD_PAD:(h + 1) * HEAD_PAD], preferred_element_type=F32)
        k_ref[h] = jnp.concatenate([kvh[:, :MLA_NOPE], k_rope], axis=-1).astype(k_ref.dtype)
        v_ref[h] = jnp.concatenate([kvh[:, MLA_NOPE:], jnp.ones_like(kvh[:, MLA_NOPE:])], axis=-1).astype(v_ref.dtype)


def _mla_proj(z, g_q, g_kv, wq_pad, wkv, cos, sin, pm, dims):
    t = z.shape[0]
    p_rows = dims[0]
    q_rank, kv_rank = g_q.shape[0], g_kv.shape[0]
    n_heads = wkv.shape[1] // HEAD_PAD
    tm = _divisor_tile(p_rows, 544)
    npb = p_rows // tm
    scale = float(MLA_QK) ** -0.5 * float(np.log2(np.e))
    const = lambda i: (0, 0)
    return pl.pallas_call(
        functools.partial(_mla_proj_kernel, n_heads=n_heads, q_rank=q_rank, kv_rank=kv_rank, scale=scale),
        grid=(t // tm,),
        in_specs=[
            pl.BlockSpec((tm, 1024), lambda i: (i, Z_MLA // 1024)),
            pl.BlockSpec((1, q_rank), const),
            pl.BlockSpec((1, kv_rank), const),
            pl.BlockSpec(wq_pad.shape, const),
            pl.BlockSpec(wkv.shape, const),
            pl.BlockSpec((tm, 128), lambda i: (i % npb, 0)),
            pl.BlockSpec((tm, 128), lambda i: (i % npb, 0)),
            pl.BlockSpec((128, 128), const),
        ],
        out_specs=[
            pl.BlockSpec((n_heads, tm, MLA_QK), lambda i: (0, i, 0)),
            pl.BlockSpec((n_heads, tm, MLA_QK), lambda i: (0, i, 0)),
            pl.BlockSpec((n_heads, tm, 2 * MLA_V), lambda i: (0, i, 0)),
        ],
        out_shape=[
            jax.ShapeDtypeStruct((n_heads, t, MLA_QK), BF),
            jax.ShapeDtypeStruct((n_heads, t, MLA_QK), BF),
            jax.ShapeDtypeStruct((n_heads, t, 2 * MLA_V), BF),
        ],
        scratch_shapes=[pltpu.VMEM(wq_pad.shape, BF), pltpu.VMEM(wkv.shape, BF)],
        compiler_params=_cparams(("arbitrary",)),
        name="mla_proj",
    )(z, g_q.reshape(1, -1), g_kv.reshape(1, -1), wq_pad, wkv, cos, sin, pm)


def _attn_kernel(q_ref, k_ref, v_ref, o_ref, s0_ref, s1_ref, p0_ref, p1_ref, *, tq, lc):
    p_rows = k_ref.shape[1]
    n_tiles = (p_rows - lc) // tq
    nt = (((1,), (1,)), ((), ()))

    def tile_start(t):
        return pl.multiple_of(lc + t * tq, lc)

    def scores(t, s_ref):
        s_ref[...] = lax.dot_general(q_ref[0, pl.ds(tile_start(t), tq), :], k_ref[0], nt, preferred_element_type=F32)

    def softmax(s_ref, p_ref):
        s = s_ref[...]
        p_ref[...] = jnp.exp2((s - jnp.max(s, axis=-1, keepdims=True)).astype(BF))

    def emit(p, v, q0, n_q):
        o = jnp.dot(p, v, preferred_element_type=F32)
        o_ref[pl.ds(q0, n_q), :] = (o[:, :MLA_V] / o[:, MLA_V:MLA_V + 1]).astype(o_ref.dtype)

    def values(t, p_ref):
        emit(p_ref[...], v_ref[0], tile_start(t), tq)

    s_ctx = lax.dot_general(q_ref[0, :lc, :], k_ref[0, :lc, :], nt, preferred_element_type=F32)
    emit(jnp.exp2((s_ctx - jnp.max(s_ctx, axis=-1, keepdims=True)).astype(BF)), v_ref[0, :lc, :], 0, lc)

    scores(0, s0_ref)
    scores(1, s1_ref)
    softmax(s0_ref, p0_ref)

    def body(kk, carry):
        t = 2 * kk
        scores(t, s0_ref)
        softmax(s1_ref, p1_ref)
        values(t - 2, p0_ref)
        scores(t + 1, s1_ref)
        softmax(s0_ref, p0_ref)
        values(t - 1, p1_ref)
        return carry

    lax.fori_loop(1, n_tiles // 2, body, 0)
    softmax(s1_ref, p1_ref)
    values(n_tiles - 2, p0_ref)
    values(n_tiles - 1, p1_ref)


def _attention(q3, k3, v3, dims):
    n_heads, t, _ = q3.shape
    p_rows, lc, n_batch = dims
    l_lat = p_rows - lc
    tq = max(c for c in range(lc, ATTN_Q_TILE + 1, lc) if l_lat % (2 * c) == 0)
    return pl.pallas_call(
        functools.partial(_attn_kernel, tq=tq, lc=lc),
        grid=(n_batch, n_heads),
        in_specs=[
            pl.BlockSpec((1, p_rows, MLA_QK), lambda b, h: (h, b, 0)),
            pl.BlockSpec((1, p_rows, MLA_QK), lambda b, h: (h, b, 0)),
            pl.BlockSpec((1, p_rows, 2 * MLA_V), lambda b, h: (h, b, 0)),
        ],
        out_specs=pl.BlockSpec((p_rows, MLA_V), lambda b, h: (b, h)),
        out_shape=jax.ShapeDtypeStruct((t, n_heads * MLA_V), BF),
        scratch_shapes=[pltpu.VMEM((tq, p_rows), F32), pltpu.VMEM((tq, p_rows), F32),
                        pltpu.VMEM((tq, p_rows), BF), pltpu.VMEM((tq, p_rows), BF)],
        compiler_params=_cparams(("parallel", "parallel")),
        name="mla_attention",
    )(q3, k3, v3)


def _hgrn_chunk(zq, zi, zf, lb, tri, o_ref, st_ref, stream, bi, n_heads):
    feeds = tri > 0.5
    f = lb + (1.0 - lb) * _sigmoid(zf.astype(F32))
    lf = jnp.log(f)
    lf1 = lf.astype(BF)
    r1 = lf - lf1.astype(F32)
    lf2 = r1.astype(BF)
    lf3 = (r1 - lf2.astype(F32)).astype(BF)
    b_all = (jnp.dot(tri, lf1, preferred_element_type=F32) + jnp.dot(tri, lf2, preferred_element_type=F32)
             + jnp.dot(tri, lf3, preferred_element_type=F32))
    tot_all = jnp.sum(lf, axis=0, keepdims=True)
    q_all = _silu(zq.astype(F32))
    nt = (((1,), (1,)), ((), ()))
    for h in range(n_heads):
        sl = slice(h * HG_DK, (h + 1) * HG_DK)
        b, tot, q, k = b_all[:, sl], tot_all[:, sl], q_all[:, sl], 1.0 - f[:, sl]
        v = zi[:, sl]
        ref = 0.5 * tot
        q_loc = (q * jnp.exp(jnp.minimum(b - ref, HG_EXP_CLAMP))).astype(BF)
        k_loc = (k * jnp.exp(jnp.minimum(ref - b, HG_EXP_CLAMP))).astype(BF)
        a = lax.dot_general(q_loc, k_loc, nt, preferred_element_type=F32)
        a = jnp.where(feeds, a, 0.0).astype(BF)
        state_t = st_ref[stream, h]
        q_in = (q * jnp.exp(b)).astype(BF)
        o = jnp.dot(a, v, preferred_element_type=F32)
        o = o + lax.dot_general(q_in, state_t.astype(BF), nt, preferred_element_type=F32)
        o_ref[bi, :, sl] = o
        k_out = (k * jnp.exp(tot - b)).astype(BF)
        upd = lax.dot_general(v, k_out, (((0,), (0,)), ((), ())), preferred_element_type=F32)
        st_ref[stream, h] = state_t * jnp.exp(tot) + upd


def _hgrn_kernel(fq_ref, fi_ref, ff_ref, bq_ref, bi_ref, bf_ref, lb_ref, tri_ref, of_ref, ob_ref, st_ref,
                 *, n_heads, n_batch):
    @pl.when(pl.program_id(0) == 0)
    def _():
        st_ref[...] = jnp.zeros_like(st_ref)

    for rev, (q_ref, i_ref, f_ref, o_ref) in enumerate(((fq_ref, fi_ref, ff_ref, of_ref),
                                                       (bq_ref, bi_ref, bf_ref, ob_ref))):
        for bi in range(n_batch):
            _hgrn_chunk(q_ref[bi], i_ref[bi], f_ref[bi], lb_ref[rev], tri_ref[rev], o_ref, st_ref,
                        rev * n_batch + bi, bi, n_heads)


def _hgrn(z, lb2, dims):
    t, nz = z.shape
    p_rows, lc, n_batch = dims
    c = HG_CHUNK
    w = lb2.shape[-1]
    n_heads = w // HG_DK
    nch, nctx = p_rows // c, lc // c
    idx = np.arange(c)
    tri = jnp.asarray(np.stack([idx[None, :] <= idx[:, None], idx[None, :] >= idx[:, None]]), BF)
    z3 = z.reshape(n_batch, p_rows, nz)

    def back(j):
        return jnp.where(j < nctx, nctx - 1 - j, nch - 1 - (j - nctx))

    def spec(col, rev):
        if rev:
            return pl.BlockSpec((n_batch, c, w), lambda j: (0, back(j), col // w))
        return pl.BlockSpec((n_batch, c, w), lambda j: (0, j, col // w))

    out_f, out_b = pl.pallas_call(
        functools.partial(_hgrn_kernel, n_heads=n_heads, n_batch=n_batch),
        grid=(nch,),
        in_specs=[
            spec(Z_HQ, False), spec(Z_HI, False), spec(Z_HFF, False),
            spec(Z_HQ, True), spec(Z_HI, True), spec(Z_HFB, True),
            pl.BlockSpec((2, 1, w), lambda j: (0, 0, 0)),
            pl.BlockSpec((2, c, c), lambda j: (0, 0, 0)),
        ],
        out_specs=[pl.BlockSpec((n_batch, c, w), lambda j: (0, j, 0)),
                   pl.BlockSpec((n_batch, c, w), lambda j: (0, back(j), 0))],
        out_shape=[jax.ShapeDtypeStruct((n_batch, p_rows, w), F32)] * 2,
        scratch_shapes=[pltpu.VMEM((2 * n_batch, n_heads, HG_DK, HG_DK), F32)],
        compiler_params=_cparams(("arbitrary",)),
        name="hgrn_scan",
    )(z3, z3, z3, z3, z3, z3, lb2.reshape(2, 1, w), tri)
    return out_f.reshape(t, w), out_b.reshape(t, w)


def _hgrn_readout_kernel(of_ref, ob_ref, zg_ref, g_ref, y_ref, *, n_heads):
    o = of_ref[...] + ob_ref[...]
    g = g_ref[...]
    gate = _silu(zg_ref[...].astype(F32))
    for h in range(n_heads):
        sl = slice(h * HG_DK, (h + 1) * HG_DK)
        y_ref[:, sl] = (_rms(o[:, sl], g[:, sl]) * gate[:, sl]).astype(y_ref.dtype)


def _hgrn_readout(o_fwd, o_bwd, z, g_norm):
    t, w = o_fwd.shape
    tm = _divisor_tile(t, 1088)
    return pl.pallas_call(
        functools.partial(_hgrn_readout_kernel, n_heads=w // HG_DK),
        grid=(t // tm,),
        in_specs=[
            pl.BlockSpec((tm, w), lambda i: (i, 0)),
            pl.BlockSpec((tm, w), lambda i: (i, 0)),
            pl.BlockSpec((tm, w), lambda i: (i, Z_HG // w)),
            pl.BlockSpec((1, w), lambda i: (0, 0)),
        ],
        out_specs=pl.BlockSpec((tm, w), lambda i: (i, 0)),
        out_shape=jax.ShapeDtypeStruct((t, w), BF),
        compiler_params=_cparams(("parallel",)),
        name="hgrn_readout",
    )(o_fwd, o_bwd, z, g_norm.reshape(1, w))


def _segment_tile(i, tc, p_rows, lc):
    per_batch = p_rows // tc
    j = i % per_batch
    nctx = lc // tc
    first = jnp.logical_or(j == 0, j == nctx)
    last = jnp.logical_or(j == nctx - 1, j == per_batch - 1)
    in_ctx = j < nctx
    seg_pos0 = jnp.where(in_ctx, j, j - nctx) * tc
    seg_len = jnp.where(in_ctx, lc, p_rows - lc)
    return first, last, seg_pos0, seg_len


def _halo_specs(tc, width, col_block, n_rows):
    per = tc // HALO
    last_blk = n_rows // HALO - 1
    return [
        pl.BlockSpec((HALO, width), lambda i: (jnp.maximum(i * per - 1, 0), col_block)),
        pl.BlockSpec((tc, width), lambda i: (i, col_block)),
        pl.BlockSpec((HALO, width), lambda i: (jnp.minimum((i + 1) * per, last_blk), col_block)),
    ]


def _conv_kernel(prev_ref, cur_ref, next_ref, w_ref, b_ref, lg_ref, lb_ref, y_ref, buf, shifted, *, tc, p_rows, lc, ch):
    first, last, _, _ = _segment_tile(pl.program_id(0), tc, p_rows, lc)

    def glu(zz):
        zz = zz.astype(F32)
        return zz[:, :ch] * _sigmoid(zz[:, ch:])

    buf[0:HALO, :] = jnp.where(first, 0.0, glu(prev_ref[...]))
    buf[HALO:HALO + tc, :] = glu(cur_ref[...])
    buf[HALO + tc:, :] = jnp.where(last, 0.0, glu(next_ref[...]))
    span = shifted.shape[1]
    for s in range(1, SUBLANES):
        shifted[s - 1] = buf[s:s + span, :]
    w = w_ref[...]
    acc = jnp.zeros((tc, ch), F32) + b_ref[...]
    for kk in range(CONV_WIDTH):
        blk, s = divmod(HALO - CONV_WIDTH // 2 + kk, SUBLANES)
        r0 = blk * SUBLANES
        win = buf[r0:r0 + tc, :] if s == 0 else shifted[s - 1, r0:r0 + tc, :]
        acc = acc + win * w[kk:kk + 1, :]
    mu = jnp.mean(acc, axis=-1, keepdims=True)
    cen = acc - mu
    var = jnp.mean(cen * cen, axis=-1, keepdims=True)
    y = cen * lax.rsqrt(var + EPS) * lg_ref[...] + lb_ref[...]
    y_ref[...] = _silu(y).astype(y_ref.dtype)


def _conv_mixer(z, w_dw, b_dw, ln_g, ln_b, dims):
    t = z.shape[0]
    p_rows, lc, _ = dims
    ch = w_dw.shape[1]
    tc = _divisor_tile(lc, 256)
    row = lambda a: a.reshape(1, ch)
    const = lambda i: (0, 0)
    return pl.pallas_call(
        functools.partial(_conv_kernel, tc=tc, p_rows=p_rows, lc=lc, ch=ch),
        grid=(t // tc,),
        in_specs=_halo_specs(tc, 2 * ch, Z_CONV // (2 * ch), t) + [
            pl.BlockSpec((CONV_WIDTH, ch), const),
            pl.BlockSpec((1, ch), const), pl.BlockSpec((1, ch), const), pl.BlockSpec((1, ch), const),
        ],
        out_specs=pl.BlockSpec((tc, ch), lambda i: (i, 0)),
        out_shape=jax.ShapeDtypeStruct((t, ch), BF),
        scratch_shapes=[pltpu.VMEM((tc + 2 * HALO, ch), F32),
                        pltpu.VMEM((SUBLANES - 1, tc + 2 * HALO - SUBLANES, ch), F32)],
        compiler_params=_cparams(("parallel",)),
        name="conv_mixer",
    )(z, z, z, w_dw, row(b_dw), row(ln_g), row(ln_b))


def _pool_kernel(prev_ref, cur_ref, next_ref, w_ref, s_ref, y_ref, buf, *, tc, p_rows, lc):
    first, last, seg_pos0, seg_len = _segment_tile(pl.program_id(0), tc, p_rows, lc)
    buf[0:HALO, :] = jnp.where(first, 0.0, prev_ref[...].astype(F32))
    buf[HALO:HALO + tc, :] = cur_ref[...].astype(F32)
    buf[HALO + tc:, :] = jnp.where(last, 0.0, next_ref[...].astype(F32))
    pos = seg_pos0 + lax.broadcasted_iota(jnp.int32, (tc, 1), 0)
    scale = s_ref[...]
    for gi, win in enumerate(POOL_WINDOWS):
        sl = slice(gi * POOL_GROUP, (gi + 1) * POOL_GROUP)
        half = win // 2
        tot = jnp.zeros((tc, POOL_GROUP), F32)
        for u in range(-half, half):
            tot = tot + buf[HALO + u:HALO + u + tc, sl]
        cnt = (jnp.minimum(pos + half, seg_len) - jnp.maximum(pos - half, 0)).astype(F32)
        pooled = tot / cnt - buf[HALO:HALO + tc, sl]
        y = jnp.dot(pooled.astype(BF), w_ref[gi].astype(BF), preferred_element_type=F32)
        y_ref[:, sl] = (y * scale[:, sl]).astype(y_ref.dtype)


def _pool_mixer(z, w_pool, scale, dims):
    t = z.shape[0]
    p_rows, lc, _ = dims
    ch = scale.shape[0]
    tc = _divisor_tile(lc, 256)
    return pl.pallas_call(
        functools.partial(_pool_kernel, tc=tc, p_rows=p_rows, lc=lc),
        grid=(t // tc,),
        in_specs=_halo_specs(tc, ch, Z_POOL // ch, t) + [
            pl.BlockSpec(w_pool.shape, lambda i: (0, 0, 0)),
            pl.BlockSpec((1, ch), lambda i: (0, 0)),
        ],
        out_specs=pl.BlockSpec((tc, ch), lambda i: (i, 0)),
        out_shape=jax.ShapeDtypeStruct((t, ch), BF),
        scratch_shapes=[pltpu.VMEM((tc + 2 * HALO, ch), F32)],
        compiler_params=_cparams(("parallel",)),
        name="pool_mixer",
    )(z, z, z, w_pool, scale.reshape(1, ch))


def _merge_kernel(y0, y1, y2, y3, g0, g1, g2, g3, w_ref, o_ref, w_s):
    @pl.when(pl.program_id(1) == 0)
    def _():
        w_s[...] = w_ref[0].astype(BF)

    acc = None
    for n, (y_ref, g_ref) in enumerate(((y0, g0), (y1, g1), (y2, g2), (y3, g3))):
        proj = jnp.dot(y_ref[...], w_s[n], preferred_element_type=F32)
        gate = 0.5 * jnp.tanh(0.5 * g_ref[...].astype(F32)) + 0.5
        term = gate * proj
        acc = term if acc is None else acc + term
    o_ref[...] = acc.astype(o_ref.dtype)


def _merge(ys, z, w_branch, l):
    t, bw = ys[0].shape
    _, nb, _, d = w_branch.shape
    tm = _divisor_tile(t, 544)
    tn = _divisor_tile(d, 1024, 128)
    npn = d // tn

    def gate_spec(n):
        return pl.BlockSpec((tm, tn), lambda j, i: (i, Z_GATE // tn + n * npn + j))

    return pl.pallas_call(
        _merge_kernel,
        grid=(npn, t // tm),
        in_specs=[pl.BlockSpec((tm, bw), lambda j, i: (i, 0))] * nb + [gate_spec(n) for n in range(nb)] + [
            pl.BlockSpec((1, nb, bw, tn), lambda j, i: (l, 0, 0, j))],
        out_specs=pl.BlockSpec((tm, tn), lambda j, i: (i, j)),
        out_shape=jax.ShapeDtypeStruct((t, d), BF),
        scratch_shapes=[pltpu.VMEM((nb, bw, tn), BF)],
        compiler_params=_cparams(("parallel", "arbitrary")),
        name="branch_merge",
    )(*ys, z, z, z, z, w_branch)


def _residual_epilogue(y, x_ref, gpost_ref, gate_ref, gnext_ref, sh_ref, sc_ref, xo_ref, ho_ref, b, n_batch, cm,
                       rows=slice(None)):
    cm = cm[rows]
    gate = _row_vec(gate_ref, b, n_batch, cm)
    x_new = x_ref[rows, :] + gate * _rms(y, gpost_ref[...])
    xo_ref[rows, :] = x_new
    if ho_ref is None:
        return None
    shift = _row_vec(sh_ref, b, n_batch, cm)
    scale = _row_vec(sc_ref, b, n_batch, cm)
    h = _rms(x_new, gnext_ref[...]) * (1.0 + scale) + shift
    ho_ref[rows, :] = h.astype(ho_ref.dtype)
    return h


def _route_top2(logits, n_experts):
    lane = lax.broadcasted_iota(jnp.int32, logits.shape, 1)
    lg = jnp.where(lane < n_experts, logits, NEG_BIG)
    m1 = jnp.max(lg, axis=-1, keepdims=True)
    i1 = jnp.min(jnp.where(lg == m1, lane, 1 << 20), axis=-1, keepdims=True)
    lg2 = jnp.where(lane == i1, NEG_BIG, lg)
    m2 = jnp.max(lg2, axis=-1, keepdims=True)
    i2 = jnp.min(jnp.where(lg2 == m2, lane, 1 << 20), axis=-1, keepdims=True)
    e2 = jnp.exp(m2 - m1)
    w1 = 1.0 / (1.0 + e2)
    w2 = e2 / (1.0 + e2)
    out = jnp.where(lane == 0, i1.astype(F32), 0.0)
    out = jnp.where(lane == 1, i2.astype(F32), out)
    out = jnp.where(lane == 2, w1, out)
    return jnp.where(lane == 3, w2, out)


def _router_logits(h, wr_ref):
    h_hi = h.astype(BF)
    h_lo = (h - h_hi.astype(F32)).astype(BF)
    return (jnp.dot(h_hi, wr_ref[0], preferred_element_type=F32) + jnp.dot(h_hi, wr_ref[1], preferred_element_type=F32)
            + jnp.dot(h_lo, wr_ref[0], preferred_element_type=F32))


def _out_proj_kernel(*refs, tm, p_rows, lc, n_batch, emit_h, n_experts):
    m_ref, w_ref, x_ref, gpost_ref, gate_ref = refs[:5]
    pos = 5
    gnext_ref = sh_ref = sc_ref = wr_ref = None
    if emit_h:
        gnext_ref, sh_ref, sc_ref = refs[pos:pos + 3]
        pos += 3
    if n_experts:
        wr_ref = refs[pos]
        pos += 1
    xo_ref = refs[pos]
    pos += 1
    ho_ref = route_ref = None
    if emit_h:
        ho_ref = refs[pos]
        pos += 1
    if n_experts:
        route_ref = refs[pos]
        pos += 1
    b, cm = _tile_rows(pl.program_id(0), tm, p_rows, lc)
    n_split = EPILOGUE_SPLIT if tm % (EPILOGUE_SPLIT * 16) == 0 else 1
    rs = tm // n_split
    groups = [slice(s * rs, (s + 1) * rs) for s in range(n_split)]
    ys = [jnp.dot(m_ref[rows, :], w_ref[0], preferred_element_type=F32) for rows in groups]
    for rows, y in zip(groups, ys):
        h = _residual_epilogue(y, x_ref, gpost_ref, gate_ref, gnext_ref, sh_ref, sc_ref, xo_ref, ho_ref, b, n_batch,
                               cm, rows)
        if n_experts:
            route_ref[rows, :] = _route_top2(_router_logits(h, wr_ref), n_experts)


def _out_proj(m, w_stack, lw, xs, g_post, mods, l, j_gate, nxt, dims, router=None, name="out_proj"):
    t, d = xs.shape
    k = m.shape[1]
    p_rows, lc, n_batch = dims
    tm = _divisor_tile(p_rows, 544 if k <= d else 272)
    const = lambda i: (0, 0)
    rowtile = pl.BlockSpec((tm, d), lambda i: (i, 0))
    args = [m, w_stack, xs, g_post.reshape(1, d), mods]
    in_specs = [pl.BlockSpec((tm, k), lambda i: (i, 0)),
                pl.BlockSpec((1, k, d), lambda i: (lw, 0, 0), pipeline_mode=pl.Buffered(1)),
                rowtile, pl.BlockSpec((1, d), const), _mod_spec(l, j_gate, d)]
    out_shape = [jax.ShapeDtypeStruct((t, d), F32)]
    out_specs = [rowtile]
    n_experts = 0
    if nxt is not None:
        g_next, l_next, j_shift, j_scale = nxt
        args += [g_next.reshape(1, d), mods, mods]
        in_specs += [pl.BlockSpec((1, d), const), _mod_spec(l_next, j_shift, d), _mod_spec(l_next, j_scale, d)]
        out_shape.append(jax.ShapeDtypeStruct((t, d), BF if router is None else F32))
        out_specs.append(rowtile)
    if router is not None:
        wr, n_experts = router
        args.append(wr)
        in_specs.append(pl.BlockSpec(wr.shape, lambda i: (0, 0, 0)))
        out_shape.append(jax.ShapeDtypeStruct((t, 128), F32))
        out_specs.append(pl.BlockSpec((tm, 128), lambda i: (i, 0)))
    return pl.pallas_call(
        functools.partial(_out_proj_kernel, tm=tm, p_rows=p_rows, lc=lc, n_batch=n_batch,
                          emit_h=nxt is not None, n_experts=n_experts),
        grid=(t // tm,),
        in_specs=in_specs,
        out_specs=out_specs,
        out_shape=out_shape,
        compiler_params=_cparams(("parallel",)),
        name=name,
    )(*args)


def _swiglu_kernel(x_ref, w1_ref, w3_ref, o_ref, w1_s, w3_s):
    @pl.when(pl.program_id(1) == 0)
    def _():
        w1_s[...] = w1_ref[0].astype(BF)
        w3_s[...] = w3_ref[0].astype(BF)

    x = x_ref[...]
    a = jnp.dot(x, w1_s[...], preferred_element_type=F32)
    g = jnp.dot(x, w3_s[...], preferred_element_type=F32)
    o_ref[...] = (_silu(a) * g).astype(o_ref.dtype)


def _swiglu_up(h, w1, w3, jl):
    t, d = h.shape
    dff = w1.shape[2]
    tm = _divisor_tile(t, 1088)
    tn = _divisor_tile(dff, 512, 128)
    return pl.pallas_call(
        _swiglu_kernel,
        grid=(dff // tn, t // tm),
        in_specs=[
            pl.BlockSpec((tm, d), lambda j, i: (i, 0)),
            pl.BlockSpec((1, d, tn), lambda j, i: (jl, 0, j)),
            pl.BlockSpec((1, d, tn), lambda j, i: (jl, 0, j)),
        ],
        out_specs=pl.BlockSpec((tm, tn), lambda j, i: (i, j)),
        out_shape=jax.ShapeDtypeStruct((t, dff), BF),
        scratch_shapes=[pltpu.VMEM((d, tn), BF), pltpu.VMEM((d, tn), BF)],
        compiler_params=_cparams(("parallel", "arbitrary")),
        name="swiglu_up",
    )(h, w1, w3)


def _gather_rows_kernel(idx_ref, src_ref, o_ref, *scratch, tr):
    base = pl.program_id(0) * tr
    dst, sem = (o_ref, scratch[0]) if len(scratch) == 1 else scratch

    def copy(r):
        return pltpu.make_async_copy(src_ref.at[pl.ds(idx_ref[base + r], 1), :], dst.at[pl.ds(r, 1), :], sem)

    def issue(g, carry):
        for u in range(GATHER_UNROLL):
            copy(g * GATHER_UNROLL + u).start(priority=u % 2)
        return carry

    lax.fori_loop(0, tr // GATHER_UNROLL, issue, 0)
    pltpu.make_async_copy(src_ref.at[pl.ds(0, tr), :], dst.at[pl.ds(0, tr), :], sem).wait()
    if dst is not o_ref:
        o_ref[...] = dst[...].astype(o_ref.dtype)


def _gather_rows(src, idx, out_dtype):
    n, d = src.shape
    assert src.dtype == F32
    rows = idx.shape[0]
    tr = _divisor_tile(rows, MOE_GATHER_ROWS, 8)
    staging = [] if out_dtype == src.dtype else [pltpu.VMEM((tr, d), src.dtype)]
    return pl.pallas_call(
        functools.partial(_gather_rows_kernel, tr=tr),
        grid_spec=pltpu.PrefetchScalarGridSpec(
            num_scalar_prefetch=1,
            grid=(rows // tr,),
            in_specs=[pl.BlockSpec(memory_space=pl.ANY)],
            out_specs=pl.BlockSpec((tr, d), lambda i, idx_ref: (i, 0)),
            scratch_shapes=staging + [pltpu.SemaphoreType.DMA(())],
        ),
        out_shape=jax.ShapeDtypeStruct((rows, d), out_dtype),
        compiler_params=_cparams(("arbitrary",)),
        name="moe_gather",
    )(idx, src)


def _expert_rows_loop(st_ref, nt_ref, tr, load_copy, store_copy, compute):
    e, n = pl.program_id(0), pl.program_id(1)
    n_exp, n_col = pl.num_programs(0), pl.num_programs(1)
    start, n_tiles = st_ref[e], nt_ref[e]
    first_step = jnp.logical_and(e == 0, n == 0)
    last_step = jnp.logical_and(e == n_exp - 1, n == n_col - 1)
    e_next = jnp.minimum(jnp.where(n == n_col - 1, e + 1, e), n_exp - 1)

    @pl.when(jnp.logical_and(first_step, n_tiles > 0))
    def _():
        load_copy(start, 0, 0).start(priority=1)

    @pl.when(n_tiles > 1)
    def _():
        load_copy(start, 1, 1).start(priority=1)

    def body(i, carry):
        load_copy(start, i, i % MOE_LOAD_SLOTS).wait()

        @pl.when(i + 2 < n_tiles)
        def _():
            load_copy(start, i + 2, (i + 2) % MOE_LOAD_SLOTS).start(priority=1)

        @pl.when(i >= 2)
        def _():
            store_copy(start, i - 2, i % 2).wait()

        compute(i % MOE_LOAD_SLOTS, i % 2)
        store_copy(start, i, i % 2).start(priority=1)
        return carry

    lax.fori_loop(0, n_tiles, body, 0)

    @pl.when(jnp.logical_and(jnp.logical_not(last_step), nt_ref[e_next] > 0))
    def _():
        load_copy(st_ref[e_next], 0, 0).start(priority=1)

    @pl.when(n_tiles >= 2)
    def _():
        store_copy(start, n_tiles - 2, n_tiles % 2).wait()

    @pl.when(n_tiles >= 1)
    def _():
        store_copy(start, n_tiles - 1, (n_tiles - 1) % 2).wait()


def _zero_tail_rows(out_hbm, zbuf, sem, first_row, col0, tn):
    zbuf[...] = jnp.zeros_like(zbuf)
    n_chunks = (out_hbm.shape[0] - first_row) // MOE_ROW_ALIGN

    def copy(c):
        r0 = pl.multiple_of(first_row + c * MOE_ROW_ALIGN, MOE_ROW_ALIGN)
        return pltpu.make_async_copy(zbuf, out_hbm.at[pl.ds(r0, MOE_ROW_ALIGN), pl.ds(col0, tn)], sem)

    def issue(c, carry):
        copy(c).start()
        return carry

    def drain(c, carry):
        copy(c).wait()
        return carry

    lax.fori_loop(0, n_chunks, issue, 0)
    lax.fori_loop(0, n_chunks, drain, 0)


def _moe_up_kernel(st_ref, nt_ref, x_hbm, w1_ref, w3_ref, a_hbm, w1_s, w3_s, xbuf, obuf, zbuf, sem_in, sem_out, sem_z,
                   *, tr, tn):
    e, n = pl.program_id(0), pl.program_id(1)
    w1_s[...] = w1_ref[0, 0].astype(BF)
    w3_s[...] = w3_ref[0, 0].astype(BF)
    col0 = pl.multiple_of(n * tn, tn)

    def rows(start, i):
        return pl.ds(pl.multiple_of(start + i * tr, MOE_ROW_ALIGN), tr)

    def load_copy(start, i, slot):
        return pltpu.make_async_copy(x_hbm.at[rows(start, i)], xbuf.at[slot], sem_in.at[slot])

    def store_copy(start, i, slot):
        return pltpu.make_async_copy(obuf.at[slot], a_hbm.at[rows(start, i), pl.ds(col0, tn)], sem_out.at[slot])

    def compute(in_slot, out_slot):
        x = xbuf[in_slot]
        a = jnp.dot(x, w1_s[...], preferred_element_type=F32)
        g = jnp.dot(x, w3_s[...], preferred_element_type=F32)
        obuf[out_slot] = (_silu(a) * g).astype(obuf.dtype)

    _expert_rows_loop(st_ref, nt_ref, tr, load_copy, store_copy, compute)

    @pl.when(e == pl.num_programs(0) - 1)
    def _():
        _zero_tail_rows(a_hbm, zbuf, sem_z, st_ref[e] + nt_ref[e] * tr, col0, tn)


def _moe_up(xs, w1, w3, jl, starts, n_tiles, tr):
    rows, d = xs.shape
    n_exp, dff = w1.shape[1], w1.shape[3]
    tn = _divisor_tile(dff, 512, 128)
    return pl.pallas_call(
        functools.partial(_moe_up_kernel, tr=tr, tn=tn),
        grid_spec=pltpu.PrefetchScalarGridSpec(
            num_scalar_prefetch=2,
            grid=(n_exp, dff // tn),
            in_specs=[
                pl.BlockSpec(memory_space=pl.ANY),
                pl.BlockSpec((1, 1, d, tn), lambda e, n, st, nt: (jl, e, 0, n)),
                pl.BlockSpec((1, 1, d, tn), lambda e, n, st, nt: (jl, e, 0, n)),
            ],
            out_specs=pl.BlockSpec(memory_space=pl.ANY),
            scratch_shapes=[pltpu.VMEM((d, tn), BF), pltpu.VMEM((d, tn), BF),
                            pltpu.VMEM((MOE_LOAD_SLOTS, tr, d), BF), pltpu.VMEM((2, tr, tn), BF),
                            pltpu.VMEM((MOE_ROW_ALIGN, tn), BF),
                            pltpu.SemaphoreType.DMA((MOE_LOAD_SLOTS,)), pltpu.SemaphoreType.DMA((2,)),
                            pltpu.SemaphoreType.DMA(())],
        ),
        out_shape=jax.ShapeDtypeStruct((rows, dff), BF),
        compiler_params=_cparams(("arbitrary", "arbitrary")),
        name="moe_up",
    )(starts, n_tiles, xs, w1, w3)


def _moe_down_kernel(st_ref, nt_ref, a_hbm, w_ref, y_hbm, w_s, abuf, obuf, zbuf, sem_in, sem_out, sem_z, *, tr, tn):
    e, n = pl.program_id(0), pl.program_id(1)
    w_s[...] = w_ref[0, 0].astype(BF)
    col0 = pl.multiple_of(n * tn, tn)

    def rows(start, i):
        return pl.ds(pl.multiple_of(start + i * tr, MOE_ROW_ALIGN), tr)

    def load_copy(start, i, slot):
        return pltpu.make_async_copy(a_hbm.at[rows(start, i)], abuf.at[slot], sem_in.at[slot])

    def store_copy(start, i, slot):
        return pltpu.make_async_copy(obuf.at[slot], y_hbm.at[rows(start, i), pl.ds(col0, tn)], sem_out.at[slot])

    def compute(in_slot, out_slot):
        obuf[out_slot] = jnp.dot(abuf[in_slot], w_s[...], preferred_element_type=F32)

    _expert_rows_loop(st_ref, nt_ref, tr, load_copy, store_copy, compute)

    @pl.when(e == pl.num_programs(0) - 1)
    def _():
        _zero_tail_rows(y_hbm, zbuf, sem_z, st_ref[e] + nt_ref[e] * tr, col0, tn)


def _moe_down(act, w2, jl, starts, n_tiles, tr):
    rows, dff = act.shape
    n_exp, d = w2.shape[1], w2.shape[3]
    tn = _divisor_tile(d, 512, 128)
    return pl.pallas_call(
        functools.partial(_moe_down_kernel, tr=tr, tn=tn),
        grid_spec=pltpu.PrefetchScalarGridSpec(
            num_scalar_prefetch=2,
            grid=(n_exp, d // tn),
            in_specs=[
                pl.BlockSpec(memory_space=pl.ANY),
                pl.BlockSpec((1, 1, dff, tn), lambda e, n, st, nt: (jl, e, 0, n)),
            ],
            out_specs=pl.BlockSpec(memory_space=pl.ANY),
            scratch_shapes=[pltpu.VMEM((dff, tn), BF), pltpu.VMEM((MOE_LOAD_SLOTS, tr, dff), BF),
                            pltpu.VMEM((2, tr, tn), F32), pltpu.VMEM((MOE_ROW_ALIGN, tn), F32),
                            pltpu.SemaphoreType.DMA((MOE_LOAD_SLOTS,)), pltpu.SemaphoreType.DMA((2,)),
                            pltpu.SemaphoreType.DMA(())],
        ),
        out_shape=jax.ShapeDtypeStruct((rows, d), F32),
        compiler_params=_cparams(("arbitrary", "arbitrary")),
        name="moe_down",
    )(starts, n_tiles, act, w2)


def _combine_kernel(*refs, tm, p_rows, lc, n_batch, emit_h):
    ya_ref, yb_ref, rt_ref, x_ref, gpost_ref, gate_ref = refs[:6]
    gnext_ref = sh_ref = sc_ref = ho_ref = None
    if emit_h:
        gnext_ref, sh_ref, sc_ref, xo_ref, ho_ref = refs[6:]
    else:
        (xo_ref,) = refs[6:]
    b, cm = _tile_rows(pl.program_id(0), tm, p_rows, lc)
    rt = rt_ref[...]
    y = rt[:, 2:3] * ya_ref[...] + rt[:, 3:4] * yb_ref[...]
    _residual_epilogue(y, x_ref, gpost_ref, gate_ref, gnext_ref, sh_ref, sc_ref, xo_ref, ho_ref, b, n_batch, cm)


def _moe_combine(y2, route, xs, g_post, mods, l, j_gate, nxt, dims):
    t, d = xs.shape
    p_rows, lc, n_batch = dims
    tm = _divisor_tile(p_rows, 544)
    nt = t // tm
    const = lambda i: (0, 0)
    rowtile = pl.BlockSpec((tm, d), lambda i: (i, 0))
    args = [y2, y2, route, xs, g_post.reshape(1, d), mods]
    in_specs = [rowtile, pl.BlockSpec((tm, d), lambda i: (i + nt, 0)), pl.BlockSpec((tm, 128), lambda i: (i, 0)),
                rowtile, pl.BlockSpec((1, d), const), _mod_spec(l, j_gate, d)]
    out_shape = [jax.ShapeDtypeStruct((t, d), F32)]
    out_specs = [rowtile]
    if nxt is not None:
        g_next, l_next, j_shift, j_scale = nxt
        args += [g_next.reshape(1, d), mods, mods]
        in_specs += [pl.BlockSpec((1, d), const), _mod_spec(l_next, j_shift, d), _mod_spec(l_next, j_scale, d)]
        out_shape.append(jax.ShapeDtypeStruct((t, d), BF))
        out_specs.append(rowtile)
    return pl.pallas_call(
        functools.partial(_combine_kernel, tm=tm, p_rows=p_rows, lc=lc, n_batch=n_batch, emit_h=nxt is not None),
        grid=(nt,),
        in_specs=in_specs,
        out_specs=out_specs,
        out_shape=out_shape,
        compiler_params=_cparams(("parallel",)),
        name="moe_combine",
    )(*args)


def _moe_ffn(h, route, w1, w3, w2, jl, xs, g_post, mods, l, nxt, dims):
    t, d = h.shape
    n_exp = w1.shape[1]
    tr = MOE_ROW_TILE
    expert = jnp.concatenate([route[:, 0], route[:, 1]]).astype(jnp.int32)
    token = jnp.concatenate([jnp.arange(t, dtype=jnp.int32)] * 2)
    onehot = (expert[:, None] == jnp.arange(n_exp, dtype=jnp.int32)[None, :]).astype(jnp.int32)
    rank = jnp.sum((jnp.cumsum(onehot, axis=0) - onehot) * onehot, axis=1)
    counts = jnp.sum(onehot, axis=0)
    padded = ((counts + MOE_ROW_ALIGN - 1) // MOE_ROW_ALIGN) * MOE_ROW_ALIGN
    ends = jnp.cumsum(padded)
    starts = (ends - padded).astype(jnp.int32)
    n_tiles = ((counts + tr - 1) // tr).astype(jnp.int32)
    dest = starts[expert] + rank
    rows = 2 * t + n_exp * MOE_ROW_ALIGN + tr
    rows = ((rows + MOE_GATHER_ROWS - 1) // MOE_GATHER_ROWS) * MOE_GATHER_ROWS
    row_token = jnp.zeros((rows,), jnp.int32).at[dest].set(token)

    xs_sorted = _gather_rows(h, row_token, BF)
    act = _moe_up(xs_sorted, w1, w3, jl, starts, n_tiles, tr)
    y_sorted = _moe_down(act, w2, jl, starts, n_tiles, tr)
    y2 = _gather_rows(y_sorted, dest, F32)
    return _moe_combine(y2, route, xs, g_post, mods, l, 5, nxt, dims)


def _pad_heads(w, n_heads, width):
    r = w.shape[0]
    w3 = w.reshape(r, n_heads, width)
    return jnp.pad(w3, ((0, 0), (0, 0), (0, HEAD_PAD - width))).reshape(r, n_heads * HEAD_PAD)


def _rope_tables(l_lat, lc):
    tt = jnp.arange(l_lat)
    row = (tt // GRID_W).astype(F32)
    col = (tt % GRID_W).astype(F32)
    axis_dim = MLA_ROPE // 2
    inv_freq = 1.0 / (ROPE_THETA ** (jnp.arange(0, axis_dim, 2, dtype=F32) / axis_dim))
    ar, ac = row[:, None] * inv_freq, col[:, None] * inv_freq
    ang = jnp.concatenate([ar, ar, ac, ac], axis=-1)
    cos = jnp.concatenate([jnp.ones((lc, MLA_ROPE), F32), jnp.cos(ang)], axis=0)
    sin = jnp.concatenate([jnp.zeros((lc, MLA_ROPE), F32), jnp.sin(ang)], axis=0)
    padc = jnp.zeros((lc + l_lat, 128 - MLA_ROPE), F32)
    pm = np.zeros((128, 128), np.float32)
    q4 = MLA_ROPE // 4
    for base in (0, 2 * q4):
        for i in range(q4):
            pm[base + q4 + i, base + i] = -1.0
            pm[base + i, base + q4 + i] = 1.0
    return jnp.concatenate([cos, padc], axis=1), jnp.concatenate([sin, padc], axis=1), jnp.asarray(pm, BF)


def kernel(x, c, ctx, c_ctx, w_mod, b_mod, g_mix_pre, g_mix_post, g_ffn_pre, g_ffn_post, w_in, mla_g_q, mla_g_kv, mla_w_uq, mla_w_ukv, hg_lb_logits, hg_g_norm, conv_w, conv_b, conv_ln_g, conv_ln_b, pool_w, pool_scale, w_branch, w_out, ffn_w1, ffn_w3, ffn_w2, moe_router, moe_w1, moe_w3, moe_w2):
    n_batch, l_lat, d = x.shape
    lc = ctx.shape[1]
    depth = w_in.shape[0]
    p_rows = lc + l_lat
    t = n_batch * p_rows
    dims = (p_rows, lc, n_batch)
    assert n_batch + 1 <= 8 and lc % HG_CHUNK == 0 and l_lat % HG_CHUNK == 0 and l_lat % GRID_W == 0
    n_heads = mla_w_uq.shape[2] // MLA_QK

    xs = jnp.concatenate([ctx, x], axis=1).reshape(t, d)
    cvec = jnp.zeros((8, d), F32).at[:n_batch].set(c).at[n_batch].set(c_ctx)
    mods = _modvecs(cvec, w_mod, b_mod)

    lb = jnp.cumsum(jax.nn.softmax(hg_lb_logits.astype(F32), axis=1), axis=1)
    lb = lb - lb[:, :1]
    cos, sin, pm = _rope_tables(l_lat, lc)

    w_out_bf = w_out.astype(BF)
    ffn_w2_bf = ffn_w2.astype(BF)

    h = _modulate(xs, g_mix_pre[0], mods, 0, 0, 1, dims)
    for l in range(depth):
        last = l == depth - 1
        z = _in_proj(h, w_in, l, BF)
        q3, k3, v3 = _mla_proj(z, mla_g_q[l], mla_g_kv[l], _pad_heads(mla_w_uq[l], n_heads, MLA_QK),
                               mla_w_ukv[l], cos, sin, pm, dims)
        y_mla = _attention(q3, k3, v3, dims)
        y_hg = _hgrn_readout(*_hgrn(z, lb[:, l], dims), z, hg_g_norm[l])
        y_conv = _conv_mixer(z, conv_w[l], conv_b[l], conv_ln_g[l], conv_ln_b[l], dims)
        y_pool = _pool_mixer(z, pool_w[l], pool_scale[l], dims)
        m = _merge([y_mla, y_hg, y_conv, y_pool], z, w_branch, l)
        j = l // 2
        nxt_ffn = (g_ffn_pre[l], l, 3, 4)
        nxt_mix = None if last else (g_mix_pre[l + 1], l + 1, 0, 1)
        if l % 2 == 0:
            xs, h2 = _out_proj(m, w_out_bf, l, xs, g_mix_post[l], mods, l, 2, nxt_ffn, dims)
            act = _swiglu_up(h2, ffn_w1, ffn_w3, j)
            res = _out_proj(act, ffn_w2_bf, j, xs, g_ffn_post[l], mods, l, 5, nxt_mix, dims, name="ffn_down")
        else:
            n_exp = moe_router.shape[2]
            wr = jnp.pad(moe_router[j], ((0, 0), (0, 128 - n_exp)))
            wr_hi = wr.astype(BF)
            wr2 = jnp.stack([wr_hi, (wr - wr_hi.astype(F32)).astype(BF)])
            xs, h2, route = _out_proj(m, w_out_bf, l, xs, g_mix_post[l], mods, l, 2, nxt_ffn, dims,
                                      router=(wr2, n_exp))
            res = _moe_ffn(h2, route, moe_w1, moe_w3, moe_w2, j, xs, g_ffn_post[l], mods, l, nxt_mix, dims)
        if last:
            (xs,) = res
        else:
            xs, h = res
    return xs.reshape(n_batch, p_rows, d)[:, lc:, :]
```

```python
import functools

import numpy as np
import jax
import jax.numpy as jnp
from jax import lax
from jax.experimental import pallas as pl
from jax.experimental.pallas import tpu as pltpu

F32 = jnp.float32
BF = jnp.bfloat16

EPS = 1e-6
GRID_W = 64
ROPE_THETA = 10000.0
N_BRANCH = 4
MLA_NOPE, MLA_ROPE, MLA_V = 128, 64, 128
MLA_QK = MLA_NOPE + MLA_ROPE
HEAD_PAD = 256
HG_DK = 128
CONV_WIDTH = 31
POOL_WINDOWS = (2, 4, 8, 16)
POOL_GROUP = 128
LANES = 128
IN_HALF = 512
IN_GROUP = 64
SUBLANES = 8
HALO = 16
HG_CHUNK = 64
HG_EXP_CLAMP = 80.0
NEG_BIG = -1e30
ATTN_Q_TILE = 512
MOE_ROW_TILE = 256
GATHER_UNROLL = 8
EPILOGUE_SPLIT = 2
MOE_LOAD_SLOTS = 3
MOE_ROW_ALIGN = 16
MOE_GATHER_ROWS = 1024

VMEM_LIMIT_V7X = 56 * 1024 * 1024

Z_CONV, Z_HQ, Z_HI, Z_HFF, Z_HFB, Z_HG, Z_POOL, Z_MLA, Z_GATE = 0, 1024, 1536, 2048, 2560, 3072, 3584, 4096, 5120
MLA_ZKV_OFF, MLA_ZR_OFF = 384, 640


def _cparams(sem, vmem=VMEM_LIMIT_V7X):
    return pltpu.CompilerParams(dimension_semantics=sem, vmem_limit_bytes=vmem)


def _divisor_tile(n, target, mult=16):
    best = None
    for t in range(mult, min(n, target) + 1, mult):
        if n % t == 0:
            best = t
    assert best is not None, (n, target, mult)
    return best


def _pick_row(m, r):
    rows = lax.broadcasted_iota(jnp.int32, m.shape, 0)
    return jnp.sum(jnp.where(rows == r, m, 0.0), axis=0, keepdims=True)


def _row_vec(m_ref, b, n_batch, ctx_mask):
    m = m_ref[0, 0]
    return jnp.where(ctx_mask, _pick_row(m, n_batch), _pick_row(m, b))


def _tile_rows(i, tm, p_rows, lc):
    per_batch = p_rows // tm
    b = i // per_batch
    pos = (i % per_batch) * tm + lax.broadcasted_iota(jnp.int32, (tm, 1), 0)
    return b, pos < lc


def _rms(x, g):
    return x * lax.rsqrt(jnp.mean(x * x, axis=-1, keepdims=True) + EPS) * g


def _sigmoid(x):
    return 1.0 / (1.0 + jnp.exp(-x))


def _silu(x):
    return x * _sigmoid(x)


def _modvec_kernel(c_ref, w_ref, b_ref, o_ref):
    s = _silu(c_ref[...])
    s_hi = s.astype(BF)
    s_lo = (s - s_hi.astype(F32)).astype(BF)
    w = w_ref[0].astype(BF)
    acc = jnp.dot(s_hi, w, preferred_element_type=F32) + jnp.dot(s_lo, w, preferred_element_type=F32)
    o_ref[0, 0] = acc + b_ref[0, 0]


def _modvecs(cvec, w_mod, b_mod):
    depth, d, n6 = w_mod.shape
    nj = n6 // d
    tn = _divisor_tile(d, 1024, 128)
    npj = d // tn
    return pl.pallas_call(
        _modvec_kernel,
        grid=(depth, nj, npj),
        in_specs=[
            pl.BlockSpec((8, d), lambda l, j, n: (0, 0)),
            pl.BlockSpec((1, d, tn), lambda l, j, n: (l, 0, j * npj + n)),
            pl.BlockSpec((1, 1, 1, tn), lambda l, j, n: (l, j, 0, n)),
        ],
        out_specs=pl.BlockSpec((1, 1, 8, tn), lambda l, j, n: (l, j, 0, n)),
        out_shape=jax.ShapeDtypeStruct((depth, nj, 8, d), F32),
        compiler_params=_cparams(("parallel", "parallel", "parallel")),
        name="modvec",
    )(cvec, w_mod, b_mod.reshape(depth, nj, 1, d))


def _modulate_kernel(x_ref, g_ref, sh_ref, sc_ref, h_ref, *, tm, p_rows, lc, n_batch):
    b, cm = _tile_rows(pl.program_id(0), tm, p_rows, lc)
    shift = _row_vec(sh_ref, b, n_batch, cm)
    scale = _row_vec(sc_ref, b, n_batch, cm)
    h_ref[...] = (_rms(x_ref[...], g_ref[...]) * (1.0 + scale) + shift).astype(h_ref.dtype)


def _mod_spec(l, j, d):
    return pl.BlockSpec((1, 1, 8, d), lambda *_: (l, j, 0, 0))


def _modulate(xs, g, mods, l, j_shift, j_scale, dims):
    t, d = xs.shape
    p_rows, lc, n_batch = dims
    tm = _divisor_tile(p_rows, 544)
    return pl.pallas_call(
        functools.partial(_modulate_kernel, tm=tm, p_rows=p_rows, lc=lc, n_batch=n_batch),
        grid=(t // tm,),
        in_specs=[
            pl.BlockSpec((tm, d), lambda i: (i, 0)),
            pl.BlockSpec((1, d), lambda i: (0, 0)),
            _mod_spec(l, j_shift, d),
            _mod_spec(l, j_scale, d),
        ],
        out_specs=pl.BlockSpec((tm, d), lambda i: (i, 0)),
        out_shape=jax.ShapeDtypeStruct((t, d), BF),
        compiler_params=_cparams(("parallel",)),
        name="modulate",
    )(xs, g.reshape(1, d), mods, mods)


def _in_proj_kernel(grp_ref, cnt_ref, x_ref, *refs):
    n_blk = len(refs) - 2
    w_refs, o_ref, w_s = refs[:n_blk], refs[n_blk], refs[n_blk + 1]
    j = pl.program_id(0)
    per_half = n_blk // 2

    @pl.when(pl.program_id(1) == 0)
    def _():
        for u in range(2):
            cnt = cnt_ref[2 * j + u]
            for i in range(per_half):
                r0 = (u * per_half + i) * IN_GROUP
                w_s[r0:r0 + IN_GROUP, :] = jnp.where(i < cnt, w_refs[u * per_half + i][0], 0.0).astype(BF)

    o_ref[...] = lax.dot_general(x_ref[...], w_s[...], (((1,), (1,)), ((), ())),
                                 preferred_element_type=F32).astype(o_ref.dtype)


def _in_proj_tables(d):
    q_rank, kv_rank = (3 * d) // 16, d // 8
    bw = d // N_BRANCH
    src, o = {}, 0
    for name, width in (("mla", q_rank + kv_rank + MLA_ROPE), ("hq", bw), ("hi", bw), ("hff", bw), ("hfb", bw),
                        ("hg", bw), ("conv", 2 * bw), ("pool", bw), ("gate", N_BRANCH * d)):
        src[name] = (o, width)
        o += width
    dst = (("conv", Z_CONV), ("hq", Z_HQ), ("hi", Z_HI), ("hff", Z_HFF), ("hfb", Z_HFB), ("hg", Z_HG),
           ("pool", Z_POOL), ("mla", Z_MLA), ("gate", Z_GATE))
    n_half = (Z_GATE + N_BRANCH * d) // IN_HALF
    grp, cnt = [0] * n_half, [0] * n_half
    for name, z0 in dst:
        s0, width = src[name]
        assert z0 % IN_HALF == 0 and s0 % IN_GROUP == 0 and width % IN_GROUP == 0
        for k in range(-(-width // IN_HALF)):
            grp[z0 // IN_HALF + k] = (s0 + k * IN_HALF) // IN_GROUP
            cnt[z0 // IN_HALF + k] = min(IN_HALF, width - k * IN_HALF) // IN_GROUP
    as_i32 = lambda v: jnp.asarray(np.array(v, np.int32))
    return as_i32(grp), as_i32(cnt)


def _in_proj(x, w_in_t, l, out_dtype, tm_target=1088):
    t, k = x.shape
    n = Z_GATE + N_BRANCH * k
    tn = 2 * IN_HALF
    per_half = IN_HALF // IN_GROUP
    tm = _divisor_tile(t, tm_target)
    tables = _in_proj_tables(k)
    last_group = w_in_t.shape[1] // IN_GROUP - 1

    def w_spec(u, i):
        return pl.BlockSpec((1, IN_GROUP, k),
                            lambda j, m, grp, cnt: (l, jnp.minimum(grp[2 * j + u] + i, last_group), 0))

    return pl.pallas_call(
        _in_proj_kernel,
        grid_spec=pltpu.PrefetchScalarGridSpec(
            num_scalar_prefetch=2,
            grid=(n // tn, t // tm),
            in_specs=[pl.BlockSpec((tm, k), lambda j, m, *_: (m, 0))]
            + [w_spec(u, i) for u in range(2) for i in range(per_half)],
            out_specs=pl.BlockSpec((tm, tn), lambda j, m, *_: (m, j)),
            scratch_shapes=[pltpu.VMEM((tn, k), BF)],
        ),
        out_shape=jax.ShapeDtypeStruct((t, n), out_dtype),
        compiler_params=_cparams(("parallel", "arbitrary")),
        name="in_proj",
    )(*tables, x, *([w_in_t] * (2 * per_half)))


def _rope128(t, cos, sin, pm):
    partner = jnp.dot(t.astype(BF), pm, preferred_element_type=F32)
    return t * cos + partner * sin


def _mla_proj_kernel(z_ref, gq_ref, gkv_ref, wq_ref, wkv_ref, cos_ref, sin_ref, pm_ref,
                     q_ref, k_ref, v_ref, wq_s, wkv_s, *, n_heads, q_rank, kv_rank, scale):
    @pl.when(pl.program_id(0) == 0)
    def _():
        wq_s[...] = wq_ref[...].astype(BF)
        wkv_s[...] = wkv_ref[...].astype(BF)

    z = z_ref[...]
    zq = z[:, :q_rank].astype(F32)
    zkv = z[:, MLA_ZKV_OFF:MLA_ZKV_OFF + kv_rank].astype(F32)
    zr = z[:, MLA_ZR_OFF:MLA_ZR_OFF + 128].astype(F32)
    cos, sin, pm = cos_ref[...], sin_ref[...], pm_ref[...]
    qn = _rms(zq, gq_ref[...]).astype(BF)
    kvn = _rms(zkv, gkv_ref[...]).astype(BF)
    k_rope = _rope128(zr, cos, sin, pm)[:, :MLA_ROPE]
    for h in range(n_heads):
        qh = jnp.dot(qn, wq_s[:, h * HEAD_PAD:(h + 1) * HEAD_PAD], preferred_element_type=F32)
        q_rope = _rope128(qh[:, MLA_NOPE:], cos, sin, pm)[:, :MLA_ROPE]
        q_ref[h] = (jnp.concatenate([qh[:, :MLA_NOPE], q_rope], axis=-1) * scale).astype(q_ref.dtype)
        kvh = jnp.dot(kvn, wkv_s[:, h * HEAD_PAD:(h + 1) * HEAD_PAD], preferred_element_type=F32)
        k_ref[h] = jnp.concatenate([kvh[:, :MLA_NOPE], k_rope], axis=-1).astype(k_ref.dtype)
        v_ref[h] = jnp.concatenate([kvh[:, MLA_NOPE:], jnp.ones_like(kvh[:, MLA_NOPE:])], axis=-1).astype(v_ref.dtype)


def _mla_proj(z, g_q, g_kv, wq_pad, wkv, cos, sin, pm, dims):
    t = z.shape[0]
    p_rows = dims[0]
    q_rank, kv_rank = g_q.shape[0], g_kv.shape[0]
    n_heads = wkv.shape[1] // HEAD_PAD
    tm = _divisor_tile(p_rows, 544)
    npb = p_rows // tm
    scale = float(MLA_QK) ** -0.5 * float(np.log2(np.e))
    const = lambda i: (0, 0)
    return pl.pallas_call(
        functools.partial(_mla_proj_kernel, n_heads=n_heads, q_rank=q_rank, kv_rank=kv_rank, scale=scale),
        grid=(t // tm,),
        in_specs=[
            pl.BlockSpec((tm, 1024), lambda i: (i, Z_MLA // 1024)),
            pl.BlockSpec((1, q_rank), const),
            pl.BlockSpec((1, kv_rank), const),
            pl.BlockSpec(wq_pad.shape, const),
            pl.BlockSpec(wkv.shape, const),
            pl.BlockSpec((tm, 128), lambda i: (i % npb, 0)),
            pl.BlockSpec((tm, 128), lambda i: (i % npb, 0)),
            pl.BlockSpec((128, 128), const),
        ],
        out_specs=[
            pl.BlockSpec((n_heads, tm, MLA_QK), lambda i: (0, i, 0)),
            pl.BlockSpec((n_heads, tm, MLA_QK), lambda i: (0, i, 0)),
            pl.BlockSpec((n_heads, tm, 2 * MLA_V), lambda i: (0, i, 0)),
        ],
        out_shape=[
            jax.ShapeDtypeStruct((n_heads, t, MLA_QK), BF),
            jax.ShapeDtypeStruct((n_heads, t, MLA_QK), BF),
            jax.ShapeDtypeStruct((n_heads, t, 2 * MLA_V), BF),
        ],
        scratch_shapes=[pltpu.VMEM(wq_pad.shape, BF), pltpu.VMEM(wkv.shape, BF)],
        compiler_params=_cparams(("arbitrary",)),
        name="mla_proj",
    )(z, g_q.reshape(1, -1), g_kv.reshape(1, -1), wq_pad, wkv, cos, sin, pm)


def _attn_kernel(q_ref, k_ref, v_ref, o_ref, s0_ref, s1_ref, p0_ref, p1_ref, *, tq, lc):
    p_rows = k_ref.shape[1]
    n_tiles = (p_rows - lc) // tq
    nt = (((1,), (1,)), ((), ()))

    def tile_start(t):
        return pl.multiple_of(lc + t * tq, lc)

    def scores(t, s_ref):
        s_ref[...] = lax.dot_general(q_ref[0, pl.ds(tile_start(t), tq), :], k_ref[0], nt, preferred_element_type=F32)

    def softmax(s_ref, p_ref):
        s = s_ref[...]
        p_ref[...] = jnp.exp2((s - jnp.max(s, axis=-1, keepdims=True)).astype(BF))

    def emit(p, v, q0, n_q):
        o = jnp.dot(p, v, preferred_element_type=F32)
        o_ref[pl.ds(q0, n_q), :] = (o[:, :MLA_V] / o[:, MLA_V:MLA_V + 1]).astype(o_ref.dtype)

    def values(t, p_ref):
        emit(p_ref[...], v_ref[0], tile_start(t), tq)

    s_ctx = lax.dot_general(q_ref[0, :lc, :], k_ref[0, :lc, :], nt, preferred_element_type=F32)
    emit(jnp.exp2((s_ctx - jnp.max(s_ctx, axis=-1, keepdims=True)).astype(BF)), v_ref[0, :lc, :], 0, lc)

    scores(0, s0_ref)
    scores(1, s1_ref)
    softmax(s0_ref, p0_ref)

    def body(kk, carry):
        t = 2 * kk
        scores(t, s0_ref)
        softmax(s1_ref, p1_ref)
        values(t - 2, p0_ref)
        scores(t + 1, s1_ref)
        softmax(s0_ref, p0_ref)
        values(t - 1, p1_ref)
        return carry

    lax.fori_loop(1, n_tiles // 2, body, 0)
    softmax(s1_ref, p1_ref)
    values(n_tiles - 2, p0_ref)
    values(n_tiles - 1, p1_ref)


def _attention(q3, k3, v3, dims):
    n_heads, t, _ = q3.shape
    p_rows, lc, n_batch = dims
    l_lat = p_rows - lc
    tq = max(c for c in range(lc, ATTN_Q_TILE + 1, lc) if l_lat % (2 * c) == 0)
    return pl.pallas_call(
        functools.partial(_attn_kernel, tq=tq, lc=lc),
        grid=(n_batch, n_heads),
        in_specs=[
            pl.BlockSpec((1, p_rows, MLA_QK), lambda b, h: (h, b, 0)),
            pl.BlockSpec((1, p_rows, MLA_QK), lambda b, h: (h, b, 0)),
            pl.BlockSpec((1, p_rows, 2 * MLA_V), lambda b, h: (h, b, 0)),
        ],
        out_specs=pl.BlockSpec((p_rows, MLA_V), lambda b, h: (b, h)),
        out_shape=jax.ShapeDtypeStruct((t, n_heads * MLA_V), BF),
        scratch_shapes=[pltpu.VMEM((tq, p_rows), F32), pltpu.VMEM((tq, p_rows), F32),
                        pltpu.VMEM((tq, p_rows), BF), pltpu.VMEM((tq, p_rows), BF)],
        compiler_params=_cparams(("parallel", "parallel")),
        name="mla_attention",
    )(q3, k3, v3)


def _hgrn_chunk(zq, zi, zf, lb, tri, o_ref, st_ref, stream, bi, n_heads):
    feeds = tri > 0.5
    f = lb + (1.0 - lb) * _sigmoid(zf.astype(F32))
    lf = jnp.log(f)
    lf1 = lf.astype(BF)
    r1 = lf - lf1.astype(F32)
    lf2 = r1.astype(BF)
    lf3 = (r1 - lf2.astype(F32)).astype(BF)
    b_all = (jnp.dot(tri, lf1, preferred_element_type=F32) + jnp.dot(tri, lf2, preferred_element_type=F32)
             + jnp.dot(tri, lf3, preferred_element_type=F32))
    tot_all = jnp.sum(lf, axis=0, keepdims=True)
    q_all = _silu(zq.astype(F32))
    nt = (((1,), (1,)), ((), ()))
    for h in range(n_heads):
        sl = slice(h * HG_DK, (h + 1) * HG_DK)
        b, tot, q, k = b_all[:, sl], tot_all[:, sl], q_all[:, sl], 1.0 - f[:, sl]
        v = zi[:, sl]
        ref = 0.5 * tot
        q_loc = (q * jnp.exp(jnp.minimum(b - ref, HG_EXP_CLAMP))).astype(BF)
        k_loc = (k * jnp.exp(jnp.minimum(ref - b, HG_EXP_CLAMP))).astype(BF)
        a = lax.dot_general(q_loc, k_loc, nt, preferred_element_type=F32)
        a = jnp.where(feeds, a, 0.0).astype(BF)
        state_t = st_ref[stream, h]
        q_in = (q * jnp.exp(b)).astype(BF)
        o = jnp.dot(a, v, preferred_element_type=F32)
        o = o + lax.dot_general(q_in, state_t.astype(BF), nt, preferred_element_type=F32)
        o_ref[bi, :, sl] = o
        k_out = (k * jnp.exp(tot - b)).astype(BF)
        upd = lax.dot_general(v, k_out, (((0,), (0,)), ((), ())), preferred_element_type=F32)
        st_ref[stream, h] = state_t * jnp.exp(tot) + upd


def _hgrn_kernel(fq_ref, fi_ref, ff_ref, bq_ref, bi_ref, bf_ref, lb_ref, tri_ref, of_ref, ob_ref, st_ref,
                 *, n_heads, n_batch):
    @pl.when(pl.program_id(0) == 0)
    def _():
        st_ref[...] = jnp.zeros_like(st_ref)

    for rev, (q_ref, i_ref, f_ref, o_ref) in enumerate(((fq_ref, fi_ref, ff_ref, of_ref),
                                                       (bq_ref, bi_ref, bf_ref, ob_ref))):
        for bi in range(n_batch):
            _hgrn_chunk(q_ref[bi], i_ref[bi], f_ref[bi], lb_ref[rev], tri_ref[rev], o_ref, st_ref,
                        rev * n_batch + bi, bi, n_heads)


def _hgrn(z, lb2, dims):
    t, nz = z.shape
    p_rows, lc, n_batch = dims
    c = HG_CHUNK
    w = lb2.shape[-1]
    n_heads = w // HG_DK
    nch, nctx = p_rows // c, lc // c
    idx = np.arange(c)
    tri = jnp.asarray(np.stack([idx[None, :] <= idx[:, None], idx[None, :] >= idx[:, None]]), BF)
    z3 = z.reshape(n_batch, p_rows, nz)

    def back(j):
        return jnp.where(j < nctx, nctx - 1 - j, nch - 1 - (j - nctx))

    def spec(col, rev):
        if rev:
            return pl.BlockSpec((n_batch, c, w), lambda j: (0, back(j), col // w))
        return pl.BlockSpec((n_batch, c, w), lambda j: (0, j, col // w))

    out_f, out_b = pl.pallas_call(
        functools.partial(_hgrn_kernel, n_heads=n_heads, n_batch=n_batch),
        grid=(nch,),
        in_specs=[
            spec(Z_HQ, False), spec(Z_HI, False), spec(Z_HFF, False),
            spec(Z_HQ, True), spec(Z_HI, True), spec(Z_HFB, True),
            pl.BlockSpec((2, 1, w), lambda j: (0, 0, 0)),
            pl.BlockSpec((2, c, c), lambda j: (0, 0, 0)),
        ],
        out_specs=[pl.BlockSpec((n_batch, c, w), lambda j: (0, j, 0)),
                   pl.BlockSpec((n_batch, c, w), lambda j: (0, back(j), 0))],
        out_shape=[jax.ShapeDtypeStruct((n_batch, p_rows, w), F32)] * 2,
        scratch_shapes=[pltpu.VMEM((2 * n_batch, n_heads, HG_DK, HG_DK), F32)],
        compiler_params=_cparams(("arbitrary",)),
        name="hgrn_scan",
    )(z3, z3, z3, z3, z3, z3, lb2.reshape(2, 1, w), tri)
    return out_f.reshape(t, w), out_b.reshape(t, w)


def _hgrn_readout_kernel(of_ref, ob_ref, zg_ref, g_ref, y_ref, *, n_heads):
    o = of_ref[...] + ob_ref[...]
    g = g_ref[...]
    gate = _silu(zg_ref[...].astype(F32))
    for h in range(n_heads):
        sl = slice(h * HG_DK, (h + 1) * HG_DK)
        y_ref[:, sl] = (_rms(o[:, sl], g[:, sl]) * gate[:, sl]).astype(y_ref.dtype)


def _hgrn_readout(o_fwd, o_bwd, z, g_norm):
    t, w = o_fwd.shape
    tm = _divisor_tile(t, 1088)
    return pl.pallas_call(
        functools.partial(_hgrn_readout_kernel, n_heads=w // HG_DK),
        grid=(t // tm,),
        in_specs=[
            pl.BlockSpec((tm, w), lambda i: (i, 0)),
            pl.BlockSpec((tm, w), lambda i: (i, 0)),
            pl.BlockSpec((tm, w), lambda i: (i, Z_HG // w)),
            pl.BlockSpec((1, w), lambda i: (0, 0)),
        ],
        out_specs=pl.BlockSpec((tm, w), lambda i: (i, 0)),
        out_shape=jax.ShapeDtypeStruct((t, w), BF),
        compiler_params=_cparams(("parallel",)),
        name="hgrn_readout",
    )(o_fwd, o_bwd, z, g_norm.reshape(1, w))


def _segment_tile(i, tc, p_rows, lc):
    per_batch = p_rows // tc
    j = i % per_batch
    nctx = lc // tc
    first = jnp.logical_or(j == 0, j == nctx)
    last = jnp.logical_or(j == nctx - 1, j == per_batch - 1)
    in_ctx = j < nctx
    seg_pos0 = jnp.where(in_ctx, j, j - nctx) * tc
    seg_len = jnp.where(in_ctx, lc, p_rows - lc)
    return first, last, seg_pos0, seg_len


def _halo_specs(tc, width, col_block, n_rows):
    per = tc // HALO
    last_blk = n_rows // HALO - 1
    return [
        pl.BlockSpec((HALO, width), lambda i: (jnp.maximum(i * per - 1, 0), col_block)),
        pl.BlockSpec((tc, width), lambda i: (i, col_block)),
        pl.BlockSpec((HALO, width), lambda i: (jnp.minimum((i + 1) * per, last_blk), col_block)),
    ]


def _conv_kernel(prev_ref, cur_ref, next_ref, w_ref, b_ref, lg_ref, lb_ref, y_ref, buf, shifted, *, tc, p_rows, lc, ch):
    first, last, _, _ = _segment_tile(pl.program_id(0), tc, p_rows, lc)

    def glu(zz):
        zz = zz.astype(F32)
        return zz[:, :ch] * _sigmoid(zz[:, ch:])

    buf[0:HALO, :] = jnp.where(first, 0.0, glu(prev_ref[...]))
    buf[HALO:HALO + tc, :] = glu(cur_ref[...])
    buf[HALO + tc:, :] = jnp.where(last, 0.0, glu(next_ref[...]))
    span = shifted.shape[1]
    for s in range(1, SUBLANES):
        shifted[s - 1] = buf[s:s + span, :]
    w = w_ref[...]
    acc = jnp.zeros((tc, ch), F32) + b_ref[...]
    for kk in range(CONV_WIDTH):
        blk, s = divmod(HALO - CONV_WIDTH // 2 + kk, SUBLANES)
        r0 = blk * SUBLANES
        win = buf[r0:r0 + tc, :] if s == 0 else shifted[s - 1, r0:r0 + tc, :]
        acc = acc + win * w[kk:kk + 1, :]
    mu = jnp.mean(acc, axis=-1, keepdims=True)
    cen = acc - mu
    var = jnp.mean(cen * cen, axis=-1, keepdims=True)
    y = cen * lax.rsqrt(var + EPS) * lg_ref[...] + lb_ref[...]
    y_ref[...] = _silu(y).astype(y_ref.dtype)


def _conv_mixer(z, w_dw, b_dw, ln_g, ln_b, dims):
    t = z.shape[0]
    p_rows, lc, _ = dims
    ch = w_dw.shape[1]
    tc = _divisor_tile(lc, 256)
    row = lambda a: a.reshape(1, ch)
    const = lambda i: (0, 0)
    return pl.pallas_call(
        functools.partial(_conv_kernel, tc=tc, p_rows=p_rows, lc=lc, ch=ch),
        grid=(t // tc,),
        in_specs=_halo_specs(tc, 2 * ch, Z_CONV // (2 * ch), t) + [
            pl.BlockSpec((CONV_WIDTH, ch), const),
            pl.BlockSpec((1, ch), const), pl.BlockSpec((1, ch), const), pl.BlockSpec((1, ch), const),
        ],
        out_specs=pl.BlockSpec((tc, ch), lambda i: (i, 0)),
        out_shape=jax.ShapeDtypeStruct((t, ch), BF),
        scratch_shapes=[pltpu.VMEM((tc + 2 * HALO, ch), F32),
                        pltpu.VMEM((SUBLANES - 1, tc + 2 * HALO - SUBLANES, ch), F32)],
        compiler_params=_cparams(("parallel",)),
        name="conv_mixer",
    )(z, z, z, w_dw, row(b_dw), row(ln_g), row(ln_b))


def _pool_kernel(prev_ref, cur_ref, next_ref, w_ref, s_ref, y_ref, buf, *, tc, p_rows, lc):
    first, last, seg_pos0, seg_len = _segment_tile(pl.program_id(0), tc, p_rows, lc)
    buf[0:HALO, :] = jnp.where(first, 0.0, prev_ref[...].astype(F32))
    buf[HALO:HALO + tc, :] = cur_ref[...].astype(F32)
    buf[HALO + tc:, :] = jnp.where(last, 0.0, next_ref[...].astype(F32))
    pos = seg_pos0 + lax.broadcasted_iota(jnp.int32, (tc, 1), 0)
    scale = s_ref[...]
    for gi, win in enumerate(POOL_WINDOWS):
        sl = slice(gi * POOL_GROUP, (gi + 1) * POOL_GROUP)
        half = win // 2
        tot = jnp.zeros((tc, POOL_GROUP), F32)
        for u in range(-half, half):
            tot = tot + buf[HALO + u:HALO + u + tc, sl]
        cnt = (jnp.minimum(pos + half, seg_len) - jnp.maximum(pos - half, 0)).astype(F32)
        pooled = tot / cnt - buf[HALO:HALO + tc, sl]
        y = jnp.dot(pooled.astype(BF), w_ref[gi].astype(BF), preferred_element_type=F32)
        y_ref[:, sl] = (y * scale[:, sl]).astype(y_ref.dtype)


def _pool_mixer(z, w_pool, scale, dims):
    t = z.shape[0]
    p_rows, lc, _ = dims
    ch = scale.shape[0]
    tc = _divisor_tile(lc, 256)
    return pl.pallas_call(
        functools.partial(_pool_kernel, tc=tc, p_rows=p_rows, lc=lc),
        grid=(t // tc,),
        in_specs=_halo_specs(tc, ch, Z_POOL // ch, t) + [
            pl.BlockSpec(w_pool.shape, lambda i: (0, 0, 0)),
            pl.BlockSpec((1, ch), lambda i: (0, 0)),
        ],
        out_specs=pl.BlockSpec((tc, ch), lambda i: (i, 0)),
        out_shape=jax.ShapeDtypeStruct((t, ch), BF),
        scratch_shapes=[pltpu.VMEM((tc + 2 * HALO, ch), F32)],
        compiler_params=_cparams(("parallel",)),
        name="pool_mixer",
    )(z, z, z, w_pool, scale.reshape(1, ch))


def _merge_kernel(y0, y1, y2, y3, g0, g1, g2, g3, w_ref, o_ref, w_s):
    @pl.when(pl.program_id(1) == 0)
    def _():
        w_s[...] = w_ref[0].astype(BF)

    acc = None
    for n, (y_ref, g_ref) in enumerate(((y0, g0), (y1, g1), (y2, g2), (y3, g3))):
        proj = jnp.dot(y_ref[...], w_s[n], preferred_element_type=F32)
        gate = 0.5 * jnp.tanh(0.5 * g_ref[...].astype(F32)) + 0.5
        term = gate * proj
        acc = term if acc is None else acc + term
    o_ref[...] = acc.astype(o_ref.dtype)


def _merge(ys, z, w_branch, l):
    t, bw = ys[0].shape
    _, nb, _, d = w_branch.shape
    tm = _divisor_tile(t, 544)
    tn = _divisor_tile(d, 1024, 128)
    npn = d // tn

    def gate_spec(n):
        return pl.BlockSpec((tm, tn), lambda j, i: (i, Z_GATE // tn + n * npn + j))

    return pl.pallas_call(
        _merge_kernel,
        grid=(npn, t // tm),
        in_specs=[pl.BlockSpec((tm, bw), lambda j, i: (i, 0))] * nb + [gate_spec(n) for n in range(nb)] + [
            pl.BlockSpec((1, nb, bw, tn), lambda j, i: (l, 0, 0, j))],
        out_specs=pl.BlockSpec((tm, tn), lambda j, i: (i, j)),
        out_shape=jax.ShapeDtypeStruct((t, d), BF),
        scratch_shapes=[pltpu.VMEM((nb, bw, tn), BF)],
        compiler_params=_cparams(("parallel", "arbitrary")),
        name="branch_merge",
    )(*ys, z, z, z, z, w_branch)


def _residual_epilogue(y, x_ref, gpost_ref, gate_ref, gnext_ref, sh_ref, sc_ref, xo_ref, ho_ref, b, n_batch, cm,
                       rows=slice(None)):
    cm = cm[rows]
    gate = _row_vec(gate_ref, b, n_batch, cm)
    x_new = x_ref[rows, :] + gate * _rms(y, gpost_ref[...])
    xo_ref[rows, :] = x_new
    if ho_ref is None:
        return None
    shift = _row_vec(sh_ref, b, n_batch, cm)
    scale = _row_vec(sc_ref, b, n_batch, cm)
    h = _rms(x_new, gnext_ref[...]) * (1.0 + scale) + shift
    ho_ref[rows, :] = h.astype(ho_ref.dtype)
    return h


def _route_top2(logits, n_experts):
    lane = lax.broadcasted_iota(jnp.int32, logits.shape, 1)
    lg = jnp.where(lane < n_experts, logits, NEG_BIG)
    m1 = jnp.max(lg, axis=-1, keepdims=True)
    i1 = jnp.min(jnp.where(lg == m1, lane, 1 << 20), axis=-1, keepdims=True)
    lg2 = jnp.where(lane == i1, NEG_BIG, lg)
    m2 = jnp.max(lg2, axis=-1, keepdims=True)
    i2 = jnp.min(jnp.where(lg2 == m2, lane, 1 << 20), axis=-1, keepdims=True)
    e2 = jnp.exp(m2 - m1)
    w1 = 1.0 / (1.0 + e2)
    w2 = e2 / (1.0 + e2)
    out = jnp.where(lane == 0, i1.astype(F32), 0.0)
    out = jnp.where(lane == 1, i2.astype(F32), out)
    out = jnp.where(lane == 2, w1, out)
    return jnp.where(lane == 3, w2, out)


def _router_logits(h, wr_ref):
    h_hi = h.astype(BF)
    h_lo = (h - h_hi.astype(F32)).astype(BF)
    return (jnp.dot(h_hi, wr_ref[0], preferred_element_type=F32) + jnp.dot(h_hi, wr_ref[1], preferred_element_type=F32)
            + jnp.dot(h_lo, wr_ref[0], preferred_element_type=F32))


def _out_proj_kernel(*refs, tm, p_rows, lc, n_batch, emit_h, n_experts):
    m_ref, w_ref, x_ref, gpost_ref, gate_ref = refs[:5]
    pos = 5
    gnext_ref = sh_ref = sc_ref = wr_ref = None
    if emit_h:
        gnext_ref, sh_ref, sc_ref = refs[pos:pos + 3]
        pos += 3
    if n_experts:
        wr_ref = refs[pos]
        pos += 1
    xo_ref = refs[pos]
    pos += 1
    ho_ref = route_ref = None
    if emit_h:
        ho_ref = refs[pos]
        pos += 1
    if n_experts:
        route_ref = refs[pos]
        pos += 1
    b, cm = _tile_rows(pl.program_id(0), tm, p_rows, lc)
    n_split = EPILOGUE_SPLIT if tm % (EPILOGUE_SPLIT * 16) == 0 else 1
    rs = tm // n_split
    groups = [slice(s * rs, (s + 1) * rs) for s in range(n_split)]
    ys = [jnp.dot(m_ref[rows, :], w_ref[0], preferred_element_type=F32) for rows in groups]
    for rows, y in zip(groups, ys):
        h = _residual_epilogue(y, x_ref, gpost_ref, gate_ref, gnext_ref, sh_ref, sc_ref, xo_ref, ho_ref, b, n_batch,
                               cm, rows)
        if n_experts:
            route_ref[rows, :] = _route_top2(_router_logits(h, wr_ref), n_experts)


def _out_proj(m, w_stack, lw, xs, g_post, mods, l, j_gate, nxt, dims, router=None, name="out_proj"):
    t, d = xs.shape
    k = m.shape[1]
    p_rows, lc, n_batch = dims
    tm = _divisor_tile(p_rows, 544 if k <= d else 272)
    const = lambda i: (0, 0)
    rowtile = pl.BlockSpec((tm, d), lambda i: (i, 0))
    args = [m, w_stack, xs, g_post.reshape(1, d), mods]
    in_specs = [pl.BlockSpec((tm, k), lambda i: (i, 0)),
                pl.BlockSpec((1, k, d), lambda i: (lw, 0, 0), pipeline_mode=pl.Buffered(1)),
                rowtile, pl.BlockSpec((1, d), const), _mod_spec(l, j_gate, d)]
    out_shape = [jax.ShapeDtypeStruct((t, d), F32)]
    out_specs = [rowtile]
    n_experts = 0
    if nxt is not None:
        g_next, l_next, j_shift, j_scale = nxt
        args += [g_next.reshape(1, d), mods, mods]
        in_specs += [pl.BlockSpec((1, d), const), _mod_spec(l_next, j_shift, d), _mod_spec(l_next, j_scale, d)]
        out_shape.append(jax.ShapeDtypeStruct((t, d), BF if router is None else F32))
        out_specs.append(rowtile)
    if router is not None:
        wr, n_experts = router
        args.append(wr)
        in_specs.append(pl.BlockSpec(wr.shape, lambda i: (0, 0, 0)))
        out_shape.append(jax.ShapeDtypeStruct((t, 128), F32))
        out_specs.append(pl.BlockSpec((tm, 128), lambda i: (i, 0)))
    return pl.pallas_call(
        functools.partial(_out_proj_kernel, tm=tm, p_rows=p_rows, lc=lc, n_batch=n_batch,
                          emit_h=nxt is not None, n_experts=n_experts),
        grid=(t // tm,),
        in_specs=in_specs,
        out_specs=out_specs,
        out_shape=out_shape,
        compiler_params=_cparams(("parallel",)),
        name=name,
    )(*args)


def _swiglu_kernel(x_ref, w1_ref, w3_ref, o_ref, w1_s, w3_s):
    @pl.when(pl.program_id(1) == 0)
    def _():
        w1_s[...] = w1_ref[0].astype(BF)
        w3_s[...] = w3_ref[0].astype(BF)

    x = x_ref[...]
    a = jnp.dot(x, w1_s[...], preferred_element_type=F32)
    g = jnp.dot(x, w3_s[...], preferred_element_type=F32)
    o_ref[...] = (_silu(a) * g).astype(o_ref.dtype)


def _swiglu_up(h, w1, w3, jl):
    t, d = h.shape
    dff = w1.shape[2]
    tm = _divisor_tile(t, 1088)
    tn = _divisor_tile(dff, 512, 128)
    return pl.pallas_call(
        _swiglu_kernel,
        grid=(dff // tn, t // tm),
        in_specs=[
            pl.BlockSpec((tm, d), lambda j, i: (i, 0)),
            pl.BlockSpec((1, d, tn), lambda j, i: (jl, 0, j)),
            pl.BlockSpec((1, d, tn), lambda j, i: (jl, 0, j)),
        ],
        out_specs=pl.BlockSpec((tm, tn), lambda j, i: (i, j)),
        out_shape=jax.ShapeDtypeStruct((t, dff), BF),
        scratch_shapes=[pltpu.VMEM((d, tn), BF), pltpu.VMEM((d, tn), BF)],
        compiler_params=_cparams(("parallel", "arbitrary")),
        name="swiglu_up",
    )(h, w1, w3)


def _gather_rows_kernel(idx_ref, src_ref, o_ref, *scratch, tr):
    base = pl.program_id(0) * tr
    dst, sem = (o_ref, scratch[0]) if len(scratch) == 1 else scratch

    def copy(r):
        return pltpu.make_async_copy(src_ref.at[pl.ds(idx_ref[base + r], 1), :], dst.at[pl.ds(r, 1), :], sem)

    def issue(g, carry):
        for u in range(GATHER_UNROLL):
            copy(g * GATHER_UNROLL + u).start(priority=u % 2)
        return carry

    lax.fori_loop(0, tr // GATHER_UNROLL, issue, 0)
    pltpu.make_async_copy(src_ref.at[pl.ds(0, tr), :], dst.at[pl.ds(0, tr), :], sem).wait()
    if dst is not o_ref:
        o_ref[...] = dst[...].astype(o_ref.dtype)


def _gather_rows(src, idx, out_dtype):
    n, d = src.shape
    assert src.dtype == F32
    rows = idx.shape[0]
    tr = _divisor_tile(rows, MOE_GATHER_ROWS, 8)
    staging = [] if out_dtype == src.dtype else [pltpu.VMEM((tr, d), src.dtype)]
    return pl.pallas_call(
        functools.partial(_gather_rows_kernel, tr=tr),
        grid_spec=pltpu.PrefetchScalarGridSpec(
            num_scalar_prefetch=1,
            grid=(rows // tr,),
            in_specs=[pl.BlockSpec(memory_space=pl.ANY)],
            out_specs=pl.BlockSpec((tr, d), lambda i, idx_ref: (i, 0)),
            scratch_shapes=staging + [pltpu.SemaphoreType.DMA(())],
        ),
        out_shape=jax.ShapeDtypeStruct((rows, d), out_dtype),
        compiler_params=_cparams(("arbitrary",)),
        name="moe_gather",
    )(idx, src)


def _expert_rows_loop(st_ref, nt_ref, tr, load_copy, store_copy, compute):
    e, n = pl.program_id(0), pl.program_id(1)
    n_exp, n_col = pl.num_programs(0), pl.num_programs(1)
    start, n_tiles = st_ref[e], nt_ref[e]
    first_step = jnp.logical_and(e == 0, n == 0)
    last_step = jnp.logical_and(e == n_exp - 1, n == n_col - 1)
    e_next = jnp.minimum(jnp.where(n == n_col - 1, e + 1, e), n_exp - 1)

    @pl.when(jnp.logical_and(first_step, n_tiles > 0))
    def _():
        load_copy(start, 0, 0).start(priority=1)

    @pl.when(n_tiles > 1)
    def _():
        load_copy(start, 1, 1).start(priority=1)

    def body(i, carry):
        load_copy(start, i, i % MOE_LOAD_SLOTS).wait()

        @pl.when(i + 2 < n_tiles)
        def _():
            load_copy(start, i + 2, (i + 2) % MOE_LOAD_SLOTS).start(priority=1)

        @pl.when(i >= 2)
        def _():
            store_copy(start, i - 2, i % 2).wait()

        compute(i % MOE_LOAD_SLOTS, i % 2)
        store_copy(start, i, i % 2).start(priority=1)
        return carry

    lax.fori_loop(0, n_tiles, body, 0)

    @pl.when(jnp.logical_and(jnp.logical_not(last_step), nt_ref[e_next] > 0))
    def _():
        load_copy(st_ref[e_next], 0, 0).start(priority=1)

    @pl.when(n_tiles >= 2)
    def _():
        store_copy(start, n_tiles - 2, n_tiles % 2).wait()

    @pl.when(n_tiles >= 1)
    def _():
        store_copy(start, n_tiles - 1, (n_tiles - 1) % 2).wait()


def _zero_tail_rows(out_hbm, zbuf, sem, first_row, col0, tn):
    zbuf[...] = jnp.zeros_like(zbuf)
    n_chunks = (out_hbm.shape[0] - first_row) // MOE_ROW_ALIGN

    def copy(c):
        r0 = pl.multiple_of(first_row + c * MOE_ROW_ALIGN, MOE_ROW_ALIGN)
        return pltpu.make_async_copy(zbuf, out_hbm.at[pl.ds(r0, MOE_ROW_ALIGN), pl.ds(col0, tn)], sem)

    def issue(c, carry):
        copy(c).start()
        return carry

    def drain(c, carry):
        copy(c).wait()
        return carry

    lax.fori_loop(0, n_chunks, issue, 0)
    lax.fori_loop(0, n_chunks, drain, 0)


def _moe_up_kernel(st_ref, nt_ref, x_hbm, w1_ref, w3_ref, a_hbm, w1_s, w3_s, xbuf, obuf, zbuf, sem_in, sem_out, sem_z,
                   *, tr, tn):
    e, n = pl.program_id(0), pl.program_id(1)
    w1_s[...] = w1_ref[0, 0].astype(BF)
    w3_s[...] = w3_ref[0, 0].astype(BF)
    col0 = pl.multiple_of(n * tn, tn)

    def rows(start, i):
        return pl.ds(pl.multiple_of(start + i * tr, MOE_ROW_ALIGN), tr)

    def load_copy(start, i, slot):
        return pltpu.make_async_copy(x_hbm.at[rows(start, i)], xbuf.at[slot], sem_in.at[slot])

    def store_copy(start, i, slot):
        return pltpu.make_async_copy(obuf.at[slot], a_hbm.at[rows(start, i), pl.ds(col0, tn)], sem_out.at[slot])

    def compute(in_slot, out_slot):
        x = xbuf[in_slot]
        a = jnp.dot(x, w1_s[...], preferred_element_type=F32)
        g = jnp.dot(x, w3_s[...], preferred_element_type=F32)
        obuf[out_slot] = (_silu(a) * g).astype(obuf.dtype)

    _expert_rows_loop(st_ref, nt_ref, tr, load_copy, store_copy, compute)

    @pl.when(e == pl.num_programs(0) - 1)
    def _():
        _zero_tail_rows(a_hbm, zbuf, sem_z, st_ref[e] + nt_ref[e] * tr, col0, tn)


def _moe_up(xs, w1, w3, jl, starts, n_tiles, tr):
    rows, d = xs.shape
    n_exp, dff = w1.shape[1], w1.shape[3]
    tn = _divisor_tile(dff, 512, 128)
    return pl.pallas_call(
        functools.partial(_moe_up_kernel, tr=tr, tn=tn),
        grid_spec=pltpu.PrefetchScalarGridSpec(
            num_scalar_prefetch=2,
            grid=(n_exp, dff // tn),
            in_specs=[
                pl.BlockSpec(memory_space=pl.ANY),
                pl.BlockSpec((1, 1, d, tn), lambda e, n, st, nt: (jl, e, 0, n)),
                pl.BlockSpec((1, 1, d, tn), lambda e, n, st, nt: (jl, e, 0, n)),
            ],
            out_specs=pl.BlockSpec(memory_space=pl.ANY),
            scratch_shapes=[pltpu.VMEM((d, tn), BF), pltpu.VMEM((d, tn), BF),
                            pltpu.VMEM((MOE_LOAD_SLOTS, tr, d), BF), pltpu.VMEM((2, tr, tn), BF),
                            pltpu.VMEM((MOE_ROW_ALIGN, tn), BF),
                            pltpu.SemaphoreType.DMA((MOE_LOAD_SLOTS,)), pltpu.SemaphoreType.DMA((2,)),
                            pltpu.SemaphoreType.DMA(())],
        ),
        out_shape=jax.ShapeDtypeStruct((rows, dff), BF),
        compiler_params=_cparams(("arbitrary", "arbitrary")),
        name="moe_up",
    )(starts, n_tiles, xs, w1, w3)


def _moe_down_kernel(st_ref, nt_ref, a_hbm, w_ref, y_hbm, w_s, abuf, obuf, zbuf, sem_in, sem_out, sem_z, *, tr, tn):
    e, n = pl.program_id(0), pl.program_id(1)
    w_s[...] = w_ref[0, 0].astype(BF)
    col0 = pl.multiple_of(n * tn, tn)

    def rows(start, i):
        return pl.ds(pl.multiple_of(start + i * tr, MOE_ROW_ALIGN), tr)

    def load_copy(start, i, slot):
        return pltpu.make_async_copy(a_hbm.at[rows(start, i)], abuf.at[slot], sem_in.at[slot])

    def store_copy(start, i, slot):
        return pltpu.make_async_copy(obuf.at[slot], y_hbm.at[rows(start, i), pl.ds(col0, tn)], sem_out.at[slot])

    def compute(in_slot, out_slot):
        obuf[out_slot] = jnp.dot(abuf[in_slot], w_s[...], preferred_element_type=F32)

    _expert_rows_loop(st_ref, nt_ref, tr, load_copy, store_copy, compute)

    @pl.when(e == pl.num_programs(0) - 1)
    def _():
        _zero_tail_rows(y_hbm, zbuf, sem_z, st_ref[e] + nt_ref[e] * tr, col0, tn)


def _moe_down(act, w2, jl, starts, n_tiles, tr):
    rows, dff = act.shape
    n_exp, d = w2.shape[1], w2.shape[3]
    tn = _divisor_tile(d, 512, 128)
    return pl.pallas_call(
        functools.partial(_moe_down_kernel, tr=tr, tn=tn),
        grid_spec=pltpu.PrefetchScalarGridSpec(
            num_scalar_prefetch=2,
            grid=(n_exp, d // tn),
            in_specs=[
                pl.BlockSpec(memory_space=pl.ANY),
                pl.BlockSpec((1, 1, dff, tn), lambda e, n, st, nt: (jl, e, 0, n)),
            ],
            out_specs=pl.BlockSpec(memory_space=pl.ANY),
            scratch_shapes=[pltpu.VMEM((dff, tn), BF), pltpu.VMEM((MOE_LOAD_SLOTS, tr, dff), BF),
                            pltpu.VMEM((2, tr, tn), F32), pltpu.VMEM((MOE_ROW_ALIGN, tn), F32),
                            pltpu.SemaphoreType.DMA((MOE_LOAD_SLOTS,)), pltpu.SemaphoreType.DMA((2,)),
                            pltpu.SemaphoreType.DMA(())],
        ),
        out_shape=jax.ShapeDtypeStruct((rows, d), F32),
        compiler_params=_cparams(("arbitrary", "arbitrary")),
        name="moe_down",
    )(starts, n_tiles, act, w2)


def _combine_kernel(*refs, tm, p_rows, lc, n_batch, emit_h):
    ya_ref, yb_ref, rt_ref, x_ref, gpost_ref, gate_ref = refs[:6]
    gnext_ref = sh_ref = sc_ref = ho_ref = None
    if emit_h:
        gnext_ref, sh_ref, sc_ref, xo_ref, ho_ref = refs[6:]
    else:
        (xo_ref,) = refs[6:]
    b, cm = _tile_rows(pl.program_id(0), tm, p_rows, lc)
    rt = rt_ref[...]
    y = rt[:, 2:3] * ya_ref[...] + rt[:, 3:4] * yb_ref[...]
    _residual_epilogue(y, x_ref, gpost_ref, gate_ref, gnext_ref, sh_ref, sc_ref, xo_ref, ho_ref, b, n_batch, cm)


def _moe_combine(y2, route, xs, g_post, mods, l, j_gate, nxt, dims):
    t, d = xs.shape
    p_rows, lc, n_batch = dims
    tm = _divisor_tile(p_rows, 544)
    nt = t // tm
    const = lambda i: (0, 0)
    rowtile = pl.BlockSpec((tm, d), lambda i: (i, 0))
    args = [y2, y2, route, xs, g_post.reshape(1, d), mods]
    in_specs = [rowtile, pl.BlockSpec((tm, d), lambda i: (i + nt, 0)), pl.BlockSpec((tm, 128), lambda i: (i, 0)),
                rowtile, pl.BlockSpec((1, d), const), _mod_spec(l, j_gate, d)]
    out_shape = [jax.ShapeDtypeStruct((t, d), F32)]
    out_specs = [rowtile]
    if nxt is not None:
        g_next, l_next, j_shift, j_scale = nxt
        args += [g_next.reshape(1, d), mods, mods]
        in_specs += [pl.BlockSpec((1, d), const), _mod_spec(l_next, j_shift, d), _mod_spec(l_next, j_scale, d)]
        out_shape.append(jax.ShapeDtypeStruct((t, d), BF))
        out_specs.append(rowtile)
    return pl.pallas_call(
        functools.partial(_combine_kernel, tm=tm, p_rows=p_rows, lc=lc, n_batch=n_batch, emit_h=nxt is not None),
        grid=(nt,),
        in_specs=in_specs,
        out_specs=out_specs,
        out_shape=out_shape,
        compiler_params=_cparams(("parallel",)),
        name="moe_combine",
    )(*args)


def _moe_ffn(h, route, w1, w3, w2, jl, xs, g_post, mods, l, nxt, dims):
    t, d = h.shape
    n_exp = w1.shape[1]
    tr = MOE_ROW_TILE
    expert = jnp.concatenate([route[:, 0], route[:, 1]]).astype(jnp.int32)
    token = jnp.concatenate([jnp.arange(t, dtype=jnp.int32)] * 2)
    onehot = (expert[:, None] == jnp.arange(n_exp, dtype=jnp.int32)[None, :]).astype(jnp.int32)
    rank = jnp.sum((jnp.cumsum(onehot, axis=0) - onehot) * onehot, axis=1)
    counts = jnp.sum(onehot, axis=0)
    padded = ((counts + MOE_ROW_ALIGN - 1) // MOE_ROW_ALIGN) * MOE_ROW_ALIGN
    ends = jnp.cumsum(padded)
    starts = (ends - padded).astype(jnp.int32)
    n_tiles = ((counts + tr - 1) // tr).astype(jnp.int32)
    dest = starts[expert] + rank
    rows = 2 * t + n_exp * MOE_ROW_ALIGN + tr
    rows = ((rows + MOE_GATHER_ROWS - 1) // MOE_GATHER_ROWS) * MOE_GATHER_ROWS
    row_token = jnp.zeros((rows,), jnp.int32).at[dest].set(token)

    xs_sorted = _gather_rows(h, row_token, BF)
    act = _moe_up(xs_sorted, w1, w3, jl, starts, n_tiles, tr)
    y_sorted = _moe_down(act, w2, jl, starts, n_tiles, tr)
    y2 = _gather_rows(y_sorted, dest, F32)
    return _moe_combine(y2, route, xs, g_post, mods, l, 5, nxt, dims)


def _pad_heads(w, n_heads, width):
    r = w.shape[0]
    w3 = w.reshape(r, n_heads, width)
    return jnp.pad(w3, ((0, 0), (0, 0), (0, HEAD_PAD - width))).reshape(r, n_heads * HEAD_PAD)


def _rope_tables(l_lat, lc):
    tt = jnp.arange(l_lat)
    row = (tt // GRID_W).astype(F32)
    col = (tt % GRID_W).astype(F32)
    axis_dim = MLA_ROPE // 2
    inv_freq = 1.0 / (ROPE_THETA ** (jnp.arange(0, axis_dim, 2, dtype=F32) / axis_dim))
    ar, ac = row[:, None] * inv_freq, col[:, None] * inv_freq
    ang = jnp.concatenate([ar, ar, ac, ac], axis=-1)
    cos = jnp.concatenate([jnp.ones((lc, MLA_ROPE), F32), jnp.cos(ang)], axis=0)
    sin = jnp.concatenate([jnp.zeros((lc, MLA_ROPE), F32), jnp.sin(ang)], axis=0)
    padc = jnp.zeros((lc + l_lat, 128 - MLA_ROPE), F32)
    pm = np.zeros((128, 128), np.float32)
    q4 = MLA_ROPE // 4
    for base in (0, 2 * q4):
        for i in range(q4):
            pm[base + q4 + i, base + i] = -1.0
            pm[base + i, base + q4 + i] = 1.0
    return jnp.concatenate([cos, padc], axis=1), jnp.concatenate([sin, padc], axis=1), jnp.asarray(pm, BF)


def kernel(x, c, ctx, c_ctx, w_mod, b_mod, g_mix_pre, g_mix_post, g_ffn_pre, g_ffn_post, w_in, mla_g_q, mla_g_kv, mla_w_uq, mla_w_ukv, hg_lb_logits, hg_g_norm, conv_w, conv_b, conv_ln_g, conv_ln_b, pool_w, pool_scale, w_branch, w_out, ffn_w1, ffn_w3, ffn_w2, moe_router, moe_w1, moe_w3, moe_w2):
    n_batch, l_lat, d = x.shape
    lc = ctx.shape[1]
    depth = w_in.shape[0]
    p_rows = lc + l_lat
    t = n_batch * p_rows
    dims = (p_rows, lc, n_batch)
    assert n_batch + 1 <= 8 and lc % HG_CHUNK == 0 and l_lat % HG_CHUNK == 0 and l_lat % GRID_W == 0
    n_heads = mla_w_uq.shape[2] // MLA_QK

    xs = jnp.concatenate([ctx, x], axis=1).reshape(t, d)
    cvec = jnp.zeros((8, d), F32).at[:n_batch].set(c).at[n_batch].set(c_ctx)
    mods = _modvecs(cvec, w_mod, b_mod)

    lb = jnp.cumsum(jax.nn.softmax(hg_lb_logits.astype(F32), axis=1), axis=1)
    lb = lb - lb[:, :1]
    cos, sin, pm = _rope_tables(l_lat, lc)

    w_in_t = jnp.swapaxes(w_in, 1, 2)
    w_out_bf = w_out.astype(BF)
    ffn_w2_bf = ffn_w2.astype(BF)

    h = _modulate(xs, g_mix_pre[0], mods, 0, 0, 1, dims)
    for l in range(depth):
        last = l == depth - 1
        z = _in_proj(h, w_in_t, l, BF)
        q3, k3, v3 = _mla_proj(z, mla_g_q[l], mla_g_kv[l], _pad_heads(mla_w_uq[l], n_heads, MLA_QK),
                               mla_w_ukv[l], cos, sin, pm, dims)
        y_mla = _attention(q3, k3, v3, dims)
        y_hg = _hgrn_readout(*_hgrn(z, lb[:, l], dims), z, hg_g_norm[l])
        y_conv = _conv_mixer(z, conv_w[l], conv_b[l], conv_ln_g[l], conv_ln_b[l], dims)
        y_pool = _pool_mixer(z, pool_w[l], pool_scale[l], dims)
        m = _merge([y_mla, y_hg, y_conv, y_pool], z, w_branch, l)
        j = l // 2
        nxt_ffn = (g_ffn_pre[l], l, 3, 4)
        nxt_mix = None if last else (g_mix_pre[l + 1], l + 1, 0, 1)
        if l % 2 == 0:
            xs, h2 = _out_proj(m, w_out_bf, l, xs, g_mix_post[l], mods, l, 2, nxt_ffn, dims)
            act = _swiglu_up(h2, ffn_w1, ffn_w3, j)
            res = _out_proj(act, ffn_w2_bf, j, xs, g_ffn_post[l], mods, l, 5, nxt_mix, dims, name="ffn_down")
        else:
            n_exp = moe_router.shape[2]
            wr = jnp.pad(moe_router[j], ((0, 0), (0, 128 - n_exp)))
            wr_hi = wr.astype(BF)
            wr2 = jnp.stack([wr_hi, (wr - wr_hi.astype(F32)).astype(BF)])
            xs, h2, route = _out_proj(m, w_out_bf, l, xs, g_mix_post[l], mods, l, 2, nxt_ffn, dims,
                                      router=(wr2, n_exp))
            res = _moe_ffn(h2, route, moe_w1, moe_w3, moe_w2, j, xs, g_ffn_post[l], mods, l, nxt_mix, dims)
        if last:
            (xs,) = res
        else:
            xs, h = res
    return xs.reshape(n_batch, p_rows, d)[:, lc:, :]
```

```python
import functools

import numpy as np
import jax
import jax.numpy as jnp
from jax import lax
from jax.experimental import pallas as pl
from jax.experimental.pallas import tpu as pltpu

F32 = jnp.float32
BF = jnp.bfloat16

EPS = 1e-6
GRID_W = 64
ROPE_THETA = 10000.0
N_BRANCH = 4
MLA_NOPE, MLA_ROPE, MLA_V = 128, 64, 128
MLA_QK = MLA_NOPE + MLA_ROPE
HEAD_PAD = 256
HG_DK = 128
CONV_WIDTH = 31
POOL_WINDOWS = (2, 4, 8, 16)
POOL_GROUP = 128
LANES = 128
IN_HALF = 512
IN_GROUP = 64
SUBLANES = 8
HALO = 16
HG_CHUNK = 64
HG_STEP_CHUNKS = 2
HG_EXP_CLAMP = 80.0
NEG_BIG = -1e30
ATTN_Q_TILE = 512
MOE_ROW_TILE = 256
GATHER_UNROLL = 8
EPILOGUE_SPLIT = 2
MOE_UP_SLOTS = 5
MOE_DOWN_SLOTS = 3
MOE_ROW_ALIGN = 16
MOE_GATHER_ROWS = 1024

VMEM_LIMIT_V7X = 56 * 1024 * 1024

Z_CONV, Z_HQ, Z_HI, Z_HFF, Z_HFB, Z_HG, Z_POOL, Z_MLA, Z_GATE = 0, 1024, 1536, 2048, 2560, 3072, 3584, 4096, 5120
MLA_ZKV_OFF, MLA_ZR_OFF = 384, 640


def _cparams(sem, vmem=VMEM_LIMIT_V7X):
    return pltpu.CompilerParams(dimension_semantics=sem, vmem_limit_bytes=vmem)


def _divisor_tile(n, target, mult=16):
    best = None
    for t in range(mult, min(n, target) + 1, mult):
        if n % t == 0:
            best = t
    assert best is not None, (n, target, mult)
    return best


def _pick_row(m, r):
    rows = lax.broadcasted_iota(jnp.int32, m.shape, 0)
    return jnp.sum(jnp.where(rows == r, m, 0.0), axis=0, keepdims=True)


def _row_vec(m_ref, b, n_batch, ctx_mask):
    m = m_ref[0, 0]
    if ctx_mask is None:
        return _pick_row(m, b)
    return jnp.where(ctx_mask, _pick_row(m, n_batch), _pick_row(m, b))


def _tile_rows(i, tm, p_rows, lc):
    per_batch = p_rows // tm
    b = i // per_batch
    pos = (i % per_batch) * tm + lax.broadcasted_iota(jnp.int32, (tm, 1), 0)
    return b, pos < lc


def _by_tile_kind(i, tm, p_rows, lc, body):
    b, cm = _tile_rows(i, tm, p_rows, lc)
    has_ctx = (i % (p_rows // tm)) * tm < lc

    @pl.when(has_ctx)
    def _():
        body(b, cm)

    @pl.when(jnp.logical_not(has_ctx))
    def _():
        body(b, None)


def _rms(x, g):
    return x * lax.rsqrt(jnp.mean(x * x, axis=-1, keepdims=True) + EPS) * g


def _sigmoid(x):
    return 1.0 / (1.0 + jnp.exp(-x))


def _silu(x):
    return x * _sigmoid(x)


def _modvec_kernel(c_ref, w_ref, b_ref, o_ref):
    s = _silu(c_ref[...])
    s_hi = s.astype(BF)
    s_lo = (s - s_hi.astype(F32)).astype(BF)
    w = w_ref[0].astype(BF)
    acc = jnp.dot(s_hi, w, preferred_element_type=F32) + jnp.dot(s_lo, w, preferred_element_type=F32)
    o_ref[0, 0] = acc + b_ref[0, 0]


def _modvecs(cvec, w_mod, b_mod):
    depth, d, n6 = w_mod.shape
    nj = n6 // d
    tn = _divisor_tile(d, 1024, 128)
    npj = d // tn
    return pl.pallas_call(
        _modvec_kernel,
        grid=(depth, nj, npj),
        in_specs=[
            pl.BlockSpec((8, d), lambda l, j, n: (0, 0)),
            pl.BlockSpec((1, d, tn), lambda l, j, n: (l, 0, j * npj + n)),
            pl.BlockSpec((1, 1, 1, tn), lambda l, j, n: (l, j, 0, n)),
        ],
        out_specs=pl.BlockSpec((1, 1, 8, tn), lambda l, j, n: (l, j, 0, n)),
        out_shape=jax.ShapeDtypeStruct((depth, nj, 8, d), F32),
        compiler_params=_cparams(("parallel", "parallel", "parallel")),
        name="modvec",
    )(cvec, w_mod, b_mod.reshape(depth, nj, 1, d))


def _modulate_kernel(x_ref, g_ref, sh_ref, sc_ref, h_ref, *, tm, p_rows, lc, n_batch):
    b, cm = _tile_rows(pl.program_id(0), tm, p_rows, lc)
    shift = _row_vec(sh_ref, b, n_batch, cm)
    scale = _row_vec(sc_ref, b, n_batch, cm)
    h_ref[...] = (_rms(x_ref[...], g_ref[...]) * (1.0 + scale) + shift).astype(h_ref.dtype)


def _mod_spec(l, j, d):
    return pl.BlockSpec((1, 1, 8, d), lambda *_: (l, j, 0, 0))


def _modulate(xs, g, mods, l, j_shift, j_scale, dims):
    t, d = xs.shape
    p_rows, lc, n_batch = dims
    tm = _divisor_tile(p_rows, 544)
    return pl.pallas_call(
        functools.partial(_modulate_kernel, tm=tm, p_rows=p_rows, lc=lc, n_batch=n_batch),
        grid=(t // tm,),
        in_specs=[
            pl.BlockSpec((tm, d), lambda i: (i, 0)),
            pl.BlockSpec((1, d), lambda i: (0, 0)),
            _mod_spec(l, j_shift, d),
            _mod_spec(l, j_scale, d),
        ],
        out_specs=pl.BlockSpec((tm, d), lambda i: (i, 0)),
        out_shape=jax.ShapeDtypeStruct((t, d), BF),
        compiler_params=_cparams(("parallel",)),
        name="modulate",
    )(xs, g.reshape(1, d), mods, mods)


def _in_proj_kernel(grp_ref, cnt_ref, x_ref, *refs):
    n_blk = len(refs) - 2
    w_refs, o_ref, w_s = refs[:n_blk], refs[n_blk], refs[n_blk + 1]
    j = pl.program_id(0)
    per_half = n_blk // 2

    @pl.when(pl.program_id(1) == 0)
    def _():
        for u in range(2):
            cnt = cnt_ref[2 * j + u]
            for i in range(per_half):
                r0 = (u * per_half + i) * IN_GROUP
                w_s[r0:r0 + IN_GROUP, :] = jnp.where(i < cnt, w_refs[u * per_half + i][0], 0.0).astype(BF)

    o_ref[...] = lax.dot_general(x_ref[...], w_s[...], (((1,), (1,)), ((), ())),
                                 preferred_element_type=F32).astype(o_ref.dtype)


def _in_proj_tables(d):
    q_rank, kv_rank = (3 * d) // 16, d // 8
    bw = d // N_BRANCH
    src, o = {}, 0
    for name, width in (("mla", q_rank + kv_rank + MLA_ROPE), ("hq", bw), ("hi", bw), ("hff", bw), ("hfb", bw),
                        ("hg", bw), ("conv", 2 * bw), ("pool", bw), ("gate", N_BRANCH * d)):
        src[name] = (o, width)
        o += width
    dst = (("conv", Z_CONV), ("hq", Z_HQ), ("hi", Z_HI), ("hff", Z_HFF), ("hfb", Z_HFB), ("hg", Z_HG),
           ("pool", Z_POOL), ("mla", Z_MLA), ("gate", Z_GATE))
    n_half = (Z_GATE + N_BRANCH * d) // IN_HALF
    grp, cnt = [0] * n_half, [0] * n_half
    for name, z0 in dst:
        s0, width = src[name]
        assert z0 % IN_HALF == 0 and s0 % IN_GROUP == 0 and width % IN_GROUP == 0
        for k in range(-(-width // IN_HALF)):
            grp[z0 // IN_HALF + k] = (s0 + k * IN_HALF) // IN_GROUP
            cnt[z0 // IN_HALF + k] = min(IN_HALF, width - k * IN_HALF) // IN_GROUP
    as_i32 = lambda v: jnp.asarray(np.array(v, np.int32))
    return as_i32(grp), as_i32(cnt)


def _in_proj(x, w_in_t, l, out_dtype, tm_target=1088):
    t, k = x.shape
    n = Z_GATE + N_BRANCH * k
    tn = 2 * IN_HALF
    per_half = IN_HALF // IN_GROUP
    tm = _divisor_tile(t, tm_target)
    tables = _in_proj_tables(k)
    last_group = w_in_t.shape[1] // IN_GROUP - 1

    def w_spec(u, i):
        return pl.BlockSpec((1, IN_GROUP, k),
                            lambda j, m, grp, cnt: (l, jnp.minimum(grp[2 * j + u] + i, last_group), 0))

    return pl.pallas_call(
        _in_proj_kernel,
        grid_spec=pltpu.PrefetchScalarGridSpec(
            num_scalar_prefetch=2,
            grid=(n // tn, t // tm),
            in_specs=[pl.BlockSpec((tm, k), lambda j, m, *_: (m, 0))]
            + [w_spec(u, i) for u in range(2) for i in range(per_half)],
            out_specs=pl.BlockSpec((tm, tn), lambda j, m, *_: (m, j)),
            scratch_shapes=[pltpu.VMEM((tn, k), BF)],
        ),
        out_shape=jax.ShapeDtypeStruct((t, n), out_dtype),
        compiler_params=_cparams(("parallel", "arbitrary")),
        name="in_proj",
    )(*tables, x, *([w_in_t] * (2 * per_half)))


def _rope128(t, cos, sin, pm):
    partner = jnp.dot(t.astype(BF), pm, preferred_element_type=F32)
    return t * cos + partner * sin


def _mla_proj_kernel(z_ref, gq_ref, gkv_ref, wq_ref, wkv_ref, cos_ref, sin_ref, pm_ref,
                     q_ref, k_ref, v_ref, wq_s, wkv_s, *, n_heads, q_rank, kv_rank, scale):
    @pl.when(pl.program_id(0) == 0)
    def _():
        wq_s[...] = wq_ref[...].astype(BF)
        wkv_s[...] = wkv_ref[...].astype(BF)

    z = z_ref[...]
    zq = z[:, :q_rank].astype(F32)
    zkv = z[:, MLA_ZKV_OFF:MLA_ZKV_OFF + kv_rank].astype(F32)
    zr = z[:, MLA_ZR_OFF:MLA_ZR_OFF + 128].astype(F32)
    cos, sin, pm = cos_ref[...], sin_ref[...], pm_ref[...]
    qn = _rms(zq, gq_ref[...]).astype(BF)
    kvn = _rms(zkv, gkv_ref[...]).astype(BF)
    k_rope = _rope128(zr, cos, sin, pm)[:, :MLA_ROPE]
    for h in range(n_heads):
        qh = jnp.dot(qn, wq_s[:, h * HEAD_PAD:(h + 1) * HEAD_PAD], preferred_element_type=F32)
        q_rope = _rope128(qh[:, MLA_NOPE:], cos, sin, pm)[:, :MLA_ROPE]
        q_ref[h] = (jnp.concatenate([qh[:, :MLA_NOPE], q_rope], axis=-1) * scale).astype(q_ref.dtype)
        kvh = jnp.dot(kvn, wkv_s[:, h * HEAD_PAD:(h + 1) * HEAD_PAD], preferred_element_type=F32)
        k_ref[h] = jnp.concatenate([kvh[:, :MLA_NOPE], k_rope], axis=-1).astype(k_ref.dtype)
        v_ref[h] = jnp.concatenate([kvh[:, MLA_NOPE:], jnp.ones_like(kvh[:, MLA_NOPE:])], axis=-1).astype(v_ref.dtype)


def _mla_proj(z, g_q, g_kv, wq_pad, wkv, cos, sin, pm, dims):
    t = z.shape[0]
    p_rows = dims[0]
    q_rank, kv_rank = g_q.shape[0], g_kv.shape[0]
    n_heads = wkv.shape[1] // HEAD_PAD
    tm = _divisor_tile(p_rows, 544)
    npb = p_rows // tm
    scale = float(MLA_QK) ** -0.5 * float(np.log2(np.e))
    const = lambda i: (0, 0)
    return pl.pallas_call(
        functools.partial(_mla_proj_kernel, n_heads=n_heads, q_rank=q_rank, kv_rank=kv_rank, scale=scale),
        grid=(t // tm,),
        in_specs=[
            pl.BlockSpec((tm, 1024), lambda i: (i, Z_MLA // 1024)),
            pl.BlockSpec((1, q_rank), const),
            pl.BlockSpec((1, kv_rank), const),
            pl.BlockSpec(wq_pad.shape, const),
            pl.BlockSpec(wkv.shape, const),
            pl.BlockSpec((tm, 128), lambda i: (i % npb, 0)),
            pl.BlockSpec((tm, 128), lambda i: (i % npb, 0)),
            pl.BlockSpec((128, 128), const),
        ],
        out_specs=[
            pl.BlockSpec((n_heads, tm, MLA_QK), lambda i: (0, i, 0)),
            pl.BlockSpec((n_heads, tm, MLA_QK), lambda i: (0, i, 0)),
            pl.BlockSpec((n_heads, tm, 2 * MLA_V), lambda i: (0, i, 0)),
        ],
        out_shape=[
            jax.ShapeDtypeStruct((n_heads, t, MLA_QK), BF),
            jax.ShapeDtypeStruct((n_heads, t, MLA_QK), BF),
            jax.ShapeDtypeStruct((n_heads, t, 2 * MLA_V), BF),
        ],
        scratch_shapes=[pltpu.VMEM(wq_pad.shape, BF), pltpu.VMEM(wkv.shape, BF)],
        compiler_params=_cparams(("arbitrary",)),
        name="mla_proj",
    )(z, g_q.reshape(1, -1), g_kv.reshape(1, -1), wq_pad, wkv, cos, sin, pm)


def _attn_kernel(q_ref, k_ref, v_ref, o_ref, s0_ref, s1_ref, p0_ref, p1_ref, *, tq, lc):
    p_rows = k_ref.shape[1]
    n_tiles = (p_rows - lc) // tq
    nt = (((1,), (1,)), ((), ()))

    def tile_start(t):
        return pl.multiple_of(lc + t * tq, lc)

    def scores(t, s_ref):
        s_ref[...] = lax.dot_general(q_ref[0, pl.ds(tile_start(t), tq), :], k_ref[0], nt, preferred_element_type=F32)

    def softmax(s_ref, p_ref):
        s = s_ref[...]
        p_ref[...] = jnp.exp2((s - jnp.max(s, axis=-1, keepdims=True)).astype(BF))

    def emit(p, v, q0, n_q):
        o = jnp.dot(p, v, preferred_element_type=F32)
        o_ref[pl.ds(q0, n_q), :] = (o[:, :MLA_V] / o[:, MLA_V:MLA_V + 1]).astype(o_ref.dtype)

    def values(t, p_ref):
        emit(p_ref[...], v_ref[0], tile_start(t), tq)

    s_ctx = lax.dot_general(q_ref[0, :lc, :], k_ref[0, :lc, :], nt, preferred_element_type=F32)
    emit(jnp.exp2((s_ctx - jnp.max(s_ctx, axis=-1, keepdims=True)).astype(BF)), v_ref[0, :lc, :], 0, lc)

    scores(0, s0_ref)
    scores(1, s1_ref)
    softmax(s0_ref, p0_ref)

    def body(kk, carry):
        t = 2 * kk
        scores(t, s0_ref)
        softmax(s1_ref, p1_ref)
        values(t - 2, p0_ref)
        scores(t + 1, s1_ref)
        softmax(s0_ref, p0_ref)
        values(t - 1, p1_ref)
        return carry

    lax.fori_loop(1, n_tiles // 2, body, 0)
    softmax(s1_ref, p1_ref)
    values(n_tiles - 2, p0_ref)
    values(n_tiles - 1, p1_ref)


def _attention(q3, k3, v3, dims):
    n_heads, t, _ = q3.shape
    p_rows, lc, n_batch = dims
    l_lat = p_rows - lc
    tq = max(c for c in range(lc, ATTN_Q_TILE + 1, lc) if l_lat % (2 * c) == 0)
    return pl.pallas_call(
        functools.partial(_attn_kernel, tq=tq, lc=lc),
        grid=(n_batch, n_heads),
        in_specs=[
            pl.BlockSpec((1, p_rows, MLA_QK), lambda b, h: (h, b, 0)),
            pl.BlockSpec((1, p_rows, MLA_QK), lambda b, h: (h, b, 0)),
            pl.BlockSpec((1, p_rows, 2 * MLA_V), lambda b, h: (h, b, 0)),
        ],
        out_specs=pl.BlockSpec((p_rows, MLA_V), lambda b, h: (b, h)),
        out_shape=jax.ShapeDtypeStruct((t, n_heads * MLA_V), BF),
        scratch_shapes=[pltpu.VMEM((tq, p_rows), F32), pltpu.VMEM((tq, p_rows), F32),
                        pltpu.VMEM((tq, p_rows), BF), pltpu.VMEM((tq, p_rows), BF)],
        compiler_params=_cparams(("parallel", "parallel")),
        name="mla_attention",
    )(q3, k3, v3)


def _hgrn_chunk(zq, zi, zf, lb, tri, o_ref, st_ref, stream, bi, rows, n_heads):
    feeds = tri > 0.5
    f = lb + (1.0 - lb) * _sigmoid(zf.astype(F32))
    lf = jnp.log(f)
    lf1 = lf.astype(BF)
    r1 = lf - lf1.astype(F32)
    lf2 = r1.astype(BF)
    lf3 = (r1 - lf2.astype(F32)).astype(BF)
    b_all = (jnp.dot(tri, lf1, preferred_element_type=F32) + jnp.dot(tri, lf2, preferred_element_type=F32)
             + jnp.dot(tri, lf3, preferred_element_type=F32))
    tot_all = jnp.sum(lf, axis=0, keepdims=True)
    q_all = _silu(zq.astype(F32))
    nt = (((1,), (1,)), ((), ()))
    for h in range(n_heads):
        sl = slice(h * HG_DK, (h + 1) * HG_DK)
        b, tot, q, k = b_all[:, sl], tot_all[:, sl], q_all[:, sl], 1.0 - f[:, sl]
        v = zi[:, sl]
        ref = 0.5 * tot
        q_loc = (q * jnp.exp(jnp.minimum(b - ref, HG_EXP_CLAMP))).astype(BF)
        k_loc = (k * jnp.exp(jnp.minimum(ref - b, HG_EXP_CLAMP))).astype(BF)
        a = lax.dot_general(q_loc, k_loc, nt, preferred_element_type=F32)
        a = jnp.where(feeds, a, 0.0).astype(BF)
        state_t = st_ref[stream, h]
        q_in = (q * jnp.exp(b)).astype(BF)
        o = jnp.dot(a, v, preferred_element_type=F32)
        o = o + lax.dot_general(q_in, state_t.astype(BF), nt, preferred_element_type=F32)
        o_ref[bi, rows, sl] = o
        k_out = (k * jnp.exp(tot - b)).astype(BF)
        upd = lax.dot_general(v, k_out, (((0,), (0,)), ((), ())), preferred_element_type=F32)
        st_ref[stream, h] = state_t * jnp.exp(tot) + upd


def _hgrn_kernel(fq_ref, fi_ref, ff_ref, bq_ref, bi_ref, bf_ref, lb_ref, tri_ref, of_ref, ob_ref, st_ref,
                 *, n_heads, n_batch):
    @pl.when(pl.program_id(0) == 0)
    def _():
        st_ref[...] = jnp.zeros_like(st_ref)

    c = tri_ref.shape[1]
    for rev, (q_ref, i_ref, f_ref, o_ref) in enumerate(((fq_ref, fi_ref, ff_ref, of_ref),
                                                       (bq_ref, bi_ref, bf_ref, ob_ref))):
        order = range(HG_STEP_CHUNKS - 1, -1, -1) if rev else range(HG_STEP_CHUNKS)
        for sub in order:
            rows = slice(sub * c, (sub + 1) * c)
            for bi in range(n_batch):
                _hgrn_chunk(q_ref[bi, rows, :], i_ref[bi, rows, :], f_ref[bi, rows, :], lb_ref[rev], tri_ref[rev],
                            o_ref, st_ref, rev * n_batch + bi, bi, rows, n_heads)


def _hgrn(z, lb2, dims):
    t, nz = z.shape
    p_rows, lc, n_batch = dims
    c = HG_CHUNK
    w = lb2.shape[-1]
    n_heads = w // HG_DK
    rb = HG_STEP_CHUNKS * c
    assert lc % rb == 0 and p_rows % rb == 0
    nch, nctx = p_rows // rb, lc // rb
    idx = np.arange(c)
    tri = jnp.asarray(np.stack([idx[None, :] <= idx[:, None], idx[None, :] >= idx[:, None]]), BF)
    z3 = z.reshape(n_batch, p_rows, nz)

    def back(j):
        return jnp.where(j < nctx, nctx - 1 - j, nch - 1 - (j - nctx))

    def spec(col, rev):
        if rev:
            return pl.BlockSpec((n_batch, rb, w), lambda j: (0, back(j), col // w))
        return pl.BlockSpec((n_batch, rb, w), lambda j: (0, j, col // w))

    out_f, out_b = pl.pallas_call(
        functools.partial(_hgrn_kernel, n_heads=n_heads, n_batch=n_batch),
        grid=(nch,),
        in_specs=[
            spec(Z_HQ, False), spec(Z_HI, False), spec(Z_HFF, False),
            spec(Z_HQ, True), spec(Z_HI, True), spec(Z_HFB, True),
            pl.BlockSpec((2, 1, w), lambda j: (0, 0, 0)),
            pl.BlockSpec((2, c, c), lambda j: (0, 0, 0)),
        ],
        out_specs=[pl.BlockSpec((n_batch, rb, w), lambda j: (0, j, 0)),
                   pl.BlockSpec((n_batch, rb, w), lambda j: (0, back(j), 0))],
        out_shape=[jax.ShapeDtypeStruct((n_batch, p_rows, w), F32)] * 2,
        scratch_shapes=[pltpu.VMEM((2 * n_batch, n_heads, HG_DK, HG_DK), F32)],
        compiler_params=_cparams(("arbitrary",)),
        name="hgrn_scan",
    )(z3, z3, z3, z3, z3, z3, lb2.reshape(2, 1, w), tri)
    return out_f.reshape(t, w), out_b.reshape(t, w)


def _hgrn_readout_kernel(of_ref, ob_ref, zg_ref, g_ref, y_ref, *, n_heads):
    o = of_ref[...] + ob_ref[...]
    g = g_ref[...]
    gate = _silu(zg_ref[...].astype(F32))
    for h in range(n_heads):
        sl = slice(h * HG_DK, (h + 1) * HG_DK)
        y_ref[:, sl] = (_rms(o[:, sl], g[:, sl]) * gate[:, sl]).astype(y_ref.dtype)


def _hgrn_readout(o_fwd, o_bwd, z, g_norm):
    t, w = o_fwd.shape
    tm = _divisor_tile(t, 1088)
    return pl.pallas_call(
        functools.partial(_hgrn_readout_kernel, n_heads=w // HG_DK),
        grid=(t // tm,),
        in_specs=[
            pl.BlockSpec((tm, w), lambda i: (i, 0)),
            pl.BlockSpec((tm, w), lambda i: (i, 0)),
            pl.BlockSpec((tm, w), lambda i: (i, Z_HG // w)),
            pl.BlockSpec((1, w), lambda i: (0, 0)),
        ],
        out_specs=pl.BlockSpec((tm, w), lambda i: (i, 0)),
        out_shape=jax.ShapeDtypeStruct((t, w), BF),
        compiler_params=_cparams(("parallel",)),
        name="hgrn_readout",
    )(o_fwd, o_bwd, z, g_norm.reshape(1, w))


def _segment_tile(i, tc, p_rows, lc):
    per_batch = p_rows // tc
    j = i % per_batch
    nctx = lc // tc
    first = jnp.logical_or(j == 0, j == nctx)
    last = jnp.logical_or(j == nctx - 1, j == per_batch - 1)
    in_ctx = j < nctx
    seg_pos0 = jnp.where(in_ctx, j, j - nctx) * tc
    seg_len = jnp.where(in_ctx, lc, p_rows - lc)
    return first, last, seg_pos0, seg_len


def _halo_specs(tc, width, col_block, n_rows):
    per = tc // HALO
    last_blk = n_rows // HALO - 1
    return [
        pl.BlockSpec((HALO, width), lambda i: (jnp.maximum(i * per - 1, 0), col_block)),
        pl.BlockSpec((tc, width), lambda i: (i, col_block)),
        pl.BlockSpec((HALO, width), lambda i: (jnp.minimum((i + 1) * per, last_blk), col_block)),
    ]


def _conv_kernel(prev_ref, cur_ref, next_ref, w_ref, b_ref, lg_ref, lb_ref, y_ref, buf, shifted, *, tc, p_rows, lc, ch):
    first, last, _, _ = _segment_tile(pl.program_id(0), tc, p_rows, lc)

    def glu(zz):
        zz = zz.astype(F32)
        return zz[:, :ch] * _sigmoid(zz[:, ch:])

    buf[0:HALO, :] = jnp.where(first, 0.0, glu(prev_ref[...]))
    buf[HALO:HALO + tc, :] = glu(cur_ref[...])
    buf[HALO + tc:, :] = jnp.where(last, 0.0, glu(next_ref[...]))
    span = shifted.shape[1]
    for s in range(1, SUBLANES):
        shifted[s - 1] = buf[s:s + span, :]
    w = w_ref[...]
    acc = jnp.zeros((tc, ch), F32) + b_ref[...]
    for kk in range(CONV_WIDTH):
        blk, s = divmod(HALO - CONV_WIDTH // 2 + kk, SUBLANES)
        r0 = blk * SUBLANES
        win = buf[r0:r0 + tc, :] if s == 0 else shifted[s - 1, r0:r0 + tc, :]
        acc = acc + win * w[kk:kk + 1, :]
    mu = jnp.mean(acc, axis=-1, keepdims=True)
    cen = acc - mu
    var = jnp.mean(cen * cen, axis=-1, keepdims=True)
    y = cen * lax.rsqrt(var + EPS) * lg_ref[...] + lb_ref[...]
    y_ref[...] = _silu(y).astype(y_ref.dtype)


def _conv_mixer(z, w_dw, b_dw, ln_g, ln_b, dims):
    t = z.shape[0]
    p_rows, lc, _ = dims
    ch = w_dw.shape[1]
    tc = _divisor_tile(lc, 256)
    row = lambda a: a.reshape(1, ch)
    const = lambda i: (0, 0)
    return pl.pallas_call(
        functools.partial(_conv_kernel, tc=tc, p_rows=p_rows, lc=lc, ch=ch),
        grid=(t // tc,),
        in_specs=_halo_specs(tc, 2 * ch, Z_CONV // (2 * ch), t) + [
            pl.BlockSpec((CONV_WIDTH, ch), const),
            pl.BlockSpec((1, ch), const), pl.BlockSpec((1, ch), const), pl.BlockSpec((1, ch), const),
        ],
        out_specs=pl.BlockSpec((tc, ch), lambda i: (i, 0)),
        out_shape=jax.ShapeDtypeStruct((t, ch), BF),
        scratch_shapes=[pltpu.VMEM((tc + 2 * HALO, ch), F32),
                        pltpu.VMEM((SUBLANES - 1, tc + 2 * HALO - SUBLANES, ch), F32)],
        compiler_params=_cparams(("parallel",)),
        name="conv_mixer",
    )(z, z, z, w_dw, row(b_dw), row(ln_g), row(ln_b))


def _pool_kernel(prev_ref, cur_ref, next_ref, w_ref, s_ref, y_ref, buf, *, tc, p_rows, lc):
    first, last, seg_pos0, seg_len = _segment_tile(pl.program_id(0), tc, p_rows, lc)
    buf[0:HALO, :] = jnp.where(first, 0.0, prev_ref[...].astype(F32))
    buf[HALO:HALO + tc, :] = cur_ref[...].astype(F32)
    buf[HALO + tc:, :] = jnp.where(last, 0.0, next_ref[...].astype(F32))
    pos = seg_pos0 + lax.broadcasted_iota(jnp.int32, (tc, 1), 0)
    scale = s_ref[...]
    for gi, win in enumerate(POOL_WINDOWS):
        sl = slice(gi * POOL_GROUP, (gi + 1) * POOL_GROUP)
        half = win // 2
        tot = jnp.zeros((tc, POOL_GROUP), F32)
        for u in range(-half, half):
            tot = tot + buf[HALO + u:HALO + u + tc, sl]
        cnt = (jnp.minimum(pos + half, seg_len) - jnp.maximum(pos - half, 0)).astype(F32)
        pooled = tot / cnt - buf[HALO:HALO + tc, sl]
        y = jnp.dot(pooled.astype(BF), w_ref[gi].astype(BF), preferred_element_type=F32)
        y_ref[:, sl] = (y * scale[:, sl]).astype(y_ref.dtype)


def _pool_mixer(z, w_pool, scale, dims):
    t = z.shape[0]
    p_rows, lc, _ = dims
    ch = scale.shape[0]
    tc = _divisor_tile(lc, 256)
    return pl.pallas_call(
        functools.partial(_pool_kernel, tc=tc, p_rows=p_rows, lc=lc),
        grid=(t // tc,),
        in_specs=_halo_specs(tc, ch, Z_POOL // ch, t) + [
            pl.BlockSpec(w_pool.shape, lambda i: (0, 0, 0)),
            pl.BlockSpec((1, ch), lambda i: (0, 0)),
        ],
        out_specs=pl.BlockSpec((tc, ch), lambda i: (i, 0)),
        out_shape=jax.ShapeDtypeStruct((t, ch), BF),
        scratch_shapes=[pltpu.VMEM((tc + 2 * HALO, ch), F32)],
        compiler_params=_cparams(("parallel",)),
        name="pool_mixer",
    )(z, z, z, w_pool, scale.reshape(1, ch))


def _merge_kernel(y0, y1, y2, y3, g0, g1, g2, g3, w_ref, o_ref, w_s):
    @pl.when(pl.program_id(1) == 0)
    def _():
        w_s[...] = w_ref[0].astype(BF)

    acc = None
    for n, (y_ref, g_ref) in enumerate(((y0, g0), (y1, g1), (y2, g2), (y3, g3))):
        proj = jnp.dot(y_ref[...], w_s[n], preferred_element_type=F32)
        gate = 0.5 * jnp.tanh(0.5 * g_ref[...].astype(F32)) + 0.5
        term = gate * proj
        acc = term if acc is None else acc + term
    o_ref[...] = acc.astype(o_ref.dtype)


def _merge(ys, z, w_branch, l):
    t, bw = ys[0].shape
    _, nb, _, d = w_branch.shape
    tm = _divisor_tile(t, 544)
    tn = _divisor_tile(d, 1024, 128)
    npn = d // tn

    def gate_spec(n):
        return pl.BlockSpec((tm, tn), lambda j, i: (i, Z_GATE // tn + n * npn + j))

    return pl.pallas_call(
        _merge_kernel,
        grid=(npn, t // tm),
        in_specs=[pl.BlockSpec((tm, bw), lambda j, i: (i, 0))] * nb + [gate_spec(n) for n in range(nb)] + [
            pl.BlockSpec((1, nb, bw, tn), lambda j, i: (l, 0, 0, j))],
        out_specs=pl.BlockSpec((tm, tn), lambda j, i: (i, j)),
        out_shape=jax.ShapeDtypeStruct((t, d), BF),
        scratch_shapes=[pltpu.VMEM((nb, bw, tn), BF)],
        compiler_params=_cparams(("parallel", "arbitrary")),
        name="branch_merge",
    )(*ys, z, z, z, z, w_branch)


def _residual_epilogue(y, x_ref, gpost_ref, gate_ref, gnext_ref, sh_ref, sc_ref, xo_ref, ho_ref, b, n_batch, cm,
                       rows=slice(None)):
    cm = None if cm is None else cm[rows]
    gate_g = _row_vec(gate_ref, b, n_batch, cm) * gpost_ref[...]
    x_new = x_ref[rows, :] + _rms(y, gate_g)
    xo_ref[rows, :] = x_new
    if ho_ref is None:
        return None
    shift = _row_vec(sh_ref, b, n_batch, cm)
    scale_g = (1.0 + _row_vec(sc_ref, b, n_batch, cm)) * gnext_ref[...]
    h = _rms(x_new, scale_g) + shift
    ho_ref[rows, :] = h.astype(ho_ref.dtype)
    return h


def _route_top2(logits, n_experts):
    lane = lax.broadcasted_iota(jnp.int32, logits.shape, 1)
    lg = jnp.where(lane < n_experts, logits, NEG_BIG)
    m1 = jnp.max(lg, axis=-1, keepdims=True)
    i1 = jnp.min(jnp.where(lg == m1, lane, 1 << 20), axis=-1, keepdims=True)
    lg2 = jnp.where(lane == i1, NEG_BIG, lg)
    m2 = jnp.max(lg2, axis=-1, keepdims=True)
    i2 = jnp.min(jnp.where(lg2 == m2, lane, 1 << 20), axis=-1, keepdims=True)
    e2 = jnp.exp(m2 - m1)
    w1 = 1.0 / (1.0 + e2)
    w2 = e2 / (1.0 + e2)
    out = jnp.where(lane == 0, i1.astype(F32), 0.0)
    out = jnp.where(lane == 1, i2.astype(F32), out)
    out = jnp.where(lane == 2, w1, out)
    return jnp.where(lane == 3, w2, out)


def _router_logits(h, wr_ref):
    h_hi = h.astype(BF)
    h_lo = (h - h_hi.astype(F32)).astype(BF)
    return (jnp.dot(h_hi, wr_ref[0], preferred_element_type=F32) + jnp.dot(h_hi, wr_ref[1], preferred_element_type=F32)
            + jnp.dot(h_lo, wr_ref[0], preferred_element_type=F32))


def _out_proj_kernel(*refs, tm, p_rows, lc, n_batch, emit_h, n_experts):
    m_ref, w_ref, x_ref, gpost_ref, gate_ref = refs[:5]
    pos = 5
    gnext_ref = sh_ref = sc_ref = wr_ref = None
    if emit_h:
        gnext_ref, sh_ref, sc_ref = refs[pos:pos + 3]
        pos += 3
    if n_experts:
        wr_ref = refs[pos]
        pos += 1
    xo_ref = refs[pos]
    pos += 1
    ho_ref = route_ref = None
    if emit_h:
        ho_ref = refs[pos]
        pos += 1
    if n_experts:
        route_ref = refs[pos]
        pos += 1
    n_split = EPILOGUE_SPLIT if tm % (EPILOGUE_SPLIT * 16) == 0 else 1
    rs = tm // n_split
    groups = [slice(s * rs, (s + 1) * rs) for s in range(n_split)]

    def body(b, cm):
        ys = [jnp.dot(m_ref[rows, :], w_ref[0], preferred_element_type=F32) for rows in groups]
        for rows, y in zip(groups, ys):
            h = _residual_epilogue(y, x_ref, gpost_ref, gate_ref, gnext_ref, sh_ref, sc_ref, xo_ref, ho_ref, b,
                                   n_batch, cm, rows)
            if n_experts:
                route_ref[rows, :] = _route_top2(_router_logits(h, wr_ref), n_experts)

    _by_tile_kind(pl.program_id(0), tm, p_rows, lc, body)


def _out_proj(m, w_stack, lw, xs, g_post, mods, l, j_gate, nxt, dims, router=None, name="out_proj"):
    t, d = xs.shape
    k = m.shape[1]
    p_rows, lc, n_batch = dims
    tm = _divisor_tile(p_rows, 544 if k <= d else 272)
    const = lambda i: (0, 0)
    rowtile = pl.BlockSpec((tm, d), lambda i: (i, 0))
    args = [m, w_stack, xs, g_post.reshape(1, d), mods]
    in_specs = [pl.BlockSpec((tm, k), lambda i: (i, 0)),
                pl.BlockSpec((1, k, d), lambda i: (lw, 0, 0), pipeline_mode=pl.Buffered(1)),
                rowtile, pl.BlockSpec((1, d), const), _mod_spec(l, j_gate, d)]
    out_shape = [jax.ShapeDtypeStruct((t, d), F32)]
    out_specs = [rowtile]
    n_experts = 0
    if nxt is not None:
        g_next, l_next, j_shift, j_scale = nxt
        args += [g_next.reshape(1, d), mods, mods]
        in_specs += [pl.BlockSpec((1, d), const), _mod_spec(l_next, j_shift, d), _mod_spec(l_next, j_scale, d)]
        out_shape.append(jax.ShapeDtypeStruct((t, d), BF if router is None else F32))
        out_specs.append(rowtile)
    if router is not None:
        wr, n_experts = router
        args.append(wr)
        in_specs.append(pl.BlockSpec(wr.shape, lambda i: (0, 0, 0)))
        out_shape.append(jax.ShapeDtypeStruct((t, 128), F32))
        out_specs.append(pl.BlockSpec((tm, 128), lambda i: (i, 0)))
    return pl.pallas_call(
        functools.partial(_out_proj_kernel, tm=tm, p_rows=p_rows, lc=lc, n_batch=n_batch,
                          emit_h=nxt is not None, n_experts=n_experts),
        grid=(t // tm,),
        in_specs=in_specs,
        out_specs=out_specs,
        out_shape=out_shape,
        compiler_params=_cparams(("parallel",)),
        name=name,
    )(*args)


def _swiglu_kernel(x_ref, w1_ref, w3_ref, o_ref, w1_s, w3_s):
    @pl.when(pl.program_id(1) == 0)
    def _():
        w1_s[...] = w1_ref[0].astype(BF)
        w3_s[...] = w3_ref[0].astype(BF)

    x = x_ref[...]
    a = jnp.dot(x, w1_s[...], preferred_element_type=F32)
    g = jnp.dot(x, w3_s[...], preferred_element_type=F32)
    o_ref[...] = (_silu(a) * g).astype(o_ref.dtype)


def _swiglu_up(h, w1, w3, jl):
    t, d = h.shape
    dff = w1.shape[2]
    tm = _divisor_tile(t, 1088)
    tn = _divisor_tile(dff, 512, 128)
    return pl.pallas_call(
        _swiglu_kernel,
        grid=(dff // tn, t // tm),
        in_specs=[
            pl.BlockSpec((tm, d), lambda j, i: (i, 0)),
            pl.BlockSpec((1, d, tn), lambda j, i: (jl, 0, j)),
            pl.BlockSpec((1, d, tn), lambda j, i: (jl, 0, j)),
        ],
        out_specs=pl.BlockSpec((tm, tn), lambda j, i: (i, j)),
        out_shape=jax.ShapeDtypeStruct((t, dff), BF),
        scratch_shapes=[pltpu.VMEM((d, tn), BF), pltpu.VMEM((d, tn), BF)],
        compiler_params=_cparams(("parallel", "arbitrary")),
        name="swiglu_up",
    )(h, w1, w3)


def _gather_rows_kernel(idx_ref, src_ref, o_ref, *scratch, tr):
    base = pl.program_id(0) * tr
    dst, sem = (o_ref, scratch[0]) if len(scratch) == 1 else scratch

    def copy(r):
        return pltpu.make_async_copy(src_ref.at[pl.ds(idx_ref[base + r], 1), :], dst.at[pl.ds(r, 1), :], sem)

    def issue(g, carry):
        for u in range(GATHER_UNROLL):
            copy(g * GATHER_UNROLL + u).start(priority=u % 2)
        return carry

    lax.fori_loop(0, tr // GATHER_UNROLL, issue, 0)
    pltpu.make_async_copy(src_ref.at[pl.ds(0, tr), :], dst.at[pl.ds(0, tr), :], sem).wait()
    if dst is not o_ref:
        o_ref[...] = dst[...].astype(o_ref.dtype)


def _gather_rows(src, idx, out_dtype):
    n, d = src.shape
    assert src.dtype == F32
    rows = idx.shape[0]
    tr = _divisor_tile(rows, MOE_GATHER_ROWS, 8)
    staging = [] if out_dtype == src.dtype else [pltpu.VMEM((tr, d), src.dtype)]
    return pl.pallas_call(
        functools.partial(_gather_rows_kernel, tr=tr),
        grid_spec=pltpu.PrefetchScalarGridSpec(
            num_scalar_prefetch=1,
            grid=(rows // tr,),
            in_specs=[pl.BlockSpec(memory_space=pl.ANY)],
            out_specs=pl.BlockSpec((tr, d), lambda i, idx_ref: (i, 0)),
            scratch_shapes=staging + [pltpu.SemaphoreType.DMA(())],
        ),
        out_shape=jax.ShapeDtypeStruct((rows, d), out_dtype),
        compiler_params=_cparams(("arbitrary",)),
        name="moe_gather",
    )(idx, src)


def _expert_rows_loop(st_ref, nt_ref, n_slots, load_copy, store_copy, compute):
    e, n = pl.program_id(0), pl.program_id(1)
    n_exp, n_col = pl.num_programs(0), pl.num_programs(1)
    start, n_tiles = st_ref[e], nt_ref[e]
    first_step = jnp.logical_and(e == 0, n == 0)
    last_step = jnp.logical_and(e == n_exp - 1, n == n_col - 1)
    e_next = jnp.minimum(jnp.where(n == n_col - 1, e + 1, e), n_exp - 1)
    ahead = n_slots - 1

    @pl.when(jnp.logical_and(first_step, n_tiles > 0))
    def _():
        load_copy(start, 0, 0).start(priority=1)

    for k in range(1, ahead):
        @pl.when(n_tiles > k)
        def _():
            load_copy(start, k, k).start(priority=1)

    def body(i, carry):
        load_copy(start, i, i % n_slots).wait()

        @pl.when(i + ahead < n_tiles)
        def _():
            load_copy(start, i + ahead, (i + ahead) % n_slots).start(priority=1)

        @pl.when(i >= 2)
        def _():
            store_copy(start, i - 2, i % 2).wait()

        compute(i % n_slots, i % 2)
        store_copy(start, i, i % 2).start(priority=1)
        return carry

    lax.fori_loop(0, n_tiles, body, 0)

    @pl.when(jnp.logical_and(jnp.logical_not(last_step), nt_ref[e_next] > 0))
    def _():
        load_copy(st_ref[e_next], 0, 0).start(priority=1)

    @pl.when(n_tiles >= 2)
    def _():
        store_copy(start, n_tiles - 2, n_tiles % 2).wait()

    @pl.when(n_tiles >= 1)
    def _():
        store_copy(start, n_tiles - 1, (n_tiles - 1) % 2).wait()


def _zero_tail_rows(out_hbm, zbuf, sem, first_row, col0, tn):
    zbuf[...] = jnp.zeros_like(zbuf)
    n_chunks = (out_hbm.shape[0] - first_row) // MOE_ROW_ALIGN

    def copy(c):
        r0 = pl.multiple_of(first_row + c * MOE_ROW_ALIGN, MOE_ROW_ALIGN)
        return pltpu.make_async_copy(zbuf, out_hbm.at[pl.ds(r0, MOE_ROW_ALIGN), pl.ds(col0, tn)], sem)

    def issue(c, carry):
        copy(c).start()
        return carry

    def drain(c, carry):
        copy(c).wait()
        return carry

    lax.fori_loop(0, n_chunks, issue, 0)
    lax.fori_loop(0, n_chunks, drain, 0)


def _moe_up_kernel(st_ref, nt_ref, x_hbm, w1_ref, w3_ref, a_hbm, w1_s, w3_s, xbuf, obuf, zbuf, sem_in, sem_out, sem_z,
                   *, tr, tn):
    e, n = pl.program_id(0), pl.program_id(1)
    w1_s[...] = w1_ref[0, 0].astype(BF)
    w3_s[...] = w3_ref[0, 0].astype(BF)
    col0 = pl.multiple_of(n * tn, tn)

    def rows(start, i):
        return pl.ds(pl.multiple_of(start + i * tr, MOE_ROW_ALIGN), tr)

    def load_copy(start, i, slot):
        return pltpu.make_async_copy(x_hbm.at[rows(start, i)], xbuf.at[slot], sem_in.at[slot])

    def store_copy(start, i, slot):
        return pltpu.make_async_copy(obuf.at[slot], a_hbm.at[rows(start, i), pl.ds(col0, tn)], sem_out.at[slot])

    def compute(in_slot, out_slot):
        x = xbuf[in_slot]
        a = jnp.dot(x, w1_s[...], preferred_element_type=F32)
        g = jnp.dot(x, w3_s[...], preferred_element_type=F32)
        obuf[out_slot] = (_silu(a) * g).astype(obuf.dtype)

    _expert_rows_loop(st_ref, nt_ref, xbuf.shape[0], load_copy, store_copy, compute)

    @pl.when(e == pl.num_programs(0) - 1)
    def _():
        _zero_tail_rows(a_hbm, zbuf, sem_z, st_ref[e] + nt_ref[e] * tr, col0, tn)


def _moe_up(xs, w1, w3, jl, starts, n_tiles, tr):
    rows, d = xs.shape
    n_exp, dff = w1.shape[1], w1.shape[3]
    tn = _divisor_tile(dff, 512, 128)
    return pl.pallas_call(
        functools.partial(_moe_up_kernel, tr=tr, tn=tn),
        grid_spec=pltpu.PrefetchScalarGridSpec(
            num_scalar_prefetch=2,
            grid=(n_exp, dff // tn),
            in_specs=[
                pl.BlockSpec(memory_space=pl.ANY),
                pl.BlockSpec((1, 1, d, tn), lambda e, n, st, nt: (jl, e, 0, n)),
                pl.BlockSpec((1, 1, d, tn), lambda e, n, st, nt: (jl, e, 0, n)),
            ],
            out_specs=pl.BlockSpec(memory_space=pl.ANY),
            scratch_shapes=[pltpu.VMEM((d, tn), BF), pltpu.VMEM((d, tn), BF),
                            pltpu.VMEM((MOE_UP_SLOTS, tr, d), BF), pltpu.VMEM((2, tr, tn), BF),
                            pltpu.VMEM((MOE_ROW_ALIGN, tn), BF),
                            pltpu.SemaphoreType.DMA((MOE_UP_SLOTS,)), pltpu.SemaphoreType.DMA((2,)),
                            pltpu.SemaphoreType.DMA(())],
        ),
        out_shape=jax.ShapeDtypeStruct((rows, dff), BF),
        compiler_params=_cparams(("arbitrary", "arbitrary")),
        name="moe_up",
    )(starts, n_tiles, xs, w1, w3)


def _moe_down_kernel(st_ref, nt_ref, a_hbm, w_ref, y_hbm, w_s, abuf, obuf, zbuf, sem_in, sem_out, sem_z, *, tr, tn):
    e, n = pl.program_id(0), pl.program_id(1)
    w_s[...] = w_ref[0, 0].astype(BF)
    col0 = pl.multiple_of(n * tn, tn)

    def rows(start, i):
        return pl.ds(pl.multiple_of(start + i * tr, MOE_ROW_ALIGN), tr)

    def load_copy(start, i, slot):
        return pltpu.make_async_copy(a_hbm.at[rows(start, i)], abuf.at[slot], sem_in.at[slot])

    def store_copy(start, i, slot):
        return pltpu.make_async_copy(obuf.at[slot], y_hbm.at[rows(start, i), pl.ds(col0, tn)], sem_out.at[slot])

    def compute(in_slot, out_slot):
        obuf[out_slot] = jnp.dot(abuf[in_slot], w_s[...], preferred_element_type=F32)

    _expert_rows_loop(st_ref, nt_ref, abuf.shape[0], load_copy, store_copy, compute)

    @pl.when(e == pl.num_programs(0) - 1)
    def _():
        _zero_tail_rows(y_hbm, zbuf, sem_z, st_ref[e] + nt_ref[e] * tr, col0, tn)


def _moe_down(act, w2, jl, starts, n_tiles, tr):
    rows, dff = act.shape
    n_exp, d = w2.shape[1], w2.shape[3]
    tn = _divisor_tile(d, 512, 128)
    return pl.pallas_call(
        functools.partial(_moe_down_kernel, tr=tr, tn=tn),
        grid_spec=pltpu.PrefetchScalarGridSpec(
            num_scalar_prefetch=2,
            grid=(n_exp, d // tn),
            in_specs=[
                pl.BlockSpec(memory_space=pl.ANY),
                pl.BlockSpec((1, 1, dff, tn), lambda e, n, st, nt: (jl, e, 0, n)),
            ],
            out_specs=pl.BlockSpec(memory_space=pl.ANY),
            scratch_shapes=[pltpu.VMEM((dff, tn), BF), pltpu.VMEM((MOE_DOWN_SLOTS, tr, dff), BF),
                            pltpu.VMEM((2, tr, tn), F32), pltpu.VMEM((MOE_ROW_ALIGN, tn), F32),
                            pltpu.SemaphoreType.DMA((MOE_DOWN_SLOTS,)), pltpu.SemaphoreType.DMA((2,)),
                            pltpu.SemaphoreType.DMA(())],
        ),
        out_shape=jax.ShapeDtypeStruct((rows, d), F32),
        compiler_params=_cparams(("arbitrary", "arbitrary")),
        name="moe_down",
    )(starts, n_tiles, act, w2)


def _combine_kernel(*refs, tm, p_rows, lc, n_batch, emit_h):
    ya_ref, yb_ref, rt_ref, x_ref, gpost_ref, gate_ref = refs[:6]
    gnext_ref = sh_ref = sc_ref = ho_ref = None
    if emit_h:
        gnext_ref, sh_ref, sc_ref, xo_ref, ho_ref = refs[6:]
    else:
        (xo_ref,) = refs[6:]
    def body(b, cm):
        rt = rt_ref[...]
        y = rt[:, 2:3] * ya_ref[...] + rt[:, 3:4] * yb_ref[...]
        _residual_epilogue(y, x_ref, gpost_ref, gate_ref, gnext_ref, sh_ref, sc_ref, xo_ref, ho_ref, b, n_batch, cm)

    _by_tile_kind(pl.program_id(0), tm, p_rows, lc, body)


def _moe_combine(y2, route, xs, g_post, mods, l, j_gate, nxt, dims):
    t, d = xs.shape
    p_rows, lc, n_batch = dims
    tm = _divisor_tile(p_rows, 544)
    nt = t // tm
    const = lambda i: (0, 0)
    rowtile = pl.BlockSpec((tm, d), lambda i: (i, 0))
    args = [y2, y2, route, xs, g_post.reshape(1, d), mods]
    in_specs = [rowtile, pl.BlockSpec((tm, d), lambda i: (i + nt, 0)), pl.BlockSpec((tm, 128), lambda i: (i, 0)),
                rowtile, pl.BlockSpec((1, d), const), _mod_spec(l, j_gate, d)]
    out_shape = [jax.ShapeDtypeStruct((t, d), F32)]
    out_specs = [rowtile]
    if nxt is not None:
        g_next, l_next, j_shift, j_scale = nxt
        args += [g_next.reshape(1, d), mods, mods]
        in_specs += [pl.BlockSpec((1, d), const), _mod_spec(l_next, j_shift, d), _mod_spec(l_next, j_scale, d)]
        out_shape.append(jax.ShapeDtypeStruct((t, d), BF))
        out_specs.append(rowtile)
    return pl.pallas_call(
        functools.partial(_combine_kernel, tm=tm, p_rows=p_rows, lc=lc, n_batch=n_batch, emit_h=nxt is not None),
        grid=(nt,),
        in_specs=in_specs,
        out_specs=out_specs,
        out_shape=out_shape,
        compiler_params=_cparams(("parallel",)),
        name="moe_combine",
    )(*args)


def _moe_ffn(h, route, w1, w3, w2, jl, xs, g_post, mods, l, nxt, dims):
    t, d = h.shape
    n_exp = w1.shape[1]
    tr = MOE_ROW_TILE
    expert = jnp.concatenate([route[:, 0], route[:, 1]]).astype(jnp.int32)
    token = jnp.concatenate([jnp.arange(t, dtype=jnp.int32)] * 2)
    onehot = (expert[:, None] == jnp.arange(n_exp, dtype=jnp.int32)[None, :]).astype(jnp.int32)
    rank = jnp.sum((jnp.cumsum(onehot, axis=0) - onehot) * onehot, axis=1)
    counts = jnp.sum(onehot, axis=0)
    padded = ((counts + MOE_ROW_ALIGN - 1) // MOE_ROW_ALIGN) * MOE_ROW_ALIGN
    ends = jnp.cumsum(padded)
    starts = (ends - padded).astype(jnp.int32)
    n_tiles = ((counts + tr - 1) // tr).astype(jnp.int32)
    dest = starts[expert] + rank
    rows = 2 * t + n_exp * MOE_ROW_ALIGN + tr
    rows = ((rows + MOE_GATHER_ROWS - 1) // MOE_GATHER_ROWS) * MOE_GATHER_ROWS
    row_token = jnp.zeros((rows,), jnp.int32).at[dest].set(token)

    xs_sorted = _gather_rows(h, row_token, BF)
    act = _moe_up(xs_sorted, w1, w3, jl, starts, n_tiles, tr)
    y_sorted = _moe_down(act, w2, jl, starts, n_tiles, tr)
    y2 = _gather_rows(y_sorted, dest, F32)
    return _moe_combine(y2, route, xs, g_post, mods, l, 5, nxt, dims)


def _pad_heads(w, n_heads, width):
    r = w.shape[0]
    w3 = w.reshape(r, n_heads, width)
    return jnp.pad(w3, ((0, 0), (0, 0), (0, HEAD_PAD - width))).reshape(r, n_heads * HEAD_PAD)


def _rope_tables(l_lat, lc):
    tt = jnp.arange(l_lat)
    row = (tt // GRID_W).astype(F32)
    col = (tt % GRID_W).astype(F32)
    axis_dim = MLA_ROPE // 2
    inv_freq = 1.0 / (ROPE_THETA ** (jnp.arange(0, axis_dim, 2, dtype=F32) / axis_dim))
    ar, ac = row[:, None] * inv_freq, col[:, None] * inv_freq
    ang = jnp.concatenate([ar, ar, ac, ac], axis=-1)
    cos = jnp.concatenate([jnp.ones((lc, MLA_ROPE), F32), jnp.cos(ang)], axis=0)
    sin = jnp.concatenate([jnp.zeros((lc, MLA_ROPE), F32), jnp.sin(ang)], axis=0)
    padc = jnp.zeros((lc + l_lat, 128 - MLA_ROPE), F32)
    pm = np.zeros((128, 128), np.float32)
    q4 = MLA_ROPE // 4
    for base in (0, 2 * q4):
        for i in range(q4):
            pm[base + q4 + i, base + i] = -1.0
            pm[base + i, base + q4 + i] = 1.0
    return jnp.concatenate([cos, padc], axis=1), jnp.concatenate([sin, padc], axis=1), jnp.asarray(pm, BF)


def kernel(x, c, ctx, c_ctx, w_mod, b_mod, g_mix_pre, g_mix_post, g_ffn_pre, g_ffn_post, w_in, mla_g_q, mla_g_kv, mla_w_uq, mla_w_ukv, hg_lb_logits, hg_g_norm, conv_w, conv_b, conv_ln_g, conv_ln_b, pool_w, pool_scale, w_branch, w_out, ffn_w1, ffn_w3, ffn_w2, moe_router, moe_w1, moe_w3, moe_w2):
    n_batch, l_lat, d = x.shape
    lc = ctx.shape[1]
    depth = w_in.shape[0]
    p_rows = lc + l_lat
    t = n_batch * p_rows
    dims = (p_rows, lc, n_batch)
    assert n_batch + 1 <= 8 and lc % HG_CHUNK == 0 and l_lat % HG_CHUNK == 0 and l_lat % GRID_W == 0
    n_heads = mla_w_uq.shape[2] // MLA_QK

    xs = jnp.concatenate([ctx, x], axis=1).reshape(t, d)
    cvec = jnp.zeros((8, d), F32).at[:n_batch].set(c).at[n_batch].set(c_ctx)
    mods = _modvecs(cvec, w_mod, b_mod)

    lb = jnp.cumsum(jax.nn.softmax(hg_lb_logits.astype(F32), axis=1), axis=1)
    lb = lb - lb[:, :1]
    cos, sin, pm = _rope_tables(l_lat, lc)

    w_in_t = jnp.swapaxes(w_in, 1, 2)
    w_out_bf = w_out.astype(BF)
    ffn_w2_bf = ffn_w2.astype(BF)

    h = _modulate(xs, g_mix_pre[0], mods, 0, 0, 1, dims)
    for l in range(depth):
        last = l == depth - 1
        z = _in_proj(h, w_in_t, l, BF)
        q3, k3, v3 = _mla_proj(z, mla_g_q[l], mla_g_kv[l], _pad_heads(mla_w_uq[l], n_heads, MLA_QK),
                               mla_w_ukv[l], cos, sin, pm, dims)
        y_mla = _attention(q3, k3, v3, dims)
        y_hg = _hgrn_readout(*_hgrn(z, lb[:, l], dims), z, hg_g_norm[l])
        y_conv = _conv_mixer(z, conv_w[l], conv_b[l], conv_ln_g[l], conv_ln_b[l], dims)
        y_pool = _pool_mixer(z, pool_w[l], pool_scale[l], dims)
        m = _merge([y_mla, y_hg, y_conv, y_pool], z, w_branch, l)
        j = l // 2
        nxt_ffn = (g_ffn_pre[l], l, 3, 4)
        nxt_mix = None if last else (g_mix_pre[l + 1], l + 1, 0, 1)
        if l % 2 == 0:
            xs, h2 = _out_proj(m, w_out_bf, l, xs, g_mix_post[l], mods, l, 2, nxt_ffn, dims)
            act = _swiglu_up(h2, ffn_w1, ffn_w3, j)
            res = _out_proj(act, ffn_w2_bf, j, xs, g_ffn_post[l], mods, l, 5, nxt_mix, dims, name="ffn_down")
        else:
            n_exp = moe_router.shape[2]
            wr = jnp.pad(moe_router[j], ((0, 0), (0, 128 - n_exp)))
            wr_hi = wr.astype(BF)
            wr2 = jnp.stack([wr_hi, (wr - wr_hi.astype(F32)).astype(BF)])
            xs, h2, route = _out_proj(m, w_out_bf, l, xs, g_mix_post[l], mods, l, 2, nxt_ffn, dims,
                                      router=(wr2, n_exp))
            res = _moe_ffn(h2, route, moe_w1, moe_w3, moe_w2, j, xs, g_ffn_post[l], mods, l, nxt_mix, dims)
        if last:
            (xs,) = res
        else:
            xs, h = res
    return xs.reshape(n_batch, p_rows, d)[:, lc:, :]
```

```python
import functools

import numpy as np
import jax
import jax.numpy as jnp
from jax import lax
from jax.experimental import pallas as pl
from jax.experimental.pallas import tpu as pltpu

F32 = jnp.float32
BF = jnp.bfloat16

EPS = 1e-6
GRID_W = 64
ROPE_THETA = 10000.0
N_BRANCH = 4
MLA_NOPE, MLA_ROPE, MLA_V = 128, 64, 128
MLA_QK = MLA_NOPE + MLA_ROPE
HEAD_PAD = 256
HG_DK = 128
CONV_WIDTH = 31
POOL_WINDOWS = (2, 4, 8, 16)
POOL_GROUP = 128
LANES = 128
IN_HALF = 512
IN_GROUP = 64
SUBLANES = 8
HALO = 16
HG_CHUNK = 64
HG_STEP_CHUNKS = 2
HG_EXP_CLAMP = 80.0
NEG_BIG = -1e30
ATTN_Q_TILE = 512
MOE_ROW_TILE = 512
GATHER_UNROLL = 8
EPILOGUE_SPLIT = 2
MOE_UP_SLOTS = 5
MOE_DOWN_SLOTS = 3
MOE_ROW_ALIGN = 16
MOE_GATHER_ROWS = 1024

VMEM_LIMIT_V7X = 56 * 1024 * 1024

Z_CONV, Z_HQ, Z_HI, Z_HFF, Z_HFB, Z_HG, Z_POOL, Z_MLA, Z_GATE = 0, 1024, 1536, 2048, 2560, 3072, 3584, 4096, 5120
MLA_ZKV_OFF, MLA_ZR_OFF = 384, 640


def _cparams(sem, vmem=VMEM_LIMIT_V7X):
    return pltpu.CompilerParams(dimension_semantics=sem, vmem_limit_bytes=vmem)


def _divisor_tile(n, target, mult=16):
    best = None
    for t in range(mult, min(n, target) + 1, mult):
        if n % t == 0:
            best = t
    assert best is not None, (n, target, mult)
    return best


def _pick_row(m, r):
    rows = lax.broadcasted_iota(jnp.int32, m.shape, 0)
    return jnp.sum(jnp.where(rows == r, m, 0.0), axis=0, keepdims=True)


def _row_vec(m_ref, b, n_batch, ctx_mask):
    m = m_ref[0, 0]
    if ctx_mask is None:
        return _pick_row(m, b)
    return jnp.where(ctx_mask, _pick_row(m, n_batch), _pick_row(m, b))


def _tile_rows(i, tm, p_rows, lc):
    per_batch = p_rows // tm
    b = i // per_batch
    pos = (i % per_batch) * tm + lax.broadcasted_iota(jnp.int32, (tm, 1), 0)
    return b, pos < lc


def _by_tile_kind(i, tm, p_rows, lc, body):
    b, cm = _tile_rows(i, tm, p_rows, lc)
    has_ctx = (i % (p_rows // tm)) * tm < lc

    @pl.when(has_ctx)
    def _():
        body(b, cm)

    @pl.when(jnp.logical_not(has_ctx))
    def _():
        body(b, None)


def _rms(x, g):
    return x * lax.rsqrt(jnp.mean(x * x, axis=-1, keepdims=True) + EPS) * g


def _sigmoid(x):
    return 1.0 / (1.0 + jnp.exp(-x))


def _silu(x):
    return x * _sigmoid(x)


def _modvec_kernel(c_ref, w_ref, b_ref, o_ref):
    s = _silu(c_ref[...])
    s_hi = s.astype(BF)
    s_lo = (s - s_hi.astype(F32)).astype(BF)
    w = w_ref[0].astype(BF)
    acc = jnp.dot(s_hi, w, preferred_element_type=F32) + jnp.dot(s_lo, w, preferred_element_type=F32)
    o_ref[0, 0] = acc + b_ref[0, 0]


def _modvecs(cvec, w_mod, b_mod):
    depth, d, n6 = w_mod.shape
    nj = n6 // d
    tn = _divisor_tile(d, 1024, 128)
    npj = d // tn
    return pl.pallas_call(
        _modvec_kernel,
        grid=(depth, nj, npj),
        in_specs=[
            pl.BlockSpec((8, d), lambda l, j, n: (0, 0)),
            pl.BlockSpec((1, d, tn), lambda l, j, n: (l, 0, j * npj + n)),
            pl.BlockSpec((1, 1, 1, tn), lambda l, j, n: (l, j, 0, n)),
        ],
        out_specs=pl.BlockSpec((1, 1, 8, tn), lambda l, j, n: (l, j, 0, n)),
        out_shape=jax.ShapeDtypeStruct((depth, nj, 8, d), F32),
        compiler_params=_cparams(("parallel", "parallel", "parallel")),
        name="modvec",
    )(cvec, w_mod, b_mod.reshape(depth, nj, 1, d))


def _modulate_kernel(x_ref, g_ref, sh_ref, sc_ref, h_ref, *, tm, p_rows, lc, n_batch):
    b, cm = _tile_rows(pl.program_id(0), tm, p_rows, lc)
    shift = _row_vec(sh_ref, b, n_batch, cm)
    scale = _row_vec(sc_ref, b, n_batch, cm)
    h_ref[...] = (_rms(x_ref[...], g_ref[...]) * (1.0 + scale) + shift).astype(h_ref.dtype)


def _mod_spec(l, j, d):
    return pl.BlockSpec((1, 1, 8, d), lambda *_: (l, j, 0, 0))


def _modulate(xs, g, mods, l, j_shift, j_scale, dims):
    t, d = xs.shape
    p_rows, lc, n_batch = dims
    tm = _divisor_tile(p_rows, 544)
    return pl.pallas_call(
        functools.partial(_modulate_kernel, tm=tm, p_rows=p_rows, lc=lc, n_batch=n_batch),
        grid=(t // tm,),
        in_specs=[
            pl.BlockSpec((tm, d), lambda i: (i, 0)),
            pl.BlockSpec((1, d), lambda i: (0, 0)),
            _mod_spec(l, j_shift, d),
            _mod_spec(l, j_scale, d),
        ],
        out_specs=pl.BlockSpec((tm, d), lambda i: (i, 0)),
        out_shape=jax.ShapeDtypeStruct((t, d), BF),
        compiler_params=_cparams(("parallel",)),
        name="modulate",
    )(xs, g.reshape(1, d), mods, mods)


def _in_proj_kernel(grp_ref, cnt_ref, x_ref, *refs):
    n_blk = len(refs) - 2
    w_refs, o_ref, w_s = refs[:n_blk], refs[n_blk], refs[n_blk + 1]
    j = pl.program_id(0)
    per_half = n_blk // 2

    @pl.when(pl.program_id(1) == 0)
    def _():
        for u in range(2):
            cnt = cnt_ref[2 * j + u]
            for i in range(per_half):
                r0 = (u * per_half + i) * IN_GROUP
                w_s[r0:r0 + IN_GROUP, :] = jnp.where(i < cnt, w_refs[u * per_half + i][0], 0.0).astype(BF)

    o_ref[...] = lax.dot_general(x_ref[...], w_s[...], (((1,), (1,)), ((), ())),
                                 preferred_element_type=F32).astype(o_ref.dtype)


def _in_proj_tables(d):
    q_rank, kv_rank = (3 * d) // 16, d // 8
    bw = d // N_BRANCH
    src, o = {}, 0
    for name, width in (("mla", q_rank + kv_rank + MLA_ROPE), ("hq", bw), ("hi", bw), ("hff", bw), ("hfb", bw),
                        ("hg", bw), ("conv", 2 * bw), ("pool", bw), ("gate", N_BRANCH * d)):
        src[name] = (o, width)
        o += width
    dst = (("conv", Z_CONV), ("hq", Z_HQ), ("hi", Z_HI), ("hff", Z_HFF), ("hfb", Z_HFB), ("hg", Z_HG),
           ("pool", Z_POOL), ("mla", Z_MLA), ("gate", Z_GATE))
    n_half = (Z_GATE + N_BRANCH * d) // IN_HALF
    grp, cnt = [0] * n_half, [0] * n_half
    for name, z0 in dst:
        s0, width = src[name]
        assert z0 % IN_HALF == 0 and s0 % IN_GROUP == 0 and width % IN_GROUP == 0
        for k in range(-(-width // IN_HALF)):
            grp[z0 // IN_HALF + k] = (s0 + k * IN_HALF) // IN_GROUP
            cnt[z0 // IN_HALF + k] = min(IN_HALF, width - k * IN_HALF) // IN_GROUP
    as_i32 = lambda v: jnp.asarray(np.array(v, np.int32))
    return as_i32(grp), as_i32(cnt)


def _in_proj(x, w_in_t, l, out_dtype, tm_target=1088):
    t, k = x.shape
    n = Z_GATE + N_BRANCH * k
    tn = 2 * IN_HALF
    per_half = IN_HALF // IN_GROUP
    tm = _divisor_tile(t, tm_target)
    tables = _in_proj_tables(k)
    last_group = w_in_t.shape[1] // IN_GROUP - 1

    def w_spec(u, i):
        return pl.BlockSpec((1, IN_GROUP, k),
                            lambda j, m, grp, cnt: (l, jnp.minimum(grp[2 * j + u] + i, last_group), 0))

    return pl.pallas_call(
        _in_proj_kernel,
        grid_spec=pltpu.PrefetchScalarGridSpec(
            num_scalar_prefetch=2,
            grid=(n // tn, t // tm),
            in_specs=[pl.BlockSpec((tm, k), lambda j, m, *_: (m, 0))]
            + [w_spec(u, i) for u in range(2) for i in range(per_half)],
            out_specs=pl.BlockSpec((tm, tn), lambda j, m, *_: (m, j)),
            scratch_shapes=[pltpu.VMEM((tn, k), BF)],
        ),
        out_shape=jax.ShapeDtypeStruct((t, n), out_dtype),
        compiler_params=_cparams(("parallel", "arbitrary")),
        name="in_proj",
    )(*tables, x, *([w_in_t] * (2 * per_half)))


def _rope128(t, cos, sin, pm):
    partner = jnp.dot(t.astype(BF), pm, preferred_element_type=F32)
    return t * cos + partner * sin


def _mla_proj_kernel(z_ref, gq_ref, gkv_ref, wq_ref, wkv_ref, cos_ref, sin_ref, pm_ref,
                     q_ref, k_ref, v_ref, wq_s, wkv_s, *, n_heads, q_rank, kv_rank, scale):
    @pl.when(pl.program_id(0) == 0)
    def _():
        wq_s[...] = wq_ref[...].astype(BF)
        wkv_s[...] = wkv_ref[...].astype(BF)

    z = z_ref[...]
    zq = z[:, :q_rank].astype(F32)
    zkv = z[:, MLA_ZKV_OFF:MLA_ZKV_OFF + kv_rank].astype(F32)
    zr = z[:, MLA_ZR_OFF:MLA_ZR_OFF + 128].astype(F32)
    cos, sin, pm = cos_ref[...], sin_ref[...], pm_ref[...]
    qn = _rms(zq, gq_ref[...]).astype(BF)
    kvn = _rms(zkv, gkv_ref[...]).astype(BF)
    k_rope = _rope128(zr, cos, sin, pm)[:, :MLA_ROPE]
    for h in range(n_heads):
        qh = jnp.dot(qn, wq_s[:, h * HEAD_PAD:(h + 1) * HEAD_PAD], preferred_element_type=F32)
        q_rope = _rope128(qh[:, MLA_NOPE:], cos, sin, pm)[:, :MLA_ROPE]
        q_ref[h] = (jnp.concatenate([qh[:, :MLA_NOPE], q_rope], axis=-1) * scale).astype(q_ref.dtype)
        kvh = jnp.dot(kvn, wkv_s[:, h * HEAD_PAD:(h + 1) * HEAD_PAD], preferred_element_type=F32)
        k_ref[h] = jnp.concatenate([kvh[:, :MLA_NOPE], k_rope], axis=-1).astype(k_ref.dtype)
        v_ref[h] = jnp.concatenate([kvh[:, MLA_NOPE:], jnp.ones_like(kvh[:, MLA_NOPE:])], axis=-1).astype(v_ref.dtype)


def _mla_proj(z, g_q, g_kv, wq_pad, wkv, cos, sin, pm, dims):
    t = z.shape[0]
    p_rows = dims[0]
    q_rank, kv_rank = g_q.shape[0], g_kv.shape[0]
    n_heads = wkv.shape[1] // HEAD_PAD
    tm = _divisor_tile(p_rows, 544)
    npb = p_rows // tm
    scale = float(MLA_QK) ** -0.5 * float(np.log2(np.e))
    const = lambda i: (0, 0)
    return pl.pallas_call(
        functools.partial(_mla_proj_kernel, n_heads=n_heads, q_rank=q_rank, kv_rank=kv_rank, scale=scale),
        grid=(t // tm,),
        in_specs=[
            pl.BlockSpec((tm, 1024), lambda i: (i, Z_MLA // 1024)),
            pl.BlockSpec((1, q_rank), const),
            pl.BlockSpec((1, kv_rank), const),
            pl.BlockSpec(wq_pad.shape, const),
            pl.BlockSpec(wkv.shape, const),
            pl.BlockSpec((tm, 128), lambda i: (i % npb, 0)),
            pl.BlockSpec((tm, 128), lambda i: (i % npb, 0)),
            pl.BlockSpec((128, 128), const),
        ],
        out_specs=[
            pl.BlockSpec((n_heads, tm, MLA_QK), lambda i: (0, i, 0)),
            pl.BlockSpec((n_heads, tm, MLA_QK), lambda i: (0, i, 0)),
            pl.BlockSpec((n_heads, tm, 2 * MLA_V), lambda i: (0, i, 0)),
        ],
        out_shape=[
            jax.ShapeDtypeStruct((n_heads, t, MLA_QK), BF),
            jax.ShapeDtypeStruct((n_heads, t, MLA_QK), BF),
            jax.ShapeDtypeStruct((n_heads, t, 2 * MLA_V), BF),
        ],
        scratch_shapes=[pltpu.VMEM(wq_pad.shape, BF), pltpu.VMEM(wkv.shape, BF)],
        compiler_params=_cparams(("arbitrary",)),
        name="mla_proj",
    )(z, g_q.reshape(1, -1), g_kv.reshape(1, -1), wq_pad, wkv, cos, sin, pm)


def _attn_kernel(q_ref, k_ref, v_ref, o_ref, s0_ref, s1_ref, p0_ref, p1_ref, *, tq, lc):
    p_rows = k_ref.shape[1]
    n_tiles = (p_rows - lc) // tq
    nt = (((1,), (1,)), ((), ()))

    def tile_start(t):
        return pl.multiple_of(lc + t * tq, lc)

    def scores(t, s_ref):
        s_ref[...] = lax.dot_general(q_ref[0, pl.ds(tile_start(t), tq), :], k_ref[0], nt, preferred_element_type=F32)

    def softmax(s_ref, p_ref):
        s = s_ref[...]
        p_ref[...] = jnp.exp2((s - jnp.max(s, axis=-1, keepdims=True)).astype(BF))

    def emit(p, v, q0, n_q):
        o = jnp.dot(p, v, preferred_element_type=F32)
        o_ref[pl.ds(q0, n_q), :] = (o[:, :MLA_V] / o[:, MLA_V:MLA_V + 1]).astype(o_ref.dtype)

    def values(t, p_ref):
        emit(p_ref[...], v_ref[0], tile_start(t), tq)

    s_ctx = lax.dot_general(q_ref[0, :lc, :], k_ref[0, :lc, :], nt, preferred_element_type=F32)
    emit(jnp.exp2((s_ctx - jnp.max(s_ctx, axis=-1, keepdims=True)).astype(BF)), v_ref[0, :lc, :], 0, lc)

    scores(0, s0_ref)
    scores(1, s1_ref)
    softmax(s0_ref, p0_ref)

    def body(kk, carry):
        t = 2 * kk
        scores(t, s0_ref)
        softmax(s1_ref, p1_ref)
        values(t - 2, p0_ref)
        scores(t + 1, s1_ref)
        softmax(s0_ref, p0_ref)
        values(t - 1, p1_ref)
        return carry

    lax.fori_loop(1, n_tiles // 2, body, 0)
    softmax(s1_ref, p1_ref)
    values(n_tiles - 2, p0_ref)
    values(n_tiles - 1, p1_ref)


def _attention(q3, k3, v3, dims):
    n_heads, t, _ = q3.shape
    p_rows, lc, n_batch = dims
    l_lat = p_rows - lc
    tq = max(c for c in range(lc, ATTN_Q_TILE + 1, lc) if l_lat % (2 * c) == 0)
    return pl.pallas_call(
        functools.partial(_attn_kernel, tq=tq, lc=lc),
        grid=(n_batch, n_heads),
        in_specs=[
            pl.BlockSpec((1, p_rows, MLA_QK), lambda b, h: (h, b, 0)),
            pl.BlockSpec((1, p_rows, MLA_QK), lambda b, h: (h, b, 0)),
            pl.BlockSpec((1, p_rows, 2 * MLA_V), lambda b, h: (h, b, 0)),
        ],
        out_specs=pl.BlockSpec((p_rows, MLA_V), lambda b, h: (b, h)),
        out_shape=jax.ShapeDtypeStruct((t, n_heads * MLA_V), BF),
        scratch_shapes=[pltpu.VMEM((tq, p_rows), F32), pltpu.VMEM((tq, p_rows), F32),
                        pltpu.VMEM((tq, p_rows), BF), pltpu.VMEM((tq, p_rows), BF)],
        compiler_params=_cparams(("parallel", "parallel")),
        name="mla_attention",
    )(q3, k3, v3)


def _hgrn_gates(zq, zf, lb, tri):
    f = lb + (1.0 - lb) * _sigmoid(zf.astype(F32))
    lf = jnp.log(f)
    lf1 = lf.astype(BF)
    r1 = lf - lf1.astype(F32)
    lf2 = r1.astype(BF)
    lf3 = (r1 - lf2.astype(F32)).astype(BF)
    b_all = (jnp.dot(tri, lf1, preferred_element_type=F32) + jnp.dot(tri, lf2, preferred_element_type=F32)
             + jnp.dot(tri, lf3, preferred_element_type=F32))
    return f, b_all, jnp.sum(lf, axis=0, keepdims=True), _silu(zq.astype(F32))


def _hgrn_chunk(zq, zi, zf, lb, tri, o_ref, oi_ref, st_ref, stream, bi, rows, n_heads):
    feeds = tri > 0.5
    f, b_all, tot_all, q_all = _hgrn_gates(zq, zf, lb, tri)
    nt = (((1,), (1,)), ((), ()))
    for h in range(n_heads):
        sl = slice(h * HG_DK, (h + 1) * HG_DK)
        b, tot, q, k = b_all[:, sl], tot_all[:, sl], q_all[:, sl], 1.0 - f[:, sl]
        v = zi[:, sl]
        ref = 0.5 * tot
        q_loc = (q * jnp.exp(jnp.minimum(b - ref, HG_EXP_CLAMP))).astype(BF)
        k_loc = (k * jnp.exp(jnp.minimum(ref - b, HG_EXP_CLAMP))).astype(BF)
        a = lax.dot_general(q_loc, k_loc, nt, preferred_element_type=F32)
        a = jnp.where(feeds, a, 0.0).astype(BF)
        state_t = st_ref[stream, h]
        q_in = (q * jnp.exp(b)).astype(BF)
        o_inter = lax.dot_general(q_in, state_t.astype(BF), nt, preferred_element_type=F32)
        oi_ref[stream, rows, sl] = o_inter
        o_ref[bi, rows, sl] = jnp.dot(a, v, preferred_element_type=F32) + o_inter
        k_out = (k * jnp.exp(tot - b)).astype(BF)
        upd = lax.dot_general(v, k_out, (((0,), (0,)), ((), ())), preferred_element_type=F32)
        st_ref[stream, h] = state_t * jnp.exp(tot) + upd
    return tot_all


def _hgrn_chunk_exact(zq, zi, zf, lb, tri, o_ref, oi_ref, qs_ref, bs_ref, stream, bi, rows, rev, n_heads):
    c = tri.shape[0]
    f, b_all, _, q_all = _hgrn_gates(zq, zf, lb, tri)
    s_idx = lax.broadcasted_iota(jnp.int32, (c, 1), 0)
    lane = lax.broadcasted_iota(jnp.int32, (c, c), 1)
    for h in range(n_heads):
        sl = slice(h * HG_DK, (h + 1) * HG_DK)
        b, k = b_all[:, sl], 1.0 - f[:, sl]
        qs_ref[...] = q_all[:, sl]
        bs_ref[...] = b

        def one_row(t, a_t):
            feeds_t = (s_idx >= t) if rev else (s_idx <= t)
            decay = jnp.exp(jnp.minimum(bs_ref[pl.ds(t, 1), :] - b, 0.0))
            col = jnp.sum(jnp.where(feeds_t, qs_ref[pl.ds(t, 1), :] * k * decay, 0.0), axis=-1, keepdims=True)
            return a_t + jnp.where(lane == t, col, 0.0)

        a_t = lax.fori_loop(0, c, one_row, jnp.zeros((c, c), F32))
        o_intra = lax.dot_general(a_t.astype(BF), zi[:, sl], (((0,), (0,)), ((), ())), preferred_element_type=F32)
        o_ref[bi, rows, sl] = o_intra + oi_ref[stream, rows, sl]


def _hgrn_kernel(fq_ref, fi_ref, ff_ref, bq_ref, bi_ref, bf_ref, lb_ref, tri_ref, of_ref, ob_ref, st_ref, oi_ref,
                 qs_ref, bs_ref, *, n_heads, n_batch):
    @pl.when(pl.program_id(0) == 0)
    def _():
        st_ref[...] = jnp.zeros_like(st_ref)

    c = tri_ref.shape[1]
    work = []
    for rev, (q_ref, i_ref, f_ref, o_ref) in enumerate(((fq_ref, fi_ref, ff_ref, of_ref),
                                                       (bq_ref, bi_ref, bf_ref, ob_ref))):
        order = range(HG_STEP_CHUNKS - 1, -1, -1) if rev else range(HG_STEP_CHUNKS)
        for sub in order:
            rows = slice(sub * c, (sub + 1) * c)
            for bi in range(n_batch):
                work.append((rev, q_ref, i_ref, f_ref, o_ref, rows, bi))

    weakest = None
    for rev, q_ref, i_ref, f_ref, o_ref, rows, bi in work:
        tot = _hgrn_chunk(q_ref[bi, rows, :], i_ref[bi, rows, :], f_ref[bi, rows, :], lb_ref[rev], tri_ref[rev],
                          o_ref, oi_ref, st_ref, rev * n_batch + bi, bi, rows, n_heads)
        weakest = tot if weakest is None else jnp.minimum(weakest, tot)

    @pl.when(jnp.min(weakest) < -2.0 * HG_EXP_CLAMP)
    def _():
        for rev, q_ref, i_ref, f_ref, o_ref, rows, bi in work:
            _hgrn_chunk_exact(q_ref[bi, rows, :], i_ref[bi, rows, :], f_ref[bi, rows, :], lb_ref[rev], tri_ref[rev],
                              o_ref, oi_ref, qs_ref, bs_ref, rev * n_batch + bi, bi, rows, bool(rev), n_heads)


def _hgrn(z, lb2, dims):
    t, nz = z.shape
    p_rows, lc, n_batch = dims
    c = HG_CHUNK
    w = lb2.shape[-1]
    n_heads = w // HG_DK
    rb = HG_STEP_CHUNKS * c
    assert lc % rb == 0 and p_rows % rb == 0
    nch, nctx = p_rows // rb, lc // rb
    idx = np.arange(c)
    tri = jnp.asarray(np.stack([idx[None, :] <= idx[:, None], idx[None, :] >= idx[:, None]]), BF)
    z3 = z.reshape(n_batch, p_rows, nz)

    def back(j):
        return jnp.where(j < nctx, nctx - 1 - j, nch - 1 - (j - nctx))

    def spec(col, rev):
        if rev:
            return pl.BlockSpec((n_batch, rb, w), lambda j: (0, back(j), col // w))
        return pl.BlockSpec((n_batch, rb, w), lambda j: (0, j, col // w))

    out_f, out_b = pl.pallas_call(
        functools.partial(_hgrn_kernel, n_heads=n_heads, n_batch=n_batch),
        grid=(nch,),
        in_specs=[
            spec(Z_HQ, False), spec(Z_HI, False), spec(Z_HFF, False),
            spec(Z_HQ, True), spec(Z_HI, True), spec(Z_HFB, True),
            pl.BlockSpec((2, 1, w), lambda j: (0, 0, 0)),
            pl.BlockSpec((2, c, c), lambda j: (0, 0, 0)),
        ],
        out_specs=[pl.BlockSpec((n_batch, rb, w), lambda j: (0, j, 0)),
                   pl.BlockSpec((n_batch, rb, w), lambda j: (0, back(j), 0))],
        out_shape=[jax.ShapeDtypeStruct((n_batch, p_rows, w), F32)] * 2,
        scratch_shapes=[pltpu.VMEM((2 * n_batch, n_heads, HG_DK, HG_DK), F32),
                        pltpu.VMEM((2 * n_batch, rb, w), F32),
                        pltpu.VMEM((c, HG_DK), F32), pltpu.VMEM((c, HG_DK), F32)],
        compiler_params=_cparams(("arbitrary",)),
        name="hgrn_scan",
    )(z3, z3, z3, z3, z3, z3, lb2.reshape(2, 1, w), tri)
    return out_f.reshape(t, w), out_b.reshape(t, w)


def _hgrn_readout_kernel(of_ref, ob_ref, zg_ref, g_ref, y_ref, *, n_heads):
    o = of_ref[...] + ob_ref[...]
    g = g_ref[...]
    gate = _silu(zg_ref[...].astype(F32))
    for h in range(n_heads):
        sl = slice(h * HG_DK, (h + 1) * HG_DK)
        y_ref[:, sl] = (_rms(o[:, sl], g[:, sl]) * gate[:, sl]).astype(y_ref.dtype)


def _hgrn_readout(o_fwd, o_bwd, z, g_norm):
    t, w = o_fwd.shape
    tm = _divisor_tile(t, 1088)
    return pl.pallas_call(
        functools.partial(_hgrn_readout_kernel, n_heads=w // HG_DK),
        grid=(t // tm,),
        in_specs=[
            pl.BlockSpec((tm, w), lambda i: (i, 0)),
            pl.BlockSpec((tm, w), lambda i: (i, 0)),
            pl.BlockSpec((tm, w), lambda i: (i, Z_HG // w)),
            pl.BlockSpec((1, w), lambda i: (0, 0)),
        ],
        out_specs=pl.BlockSpec((tm, w), lambda i: (i, 0)),
        out_shape=jax.ShapeDtypeStruct((t, w), BF),
        compiler_params=_cparams(("parallel",)),
        name="hgrn_readout",
    )(o_fwd, o_bwd, z, g_norm.reshape(1, w))


def _segment_tile(i, tc, p_rows, lc):
    per_batch = p_rows // tc
    j = i % per_batch
    nctx = lc // tc
    first = jnp.logical_or(j == 0, j == nctx)
    last = jnp.logical_or(j == nctx - 1, j == per_batch - 1)
    in_ctx = j < nctx
    seg_pos0 = jnp.where(in_ctx, j, j - nctx) * tc
    seg_len = jnp.where(in_ctx, lc, p_rows - lc)
    return first, last, seg_pos0, seg_len


def _halo_specs(tc, width, col_block, n_rows):
    per = tc // HALO
    last_blk = n_rows // HALO - 1
    return [
        pl.BlockSpec((HALO, width), lambda i: (jnp.maximum(i * per - 1, 0), col_block)),
        pl.BlockSpec((tc, width), lambda i: (i, col_block)),
        pl.BlockSpec((HALO, width), lambda i: (jnp.minimum((i + 1) * per, last_blk), col_block)),
    ]


def _conv_kernel(prev_ref, cur_ref, next_ref, w_ref, b_ref, lg_ref, lb_ref, y_ref, buf, shifted, *, tc, p_rows, lc, ch):
    first, last, _, _ = _segment_tile(pl.program_id(0), tc, p_rows, lc)

    def glu(zz):
        zz = zz.astype(F32)
        return zz[:, :ch] * _sigmoid(zz[:, ch:])

    buf[0:HALO, :] = jnp.where(first, 0.0, glu(prev_ref[...]))
    buf[HALO:HALO + tc, :] = glu(cur_ref[...])
    buf[HALO + tc:, :] = jnp.where(last, 0.0, glu(next_ref[...]))
    span = shifted.shape[1]
    for s in range(1, SUBLANES):
        shifted[s - 1] = buf[s:s + span, :]
    w = w_ref[...]
    acc = jnp.zeros((tc, ch), F32) + b_ref[...]
    for kk in range(CONV_WIDTH):
        blk, s = divmod(HALO - CONV_WIDTH // 2 + kk, SUBLANES)
        r0 = blk * SUBLANES
        win = buf[r0:r0 + tc, :] if s == 0 else shifted[s - 1, r0:r0 + tc, :]
        acc = acc + win * w[kk:kk + 1, :]
    mu = jnp.mean(acc, axis=-1, keepdims=True)
    cen = acc - mu
    var = jnp.mean(cen * cen, axis=-1, keepdims=True)
    y = cen * lax.rsqrt(var + EPS) * lg_ref[...] + lb_ref[...]
    y_ref[...] = _silu(y).astype(y_ref.dtype)


def _conv_mixer(z, w_dw, b_dw, ln_g, ln_b, dims):
    t = z.shape[0]
    p_rows, lc, _ = dims
    ch = w_dw.shape[1]
    tc = _divisor_tile(lc, 256)
    row = lambda a: a.reshape(1, ch)
    const = lambda i: (0, 0)
    return pl.pallas_call(
        functools.partial(_conv_kernel, tc=tc, p_rows=p_rows, lc=lc, ch=ch),
        grid=(t // tc,),
        in_specs=_halo_specs(tc, 2 * ch, Z_CONV // (2 * ch), t) + [
            pl.BlockSpec((CONV_WIDTH, ch), const),
            pl.BlockSpec((1, ch), const), pl.BlockSpec((1, ch), const), pl.BlockSpec((1, ch), const),
        ],
        out_specs=pl.BlockSpec((tc, ch), lambda i: (i, 0)),
        out_shape=jax.ShapeDtypeStruct((t, ch), BF),
        scratch_shapes=[pltpu.VMEM((tc + 2 * HALO, ch), F32),
                        pltpu.VMEM((SUBLANES - 1, tc + 2 * HALO - SUBLANES, ch), F32)],
        compiler_params=_cparams(("parallel",)),
        name="conv_mixer",
    )(z, z, z, w_dw, row(b_dw), row(ln_g), row(ln_b))


def _pool_kernel(prev_ref, cur_ref, next_ref, w_ref, s_ref, y_ref, buf, *, tc, p_rows, lc):
    first, last, seg_pos0, seg_len = _segment_tile(pl.program_id(0), tc, p_rows, lc)
    buf[0:HALO, :] = jnp.where(first, 0.0, prev_ref[...].astype(F32))
    buf[HALO:HALO + tc, :] = cur_ref[...].astype(F32)
    buf[HALO + tc:, :] = jnp.where(last, 0.0, next_ref[...].astype(F32))
    pos = seg_pos0 + lax.broadcasted_iota(jnp.int32, (tc, 1), 0)
    scale = s_ref[...]
    for gi, win in enumerate(POOL_WINDOWS):
        sl = slice(gi * POOL_GROUP, (gi + 1) * POOL_GROUP)
        half = win // 2
        tot = jnp.zeros((tc, POOL_GROUP), F32)
        for u in range(-half, half):
            tot = tot + buf[HALO + u:HALO + u + tc, sl]
        cnt = (jnp.minimum(pos + half, seg_len) - jnp.maximum(pos - half, 0)).astype(F32)
        pooled = tot / cnt - buf[HALO:HALO + tc, sl]
        y = jnp.dot(pooled.astype(BF), w_ref[gi].astype(BF), preferred_element_type=F32)
        y_ref[:, sl] = (y * scale[:, sl]).astype(y_ref.dtype)


def _pool_mixer(z, w_pool, scale, dims):
    t = z.shape[0]
    p_rows, lc, _ = dims
    ch = scale.shape[0]
    tc = _divisor_tile(lc, 256)
    return pl.pallas_call(
        functools.partial(_pool_kernel, tc=tc, p_rows=p_rows, lc=lc),
        grid=(t // tc,),
        in_specs=_halo_specs(tc, ch, Z_POOL // ch, t) + [
            pl.BlockSpec(w_pool.shape, lambda i: (0, 0, 0)),
            pl.BlockSpec((1, ch), lambda i: (0, 0)),
        ],
        out_specs=pl.BlockSpec((tc, ch), lambda i: (i, 0)),
        out_shape=jax.ShapeDtypeStruct((t, ch), BF),
        scratch_shapes=[pltpu.VMEM((tc + 2 * HALO, ch), F32)],
        compiler_params=_cparams(("parallel",)),
        name="pool_mixer",
    )(z, z, z, w_pool, scale.reshape(1, ch))


def _merge_kernel(y0, y1, y2, y3, g0, g1, g2, g3, w_ref, o_ref, w_s):
    @pl.when(pl.program_id(1) == 0)
    def _():
        w_s[...] = w_ref[0].astype(BF)

    acc = None
    for n, (y_ref, g_ref) in enumerate(((y0, g0), (y1, g1), (y2, g2), (y3, g3))):
        proj = jnp.dot(y_ref[...], w_s[n], preferred_element_type=F32)
        gate = 0.5 * jnp.tanh(0.5 * g_ref[...].astype(F32)) + 0.5
        term = gate * proj
        acc = term if acc is None else acc + term
    o_ref[...] = acc.astype(o_ref.dtype)


def _merge(ys, z, w_branch, l):
    t, bw = ys[0].shape
    _, nb, _, d = w_branch.shape
    tm = _divisor_tile(t, 544)
    tn = _divisor_tile(d, 1024, 128)
    npn = d // tn

    def gate_spec(n):
        return pl.BlockSpec((tm, tn), lambda j, i: (i, Z_GATE // tn + n * npn + j))

    return pl.pallas_call(
        _merge_kernel,
        grid=(npn, t // tm),
        in_specs=[pl.BlockSpec((tm, bw), lambda j, i: (i, 0))] * nb + [gate_spec(n) for n in range(nb)] + [
            pl.BlockSpec((1, nb, bw, tn), lambda j, i: (l, 0, 0, j))],
        out_specs=pl.BlockSpec((tm, tn), lambda j, i: (i, j)),
        out_shape=jax.ShapeDtypeStruct((t, d), BF),
        scratch_shapes=[pltpu.VMEM((nb, bw, tn), BF)],
        compiler_params=_cparams(("parallel", "arbitrary")),
        name="branch_merge",
    )(*ys, z, z, z, z, w_branch)


def _residual_epilogue(y, x_ref, gpost_ref, gate_ref, gnext_ref, sh_ref, sc_ref, xo_ref, ho_ref, b, n_batch, cm,
                       rows=slice(None)):
    cm = None if cm is None else cm[rows]
    gate_g = _row_vec(gate_ref, b, n_batch, cm) * gpost_ref[...]
    x_new = x_ref[rows, :] + _rms(y, gate_g)
    xo_ref[rows, :] = x_new
    if ho_ref is None:
        return None
    shift = _row_vec(sh_ref, b, n_batch, cm)
    scale_g = (1.0 + _row_vec(sc_ref, b, n_batch, cm)) * gnext_ref[...]
    h = _rms(x_new, scale_g) + shift
    ho_ref[rows, :] = h.astype(ho_ref.dtype)
    return h


def _route_top2(logits, n_experts):
    lane = lax.broadcasted_iota(jnp.int32, logits.shape, 1)
    lg = jnp.where(lane < n_experts, logits, NEG_BIG)
    m1 = jnp.max(lg, axis=-1, keepdims=True)
    i1 = jnp.min(jnp.where(lg == m1, lane, 1 << 20), axis=-1, keepdims=True)
    lg2 = jnp.where(lane == i1, NEG_BIG, lg)
    m2 = jnp.max(lg2, axis=-1, keepdims=True)
    i2 = jnp.min(jnp.where(lg2 == m2, lane, 1 << 20), axis=-1, keepdims=True)
    e2 = jnp.exp(m2 - m1)
    w1 = 1.0 / (1.0 + e2)
    w2 = e2 / (1.0 + e2)
    out = jnp.where(lane == 0, i1.astype(F32), 0.0)
    out = jnp.where(lane == 1, i2.astype(F32), out)
    out = jnp.where(lane == 2, w1, out)
    return jnp.where(lane == 3, w2, out)


def _router_logits(h, wr_ref):
    h_hi = h.astype(BF)
    h_lo = (h - h_hi.astype(F32)).astype(BF)
    return (jnp.dot(h_hi, wr_ref[0], preferred_element_type=F32) + jnp.dot(h_hi, wr_ref[1], preferred_element_type=F32)
            + jnp.dot(h_lo, wr_ref[0], preferred_element_type=F32))


def _out_proj_kernel(*refs, tm, p_rows, lc, n_batch, emit_h, n_experts):
    m_ref, w_ref, x_ref, gpost_ref, gate_ref = refs[:5]
    pos = 5
    gnext_ref = sh_ref = sc_ref = wr_ref = None
    if emit_h:
        gnext_ref, sh_ref, sc_ref = refs[pos:pos + 3]
        pos += 3
    if n_experts:
        wr_ref = refs[pos]
        pos += 1
    xo_ref = refs[pos]
    pos += 1
    ho_ref = route_ref = None
    if emit_h:
        ho_ref = refs[pos]
        pos += 1
    if n_experts:
        route_ref = refs[pos]
        pos += 1
    n_split = EPILOGUE_SPLIT if tm % (EPILOGUE_SPLIT * 16) == 0 else 1
    rs = tm // n_split
    groups = [slice(s * rs, (s + 1) * rs) for s in range(n_split)]

    def body(b, cm):
        ys = [jnp.dot(m_ref[rows, :], w_ref[0], preferred_element_type=F32) for rows in groups]
        for rows, y in zip(groups, ys):
            h = _residual_epilogue(y, x_ref, gpost_ref, gate_ref, gnext_ref, sh_ref, sc_ref, xo_ref, ho_ref, b,
                                   n_batch, cm, rows)
            if n_experts:
                route_ref[rows, :] = _route_top2(_router_logits(h, wr_ref), n_experts)

    _by_tile_kind(pl.program_id(0), tm, p_rows, lc, body)


def _out_proj(m, w_stack, lw, xs, g_post, mods, l, j_gate, nxt, dims, router=None, name="out_proj"):
    t, d = xs.shape
    k = m.shape[1]
    p_rows, lc, n_batch = dims
    tm = _divisor_tile(p_rows, 544 if k <= d else 272)
    const = lambda i: (0, 0)
    rowtile = pl.BlockSpec((tm, d), lambda i: (i, 0))
    args = [m, w_stack, xs, g_post.reshape(1, d), mods]
    in_specs = [pl.BlockSpec((tm, k), lambda i: (i, 0)),
                pl.BlockSpec((1, k, d), lambda i: (lw, 0, 0), pipeline_mode=pl.Buffered(1)),
                rowtile, pl.BlockSpec((1, d), const), _mod_spec(l, j_gate, d)]
    out_shape = [jax.ShapeDtypeStruct((t, d), F32)]
    out_specs = [rowtile]
    n_experts = 0
    if nxt is not None:
        g_next, l_next, j_shift, j_scale = nxt
        args += [g_next.reshape(1, d), mods, mods]
        in_specs += [pl.BlockSpec((1, d), const), _mod_spec(l_next, j_shift, d), _mod_spec(l_next, j_scale, d)]
        out_shape.append(jax.ShapeDtypeStruct((t, d), BF if router is None else F32))
        out_specs.append(rowtile)
    if router is not None:
        wr, n_experts = router
        args.append(wr)
        in_specs.append(pl.BlockSpec(wr.shape, lambda i: (0, 0, 0)))
        out_shape.append(jax.ShapeDtypeStruct((t, 128), F32))
        out_specs.append(pl.BlockSpec((tm, 128), lambda i: (i, 0)))
    return pl.pallas_call(
        functools.partial(_out_proj_kernel, tm=tm, p_rows=p_rows, lc=lc, n_batch=n_batch,
                          emit_h=nxt is not None, n_experts=n_experts),
        grid=(t // tm,),
        in_specs=in_specs,
        out_specs=out_specs,
        out_shape=out_shape,
        compiler_params=_cparams(("parallel",)),
        name=name,
    )(*args)


def _swiglu_kernel(x_ref, w1_ref, w3_ref, o_ref, w1_s, w3_s):
    @pl.when(pl.program_id(1) == 0)
    def _():
        w1_s[...] = w1_ref[0].astype(BF)
        w3_s[...] = w3_ref[0].astype(BF)

    x = x_ref[...]
    a = jnp.dot(x, w1_s[...], preferred_element_type=F32)
    g = jnp.dot(x, w3_s[...], preferred_element_type=F32)
    o_ref[...] = (_silu(a) * g).astype(o_ref.dtype)


def _swiglu_up(h, w1, w3, jl):
    t, d = h.shape
    dff = w1.shape[2]
    tm = _divisor_tile(t, 1088)
    tn = _divisor_tile(dff, 512, 128)
    return pl.pallas_call(
        _swiglu_kernel,
        grid=(dff // tn, t // tm),
        in_specs=[
            pl.BlockSpec((tm, d), lambda j, i: (i, 0)),
            pl.BlockSpec((1, d, tn), lambda j, i: (jl, 0, j)),
            pl.BlockSpec((1, d, tn), lambda j, i: (jl, 0, j)),
        ],
        out_specs=pl.BlockSpec((tm, tn), lambda j, i: (i, j)),
        out_shape=jax.ShapeDtypeStruct((t, dff), BF),
        scratch_shapes=[pltpu.VMEM((d, tn), BF), pltpu.VMEM((d, tn), BF)],
        compiler_params=_cparams(("parallel", "arbitrary")),
        name="swiglu_up",
    )(h, w1, w3)


def _gather_rows_kernel(idx_ref, src_ref, o_ref, *scratch, tr):
    base = pl.program_id(0) * tr
    dst, sem = (o_ref, scratch[0]) if len(scratch) == 1 else scratch

    def copy(r):
        return pltpu.make_async_copy(src_ref.at[pl.ds(idx_ref[base + r], 1), :], dst.at[pl.ds(r, 1), :], sem)

    def issue(g, carry):
        for u in range(GATHER_UNROLL):
            copy(g * GATHER_UNROLL + u).start(priority=u % 2)
        return carry

    lax.fori_loop(0, tr // GATHER_UNROLL, issue, 0)
    pltpu.make_async_copy(src_ref.at[pl.ds(0, tr), :], dst.at[pl.ds(0, tr), :], sem).wait()
    if dst is not o_ref:
        o_ref[...] = dst[...].astype(o_ref.dtype)


def _gather_rows(src, idx, out_dtype):
    n, d = src.shape
    assert src.dtype == F32
    rows = idx.shape[0]
    tr = _divisor_tile(rows, MOE_GATHER_ROWS, 8)
    staging = [] if out_dtype == src.dtype else [pltpu.VMEM((tr, d), src.dtype)]
    return pl.pallas_call(
        functools.partial(_gather_rows_kernel, tr=tr),
        grid_spec=pltpu.PrefetchScalarGridSpec(
            num_scalar_prefetch=1,
            grid=(rows // tr,),
            in_specs=[pl.BlockSpec(memory_space=pl.ANY)],
            out_specs=pl.BlockSpec((tr, d), lambda i, idx_ref: (i, 0)),
            scratch_shapes=staging + [pltpu.SemaphoreType.DMA(())],
        ),
        out_shape=jax.ShapeDtypeStruct((rows, d), out_dtype),
        compiler_params=_cparams(("arbitrary",)),
        name="moe_gather",
    )(idx, src)


def _expert_rows_loop(st_ref, nt_ref, n_slots, load_copy, store_copy, compute):
    e, n = pl.program_id(0), pl.program_id(1)
    n_exp, n_col = pl.num_programs(0), pl.num_programs(1)
    start, n_tiles = st_ref[e], nt_ref[e]
    first_step = jnp.logical_and(e == 0, n == 0)
    last_step = jnp.logical_and(e == n_exp - 1, n == n_col - 1)
    e_next = jnp.minimum(jnp.where(n == n_col - 1, e + 1, e), n_exp - 1)
    ahead = n_slots - 1

    @pl.when(jnp.logical_and(first_step, n_tiles > 0))
    def _():
        load_copy(start, 0, 0).start(priority=1)

    for k in range(1, ahead):
        @pl.when(n_tiles > k)
        def _():
            load_copy(start, k, k).start(priority=1)

    def body(i, carry):
        load_copy(start, i, i % n_slots).wait()

        @pl.when(i + ahead < n_tiles)
        def _():
            load_copy(start, i + ahead, (i + ahead) % n_slots).start(priority=1)

        @pl.when(i >= 2)
        def _():
            store_copy(start, i - 2, i % 2).wait()

        compute(i % n_slots, i % 2)
        store_copy(start, i, i % 2).start(priority=1)
        return carry

    lax.fori_loop(0, n_tiles, body, 0)

    @pl.when(jnp.logical_and(jnp.logical_not(last_step), nt_ref[e_next] > 0))
    def _():
        load_copy(st_ref[e_next], 0, 0).start(priority=1)

    @pl.when(n_tiles >= 2)
    def _():
        store_copy(start, n_tiles - 2, n_tiles % 2).wait()

    @pl.when(n_tiles >= 1)
    def _():
        store_copy(start, n_tiles - 1, (n_tiles - 1) % 2).wait()


def _zero_tail_rows(out_hbm, zbuf, sem, first_row, col0, tn):
    zbuf[...] = jnp.zeros_like(zbuf)
    n_chunks = (out_hbm.shape[0] - first_row) // MOE_ROW_ALIGN

    def copy(c):
        r0 = pl.multiple_of(first_row + c * MOE_ROW_ALIGN, MOE_ROW_ALIGN)
        return pltpu.make_async_copy(zbuf, out_hbm.at[pl.ds(r0, MOE_ROW_ALIGN), pl.ds(col0, tn)], sem)

    def issue(c, carry):
        copy(c).start()
        return carry

    def drain(c, carry):
        copy(c).wait()
        return carry

    lax.fori_loop(0, n_chunks, issue, 0)
    lax.fori_loop(0, n_chunks, drain, 0)


def _moe_up_kernel(st_ref, nt_ref, x_hbm, w1_ref, w3_ref, a_hbm, w1_s, w3_s, xbuf, obuf, zbuf, sem_in, sem_out, sem_z,
                   *, tr, tn):
    e, n = pl.program_id(0), pl.program_id(1)
    w1_s[...] = w1_ref[0, 0].astype(BF)
    w3_s[...] = w3_ref[0, 0].astype(BF)
    col0 = pl.multiple_of(n * tn, tn)

    def rows(start, i):
        return pl.ds(pl.multiple_of(start + i * tr, MOE_ROW_ALIGN), tr)

    def load_copy(start, i, slot):
        return pltpu.make_async_copy(x_hbm.at[rows(start, i)], xbuf.at[slot], sem_in.at[slot])

    def store_copy(start, i, slot):
        return pltpu.make_async_copy(obuf.at[slot], a_hbm.at[rows(start, i), pl.ds(col0, tn)], sem_out.at[slot])

    def compute(in_slot, out_slot):
        x = xbuf[in_slot]
        a = jnp.dot(x, w1_s[...], preferred_element_type=F32)
        g = jnp.dot(x, w3_s[...], preferred_element_type=F32)
        obuf[out_slot] = (_silu(a) * g).astype(obuf.dtype)

    _expert_rows_loop(st_ref, nt_ref, xbuf.shape[0], load_copy, store_copy, compute)

    @pl.when(e == pl.num_programs(0) - 1)
    def _():
        _zero_tail_rows(a_hbm, zbuf, sem_z, st_ref[e] + nt_ref[e] * tr, col0, tn)


def _moe_up(xs, w1, w3, jl, starts, n_tiles, tr):
    rows, d = xs.shape
    n_exp, dff = w1.shape[1], w1.shape[3]
    tn = _divisor_tile(dff, 512, 128)
    return pl.pallas_call(
        functools.partial(_moe_up_kernel, tr=tr, tn=tn),
        grid_spec=pltpu.PrefetchScalarGridSpec(
            num_scalar_prefetch=2,
            grid=(n_exp, dff // tn),
            in_specs=[
                pl.BlockSpec(memory_space=pl.ANY),
                pl.BlockSpec((1, 1, d, tn), lambda e, n, st, nt: (jl, e, 0, n)),
                pl.BlockSpec((1, 1, d, tn), lambda e, n, st, nt: (jl, e, 0, n)),
            ],
            out_specs=pl.BlockSpec(memory_space=pl.ANY),
            scratch_shapes=[pltpu.VMEM((d, tn), BF), pltpu.VMEM((d, tn), BF),
                            pltpu.VMEM((MOE_UP_SLOTS, tr, d), BF), pltpu.VMEM((2, tr, tn), BF),
                            pltpu.VMEM((MOE_ROW_ALIGN, tn), BF),
                            pltpu.SemaphoreType.DMA((MOE_UP_SLOTS,)), pltpu.SemaphoreType.DMA((2,)),
                            pltpu.SemaphoreType.DMA(())],
        ),
        out_shape=jax.ShapeDtypeStruct((rows, dff), BF),
        compiler_params=_cparams(("arbitrary", "arbitrary")),
        name="moe_up",
    )(starts, n_tiles, xs, w1, w3)


def _moe_down_kernel(st_ref, nt_ref, a_hbm, w_ref, y_hbm, w_s, abuf, obuf, zbuf, sem_in, sem_out, sem_z, *, tr, tn):
    e, n = pl.program_id(0), pl.program_id(1)
    w_s[...] = w_ref[0, 0].astype(BF)
    col0 = pl.multiple_of(n * tn, tn)

    def rows(start, i):
        return pl.ds(pl.multiple_of(start + i * tr, MOE_ROW_ALIGN), tr)

    def load_copy(start, i, slot):
        return pltpu.make_async_copy(a_hbm.at[rows(start, i)], abuf.at[slot], sem_in.at[slot])

    def store_copy(start, i, slot):
        return pltpu.make_async_copy(obuf.at[slot], y_hbm.at[rows(start, i), pl.ds(col0, tn)], sem_out.at[slot])

    def compute(in_slot, out_slot):
        obuf[out_slot] = jnp.dot(abuf[in_slot], w_s[...], preferred_element_type=F32)

    _expert_rows_loop(st_ref, nt_ref, abuf.shape[0], load_copy, store_copy, compute)

    @pl.when(e == pl.num_programs(0) - 1)
    def _():
        _zero_tail_rows(y_hbm, zbuf, sem_z, st_ref[e] + nt_ref[e] * tr, col0, tn)


def _moe_down(act, w2, jl, starts, n_tiles, tr):
    rows, dff = act.shape
    n_exp, d = w2.shape[1], w2.shape[3]
    tn = _divisor_tile(d, 512, 128)
    return pl.pallas_call(
        functools.partial(_moe_down_kernel, tr=tr, tn=tn),
        grid_spec=pltpu.PrefetchScalarGridSpec(
            num_scalar_prefetch=2,
            grid=(n_exp, d // tn),
            in_specs=[
                pl.BlockSpec(memory_space=pl.ANY),
                pl.BlockSpec((1, 1, dff, tn), lambda e, n, st, nt: (jl, e, 0, n)),
            ],
            out_specs=pl.BlockSpec(memory_space=pl.ANY),
            scratch_shapes=[pltpu.VMEM((dff, tn), BF), pltpu.VMEM((MOE_DOWN_SLOTS, tr, dff), BF),
                            pltpu.VMEM((2, tr, tn), F32), pltpu.VMEM((MOE_ROW_ALIGN, tn), F32),
                            pltpu.SemaphoreType.DMA((MOE_DOWN_SLOTS,)), pltpu.SemaphoreType.DMA((2,)),
                            pltpu.SemaphoreType.DMA(())],
        ),
        out_shape=jax.ShapeDtypeStruct((rows, d), F32),
        compiler_params=_cparams(("arbitrary", "arbitrary")),
        name="moe_down",
    )(starts, n_tiles, act, w2)


def _combine_kernel(*refs, tm, p_rows, lc, n_batch, emit_h):
    ya_ref, yb_ref, rt_ref, x_ref, gpost_ref, gate_ref = refs[:6]
    gnext_ref = sh_ref = sc_ref = ho_ref = None
    if emit_h:
        gnext_ref, sh_ref, sc_ref, xo_ref, ho_ref = refs[6:]
    else:
        (xo_ref,) = refs[6:]
    def body(b, cm):
        rt = rt_ref[...]
        y = rt[:, 2:3] * ya_ref[...] + rt[:, 3:4] * yb_ref[...]
        _residual_epilogue(y, x_ref, gpost_ref, gate_ref, gnext_ref, sh_ref, sc_ref, xo_ref, ho_ref, b, n_batch, cm)

    _by_tile_kind(pl.program_id(0), tm, p_rows, lc, body)


def _moe_combine(y2, route, xs, g_post, mods, l, j_gate, nxt, dims):
    t, d = xs.shape
    p_rows, lc, n_batch = dims
    tm = _divisor_tile(p_rows, 544)
    nt = t // tm
    const = lambda i: (0, 0)
    rowtile = pl.BlockSpec((tm, d), lambda i: (i, 0))
    args = [y2, y2, route, xs, g_post.reshape(1, d), mods]
    in_specs = [rowtile, pl.BlockSpec((tm, d), lambda i: (i + nt, 0)), pl.BlockSpec((tm, 128), lambda i: (i, 0)),
                rowtile, pl.BlockSpec((1, d), const), _mod_spec(l, j_gate, d)]
    out_shape = [jax.ShapeDtypeStruct((t, d), F32)]
    out_specs = [rowtile]
    if nxt is not None:
        g_next, l_next, j_shift, j_scale = nxt
        args += [g_next.reshape(1, d), mods, mods]
        in_specs += [pl.BlockSpec((1, d), const), _mod_spec(l_next, j_shift, d), _mod_spec(l_next, j_scale, d)]
        out_shape.append(jax.ShapeDtypeStruct((t, d), BF))
        out_specs.append(rowtile)
    return pl.pallas_call(
        functools.partial(_combine_kernel, tm=tm, p_rows=p_rows, lc=lc, n_batch=n_batch, emit_h=nxt is not None),
        grid=(nt,),
        in_specs=in_specs,
        out_specs=out_specs,
        out_shape=out_shape,
        compiler_params=_cparams(("parallel",)),
        name="moe_combine",
    )(*args)


def _moe_ffn(h, route, w1, w3, w2, jl, xs, g_post, mods, l, nxt, dims):
    t, d = h.shape
    n_exp = w1.shape[1]
    tr = MOE_ROW_TILE
    expert = jnp.concatenate([route[:, 0], route[:, 1]]).astype(jnp.int32)
    token = jnp.concatenate([jnp.arange(t, dtype=jnp.int32)] * 2)
    onehot = (expert[:, None] == jnp.arange(n_exp, dtype=jnp.int32)[None, :]).astype(jnp.int32)
    rank = jnp.sum((jnp.cumsum(onehot, axis=0) - onehot) * onehot, axis=1)
    counts = jnp.sum(onehot, axis=0)
    padded = ((counts + MOE_ROW_ALIGN - 1) // MOE_ROW_ALIGN) * MOE_ROW_ALIGN
    ends = jnp.cumsum(padded)
    starts = (ends - padded).astype(jnp.int32)
    n_tiles = ((counts + tr - 1) // tr).astype(jnp.int32)
    dest = starts[expert] + rank
    rows = 2 * t + n_exp * MOE_ROW_ALIGN + tr
    rows = ((rows + MOE_GATHER_ROWS - 1) // MOE_GATHER_ROWS) * MOE_GATHER_ROWS
    row_token = jnp.zeros((rows,), jnp.int32).at[dest].set(token)

    xs_sorted = _gather_rows(h, row_token, BF)
    act = _moe_up(xs_sorted, w1, w3, jl, starts, n_tiles, tr)
    y_sorted = _moe_down(act, w2, jl, starts, n_tiles, tr)
    y2 = _gather_rows(y_sorted, dest, F32)
    return _moe_combine(y2, route, xs, g_post, mods, l, 5, nxt, dims)


def _pad_heads(w, n_heads, width):
    r = w.shape[0]
    w3 = w.reshape(r, n_heads, width)
    return jnp.pad(w3, ((0, 0), (0, 0), (0, HEAD_PAD - width))).reshape(r, n_heads * HEAD_PAD)


def _rope_tables(l_lat, lc):
    tt = jnp.arange(l_lat)
    row = (tt // GRID_W).astype(F32)
    col = (tt % GRID_W).astype(F32)
    axis_dim = MLA_ROPE // 2
    inv_freq = 1.0 / (ROPE_THETA ** (jnp.arange(0, axis_dim, 2, dtype=F32) / axis_dim))
    ar, ac = row[:, None] * inv_freq, col[:, None] * inv_freq
    ang = jnp.concatenate([ar, ar, ac, ac], axis=-1)
    cos = jnp.concatenate([jnp.ones((lc, MLA_ROPE), F32), jnp.cos(ang)], axis=0)
    sin = jnp.concatenate([jnp.zeros((lc, MLA_ROPE), F32), jnp.sin(ang)], axis=0)
    padc = jnp.zeros((lc + l_lat, 128 - MLA_ROPE), F32)
    pm = np.zeros((128, 128), np.float32)
    q4 = MLA_ROPE // 4
    for base in (0, 2 * q4):
        for i in range(q4):
            pm[base + q4 + i, base + i] = -1.0
            pm[base + i, base + q4 + i] = 1.0
    return jnp.concatenate([cos, padc], axis=1), jnp.concatenate([sin, padc], axis=1), jnp.asarray(pm, BF)


def kernel(x, c, ctx, c_ctx, w_mod, b_mod, g_mix_pre, g_mix_post, g_ffn_pre, g_ffn_post, w_in, mla_g_q, mla_g_kv, mla_w_uq, mla_w_ukv, hg_lb_logits, hg_g_norm, conv_w, conv_b, conv_ln_g, conv_ln_b, pool_w, pool_scale, w_branch, w_out, ffn_w1, ffn_w3, ffn_w2, moe_router, moe_w1, moe_w3, moe_w2):
    n_batch, l_lat, d = x.shape
    lc = ctx.shape[1]
    depth = w_in.shape[0]
    p_rows = lc + l_lat
    t = n_batch * p_rows
    dims = (p_rows, lc, n_batch)
    assert n_batch + 1 <= 8 and lc % HG_CHUNK == 0 and l_lat % HG_CHUNK == 0 and l_lat % GRID_W == 0
    n_heads = mla_w_uq.shape[2] // MLA_QK

    xs = jnp.concatenate([ctx, x], axis=1).reshape(t, d)
    cvec = jnp.zeros((8, d), F32).at[:n_batch].set(c).at[n_batch].set(c_ctx)
    mods = _modvecs(cvec, w_mod, b_mod)

    lb = jnp.cumsum(jax.nn.softmax(hg_lb_logits.astype(F32), axis=1), axis=1)
    lb = lb - lb[:, :1]
    cos, sin, pm = _rope_tables(l_lat, lc)

    w_in_t = jnp.swapaxes(w_in, 1, 2)
    w_out_bf = w_out.astype(BF)
    ffn_w2_bf = ffn_w2.astype(BF)

    h = _modulate(xs, g_mix_pre[0], mods, 0, 0, 1, dims)
    for l in range(depth):
        last = l == depth - 1
        z = _in_proj(h, w_in_t, l, BF)
        q3, k3, v3 = _mla_proj(z, mla_g_q[l], mla_g_kv[l], _pad_heads(mla_w_uq[l], n_heads, MLA_QK),
                               mla_w_ukv[l], cos, sin, pm, dims)
        y_mla = _attention(q3, k3, v3, dims)
        y_hg = _hgrn_readout(*_hgrn(z, lb[:, l], dims), z, hg_g_norm[l])
        y_conv = _conv_mixer(z, conv_w[l], conv_b[l], conv_ln_g[l], conv_ln_b[l], dims)
        y_pool = _pool_mixer(z, pool_w[l], pool_scale[l], dims)
        m = _merge([y_mla, y_hg, y_conv, y_pool], z, w_branch, l)
        j = l // 2
        nxt_ffn = (g_ffn_pre[l], l, 3, 4)
        nxt_mix = None if last else (g_mix_pre[l + 1], l + 1, 0, 1)
        if l % 2 == 0:
            xs, h2 = _out_proj(m, w_out_bf, l, xs, g_mix_post[l], mods, l, 2, nxt_ffn, dims)
            act = _swiglu_up(h2, ffn_w1, ffn_w3, j)
            res = _out_proj(act, ffn_w2_bf, j, xs, g_ffn_post[l], mods, l, 5, nxt_mix, dims, name="ffn_down")
        else:
            n_exp = moe_router.shape[2]
            wr = jnp.pad(moe_router[j], ((0, 0), (0, 128 - n_exp)))
            wr_hi = wr.astype(BF)
            wr2 = jnp.stack([wr_hi, (wr - wr_hi.astype(F32)).astype(BF)])
            xs, h2, route = _out_proj(m, w_out_bf, l, xs, g_mix_post[l], mods, l, 2, nxt_ffn, dims,
                                      router=(wr2, n_exp))
            res = _moe_ffn(h2, route, moe_w1, moe_w3, moe_w2, j, xs, g_ffn_post[l], mods, l, nxt_mix, dims)
        if last:
            (xs,) = res
        else:
            xs, h = res
    return xs.reshape(n_batch, p_rows, d)[:, lc:, :]
```

```python
import functools

import numpy as np
import jax
import jax.numpy as jnp
from jax import lax
from jax.experimental import pallas as pl
from jax.experimental.pallas import tpu as pltpu

F32 = jnp.float32
BF = jnp.bfloat16

EPS = 1e-6
GRID_W = 64
ROPE_THETA = 10000.0
N_BRANCH = 4
MLA_NOPE, MLA_ROPE, MLA_V = 128, 64, 128
MLA_QK = MLA_NOPE + MLA_ROPE
HEAD_PAD = 256
HG_DK = 128
CONV_WIDTH = 31
POOL_WINDOWS = (2, 4, 8, 16)
POOL_GROUP = 128
LANES = 128
IN_HALF = 512
IN_GROUP = 64
SUBLANES = 8
HALO = 16
HG_CHUNK = 64
HG_STEP_CHUNKS = 2
HG_EXP_CLAMP = 80.0
NEG_BIG = -1e30
ATTN_Q_TILE = 512
MOE_UP_ROWS = 512
MOE_DOWN_ROWS = 256
GATHER_UNROLL = 8
EPILOGUE_SPLIT = 2
MOE_UP_SLOTS = 5
MOE_DOWN_SLOTS = 3
MOE_ROW_ALIGN = 16
MOE_GATHER_ROWS = 1024

VMEM_LIMIT_V7X = 56 * 1024 * 1024

Z_CONV, Z_HQ, Z_HI, Z_HFF, Z_HFB, Z_HG, Z_POOL, Z_MLA, Z_GATE = 0, 1024, 1536, 2048, 2560, 3072, 3584, 4096, 5120
MLA_ZKV_OFF, MLA_ZR_OFF = 384, 640


def _cparams(sem, vmem=VMEM_LIMIT_V7X):
    return pltpu.CompilerParams(dimension_semantics=sem, vmem_limit_bytes=vmem)


def _divisor_tile(n, target, mult=16):
    best = None
    for t in range(mult, min(n, target) + 1, mult):
        if n % t == 0:
            best = t
    assert best is not None, (n, target, mult)
    return best


def _pick_row(m, r):
    rows = lax.broadcasted_iota(jnp.int32, m.shape, 0)
    return jnp.sum(jnp.where(rows == r, m, 0.0), axis=0, keepdims=True)


def _row_vec(m_ref, b, n_batch, ctx_mask):
    m = m_ref[0, 0]
    if ctx_mask is None:
        return _pick_row(m, b)
    return jnp.where(ctx_mask, _pick_row(m, n_batch), _pick_row(m, b))


def _tile_rows(i, tm, p_rows, lc):
    per_batch = p_rows // tm
    b = i // per_batch
    pos = (i % per_batch) * tm + lax.broadcasted_iota(jnp.int32, (tm, 1), 0)
    return b, pos < lc


def _by_tile_kind(i, tm, p_rows, lc, body):
    b, cm = _tile_rows(i, tm, p_rows, lc)
    has_ctx = (i % (p_rows // tm)) * tm < lc

    @pl.when(has_ctx)
    def _():
        body(b, cm)

    @pl.when(jnp.logical_not(has_ctx))
    def _():
        body(b, None)


def _rms(x, g):
    return x * lax.rsqrt(jnp.mean(x * x, axis=-1, keepdims=True) + EPS) * g


def _sigmoid(x):
    return 1.0 / (1.0 + jnp.exp(-x))


def _silu(x):
    return x * _sigmoid(x)


def _modvec_kernel(c_ref, w_ref, b_ref, o_ref):
    s = _silu(c_ref[...])
    s_hi = s.astype(BF)
    s_lo = (s - s_hi.astype(F32)).astype(BF)
    w = w_ref[0].astype(BF)
    acc = jnp.dot(s_hi, w, preferred_element_type=F32) + jnp.dot(s_lo, w, preferred_element_type=F32)
    o_ref[0, 0] = acc + b_ref[0, 0]


def _modvecs(cvec, w_mod, b_mod):
    depth, d, n6 = w_mod.shape
    nj = n6 // d
    tn = _divisor_tile(d, 1024, 128)
    npj = d // tn
    return pl.pallas_call(
        _modvec_kernel,
        grid=(depth, nj, npj),
        in_specs=[
            pl.BlockSpec((8, d), lambda l, j, n: (0, 0)),
            pl.BlockSpec((1, d, tn), lambda l, j, n: (l, 0, j * npj + n)),
            pl.BlockSpec((1, 1, 1, tn), lambda l, j, n: (l, j, 0, n)),
        ],
        out_specs=pl.BlockSpec((1, 1, 8, tn), lambda l, j, n: (l, j, 0, n)),
        out_shape=jax.ShapeDtypeStruct((depth, nj, 8, d), F32),
        compiler_params=_cparams(("parallel", "parallel", "parallel")),
        name="modvec",
    )(cvec, w_mod, b_mod.reshape(depth, nj, 1, d))


def _modulate_kernel(x_ref, g_ref, sh_ref, sc_ref, h_ref, *, tm, p_rows, lc, n_batch):
    b, cm = _tile_rows(pl.program_id(0), tm, p_rows, lc)
    shift = _row_vec(sh_ref, b, n_batch, cm)
    scale = _row_vec(sc_ref, b, n_batch, cm)
    h_ref[...] = (_rms(x_ref[...], g_ref[...]) * (1.0 + scale) + shift).astype(h_ref.dtype)


def _mod_spec(l, j, d):
    return pl.BlockSpec((1, 1, 8, d), lambda *_: (l, j, 0, 0))


def _modulate(xs, g, mods, l, j_shift, j_scale, dims):
    t, d = xs.shape
    p_rows, lc, n_batch = dims
    tm = _divisor_tile(p_rows, 544)
    return pl.pallas_call(
        functools.partial(_modulate_kernel, tm=tm, p_rows=p_rows, lc=lc, n_batch=n_batch),
        grid=(t // tm,),
        in_specs=[
            pl.BlockSpec((tm, d), lambda i: (i, 0)),
            pl.BlockSpec((1, d), lambda i: (0, 0)),
            _mod_spec(l, j_shift, d),
            _mod_spec(l, j_scale, d),
        ],
        out_specs=pl.BlockSpec((tm, d), lambda i: (i, 0)),
        out_shape=jax.ShapeDtypeStruct((t, d), BF),
        compiler_params=_cparams(("parallel",)),
        name="modulate",
    )(xs, g.reshape(1, d), mods, mods)


def _in_proj_kernel(grp_ref, cnt_ref, x_ref, *refs):
    n_blk = len(refs) - 2
    w_refs, o_ref, w_s = refs[:n_blk], refs[n_blk], refs[n_blk + 1]
    j = pl.program_id(0)
    per_half = n_blk // 2

    @pl.when(pl.program_id(1) == 0)
    def _():
        for u in range(2):
            cnt = cnt_ref[2 * j + u]
            for i in range(per_half):
                r0 = (u * per_half + i) * IN_GROUP
                w_s[r0:r0 + IN_GROUP, :] = jnp.where(i < cnt, w_refs[u * per_half + i][0], 0.0).astype(BF)

    o_ref[...] = lax.dot_general(x_ref[...], w_s[...], (((1,), (1,)), ((), ())),
                                 preferred_element_type=F32).astype(o_ref.dtype)


def _in_proj_tables(d):
    q_rank, kv_rank = (3 * d) // 16, d // 8
    bw = d // N_BRANCH
    src, o = {}, 0
    for name, width in (("mla", q_rank + kv_rank + MLA_ROPE), ("hq", bw), ("hi", bw), ("hff", bw), ("hfb", bw),
                        ("hg", bw), ("conv", 2 * bw), ("pool", bw), ("gate", N_BRANCH * d)):
        src[name] = (o, width)
        o += width
    dst = (("conv", Z_CONV), ("hq", Z_HQ), ("hi", Z_HI), ("hff", Z_HFF), ("hfb", Z_HFB), ("hg", Z_HG),
           ("pool", Z_POOL), ("mla", Z_MLA), ("gate", Z_GATE))
    n_half = (Z_GATE + N_BRANCH * d) // IN_HALF
    grp, cnt = [0] * n_half, [0] * n_half
    for name, z0 in dst:
        s0, width = src[name]
        assert z0 % IN_HALF == 0 and s0 % IN_GROUP == 0 and width % IN_GROUP == 0
        for k in range(-(-width // IN_HALF)):
            grp[z0 // IN_HALF + k] = (s0 + k * IN_HALF) // IN_GROUP
            cnt[z0 // IN_HALF + k] = min(IN_HALF, width - k * IN_HALF) // IN_GROUP
    as_i32 = lambda v: jnp.asarray(np.array(v, np.int32))
    return as_i32(grp), as_i32(cnt)


def _in_proj(x, w_in_t, l, out_dtype, tm_target=1088):
    t, k = x.shape
    n = Z_GATE + N_BRANCH * k
    tn = 2 * IN_HALF
    per_half = IN_HALF // IN_GROUP
    tm = _divisor_tile(t, tm_target)
    tables = _in_proj_tables(k)
    last_group = w_in_t.shape[1] // IN_GROUP - 1

    def w_spec(u, i):
        return pl.BlockSpec((1, IN_GROUP, k),
                            lambda j, m, grp, cnt: (l, jnp.minimum(grp[2 * j + u] + i, last_group), 0))

    return pl.pallas_call(
        _in_proj_kernel,
        grid_spec=pltpu.PrefetchScalarGridSpec(
            num_scalar_prefetch=2,
            grid=(n // tn, t // tm),
            in_specs=[pl.BlockSpec((tm, k), lambda j, m, *_: (m, 0))]
            + [w_spec(u, i) for u in range(2) for i in range(per_half)],
            out_specs=pl.BlockSpec((tm, tn), lambda j, m, *_: (m, j)),
            scratch_shapes=[pltpu.VMEM((tn, k), BF)],
        ),
        out_shape=jax.ShapeDtypeStruct((t, n), out_dtype),
        compiler_params=_cparams(("parallel", "arbitrary")),
        name="in_proj",
    )(*tables, x, *([w_in_t] * (2 * per_half)))


def _rope128(t, cos, sin, pm):
    partner = jnp.dot(t.astype(BF), pm, preferred_element_type=F32)
    return t * cos + partner * sin


def _mla_proj_kernel(z_ref, gq_ref, gkv_ref, wq_ref, wkv_ref, cos_ref, sin_ref, pm_ref,
                     q_ref, k_ref, v_ref, wq_s, wkv_s, *, n_heads, q_rank, kv_rank, scale):
    @pl.when(pl.program_id(0) == 0)
    def _():
        wq_s[...] = wq_ref[...].astype(BF)
        wkv_s[...] = wkv_ref[...].astype(BF)

    z = z_ref[...]
    zq = z[:, :q_rank].astype(F32)
    zkv = z[:, MLA_ZKV_OFF:MLA_ZKV_OFF + kv_rank].astype(F32)
    zr = z[:, MLA_ZR_OFF:MLA_ZR_OFF + 128].astype(F32)
    cos, sin, pm = cos_ref[...], sin_ref[...], pm_ref[...]
    qn = _rms(zq, gq_ref[...]).astype(BF)
    kvn = _rms(zkv, gkv_ref[...]).astype(BF)
    k_rope = _rope128(zr, cos, sin, pm)[:, :MLA_ROPE]
    for h in range(n_heads):
        qh = jnp.dot(qn, wq_s[:, h * HEAD_PAD:(h + 1) * HEAD_PAD], preferred_element_type=F32)
        q_rope = _rope128(qh[:, MLA_NOPE:], cos, sin, pm)[:, :MLA_ROPE]
        q_ref[h] = (jnp.concatenate([qh[:, :MLA_NOPE], q_rope], axis=-1) * scale).astype(q_ref.dtype)
        kvh = jnp.dot(kvn, wkv_s[:, h * HEAD_PAD:(h + 1) * HEAD_PAD], preferred_element_type=F32)
        k_ref[h] = jnp.concatenate([kvh[:, :MLA_NOPE], k_rope], axis=-1).astype(k_ref.dtype)
        v_ref[h] = jnp.concatenate([kvh[:, MLA_NOPE:], jnp.ones_like(kvh[:, MLA_NOPE:])], axis=-1).astype(v_ref.dtype)


def _mla_proj(z, g_q, g_kv, wq_pad, wkv, cos, sin, pm, dims):
    t = z.shape[0]
    p_rows = dims[0]
    q_rank, kv_rank = g_q.shape[0], g_kv.shape[0]
    n_heads = wkv.shape[1] // HEAD_PAD
    tm = _divisor_tile(p_rows, 544)
    npb = p_rows // tm
    scale = float(MLA_QK) ** -0.5 * float(np.log2(np.e))
    const = lambda i: (0, 0)
    return pl.pallas_call(
        functools.partial(_mla_proj_kernel, n_heads=n_heads, q_rank=q_rank, kv_rank=kv_rank, scale=scale),
        grid=(t // tm,),
        in_specs=[
            pl.BlockSpec((tm, 1024), lambda i: (i, Z_MLA // 1024)),
            pl.BlockSpec((1, q_rank), const),
            pl.BlockSpec((1, kv_rank), const),
            pl.BlockSpec(wq_pad.shape, const),
            pl.BlockSpec(wkv.shape, const),
            pl.BlockSpec((tm, 128), lambda i: (i % npb, 0)),
            pl.BlockSpec((tm, 128), lambda i: (i % npb, 0)),
            pl.BlockSpec((128, 128), const),
        ],
        out_specs=[
            pl.BlockSpec((n_heads, tm, MLA_QK), lambda i: (0, i, 0)),
            pl.BlockSpec((n_heads, tm, MLA_QK), lambda i: (0, i, 0)),
            pl.BlockSpec((n_heads, tm, 2 * MLA_V), lambda i: (0, i, 0)),
        ],
        out_shape=[
            jax.ShapeDtypeStruct((n_heads, t, MLA_QK), BF),
            jax.ShapeDtypeStruct((n_heads, t, MLA_QK), BF),
            jax.ShapeDtypeStruct((n_heads, t, 2 * MLA_V), BF),
        ],
        scratch_shapes=[pltpu.VMEM(wq_pad.shape, BF), pltpu.VMEM(wkv.shape, BF)],
        compiler_params=_cparams(("arbitrary",)),
        name="mla_proj",
    )(z, g_q.reshape(1, -1), g_kv.reshape(1, -1), wq_pad, wkv, cos, sin, pm)


def _attn_kernel(q_ref, k_ref, v_ref, o_ref, s0_ref, s1_ref, p0_ref, p1_ref, *, tq, lc):
    p_rows = k_ref.shape[1]
    n_tiles = (p_rows - lc) // tq
    nt = (((1,), (1,)), ((), ()))

    def tile_start(t):
        return pl.multiple_of(lc + t * tq, lc)

    def scores(t, s_ref):
        s_ref[...] = lax.dot_general(q_ref[0, pl.ds(tile_start(t), tq), :], k_ref[0], nt, preferred_element_type=F32)

    def softmax(s_ref, p_ref):
        s = s_ref[...]
        p_ref[...] = jnp.exp2((s - jnp.max(s, axis=-1, keepdims=True)).astype(BF))

    def emit(p, v, q0, n_q):
        o = jnp.dot(p, v, preferred_element_type=F32)
        o_ref[pl.ds(q0, n_q), :] = (o[:, :MLA_V] / o[:, MLA_V:MLA_V + 1]).astype(o_ref.dtype)

    def values(t, p_ref):
        emit(p_ref[...], v_ref[0], tile_start(t), tq)

    s_ctx = lax.dot_general(q_ref[0, :lc, :], k_ref[0, :lc, :], nt, preferred_element_type=F32)
    emit(jnp.exp2((s_ctx - jnp.max(s_ctx, axis=-1, keepdims=True)).astype(BF)), v_ref[0, :lc, :], 0, lc)

    scores(0, s0_ref)
    scores(1, s1_ref)
    softmax(s0_ref, p0_ref)

    def body(kk, carry):
        t = 2 * kk
        scores(t, s0_ref)
        softmax(s1_ref, p1_ref)
        values(t - 2, p0_ref)
        scores(t + 1, s1_ref)
        softmax(s0_ref, p0_ref)
        values(t - 1, p1_ref)
        return carry

    lax.fori_loop(1, n_tiles // 2, body, 0)
    softmax(s1_ref, p1_ref)
    values(n_tiles - 2, p0_ref)
    values(n_tiles - 1, p1_ref)


def _attention(q3, k3, v3, dims):
    n_heads, t, _ = q3.shape
    p_rows, lc, n_batch = dims
    l_lat = p_rows - lc
    tq = max(c for c in range(lc, ATTN_Q_TILE + 1, lc) if l_lat % (2 * c) == 0)
    return pl.pallas_call(
        functools.partial(_attn_kernel, tq=tq, lc=lc),
        grid=(n_batch, n_heads),
        in_specs=[
            pl.BlockSpec((1, p_rows, MLA_QK), lambda b, h: (h, b, 0)),
            pl.BlockSpec((1, p_rows, MLA_QK), lambda b, h: (h, b, 0)),
            pl.BlockSpec((1, p_rows, 2 * MLA_V), lambda b, h: (h, b, 0)),
        ],
        out_specs=pl.BlockSpec((p_rows, MLA_V), lambda b, h: (b, h)),
        out_shape=jax.ShapeDtypeStruct((t, n_heads * MLA_V), BF),
        scratch_shapes=[pltpu.VMEM((tq, p_rows), F32), pltpu.VMEM((tq, p_rows), F32),
                        pltpu.VMEM((tq, p_rows), BF), pltpu.VMEM((tq, p_rows), BF)],
        compiler_params=_cparams(("parallel", "parallel")),
        name="mla_attention",
    )(q3, k3, v3)


def _hgrn_gates(zq, zf, lb, tri):
    f = lb + (1.0 - lb) * _sigmoid(zf.astype(F32))
    lf = jnp.log(f)
    lf1 = lf.astype(BF)
    r1 = lf - lf1.astype(F32)
    lf2 = r1.astype(BF)
    lf3 = (r1 - lf2.astype(F32)).astype(BF)
    b_all = (jnp.dot(tri, lf1, preferred_element_type=F32) + jnp.dot(tri, lf2, preferred_element_type=F32)
             + jnp.dot(tri, lf3, preferred_element_type=F32))
    return f, b_all, jnp.sum(lf, axis=0, keepdims=True), _silu(zq.astype(F32))


def _hgrn_chunk(zq, zi, zf, lb, tri, o_ref, oi_ref, st_ref, stream, bi, rows, n_heads):
    feeds = tri > 0.5
    f, b_all, tot_all, q_all = _hgrn_gates(zq, zf, lb, tri)
    nt = (((1,), (1,)), ((), ()))
    for h in range(n_heads):
        sl = slice(h * HG_DK, (h + 1) * HG_DK)
        b, tot, q, k = b_all[:, sl], tot_all[:, sl], q_all[:, sl], 1.0 - f[:, sl]
        v = zi[:, sl]
        ref = 0.5 * tot
        q_loc = (q * jnp.exp(jnp.minimum(b - ref, HG_EXP_CLAMP))).astype(BF)
        k_loc = (k * jnp.exp(jnp.minimum(ref - b, HG_EXP_CLAMP))).astype(BF)
        a = lax.dot_general(q_loc, k_loc, nt, preferred_element_type=F32)
        a = jnp.where(feeds, a, 0.0).astype(BF)
        state_t = st_ref[stream, h]
        q_in = (q * jnp.exp(b)).astype(BF)
        o_inter = lax.dot_general(q_in, state_t.astype(BF), nt, preferred_element_type=F32)
        oi_ref[stream, rows, sl] = o_inter
        o_ref[bi, rows, sl] = jnp.dot(a, v, preferred_element_type=F32) + o_inter
        k_out = (k * jnp.exp(tot - b)).astype(BF)
        upd = lax.dot_general(v, k_out, (((0,), (0,)), ((), ())), preferred_element_type=F32)
        st_ref[stream, h] = state_t * jnp.exp(tot) + upd
    return tot_all


def _hgrn_chunk_exact(zq, zi, zf, lb, tri, o_ref, oi_ref, qs_ref, bs_ref, stream, bi, rows, rev, n_heads):
    c = tri.shape[0]
    f, b_all, _, q_all = _hgrn_gates(zq, zf, lb, tri)
    s_idx = lax.broadcasted_iota(jnp.int32, (c, 1), 0)
    lane = lax.broadcasted_iota(jnp.int32, (c, c), 1)
    for h in range(n_heads):
        sl = slice(h * HG_DK, (h + 1) * HG_DK)
        b, k = b_all[:, sl], 1.0 - f[:, sl]
        qs_ref[...] = q_all[:, sl]
        bs_ref[...] = b

        def one_row(t, a_t):
            feeds_t = (s_idx >= t) if rev else (s_idx <= t)
            decay = jnp.exp(jnp.minimum(bs_ref[pl.ds(t, 1), :] - b, 0.0))
            col = jnp.sum(jnp.where(feeds_t, qs_ref[pl.ds(t, 1), :] * k * decay, 0.0), axis=-1, keepdims=True)
            return a_t + jnp.where(lane == t, col, 0.0)

        a_t = lax.fori_loop(0, c, one_row, jnp.zeros((c, c), F32))
        o_intra = lax.dot_general(a_t.astype(BF), zi[:, sl], (((0,), (0,)), ((), ())), preferred_element_type=F32)
        o_ref[bi, rows, sl] = o_intra + oi_ref[stream, rows, sl]


def _hgrn_kernel(fq_ref, fi_ref, ff_ref, bq_ref, bi_ref, bf_ref, lb_ref, tri_ref, of_ref, ob_ref, st_ref, oi_ref,
                 qs_ref, bs_ref, *, n_heads, n_batch):
    @pl.when(pl.program_id(0) == 0)
    def _():
        st_ref[...] = jnp.zeros_like(st_ref)

    c = tri_ref.shape[1]
    work = []
    for rev, (q_ref, i_ref, f_ref, o_ref) in enumerate(((fq_ref, fi_ref, ff_ref, of_ref),
                                                       (bq_ref, bi_ref, bf_ref, ob_ref))):
        order = range(HG_STEP_CHUNKS - 1, -1, -1) if rev else range(HG_STEP_CHUNKS)
        for sub in order:
            rows = slice(sub * c, (sub + 1) * c)
            for bi in range(n_batch):
                work.append((rev, q_ref, i_ref, f_ref, o_ref, rows, bi))

    weakest = None
    for rev, q_ref, i_ref, f_ref, o_ref, rows, bi in work:
        tot = _hgrn_chunk(q_ref[bi, rows, :], i_ref[bi, rows, :], f_ref[bi, rows, :], lb_ref[rev], tri_ref[rev],
                          o_ref, oi_ref, st_ref, rev * n_batch + bi, bi, rows, n_heads)
        weakest = tot if weakest is None else jnp.minimum(weakest, tot)

    @pl.when(jnp.min(weakest) < -2.0 * HG_EXP_CLAMP)
    def _():
        for rev, q_ref, i_ref, f_ref, o_ref, rows, bi in work:
            _hgrn_chunk_exact(q_ref[bi, rows, :], i_ref[bi, rows, :], f_ref[bi, rows, :], lb_ref[rev], tri_ref[rev],
                              o_ref, oi_ref, qs_ref, bs_ref, rev * n_batch + bi, bi, rows, bool(rev), n_heads)


def _hgrn(z, lb2, dims):
    t, nz = z.shape
    p_rows, lc, n_batch = dims
    c = HG_CHUNK
    w = lb2.shape[-1]
    n_heads = w // HG_DK
    rb = HG_STEP_CHUNKS * c
    assert lc % rb == 0 and p_rows % rb == 0
    nch, nctx = p_rows // rb, lc // rb
    idx = np.arange(c)
    tri = jnp.asarray(np.stack([idx[None, :] <= idx[:, None], idx[None, :] >= idx[:, None]]), BF)
    z3 = z.reshape(n_batch, p_rows, nz)

    def back(j):
        return jnp.where(j < nctx, nctx - 1 - j, nch - 1 - (j - nctx))

    def spec(col, rev):
        if rev:
            return pl.BlockSpec((n_batch, rb, w), lambda j: (0, back(j), col // w))
        return pl.BlockSpec((n_batch, rb, w), lambda j: (0, j, col // w))

    out_f, out_b = pl.pallas_call(
        functools.partial(_hgrn_kernel, n_heads=n_heads, n_batch=n_batch),
        grid=(nch,),
        in_specs=[
            spec(Z_HQ, False), spec(Z_HI, False), spec(Z_HFF, False),
            spec(Z_HQ, True), spec(Z_HI, True), spec(Z_HFB, True),
            pl.BlockSpec((2, 1, w), lambda j: (0, 0, 0)),
            pl.BlockSpec((2, c, c), lambda j: (0, 0, 0)),
        ],
        out_specs=[pl.BlockSpec((n_batch, rb, w), lambda j: (0, j, 0)),
                   pl.BlockSpec((n_batch, rb, w), lambda j: (0, back(j), 0))],
        out_shape=[jax.ShapeDtypeStruct((n_batch, p_rows, w), F32)] * 2,
        scratch_shapes=[pltpu.VMEM((2 * n_batch, n_heads, HG_DK, HG_DK), F32),
                        pltpu.VMEM((2 * n_batch, rb, w), F32),
                        pltpu.VMEM((c, HG_DK), F32), pltpu.VMEM((c, HG_DK), F32)],
        compiler_params=_cparams(("arbitrary",)),
        name="hgrn_scan",
    )(z3, z3, z3, z3, z3, z3, lb2.reshape(2, 1, w), tri)
    return out_f.reshape(t, w), out_b.reshape(t, w)


def _hgrn_readout_kernel(of_ref, ob_ref, zg_ref, g_ref, y_ref, *, n_heads):
    o = of_ref[...] + ob_ref[...]
    g = g_ref[...]
    gate = _silu(zg_ref[...].astype(F32))
    for h in range(n_heads):
        sl = slice(h * HG_DK, (h + 1) * HG_DK)
        y_ref[:, sl] = (_rms(o[:, sl], g[:, sl]) * gate[:, sl]).astype(y_ref.dtype)


def _hgrn_readout(o_fwd, o_bwd, z, g_norm):
    t, w = o_fwd.shape
    tm = _divisor_tile(t, 1088)
    return pl.pallas_call(
        functools.partial(_hgrn_readout_kernel, n_heads=w // HG_DK),
        grid=(t // tm,),
        in_specs=[
            pl.BlockSpec((tm, w), lambda i: (i, 0)),
            pl.BlockSpec((tm, w), lambda i: (i, 0)),
            pl.BlockSpec((tm, w), lambda i: (i, Z_HG // w)),
            pl.BlockSpec((1, w), lambda i: (0, 0)),
        ],
        out_specs=pl.BlockSpec((tm, w), lambda i: (i, 0)),
        out_shape=jax.ShapeDtypeStruct((t, w), BF),
        compiler_params=_cparams(("parallel",)),
        name="hgrn_readout",
    )(o_fwd, o_bwd, z, g_norm.reshape(1, w))


def _segment_tile(i, tc, p_rows, lc):
    per_batch = p_rows // tc
    j = i % per_batch
    nctx = lc // tc
    first = jnp.logical_or(j == 0, j == nctx)
    last = jnp.logical_or(j == nctx - 1, j == per_batch - 1)
    in_ctx = j < nctx
    seg_pos0 = jnp.where(in_ctx, j, j - nctx) * tc
    seg_len = jnp.where(in_ctx, lc, p_rows - lc)
    return first, last, seg_pos0, seg_len


def _halo_specs(tc, width, col_block, n_rows):
    per = tc // HALO
    last_blk = n_rows // HALO - 1
    return [
        pl.BlockSpec((HALO, width), lambda i: (jnp.maximum(i * per - 1, 0), col_block)),
        pl.BlockSpec((tc, width), lambda i: (i, col_block)),
        pl.BlockSpec((HALO, width), lambda i: (jnp.minimum((i + 1) * per, last_blk), col_block)),
    ]


def _conv_kernel(prev_ref, cur_ref, next_ref, w_ref, b_ref, lg_ref, lb_ref, y_ref, buf, shifted, *, tc, p_rows, lc, ch):
    first, last, _, _ = _segment_tile(pl.program_id(0), tc, p_rows, lc)

    def glu(zz):
        zz = zz.astype(F32)
        return zz[:, :ch] * _sigmoid(zz[:, ch:])

    buf[0:HALO, :] = jnp.where(first, 0.0, glu(prev_ref[...]))
    buf[HALO:HALO + tc, :] = glu(cur_ref[...])
    buf[HALO + tc:, :] = jnp.where(last, 0.0, glu(next_ref[...]))
    span = shifted.shape[1]
    for s in range(1, SUBLANES):
        shifted[s - 1] = buf[s:s + span, :]
    w = w_ref[...]
    acc = jnp.zeros((tc, ch), F32) + b_ref[...]
    for kk in range(CONV_WIDTH):
        blk, s = divmod(HALO - CONV_WIDTH // 2 + kk, SUBLANES)
        r0 = blk * SUBLANES
        win = buf[r0:r0 + tc, :] if s == 0 else shifted[s - 1, r0:r0 + tc, :]
        acc = acc + win * w[kk:kk + 1, :]
    mu = jnp.mean(acc, axis=-1, keepdims=True)
    cen = acc - mu
    var = jnp.mean(cen * cen, axis=-1, keepdims=True)
    y = cen * lax.rsqrt(var + EPS) * lg_ref[...] + lb_ref[...]
    y_ref[...] = _silu(y).astype(y_ref.dtype)


def _conv_mixer(z, w_dw, b_dw, ln_g, ln_b, dims):
    t = z.shape[0]
    p_rows, lc, _ = dims
    ch = w_dw.shape[1]
    tc = _divisor_tile(lc, 256)
    row = lambda a: a.reshape(1, ch)
    const = lambda i: (0, 0)
    return pl.pallas_call(
        functools.partial(_conv_kernel, tc=tc, p_rows=p_rows, lc=lc, ch=ch),
        grid=(t // tc,),
        in_specs=_halo_specs(tc, 2 * ch, Z_CONV // (2 * ch), t) + [
            pl.BlockSpec((CONV_WIDTH, ch), const),
            pl.BlockSpec((1, ch), const), pl.BlockSpec((1, ch), const), pl.BlockSpec((1, ch), const),
        ],
        out_specs=pl.BlockSpec((tc, ch), lambda i: (i, 0)),
        out_shape=jax.ShapeDtypeStruct((t, ch), BF),
        scratch_shapes=[pltpu.VMEM((tc + 2 * HALO, ch), F32),
                        pltpu.VMEM((SUBLANES - 1, tc + 2 * HALO - SUBLANES, ch), F32)],
        compiler_params=_cparams(("parallel",)),
        name="conv_mixer",
    )(z, z, z, w_dw, row(b_dw), row(ln_g), row(ln_b))


def _pool_kernel(prev_ref, cur_ref, next_ref, w_ref, s_ref, y_ref, buf, *, tc, p_rows, lc):
    first, last, seg_pos0, seg_len = _segment_tile(pl.program_id(0), tc, p_rows, lc)
    buf[0:HALO, :] = jnp.where(first, 0.0, prev_ref[...].astype(F32))
    buf[HALO:HALO + tc, :] = cur_ref[...].astype(F32)
    buf[HALO + tc:, :] = jnp.where(last, 0.0, next_ref[...].astype(F32))
    pos = seg_pos0 + lax.broadcasted_iota(jnp.int32, (tc, 1), 0)
    scale = s_ref[...]
    for gi, win in enumerate(POOL_WINDOWS):
        sl = slice(gi * POOL_GROUP, (gi + 1) * POOL_GROUP)
        half = win // 2
        tot = jnp.zeros((tc, POOL_GROUP), F32)
        for u in range(-half, half):
            tot = tot + buf[HALO + u:HALO + u + tc, sl]
        cnt = (jnp.minimum(pos + half, seg_len) - jnp.maximum(pos - half, 0)).astype(F32)
        pooled = tot / cnt - buf[HALO:HALO + tc, sl]
        y = jnp.dot(pooled.astype(BF), w_ref[gi].astype(BF), preferred_element_type=F32)
        y_ref[:, sl] = (y * scale[:, sl]).astype(y_ref.dtype)


def _pool_mixer(z, w_pool, scale, dims):
    t = z.shape[0]
    p_rows, lc, _ = dims
    ch = scale.shape[0]
    tc = _divisor_tile(lc, 256)
    return pl.pallas_call(
        functools.partial(_pool_kernel, tc=tc, p_rows=p_rows, lc=lc),
        grid=(t // tc,),
        in_specs=_halo_specs(tc, ch, Z_POOL // ch, t) + [
            pl.BlockSpec(w_pool.shape, lambda i: (0, 0, 0)),
            pl.BlockSpec((1, ch), lambda i: (0, 0)),
        ],
        out_specs=pl.BlockSpec((tc, ch), lambda i: (i, 0)),
        out_shape=jax.ShapeDtypeStruct((t, ch), BF),
        scratch_shapes=[pltpu.VMEM((tc + 2 * HALO, ch), F32)],
        compiler_params=_cparams(("parallel",)),
        name="pool_mixer",
    )(z, z, z, w_pool, scale.reshape(1, ch))


def _merge_kernel(y0, y1, y2, y3, g0, g1, g2, g3, w_ref, o_ref, w_s):
    @pl.when(pl.program_id(1) == 0)
    def _():
        w_s[...] = w_ref[0].astype(BF)

    acc = None
    for n, (y_ref, g_ref) in enumerate(((y0, g0), (y1, g1), (y2, g2), (y3, g3))):
        proj = jnp.dot(y_ref[...], w_s[n], preferred_element_type=F32)
        gate = 0.5 * jnp.tanh(0.5 * g_ref[...].astype(F32)) + 0.5
        term = gate * proj
        acc = term if acc is None else acc + term
    o_ref[...] = acc.astype(o_ref.dtype)


def _merge(ys, z, w_branch, l):
    t, bw = ys[0].shape
    _, nb, _, d = w_branch.shape
    tm = _divisor_tile(t, 544)
    tn = _divisor_tile(d, 1024, 128)
    npn = d // tn

    def gate_spec(n):
        return pl.BlockSpec((tm, tn), lambda j, i: (i, Z_GATE // tn + n * npn + j))

    return pl.pallas_call(
        _merge_kernel,
        grid=(npn, t // tm),
        in_specs=[pl.BlockSpec((tm, bw), lambda j, i: (i, 0))] * nb + [gate_spec(n) for n in range(nb)] + [
            pl.BlockSpec((1, nb, bw, tn), lambda j, i: (l, 0, 0, j))],
        out_specs=pl.BlockSpec((tm, tn), lambda j, i: (i, j)),
        out_shape=jax.ShapeDtypeStruct((t, d), BF),
        scratch_shapes=[pltpu.VMEM((nb, bw, tn), BF)],
        compiler_params=_cparams(("parallel", "arbitrary")),
        name="branch_merge",
    )(*ys, z, z, z, z, w_branch)


def _residual_epilogue(y, x_ref, gpost_ref, gate_ref, gnext_ref, sh_ref, sc_ref, xo_ref, ho_ref, b, n_batch, cm,
                       rows=slice(None)):
    cm = None if cm is None else cm[rows]
    gate_g = _row_vec(gate_ref, b, n_batch, cm) * gpost_ref[...]
    x_new = x_ref[rows, :] + _rms(y, gate_g)
    xo_ref[rows, :] = x_new
    if ho_ref is None:
        return None
    shift = _row_vec(sh_ref, b, n_batch, cm)
    scale_g = (1.0 + _row_vec(sc_ref, b, n_batch, cm)) * gnext_ref[...]
    h = _rms(x_new, scale_g) + shift
    ho_ref[rows, :] = h.astype(ho_ref.dtype)
    return h


def _route_top2(logits, n_experts):
    lane = lax.broadcasted_iota(jnp.int32, logits.shape, 1)
    lg = jnp.where(lane < n_experts, logits, NEG_BIG)
    m1 = jnp.max(lg, axis=-1, keepdims=True)
    i1 = jnp.min(jnp.where(lg == m1, lane, 1 << 20), axis=-1, keepdims=True)
    lg2 = jnp.where(lane == i1, NEG_BIG, lg)
    m2 = jnp.max(lg2, axis=-1, keepdims=True)
    i2 = jnp.min(jnp.where(lg2 == m2, lane, 1 << 20), axis=-1, keepdims=True)
    e2 = jnp.exp(m2 - m1)
    w1 = 1.0 / (1.0 + e2)
    w2 = e2 / (1.0 + e2)
    out = jnp.where(lane == 0, i1.astype(F32), 0.0)
    out = jnp.where(lane == 1, i2.astype(F32), out)
    out = jnp.where(lane == 2, w1, out)
    return jnp.where(lane == 3, w2, out)


def _router_logits(h, wr_ref):
    h_hi = h.astype(BF)
    h_lo = (h - h_hi.astype(F32)).astype(BF)
    return (jnp.dot(h_hi, wr_ref[0], preferred_element_type=F32) + jnp.dot(h_hi, wr_ref[1], preferred_element_type=F32)
            + jnp.dot(h_lo, wr_ref[0], preferred_element_type=F32))


def _out_proj_kernel(*refs, tm, p_rows, lc, n_batch, emit_h, n_experts):
    m_ref, w_ref, x_ref, gpost_ref, gate_ref = refs[:5]
    pos = 5
    gnext_ref = sh_ref = sc_ref = wr_ref = None
    if emit_h:
        gnext_ref, sh_ref, sc_ref = refs[pos:pos + 3]
        pos += 3
    if n_experts:
        wr_ref = refs[pos]
        pos += 1
    xo_ref = refs[pos]
    pos += 1
    ho_ref = route_ref = None
    if emit_h:
        ho_ref = refs[pos]
        pos += 1
    if n_experts:
        route_ref = refs[pos]
        pos += 1
    n_split = EPILOGUE_SPLIT if tm % (EPILOGUE_SPLIT * 16) == 0 else 1
    rs = tm // n_split
    groups = [slice(s * rs, (s + 1) * rs) for s in range(n_split)]

    def body(b, cm):
        ys = [jnp.dot(m_ref[rows, :], w_ref[0], preferred_element_type=F32) for rows in groups]
        for rows, y in zip(groups, ys):
            h = _residual_epilogue(y, x_ref, gpost_ref, gate_ref, gnext_ref, sh_ref, sc_ref, xo_ref, ho_ref, b,
                                   n_batch, cm, rows)
            if n_experts:
                route_ref[rows, :] = _route_top2(_router_logits(h, wr_ref), n_experts)

    _by_tile_kind(pl.program_id(0), tm, p_rows, lc, body)


def _out_proj(m, w_stack, lw, xs, g_post, mods, l, j_gate, nxt, dims, router=None, name="out_proj"):
    t, d = xs.shape
    k = m.shape[1]
    p_rows, lc, n_batch = dims
    tm = _divisor_tile(p_rows, 544 if k <= d else 272)
    const = lambda i: (0, 0)
    rowtile = pl.BlockSpec((tm, d), lambda i: (i, 0))
    args = [m, w_stack, xs, g_post.reshape(1, d), mods]
    in_specs = [pl.BlockSpec((tm, k), lambda i: (i, 0)),
                pl.BlockSpec((1, k, d), lambda i: (lw, 0, 0), pipeline_mode=pl.Buffered(1)),
                rowtile, pl.BlockSpec((1, d), const), _mod_spec(l, j_gate, d)]
    out_shape = [jax.ShapeDtypeStruct((t, d), F32)]
    out_specs = [rowtile]
    n_experts = 0
    if nxt is not None:
        g_next, l_next, j_shift, j_scale = nxt
        args += [g_next.reshape(1, d), mods, mods]
        in_specs += [pl.BlockSpec((1, d), const), _mod_spec(l_next, j_shift, d), _mod_spec(l_next, j_scale, d)]
        out_shape.append(jax.ShapeDtypeStruct((t, d), BF if router is None else F32))
        out_specs.append(rowtile)
    if router is not None:
        wr, n_experts = router
        args.append(wr)
        in_specs.append(pl.BlockSpec(wr.shape, lambda i: (0, 0, 0)))
        out_shape.append(jax.ShapeDtypeStruct((t, 128), F32))
        out_specs.append(pl.BlockSpec((tm, 128), lambda i: (i, 0)))
    return pl.pallas_call(
        functools.partial(_out_proj_kernel, tm=tm, p_rows=p_rows, lc=lc, n_batch=n_batch,
                          emit_h=nxt is not None, n_experts=n_experts),
        grid=(t // tm,),
        in_specs=in_specs,
        out_specs=out_specs,
        out_shape=out_shape,
        compiler_params=_cparams(("parallel",)),
        name=name,
    )(*args)


def _swiglu_kernel(x_ref, w1_ref, w3_ref, o_ref, w1_s, w3_s):
    @pl.when(pl.program_id(1) == 0)
    def _():
        w1_s[...] = w1_ref[0].astype(BF)
        w3_s[...] = w3_ref[0].astype(BF)

    x = x_ref[...]
    a = jnp.dot(x, w1_s[...], preferred_element_type=F32)
    g = jnp.dot(x, w3_s[...], preferred_element_type=F32)
    o_ref[...] = (_silu(a) * g).astype(o_ref.dtype)


def _swiglu_up(h, w1, w3, jl):
    t, d = h.shape
    dff = w1.shape[2]
    tm = _divisor_tile(t, 1088)
    tn = _divisor_tile(dff, 512, 128)
    return pl.pallas_call(
        _swiglu_kernel,
        grid=(dff // tn, t // tm),
        in_specs=[
            pl.BlockSpec((tm, d), lambda j, i: (i, 0)),
            pl.BlockSpec((1, d, tn), lambda j, i: (jl, 0, j)),
            pl.BlockSpec((1, d, tn), lambda j, i: (jl, 0, j)),
        ],
        out_specs=pl.BlockSpec((tm, tn), lambda j, i: (i, j)),
        out_shape=jax.ShapeDtypeStruct((t, dff), BF),
        scratch_shapes=[pltpu.VMEM((d, tn), BF), pltpu.VMEM((d, tn), BF)],
        compiler_params=_cparams(("parallel", "arbitrary")),
        name="swiglu_up",
    )(h, w1, w3)


def _gather_rows_kernel(idx_ref, src_ref, o_ref, *scratch, tr):
    base = pl.program_id(0) * tr
    dst, sem = (o_ref, scratch[0]) if len(scratch) == 1 else scratch

    def copy(r):
        return pltpu.make_async_copy(src_ref.at[pl.ds(idx_ref[base + r], 1), :], dst.at[pl.ds(r, 1), :], sem)

    def issue(g, carry):
        for u in range(GATHER_UNROLL):
            copy(g * GATHER_UNROLL + u).start(priority=u % 2)
        return carry

    lax.fori_loop(0, tr // GATHER_UNROLL, issue, 0)
    pltpu.make_async_copy(src_ref.at[pl.ds(0, tr), :], dst.at[pl.ds(0, tr), :], sem).wait()
    if dst is not o_ref:
        o_ref[...] = dst[...].astype(o_ref.dtype)


def _gather_rows(src, idx, out_dtype):
    n, d = src.shape
    assert src.dtype == F32
    rows = idx.shape[0]
    tr = _divisor_tile(rows, MOE_GATHER_ROWS, 8)
    staging = [] if out_dtype == src.dtype else [pltpu.VMEM((tr, d), src.dtype)]
    return pl.pallas_call(
        functools.partial(_gather_rows_kernel, tr=tr),
        grid_spec=pltpu.PrefetchScalarGridSpec(
            num_scalar_prefetch=1,
            grid=(rows // tr,),
            in_specs=[pl.BlockSpec(memory_space=pl.ANY)],
            out_specs=pl.BlockSpec((tr, d), lambda i, idx_ref: (i, 0)),
            scratch_shapes=staging + [pltpu.SemaphoreType.DMA(())],
        ),
        out_shape=jax.ShapeDtypeStruct((rows, d), out_dtype),
        compiler_params=_cparams(("arbitrary",)),
        name="moe_gather",
    )(idx, src)


def _expert_rows_loop(st_ref, nt_ref, n_slots, load_copy, store_copy, compute):
    e, n = pl.program_id(0), pl.program_id(1)
    n_exp, n_col = pl.num_programs(0), pl.num_programs(1)
    start, n_tiles = st_ref[e], nt_ref[e]
    first_step = jnp.logical_and(e == 0, n == 0)
    last_step = jnp.logical_and(e == n_exp - 1, n == n_col - 1)
    e_next = jnp.minimum(jnp.where(n == n_col - 1, e + 1, e), n_exp - 1)
    ahead = n_slots - 1

    @pl.when(jnp.logical_and(first_step, n_tiles > 0))
    def _():
        load_copy(start, 0, 0).start(priority=1)

    for k in range(1, ahead):
        @pl.when(n_tiles > k)
        def _():
            load_copy(start, k, k).start(priority=1)

    def body(i, carry):
        load_copy(start, i, i % n_slots).wait()

        @pl.when(i + ahead < n_tiles)
        def _():
            load_copy(start, i + ahead, (i + ahead) % n_slots).start(priority=1)

        @pl.when(i >= 2)
        def _():
            store_copy(start, i - 2, i % 2).wait()

        compute(i % n_slots, i % 2)
        store_copy(start, i, i % 2).start(priority=1)
        return carry

    lax.fori_loop(0, n_tiles, body, 0)

    @pl.when(jnp.logical_and(jnp.logical_not(last_step), nt_ref[e_next] > 0))
    def _():
        load_copy(st_ref[e_next], 0, 0).start(priority=1)

    @pl.when(n_tiles >= 2)
    def _():
        store_copy(start, n_tiles - 2, n_tiles % 2).wait()

    @pl.when(n_tiles >= 1)
    def _():
        store_copy(start, n_tiles - 1, (n_tiles - 1) % 2).wait()


def _zero_tail_rows(out_hbm, zbuf, sem, first_row, col0, tn):
    zbuf[...] = jnp.zeros_like(zbuf)
    n_chunks = (out_hbm.shape[0] - first_row) // MOE_ROW_ALIGN

    def copy(c):
        r0 = pl.multiple_of(first_row + c * MOE_ROW_ALIGN, MOE_ROW_ALIGN)
        return pltpu.make_async_copy(zbuf, out_hbm.at[pl.ds(r0, MOE_ROW_ALIGN), pl.ds(col0, tn)], sem)

    def issue(c, carry):
        copy(c).start()
        return carry

    def drain(c, carry):
        copy(c).wait()
        return carry

    lax.fori_loop(0, n_chunks, issue, 0)
    lax.fori_loop(0, n_chunks, drain, 0)


def _moe_up_kernel(st_ref, nt_ref, x_hbm, w1_ref, w3_ref, a_hbm, w1_s, w3_s, xbuf, obuf, zbuf, sem_in, sem_out, sem_z,
                   *, tr, tn):
    e, n = pl.program_id(0), pl.program_id(1)
    w1_s[...] = w1_ref[0, 0].astype(BF)
    w3_s[...] = w3_ref[0, 0].astype(BF)
    col0 = pl.multiple_of(n * tn, tn)

    def rows(start, i):
        return pl.ds(pl.multiple_of(start + i * tr, MOE_ROW_ALIGN), tr)

    def load_copy(start, i, slot):
        return pltpu.make_async_copy(x_hbm.at[rows(start, i)], xbuf.at[slot], sem_in.at[slot])

    def store_copy(start, i, slot):
        return pltpu.make_async_copy(obuf.at[slot], a_hbm.at[rows(start, i), pl.ds(col0, tn)], sem_out.at[slot])

    def compute(in_slot, out_slot):
        x = xbuf[in_slot]
        a = jnp.dot(x, w1_s[...], preferred_element_type=F32)
        g = jnp.dot(x, w3_s[...], preferred_element_type=F32)
        obuf[out_slot] = (_silu(a) * g).astype(obuf.dtype)

    _expert_rows_loop(st_ref, nt_ref, xbuf.shape[0], load_copy, store_copy, compute)

    @pl.when(e == pl.num_programs(0) - 1)
    def _():
        _zero_tail_rows(a_hbm, zbuf, sem_z, st_ref[e] + nt_ref[e] * tr, col0, tn)


def _moe_up(xs, w1, w3, jl, starts, n_tiles, tr):
    rows, d = xs.shape
    n_exp, dff = w1.shape[1], w1.shape[3]
    tn = _divisor_tile(dff, 512, 128)
    return pl.pallas_call(
        functools.partial(_moe_up_kernel, tr=tr, tn=tn),
        grid_spec=pltpu.PrefetchScalarGridSpec(
            num_scalar_prefetch=2,
            grid=(n_exp, dff // tn),
            in_specs=[
                pl.BlockSpec(memory_space=pl.ANY),
                pl.BlockSpec((1, 1, d, tn), lambda e, n, st, nt: (jl, e, 0, n)),
                pl.BlockSpec((1, 1, d, tn), lambda e, n, st, nt: (jl, e, 0, n)),
            ],
            out_specs=pl.BlockSpec(memory_space=pl.ANY),
            scratch_shapes=[pltpu.VMEM((d, tn), BF), pltpu.VMEM((d, tn), BF),
                            pltpu.VMEM((MOE_UP_SLOTS, tr, d), BF), pltpu.VMEM((2, tr, tn), BF),
                            pltpu.VMEM((MOE_ROW_ALIGN, tn), BF),
                            pltpu.SemaphoreType.DMA((MOE_UP_SLOTS,)), pltpu.SemaphoreType.DMA((2,)),
                            pltpu.SemaphoreType.DMA(())],
        ),
        out_shape=jax.ShapeDtypeStruct((rows, dff), BF),
        compiler_params=_cparams(("arbitrary", "arbitrary")),
        name="moe_up",
    )(starts, n_tiles, xs, w1, w3)


def _moe_down_kernel(st_ref, nt_ref, a_hbm, w_ref, y_hbm, w_s, abuf, obuf, zbuf, sem_in, sem_out, sem_z, *, tr, tn):
    e, n = pl.program_id(0), pl.program_id(1)
    w_s[...] = w_ref[0, 0].astype(BF)
    col0 = pl.multiple_of(n * tn, tn)

    def rows(start, i):
        return pl.ds(pl.multiple_of(start + i * tr, MOE_ROW_ALIGN), tr)

    def load_copy(start, i, slot):
        return pltpu.make_async_copy(a_hbm.at[rows(start, i)], abuf.at[slot], sem_in.at[slot])

    def store_copy(start, i, slot):
        return pltpu.make_async_copy(obuf.at[slot], y_hbm.at[rows(start, i), pl.ds(col0, tn)], sem_out.at[slot])

    def compute(in_slot, out_slot):
        obuf[out_slot] = jnp.dot(abuf[in_slot], w_s[...], preferred_element_type=F32)

    _expert_rows_loop(st_ref, nt_ref, abuf.shape[0], load_copy, store_copy, compute)

    @pl.when(e == pl.num_programs(0) - 1)
    def _():
        _zero_tail_rows(y_hbm, zbuf, sem_z, st_ref[e] + nt_ref[e] * tr, col0, tn)


def _moe_down(act, w2, jl, starts, n_tiles, tr):
    rows, dff = act.shape
    n_exp, d = w2.shape[1], w2.shape[3]
    tn = _divisor_tile(d, 512, 128)
    return pl.pallas_call(
        functools.partial(_moe_down_kernel, tr=tr, tn=tn),
        grid_spec=pltpu.PrefetchScalarGridSpec(
            num_scalar_prefetch=2,
            grid=(n_exp, d // tn),
            in_specs=[
                pl.BlockSpec(memory_space=pl.ANY),
                pl.BlockSpec((1, 1, dff, tn), lambda e, n, st, nt: (jl, e, 0, n)),
            ],
            out_specs=pl.BlockSpec(memory_space=pl.ANY),
            scratch_shapes=[pltpu.VMEM((dff, tn), BF), pltpu.VMEM((MOE_DOWN_SLOTS, tr, dff), BF),
                            pltpu.VMEM((2, tr, tn), F32), pltpu.VMEM((MOE_ROW_ALIGN, tn), F32),
                            pltpu.SemaphoreType.DMA((MOE_DOWN_SLOTS,)), pltpu.SemaphoreType.DMA((2,)),
                            pltpu.SemaphoreType.DMA(())],
        ),
        out_shape=jax.ShapeDtypeStruct((rows, d), F32),
        compiler_params=_cparams(("arbitrary", "arbitrary")),
        name="moe_down",
    )(starts, n_tiles, act, w2)


def _combine_kernel(*refs, tm, p_rows, lc, n_batch, emit_h):
    ya_ref, yb_ref, rt_ref, x_ref, gpost_ref, gate_ref = refs[:6]
    gnext_ref = sh_ref = sc_ref = ho_ref = None
    if emit_h:
        gnext_ref, sh_ref, sc_ref, xo_ref, ho_ref = refs[6:]
    else:
        (xo_ref,) = refs[6:]
    def body(b, cm):
        rt = rt_ref[...]
        y = rt[:, 2:3] * ya_ref[...] + rt[:, 3:4] * yb_ref[...]
        _residual_epilogue(y, x_ref, gpost_ref, gate_ref, gnext_ref, sh_ref, sc_ref, xo_ref, ho_ref, b, n_batch, cm)

    _by_tile_kind(pl.program_id(0), tm, p_rows, lc, body)


def _moe_combine(y2, route, xs, g_post, mods, l, j_gate, nxt, dims):
    t, d = xs.shape
    p_rows, lc, n_batch = dims
    tm = _divisor_tile(p_rows, 544)
    nt = t // tm
    const = lambda i: (0, 0)
    rowtile = pl.BlockSpec((tm, d), lambda i: (i, 0))
    args = [y2, y2, route, xs, g_post.reshape(1, d), mods]
    in_specs = [rowtile, pl.BlockSpec((tm, d), lambda i: (i + nt, 0)), pl.BlockSpec((tm, 128), lambda i: (i, 0)),
                rowtile, pl.BlockSpec((1, d), const), _mod_spec(l, j_gate, d)]
    out_shape = [jax.ShapeDtypeStruct((t, d), F32)]
    out_specs = [rowtile]
    if nxt is not None:
        g_next, l_next, j_shift, j_scale = nxt
        args += [g_next.reshape(1, d), mods, mods]
        in_specs += [pl.BlockSpec((1, d), const), _mod_spec(l_next, j_shift, d), _mod_spec(l_next, j_scale, d)]
        out_shape.append(jax.ShapeDtypeStruct((t, d), BF))
        out_specs.append(rowtile)
    return pl.pallas_call(
        functools.partial(_combine_kernel, tm=tm, p_rows=p_rows, lc=lc, n_batch=n_batch, emit_h=nxt is not None),
        grid=(nt,),
        in_specs=in_specs,
        out_specs=out_specs,
        out_shape=out_shape,
        compiler_params=_cparams(("parallel",)),
        name="moe_combine",
    )(*args)


def _moe_ffn(h, route, w1, w3, w2, jl, xs, g_post, mods, l, nxt, dims):
    t, d = h.shape
    n_exp = w1.shape[1]
    expert = jnp.concatenate([route[:, 0], route[:, 1]]).astype(jnp.int32)
    token = jnp.concatenate([jnp.arange(t, dtype=jnp.int32)] * 2)
    onehot = (expert[:, None] == jnp.arange(n_exp, dtype=jnp.int32)[None, :]).astype(jnp.int32)
    rank = jnp.sum((jnp.cumsum(onehot, axis=0) - onehot) * onehot, axis=1)
    counts = jnp.sum(onehot, axis=0)
    padded = ((counts + MOE_ROW_ALIGN - 1) // MOE_ROW_ALIGN) * MOE_ROW_ALIGN
    ends = jnp.cumsum(padded)
    starts = (ends - padded).astype(jnp.int32)
    tiles_up = ((counts + MOE_UP_ROWS - 1) // MOE_UP_ROWS).astype(jnp.int32)
    tiles_down = ((counts + MOE_DOWN_ROWS - 1) // MOE_DOWN_ROWS).astype(jnp.int32)
    dest = starts[expert] + rank
    rows = 2 * t + n_exp * MOE_ROW_ALIGN + max(MOE_UP_ROWS, MOE_DOWN_ROWS)
    rows = ((rows + MOE_GATHER_ROWS - 1) // MOE_GATHER_ROWS) * MOE_GATHER_ROWS
    row_token = jnp.zeros((rows,), jnp.int32).at[dest].set(token)

    xs_sorted = _gather_rows(h, row_token, BF)
    act = _moe_up(xs_sorted, w1, w3, jl, starts, tiles_up, MOE_UP_ROWS)
    y_sorted = _moe_down(act, w2, jl, starts, tiles_down, MOE_DOWN_ROWS)
    y2 = _gather_rows(y_sorted, dest, F32)
    return _moe_combine(y2, route, xs, g_post, mods, l, 5, nxt, dims)


def _pad_heads(w, n_heads, width):
    r = w.shape[0]
    w3 = w.reshape(r, n_heads, width)
    return jnp.pad(w3, ((0, 0), (0, 0), (0, HEAD_PAD - width))).reshape(r, n_heads * HEAD_PAD)


def _rope_tables(l_lat, lc):
    tt = jnp.arange(l_lat)
    row = (tt // GRID_W).astype(F32)
    col = (tt % GRID_W).astype(F32)
    axis_dim = MLA_ROPE // 2
    inv_freq = 1.0 / (ROPE_THETA ** (jnp.arange(0, axis_dim, 2, dtype=F32) / axis_dim))
    ar, ac = row[:, None] * inv_freq, col[:, None] * inv_freq
    ang = jnp.concatenate([ar, ar, ac, ac], axis=-1)
    cos = jnp.concatenate([jnp.ones((lc, MLA_ROPE), F32), jnp.cos(ang)], axis=0)
    sin = jnp.concatenate([jnp.zeros((lc, MLA_ROPE), F32), jnp.sin(ang)], axis=0)
    padc = jnp.zeros((lc + l_lat, 128 - MLA_ROPE), F32)
    pm = np.zeros((128, 128), np.float32)
    q4 = MLA_ROPE // 4
    for base in (0, 2 * q4):
        for i in range(q4):
            pm[base + q4 + i, base + i] = -1.0
            pm[base + i, base + q4 + i] = 1.0
    return jnp.concatenate([cos, padc], axis=1), jnp.concatenate([sin, padc], axis=1), jnp.asarray(pm, BF)


def kernel(x, c, ctx, c_ctx, w_mod, b_mod, g_mix_pre, g_mix_post, g_ffn_pre, g_ffn_post, w_in, mla_g_q, mla_g_kv, mla_w_uq, mla_w_ukv, hg_lb_logits, hg_g_norm, conv_w, conv_b, conv_ln_g, conv_ln_b, pool_w, pool_scale, w_branch, w_out, ffn_w1, ffn_w3, ffn_w2, moe_router, moe_w1, moe_w3, moe_w2):
    n_batch, l_lat, d = x.shape
    lc = ctx.shape[1]
    depth = w_in.shape[0]
    p_rows = lc + l_lat
    t = n_batch * p_rows
    dims = (p_rows, lc, n_batch)
    assert n_batch + 1 <= 8 and lc % HG_CHUNK == 0 and l_lat % HG_CHUNK == 0 and l_lat % GRID_W == 0
    n_heads = mla_w_uq.shape[2] // MLA_QK

    xs = jnp.concatenate([ctx, x], axis=1).reshape(t, d)
    cvec = jnp.zeros((8, d), F32).at[:n_batch].set(c).at[n_batch].set(c_ctx)
    mods = _modvecs(cvec, w_mod, b_mod)

    lb = jnp.cumsum(jax.nn.softmax(hg_lb_logits.astype(F32), axis=1), axis=1)
    lb = lb - lb[:, :1]
    cos, sin, pm = _rope_tables(l_lat, lc)

    w_in_t = jnp.swapaxes(w_in, 1, 2)
    w_out_bf = w_out.astype(BF)
    ffn_w2_bf = ffn_w2.astype(BF)

    h = _modulate(xs, g_mix_pre[0], mods, 0, 0, 1, dims)
    for l in range(depth):
        last = l == depth - 1
        z = _in_proj(h, w_in_t, l, BF)
        q3, k3, v3 = _mla_proj(z, mla_g_q[l], mla_g_kv[l], _pad_heads(mla_w_uq[l], n_heads, MLA_QK),
                               mla_w_ukv[l], cos, sin, pm, dims)
        y_mla = _attention(q3, k3, v3, dims)
        y_hg = _hgrn_readout(*_hgrn(z, lb[:, l], dims), z, hg_g_norm[l])
        y_conv = _conv_mixer(z, conv_w[l], conv_b[l], conv_ln_g[l], conv_ln_b[l], dims)
        y_pool = _pool_mixer(z, pool_w[l], pool_scale[l], dims)
        m = _merge([y_mla, y_hg, y_conv, y_pool], z, w_branch, l)
        j = l // 2
        nxt_ffn = (g_ffn_pre[l], l, 3, 4)
        nxt_mix = None if last else (g_mix_pre[l + 1], l + 1, 0, 1)
        if l % 2 == 0:
            xs, h2 = _out_proj(m, w_out_bf, l, xs, g_mix_post[l], mods, l, 2, nxt_ffn, dims)
            act = _swiglu_up(h2, ffn_w1, ffn_w3, j)
            res = _out_proj(act, ffn_w2_bf, j, xs, g_ffn_post[l], mods, l, 5, nxt_mix, dims, name="ffn_down")
        else:
            n_exp = moe_router.shape[2]
            wr = jnp.pad(moe_router[j], ((0, 0), (0, 128 - n_exp)))
            wr_hi = wr.astype(BF)
            wr2 = jnp.stack([wr_hi, (wr - wr_hi.astype(F32)).astype(BF)])
            xs, h2, route = _out_proj(m, w_out_bf, l, xs, g_mix_post[l], mods, l, 2, nxt_ffn, dims,
                                      router=(wr2, n_exp))
            res = _moe_ffn(h2, route, moe_w1, moe_w3, moe_w2, j, xs, g_ffn_post[l], mods, l, nxt_mix, dims)
        if last:
            (xs,) = res
        else:
            xs, h = res
    return xs.reshape(n_batch, p_rows, d)[:, lc:, :]
```

```python
import functools

import numpy as np
import jax
import jax.numpy as jnp
from jax import lax
from jax.experimental import pallas as pl
from jax.experimental.pallas import tpu as pltpu

F32 = jnp.float32
BF = jnp.bfloat16

EPS = 1e-6
GRID_W = 64
ROPE_THETA = 10000.0
N_BRANCH = 4
MLA_NOPE, MLA_ROPE, MLA_V = 128, 64, 128
MLA_QK = MLA_NOPE + MLA_ROPE
HEAD_PAD = 256
HG_DK = 128
CONV_WIDTH = 31
POOL_WINDOWS = (2, 4, 8, 16)
POOL_GROUP = 128
LANES = 128
IN_HALF = 512
IN_GROUP = 64
SUBLANES = 8
HALO = 16
HG_CHUNK = 64
HG_STEP_CHUNKS = 4
HG_EXP_CLAMP = 80.0
NEG_BIG = -1e30
ATTN_Q_TILE = 512
MOE_UP_ROWS = 512
MOE_DOWN_ROWS = 256
GATHER_UNROLL = 8
EPILOGUE_SPLIT = 2
MOE_UP_SLOTS = 5
MOE_DOWN_SLOTS = 3
MOE_ROW_ALIGN = 16
MOE_GATHER_ROWS = 1024

VMEM_LIMIT_V7X = 56 * 1024 * 1024

Z_CONV, Z_HQ, Z_HI, Z_HFF, Z_HFB, Z_HG, Z_POOL, Z_MLA, Z_GATE = 0, 1024, 1536, 2048, 2560, 3072, 3584, 4096, 5120
MLA_ZKV_OFF, MLA_ZR_OFF = 384, 640


def _cparams(sem, vmem=VMEM_LIMIT_V7X):
    return pltpu.CompilerParams(dimension_semantics=sem, vmem_limit_bytes=vmem)


def _divisor_tile(n, target, mult=16):
    best = None
    for t in range(mult, min(n, target) + 1, mult):
        if n % t == 0:
            best = t
    assert best is not None, (n, target, mult)
    return best


def _pick_row(m, r):
    rows = lax.broadcasted_iota(jnp.int32, m.shape, 0)
    return jnp.sum(jnp.where(rows == r, m, 0.0), axis=0, keepdims=True)


def _row_vec(m_ref, b, n_batch, ctx_mask):
    m = m_ref[0, 0]
    if ctx_mask is None:
        return _pick_row(m, b)
    return jnp.where(ctx_mask, _pick_row(m, n_batch), _pick_row(m, b))


def _tile_rows(i, tm, p_rows, lc):
    per_batch = p_rows // tm
    b = i // per_batch
    pos = (i % per_batch) * tm + lax.broadcasted_iota(jnp.int32, (tm, 1), 0)
    return b, pos < lc


def _by_tile_kind(i, tm, p_rows, lc, body):
    b, cm = _tile_rows(i, tm, p_rows, lc)
    has_ctx = (i % (p_rows // tm)) * tm < lc

    @pl.when(has_ctx)
    def _():
        body(b, cm)

    @pl.when(jnp.logical_not(has_ctx))
    def _():
        body(b, None)


def _rms(x, g):
    return x * lax.rsqrt(jnp.mean(x * x, axis=-1, keepdims=True) + EPS) * g


def _sigmoid(x):
    return 1.0 / (1.0 + jnp.exp(-x))


def _silu(x):
    return x * _sigmoid(x)


def _modvec_kernel(c_ref, w_ref, b_ref, o_ref):
    s = _silu(c_ref[...])
    s_hi = s.astype(BF)
    s_lo = (s - s_hi.astype(F32)).astype(BF)
    w = w_ref[0].astype(BF)
    acc = jnp.dot(s_hi, w, preferred_element_type=F32) + jnp.dot(s_lo, w, preferred_element_type=F32)
    o_ref[0, 0] = acc + b_ref[0, 0]


def _modvecs(cvec, w_mod, b_mod):
    depth, d, n6 = w_mod.shape
    nj = n6 // d
    tn = _divisor_tile(d, 1024, 128)
    npj = d // tn
    return pl.pallas_call(
        _modvec_kernel,
        grid=(depth, nj, npj),
        in_specs=[
            pl.BlockSpec((8, d), lambda l, j, n: (0, 0)),
            pl.BlockSpec((1, d, tn), lambda l, j, n: (l, 0, j * npj + n)),
            pl.BlockSpec((1, 1, 1, tn), lambda l, j, n: (l, j, 0, n)),
        ],
        out_specs=pl.BlockSpec((1, 1, 8, tn), lambda l, j, n: (l, j, 0, n)),
        out_shape=jax.ShapeDtypeStruct((depth, nj, 8, d), F32),
        compiler_params=_cparams(("parallel", "parallel", "parallel")),
        name="modvec",
    )(cvec, w_mod, b_mod.reshape(depth, nj, 1, d))


def _modulate_kernel(x_ref, g_ref, sh_ref, sc_ref, h_ref, *, tm, p_rows, lc, n_batch):
    b, cm = _tile_rows(pl.program_id(0), tm, p_rows, lc)
    shift = _row_vec(sh_ref, b, n_batch, cm)
    scale = _row_vec(sc_ref, b, n_batch, cm)
    h_ref[...] = (_rms(x_ref[...], g_ref[...]) * (1.0 + scale) + shift).astype(h_ref.dtype)


def _mod_spec(l, j, d):
    return pl.BlockSpec((1, 1, 8, d), lambda *_: (l, j, 0, 0))


def _modulate(xs, g, mods, l, j_shift, j_scale, dims):
    t, d = xs.shape
    p_rows, lc, n_batch = dims
    tm = _divisor_tile(p_rows, 544)
    return pl.pallas_call(
        functools.partial(_modulate_kernel, tm=tm, p_rows=p_rows, lc=lc, n_batch=n_batch),
        grid=(t // tm,),
        in_specs=[
            pl.BlockSpec((tm, d), lambda i: (i, 0)),
            pl.BlockSpec((1, d), lambda i: (0, 0)),
            _mod_spec(l, j_shift, d),
            _mod_spec(l, j_scale, d),
        ],
        out_specs=pl.BlockSpec((tm, d), lambda i: (i, 0)),
        out_shape=jax.ShapeDtypeStruct((t, d), BF),
        compiler_params=_cparams(("parallel",)),
        name="modulate",
    )(xs, g.reshape(1, d), mods, mods)


def _in_proj_kernel(grp_ref, cnt_ref, x_ref, *refs):
    n_blk = len(refs) - 2
    w_refs, o_ref, w_s = refs[:n_blk], refs[n_blk], refs[n_blk + 1]
    j = pl.program_id(0)
    per_half = n_blk // 2

    @pl.when(pl.program_id(1) == 0)
    def _():
        for u in range(2):
            cnt = cnt_ref[2 * j + u]
            for i in range(per_half):
                r0 = (u * per_half + i) * IN_GROUP
                w_s[r0:r0 + IN_GROUP, :] = jnp.where(i < cnt, w_refs[u * per_half + i][0], 0.0).astype(BF)

    o_ref[...] = lax.dot_general(x_ref[...], w_s[...], (((1,), (1,)), ((), ())),
                                 preferred_element_type=F32).astype(o_ref.dtype)


def _in_proj_tables(d):
    q_rank, kv_rank = (3 * d) // 16, d // 8
    bw = d // N_BRANCH
    src, o = {}, 0
    for name, width in (("mla", q_rank + kv_rank + MLA_ROPE), ("hq", bw), ("hi", bw), ("hff", bw), ("hfb", bw),
                        ("hg", bw), ("conv", 2 * bw), ("pool", bw), ("gate", N_BRANCH * d)):
        src[name] = (o, width)
        o += width
    dst = (("conv", Z_CONV), ("hq", Z_HQ), ("hi", Z_HI), ("hff", Z_HFF), ("hfb", Z_HFB), ("hg", Z_HG),
           ("pool", Z_POOL), ("mla", Z_MLA), ("gate", Z_GATE))
    n_half = (Z_GATE + N_BRANCH * d) // IN_HALF
    grp, cnt = [0] * n_half, [0] * n_half
    for name, z0 in dst:
        s0, width = src[name]
        assert z0 % IN_HALF == 0 and s0 % IN_GROUP == 0 and width % IN_GROUP == 0
        for k in range(-(-width // IN_HALF)):
            grp[z0 // IN_HALF + k] = (s0 + k * IN_HALF) // IN_GROUP
            cnt[z0 // IN_HALF + k] = min(IN_HALF, width - k * IN_HALF) // IN_GROUP
    as_i32 = lambda v: jnp.asarray(np.array(v, np.int32))
    return as_i32(grp), as_i32(cnt)


def _in_proj(x, w_in_t, l, out_dtype, tm_target=1088):
    t, k = x.shape
    n = Z_GATE + N_BRANCH * k
    tn = 2 * IN_HALF
    per_half = IN_HALF // IN_GROUP
    tm = _divisor_tile(t, tm_target)
    tables = _in_proj_tables(k)
    last_group = w_in_t.shape[1] // IN_GROUP - 1

    def w_spec(u, i):
        return pl.BlockSpec((1, IN_GROUP, k),
                            lambda j, m, grp, cnt: (l, jnp.minimum(grp[2 * j + u] + i, last_group), 0))

    return pl.pallas_call(
        _in_proj_kernel,
        grid_spec=pltpu.PrefetchScalarGridSpec(
            num_scalar_prefetch=2,
            grid=(n // tn, t // tm),
            in_specs=[pl.BlockSpec((tm, k), lambda j, m, *_: (m, 0))]
            + [w_spec(u, i) for u in range(2) for i in range(per_half)],
            out_specs=pl.BlockSpec((tm, tn), lambda j, m, *_: (m, j)),
            scratch_shapes=[pltpu.VMEM((tn, k), BF)],
        ),
        out_shape=jax.ShapeDtypeStruct((t, n), out_dtype),
        compiler_params=_cparams(("parallel", "arbitrary")),
        name="in_proj",
    )(*tables, x, *([w_in_t] * (2 * per_half)))


def _rope128(t, cos, sin, pm):
    partner = jnp.dot(t.astype(BF), pm, preferred_element_type=F32)
    return t * cos + partner * sin


def _mla_proj_kernel(z_ref, gq_ref, gkv_ref, wq_ref, wkv_ref, cos_ref, sin_ref, pm_ref,
                     q_ref, k_ref, v_ref, wq_s, wkv_s, *, n_heads, q_rank, kv_rank, scale):
    @pl.when(pl.program_id(0) == 0)
    def _():
        wq_s[...] = wq_ref[...].astype(BF)
        wkv_s[...] = wkv_ref[...].astype(BF)

    z = z_ref[...]
    zq = z[:, :q_rank].astype(F32)
    zkv = z[:, MLA_ZKV_OFF:MLA_ZKV_OFF + kv_rank].astype(F32)
    zr = z[:, MLA_ZR_OFF:MLA_ZR_OFF + 128].astype(F32)
    cos, sin, pm = cos_ref[...], sin_ref[...], pm_ref[...]
    qn = _rms(zq, gq_ref[...]).astype(BF)
    kvn = _rms(zkv, gkv_ref[...]).astype(BF)
    k_rope = _rope128(zr, cos, sin, pm)[:, :MLA_ROPE]
    for h in range(n_heads):
        qh = jnp.dot(qn, wq_s[:, h * HEAD_PAD:(h + 1) * HEAD_PAD], preferred_element_type=F32)
        q_rope = _rope128(qh[:, MLA_NOPE:], cos, sin, pm)[:, :MLA_ROPE]
        q_ref[h] = (jnp.concatenate([qh[:, :MLA_NOPE], q_rope], axis=-1) * scale).astype(q_ref.dtype)
        kvh = jnp.dot(kvn, wkv_s[:, h * HEAD_PAD:(h + 1) * HEAD_PAD], preferred_element_type=F32)
        k_ref[h] = jnp.concatenate([kvh[:, :MLA_NOPE], k_rope], axis=-1).astype(k_ref.dtype)
        v_ref[h] = jnp.concatenate([kvh[:, MLA_NOPE:], jnp.ones_like(kvh[:, MLA_NOPE:])], axis=-1).astype(v_ref.dtype)


def _mla_proj(z, g_q, g_kv, wq_pad, wkv, cos, sin, pm, dims):
    t = z.shape[0]
    p_rows = dims[0]
    q_rank, kv_rank = g_q.shape[0], g_kv.shape[0]
    n_heads = wkv.shape[1] // HEAD_PAD
    tm = _divisor_tile(p_rows, 544)
    npb = p_rows // tm
    scale = float(MLA_QK) ** -0.5 * float(np.log2(np.e))
    const = lambda i: (0, 0)
    return pl.pallas_call(
        functools.partial(_mla_proj_kernel, n_heads=n_heads, q_rank=q_rank, kv_rank=kv_rank, scale=scale),
        grid=(t // tm,),
        in_specs=[
            pl.BlockSpec((tm, 1024), lambda i: (i, Z_MLA // 1024)),
            pl.BlockSpec((1, q_rank), const),
            pl.BlockSpec((1, kv_rank), const),
            pl.BlockSpec(wq_pad.shape, const),
            pl.BlockSpec(wkv.shape, const),
            pl.BlockSpec((tm, 128), lambda i: (i % npb, 0)),
            pl.BlockSpec((tm, 128), lambda i: (i % npb, 0)),
            pl.BlockSpec((128, 128), const),
        ],
        out_specs=[
            pl.BlockSpec((n_heads, tm, MLA_QK), lambda i: (0, i, 0)),
            pl.BlockSpec((n_heads, tm, MLA_QK), lambda i: (0, i, 0)),
            pl.BlockSpec((n_heads, tm, 2 * MLA_V), lambda i: (0, i, 0)),
        ],
        out_shape=[
            jax.ShapeDtypeStruct((n_heads, t, MLA_QK), BF),
            jax.ShapeDtypeStruct((n_heads, t, MLA_QK), BF),
            jax.ShapeDtypeStruct((n_heads, t, 2 * MLA_V), BF),
        ],
        scratch_shapes=[pltpu.VMEM(wq_pad.shape, BF), pltpu.VMEM(wkv.shape, BF)],
        compiler_params=_cparams(("arbitrary",)),
        name="mla_proj",
    )(z, g_q.reshape(1, -1), g_kv.reshape(1, -1), wq_pad, wkv, cos, sin, pm)


def _attn_kernel(q_ref, k_ref, v_ref, o_ref, s0_ref, s1_ref, p0_ref, p1_ref, *, tq, lc):
    p_rows = k_ref.shape[1]
    n_tiles = (p_rows - lc) // tq
    nt = (((1,), (1,)), ((), ()))

    def tile_start(t):
        return pl.multiple_of(lc + t * tq, lc)

    def scores(t, s_ref):
        s_ref[...] = lax.dot_general(q_ref[0, pl.ds(tile_start(t), tq), :], k_ref[0], nt, preferred_element_type=F32)

    def softmax(s_ref, p_ref):
        s = s_ref[...]
        p_ref[...] = jnp.exp2((s - jnp.max(s, axis=-1, keepdims=True)).astype(BF))

    def emit(p, v, q0, n_q):
        o = jnp.dot(p, v, preferred_element_type=F32)
        o_ref[pl.ds(q0, n_q), :] = (o[:, :MLA_V] / o[:, MLA_V:MLA_V + 1]).astype(o_ref.dtype)

    def values(t, p_ref):
        emit(p_ref[...], v_ref[0], tile_start(t), tq)

    s_ctx = lax.dot_general(q_ref[0, :lc, :], k_ref[0, :lc, :], nt, preferred_element_type=F32)
    emit(jnp.exp2((s_ctx - jnp.max(s_ctx, axis=-1, keepdims=True)).astype(BF)), v_ref[0, :lc, :], 0, lc)

    scores(0, s0_ref)
    scores(1, s1_ref)
    softmax(s0_ref, p0_ref)

    def body(kk, carry):
        t = 2 * kk
        scores(t, s0_ref)
        softmax(s1_ref, p1_ref)
        values(t - 2, p0_ref)
        scores(t + 1, s1_ref)
        softmax(s0_ref, p0_ref)
        values(t - 1, p1_ref)
        return carry

    lax.fori_loop(1, n_tiles // 2, body, 0)
    softmax(s1_ref, p1_ref)
    values(n_tiles - 2, p0_ref)
    values(n_tiles - 1, p1_ref)


def _attention(q3, k3, v3, dims):
    n_heads, t, _ = q3.shape
    p_rows, lc, n_batch = dims
    l_lat = p_rows - lc
    tq = max(c for c in range(lc, ATTN_Q_TILE + 1, lc) if l_lat % (2 * c) == 0)
    return pl.pallas_call(
        functools.partial(_attn_kernel, tq=tq, lc=lc),
        grid=(n_batch, n_heads),
        in_specs=[
            pl.BlockSpec((1, p_rows, MLA_QK), lambda b, h: (h, b, 0)),
            pl.BlockSpec((1, p_rows, MLA_QK), lambda b, h: (h, b, 0)),
            pl.BlockSpec((1, p_rows, 2 * MLA_V), lambda b, h: (h, b, 0)),
        ],
        out_specs=pl.BlockSpec((p_rows, MLA_V), lambda b, h: (b, h)),
        out_shape=jax.ShapeDtypeStruct((t, n_heads * MLA_V), BF),
        scratch_shapes=[pltpu.VMEM((tq, p_rows), F32), pltpu.VMEM((tq, p_rows), F32),
                        pltpu.VMEM((tq, p_rows), BF), pltpu.VMEM((tq, p_rows), BF)],
        compiler_params=_cparams(("parallel", "parallel")),
        name="mla_attention",
    )(q3, k3, v3)


def _hgrn_gates(zq, zf, lb, tri):
    f = lb + (1.0 - lb) * _sigmoid(zf.astype(F32))
    lf = jnp.log(f)
    lf1 = lf.astype(BF)
    r1 = lf - lf1.astype(F32)
    lf2 = r1.astype(BF)
    lf3 = (r1 - lf2.astype(F32)).astype(BF)
    b_all = (jnp.dot(tri, lf1, preferred_element_type=F32) + jnp.dot(tri, lf2, preferred_element_type=F32)
             + jnp.dot(tri, lf3, preferred_element_type=F32))
    return f, b_all, jnp.sum(lf, axis=0, keepdims=True), _silu(zq.astype(F32))


def _hgrn_chunk(zq, zi, zf, lb, tri, o_ref, oi_ref, st_ref, stream, bi, rows, n_heads):
    feeds = tri > 0.5
    f, b_all, tot_all, q_all = _hgrn_gates(zq, zf, lb, tri)
    nt = (((1,), (1,)), ((), ()))
    for h in range(n_heads):
        sl = slice(h * HG_DK, (h + 1) * HG_DK)
        b, tot, q, k = b_all[:, sl], tot_all[:, sl], q_all[:, sl], 1.0 - f[:, sl]
        v = zi[:, sl]
        ref = 0.5 * tot
        q_loc = (q * jnp.exp(jnp.minimum(b - ref, HG_EXP_CLAMP))).astype(BF)
        k_loc = (k * jnp.exp(jnp.minimum(ref - b, HG_EXP_CLAMP))).astype(BF)
        a = lax.dot_general(q_loc, k_loc, nt, preferred_element_type=F32)
        a = jnp.where(feeds, a, 0.0).astype(BF)
        state_t = st_ref[stream, h]
        q_in = (q * jnp.exp(b)).astype(BF)
        o_inter = lax.dot_general(q_in, state_t.astype(BF), nt, preferred_element_type=F32)
        oi_ref[stream, rows, sl] = o_inter
        o_ref[bi, rows, sl] = jnp.dot(a, v, preferred_element_type=F32) + o_inter
        k_out = (k * jnp.exp(tot - b)).astype(BF)
        upd = lax.dot_general(v, k_out, (((0,), (0,)), ((), ())), preferred_element_type=F32)
        st_ref[stream, h] = state_t * jnp.exp(tot) + upd
    return tot_all


def _hgrn_chunk_exact(zq, zi, zf, lb, tri, o_ref, oi_ref, qs_ref, bs_ref, stream, bi, rows, rev, n_heads):
    c = tri.shape[0]
    f, b_all, _, q_all = _hgrn_gates(zq, zf, lb, tri)
    s_idx = lax.broadcasted_iota(jnp.int32, (c, 1), 0)
    lane = lax.broadcasted_iota(jnp.int32, (c, c), 1)
    for h in range(n_heads):
        sl = slice(h * HG_DK, (h + 1) * HG_DK)
        b, k = b_all[:, sl], 1.0 - f[:, sl]
        qs_ref[...] = q_all[:, sl]
        bs_ref[...] = b

        def one_row(t, a_t):
            feeds_t = (s_idx >= t) if rev else (s_idx <= t)
            decay = jnp.exp(jnp.minimum(bs_ref[pl.ds(t, 1), :] - b, 0.0))
            col = jnp.sum(jnp.where(feeds_t, qs_ref[pl.ds(t, 1), :] * k * decay, 0.0), axis=-1, keepdims=True)
            return a_t + jnp.where(lane == t, col, 0.0)

        a_t = lax.fori_loop(0, c, one_row, jnp.zeros((c, c), F32))
        o_intra = lax.dot_general(a_t.astype(BF), zi[:, sl], (((0,), (0,)), ((), ())), preferred_element_type=F32)
        o_ref[bi, rows, sl] = o_intra + oi_ref[stream, rows, sl]


def _hgrn_kernel(fq_ref, fi_ref, ff_ref, bq_ref, bi_ref, bf_ref, lb_ref, tri_ref, of_ref, ob_ref, st_ref, oi_ref,
                 qs_ref, bs_ref, *, n_heads, n_batch):
    @pl.when(pl.program_id(0) == 0)
    def _():
        st_ref[...] = jnp.zeros_like(st_ref)

    c = tri_ref.shape[1]
    work = []
    for rev, (q_ref, i_ref, f_ref, o_ref) in enumerate(((fq_ref, fi_ref, ff_ref, of_ref),
                                                       (bq_ref, bi_ref, bf_ref, ob_ref))):
        order = range(HG_STEP_CHUNKS - 1, -1, -1) if rev else range(HG_STEP_CHUNKS)
        for sub in order:
            rows = slice(sub * c, (sub + 1) * c)
            for bi in range(n_batch):
                work.append((rev, q_ref, i_ref, f_ref, o_ref, rows, bi))

    weakest = None
    for rev, q_ref, i_ref, f_ref, o_ref, rows, bi in work:
        tot = _hgrn_chunk(q_ref[bi, rows, :], i_ref[bi, rows, :], f_ref[bi, rows, :], lb_ref[rev], tri_ref[rev],
                          o_ref, oi_ref, st_ref, rev * n_batch + bi, bi, rows, n_heads)
        weakest = tot if weakest is None else jnp.minimum(weakest, tot)

    @pl.when(jnp.min(weakest) < -2.0 * HG_EXP_CLAMP)
    def _():
        for rev, q_ref, i_ref, f_ref, o_ref, rows, bi in work:
            _hgrn_chunk_exact(q_ref[bi, rows, :], i_ref[bi, rows, :], f_ref[bi, rows, :], lb_ref[rev], tri_ref[rev],
                              o_ref, oi_ref, qs_ref, bs_ref, rev * n_batch + bi, bi, rows, bool(rev), n_heads)


def _hgrn(z, lb2, dims):
    t, nz = z.shape
    p_rows, lc, n_batch = dims
    c = HG_CHUNK
    w = lb2.shape[-1]
    n_heads = w // HG_DK
    rb = HG_STEP_CHUNKS * c
    assert lc % rb == 0 and p_rows % rb == 0
    nch, nctx = p_rows // rb, lc // rb
    idx = np.arange(c)
    tri = jnp.asarray(np.stack([idx[None, :] <= idx[:, None], idx[None, :] >= idx[:, None]]), BF)
    z3 = z.reshape(n_batch, p_rows, nz)

    def back(j):
        return jnp.where(j < nctx, nctx - 1 - j, nch - 1 - (j - nctx))

    def spec(col, rev):
        if rev:
            return pl.BlockSpec((n_batch, rb, w), lambda j: (0, back(j), col // w))
        return pl.BlockSpec((n_batch, rb, w), lambda j: (0, j, col // w))

    out_f, out_b = pl.pallas_call(
        functools.partial(_hgrn_kernel, n_heads=n_heads, n_batch=n_batch),
        grid=(nch,),
        in_specs=[
            spec(Z_HQ, False), spec(Z_HI, False), spec(Z_HFF, False),
            spec(Z_HQ, True), spec(Z_HI, True), spec(Z_HFB, True),
            pl.BlockSpec((2, 1, w), lambda j: (0, 0, 0)),
            pl.BlockSpec((2, c, c), lambda j: (0, 0, 0)),
        ],
        out_specs=[pl.BlockSpec((n_batch, rb, w), lambda j: (0, j, 0)),
                   pl.BlockSpec((n_batch, rb, w), lambda j: (0, back(j), 0))],
        out_shape=[jax.ShapeDtypeStruct((n_batch, p_rows, w), F32)] * 2,
        scratch_shapes=[pltpu.VMEM((2 * n_batch, n_heads, HG_DK, HG_DK), F32),
                        pltpu.VMEM((2 * n_batch, rb, w), F32),
                        pltpu.VMEM((c, HG_DK), F32), pltpu.VMEM((c, HG_DK), F32)],
        compiler_params=_cparams(("arbitrary",)),
        name="hgrn_scan",
    )(z3, z3, z3, z3, z3, z3, lb2.reshape(2, 1, w), tri)
    return out_f.reshape(t, w), out_b.reshape(t, w)


def _hgrn_readout_kernel(of_ref, ob_ref, zg_ref, g_ref, y_ref, *, n_heads):
    o = of_ref[...] + ob_ref[...]
    g = g_ref[...]
    gate = _silu(zg_ref[...].astype(F32))
    for h in range(n_heads):
        sl = slice(h * HG_DK, (h + 1) * HG_DK)
        y_ref[:, sl] = (_rms(o[:, sl], g[:, sl]) * gate[:, sl]).astype(y_ref.dtype)


def _hgrn_readout(o_fwd, o_bwd, z, g_norm):
    t, w = o_fwd.shape
    tm = _divisor_tile(t, 1088)
    return pl.pallas_call(
        functools.partial(_hgrn_readout_kernel, n_heads=w // HG_DK),
        grid=(t // tm,),
        in_specs=[
            pl.BlockSpec((tm, w), lambda i: (i, 0)),
            pl.BlockSpec((tm, w), lambda i: (i, 0)),
            pl.BlockSpec((tm, w), lambda i: (i, Z_HG // w)),
            pl.BlockSpec((1, w), lambda i: (0, 0)),
        ],
        out_specs=pl.BlockSpec((tm, w), lambda i: (i, 0)),
        out_shape=jax.ShapeDtypeStruct((t, w), BF),
        compiler_params=_cparams(("parallel",)),
        name="hgrn_readout",
    )(o_fwd, o_bwd, z, g_norm.reshape(1, w))


def _segment_tile(i, tc, p_rows, lc):
    per_batch = p_rows // tc
    j = i % per_batch
    nctx = lc // tc
    first = jnp.logical_or(j == 0, j == nctx)
    last = jnp.logical_or(j == nctx - 1, j == per_batch - 1)
    in_ctx = j < nctx
    seg_pos0 = jnp.where(in_ctx, j, j - nctx) * tc
    seg_len = jnp.where(in_ctx, lc, p_rows - lc)
    return first, last, seg_pos0, seg_len


def _halo_specs(tc, width, col_block, n_rows):
    per = tc // HALO
    last_blk = n_rows // HALO - 1
    return [
        pl.BlockSpec((HALO, width), lambda i: (jnp.maximum(i * per - 1, 0), col_block)),
        pl.BlockSpec((tc, width), lambda i: (i, col_block)),
        pl.BlockSpec((HALO, width), lambda i: (jnp.minimum((i + 1) * per, last_blk), col_block)),
    ]


def _conv_kernel(prev_ref, cur_ref, next_ref, w_ref, b_ref, lg_ref, lb_ref, y_ref, buf, shifted, *, tc, p_rows, lc, ch):
    first, last, _, _ = _segment_tile(pl.program_id(0), tc, p_rows, lc)

    def glu(zz):
        zz = zz.astype(F32)
        return zz[:, :ch] * _sigmoid(zz[:, ch:])

    buf[0:HALO, :] = jnp.where(first, 0.0, glu(prev_ref[...]))
    buf[HALO:HALO + tc, :] = glu(cur_ref[...])
    buf[HALO + tc:, :] = jnp.where(last, 0.0, glu(next_ref[...]))
    span = shifted.shape[1]
    for s in range(1, SUBLANES):
        shifted[s - 1] = buf[s:s + span, :]
    w = w_ref[...]
    acc = jnp.zeros((tc, ch), F32) + b_ref[...]
    for kk in range(CONV_WIDTH):
        blk, s = divmod(HALO - CONV_WIDTH // 2 + kk, SUBLANES)
        r0 = blk * SUBLANES
        win = buf[r0:r0 + tc, :] if s == 0 else shifted[s - 1, r0:r0 + tc, :]
        acc = acc + win * w[kk:kk + 1, :]
    mu = jnp.mean(acc, axis=-1, keepdims=True)
    cen = acc - mu
    var = jnp.mean(cen * cen, axis=-1, keepdims=True)
    y = cen * lax.rsqrt(var + EPS) * lg_ref[...] + lb_ref[...]
    y_ref[...] = _silu(y).astype(y_ref.dtype)


def _conv_mixer(z, w_dw, b_dw, ln_g, ln_b, dims):
    t = z.shape[0]
    p_rows, lc, _ = dims
    ch = w_dw.shape[1]
    tc = _divisor_tile(lc, 256)
    row = lambda a: a.reshape(1, ch)
    const = lambda i: (0, 0)
    return pl.pallas_call(
        functools.partial(_conv_kernel, tc=tc, p_rows=p_rows, lc=lc, ch=ch),
        grid=(t // tc,),
        in_specs=_halo_specs(tc, 2 * ch, Z_CONV // (2 * ch), t) + [
            pl.BlockSpec((CONV_WIDTH, ch), const),
            pl.BlockSpec((1, ch), const), pl.BlockSpec((1, ch), const), pl.BlockSpec((1, ch), const),
        ],
        out_specs=pl.BlockSpec((tc, ch), lambda i: (i, 0)),
        out_shape=jax.ShapeDtypeStruct((t, ch), BF),
        scratch_shapes=[pltpu.VMEM((tc + 2 * HALO, ch), F32),
                        pltpu.VMEM((SUBLANES - 1, tc + 2 * HALO - SUBLANES, ch), F32)],
        compiler_params=_cparams(("parallel",)),
        name="conv_mixer",
    )(z, z, z, w_dw, row(b_dw), row(ln_g), row(ln_b))


def _pool_kernel(prev_ref, cur_ref, next_ref, w_ref, s_ref, y_ref, buf, *, tc, p_rows, lc):
    first, last, seg_pos0, seg_len = _segment_tile(pl.program_id(0), tc, p_rows, lc)
    buf[0:HALO, :] = jnp.where(first, 0.0, prev_ref[...].astype(F32))
    buf[HALO:HALO + tc, :] = cur_ref[...].astype(F32)
    buf[HALO + tc:, :] = jnp.where(last, 0.0, next_ref[...].astype(F32))
    pos = seg_pos0 + lax.broadcasted_iota(jnp.int32, (tc, 1), 0)
    scale = s_ref[...]
    for gi, win in enumerate(POOL_WINDOWS):
        sl = slice(gi * POOL_GROUP, (gi + 1) * POOL_GROUP)
        half = win // 2
        tot = jnp.zeros((tc, POOL_GROUP), F32)
        for u in range(-half, half):
            tot = tot + buf[HALO + u:HALO + u + tc, sl]
        cnt = (jnp.minimum(pos + half, seg_len) - jnp.maximum(pos - half, 0)).astype(F32)
        pooled = tot / cnt - buf[HALO:HALO + tc, sl]
        y = jnp.dot(pooled.astype(BF), w_ref[gi].astype(BF), preferred_element_type=F32)
        y_ref[:, sl] = (y * scale[:, sl]).astype(y_ref.dtype)


def _pool_mixer(z, w_pool, scale, dims):
    t = z.shape[0]
    p_rows, lc, _ = dims
    ch = scale.shape[0]
    tc = _divisor_tile(lc, 256)
    return pl.pallas_call(
        functools.partial(_pool_kernel, tc=tc, p_rows=p_rows, lc=lc),
        grid=(t // tc,),
        in_specs=_halo_specs(tc, ch, Z_POOL // ch, t) + [
            pl.BlockSpec(w_pool.shape, lambda i: (0, 0, 0)),
            pl.BlockSpec((1, ch), lambda i: (0, 0)),
        ],
        out_specs=pl.BlockSpec((tc, ch), lambda i: (i, 0)),
        out_shape=jax.ShapeDtypeStruct((t, ch), BF),
        scratch_shapes=[pltpu.VMEM((tc + 2 * HALO, ch), F32)],
        compiler_params=_cparams(("parallel",)),
        name="pool_mixer",
    )(z, z, z, w_pool, scale.reshape(1, ch))


def _merge_kernel(y0, y1, y2, y3, g0, g1, g2, g3, w_ref, o_ref, w_s):
    @pl.when(pl.program_id(1) == 0)
    def _():
        w_s[...] = w_ref[0].astype(BF)

    acc = None
    for n, (y_ref, g_ref) in enumerate(((y0, g0), (y1, g1), (y2, g2), (y3, g3))):
        proj = jnp.dot(y_ref[...], w_s[n], preferred_element_type=F32)
        gate = 0.5 * jnp.tanh(0.5 * g_ref[...].astype(F32)) + 0.5
        term = gate * proj
        acc = term if acc is None else acc + term
    o_ref[...] = acc.astype(o_ref.dtype)


def _merge(ys, z, w_branch, l):
    t, bw = ys[0].shape
    _, nb, _, d = w_branch.shape
    tm = _divisor_tile(t, 544)
    tn = _divisor_tile(d, 1024, 128)
    npn = d // tn

    def gate_spec(n):
        return pl.BlockSpec((tm, tn), lambda j, i: (i, Z_GATE // tn + n * npn + j))

    return pl.pallas_call(
        _merge_kernel,
        grid=(npn, t // tm),
        in_specs=[pl.BlockSpec((tm, bw), lambda j, i: (i, 0))] * nb + [gate_spec(n) for n in range(nb)] + [
            pl.BlockSpec((1, nb, bw, tn), lambda j, i: (l, 0, 0, j))],
        out_specs=pl.BlockSpec((tm, tn), lambda j, i: (i, j)),
        out_shape=jax.ShapeDtypeStruct((t, d), BF),
        scratch_shapes=[pltpu.VMEM((nb, bw, tn), BF)],
        compiler_params=_cparams(("parallel", "arbitrary")),
        name="branch_merge",
    )(*ys, z, z, z, z, w_branch)


def _residual_epilogue(y, x_ref, gpost_ref, gate_ref, gnext_ref, sh_ref, sc_ref, xo_ref, ho_ref, b, n_batch, cm,
                       rows=slice(None)):
    cm = None if cm is None else cm[rows]
    gate_g = _row_vec(gate_ref, b, n_batch, cm) * gpost_ref[...]
    x_new = x_ref[rows, :] + _rms(y, gate_g)
    xo_ref[rows, :] = x_new
    if ho_ref is None:
        return None
    shift = _row_vec(sh_ref, b, n_batch, cm)
    scale_g = (1.0 + _row_vec(sc_ref, b, n_batch, cm)) * gnext_ref[...]
    h = _rms(x_new, scale_g) + shift
    ho_ref[rows, :] = h.astype(ho_ref.dtype)
    return h


def _route_top2(logits, n_experts):
    lane = lax.broadcasted_iota(jnp.int32, logits.shape, 1)
    lg = jnp.where(lane < n_experts, logits, NEG_BIG)
    m1 = jnp.max(lg, axis=-1, keepdims=True)
    i1 = jnp.min(jnp.where(lg == m1, lane, 1 << 20), axis=-1, keepdims=True)
    lg2 = jnp.where(lane == i1, NEG_BIG, lg)
    m2 = jnp.max(lg2, axis=-1, keepdims=True)
    i2 = jnp.min(jnp.where(lg2 == m2, lane, 1 << 20), axis=-1, keepdims=True)
    e2 = jnp.exp(m2 - m1)
    w1 = 1.0 / (1.0 + e2)
    w2 = e2 / (1.0 + e2)
    out = jnp.where(lane == 0, i1.astype(F32), 0.0)
    out = jnp.where(lane == 1, i2.astype(F32), out)
    out = jnp.where(lane == 2, w1, out)
    return jnp.where(lane == 3, w2, out)


def _router_logits(h, wr_ref):
    h_hi = h.astype(BF)
    h_lo = (h - h_hi.astype(F32)).astype(BF)
    return (jnp.dot(h_hi, wr_ref[0], preferred_element_type=F32) + jnp.dot(h_hi, wr_ref[1], preferred_element_type=F32)
            + jnp.dot(h_lo, wr_ref[0], preferred_element_type=F32))


def _out_proj_kernel(*refs, tm, p_rows, lc, n_batch, emit_h, n_experts):
    m_ref, w_ref, x_ref, gpost_ref, gate_ref = refs[:5]
    pos = 5
    gnext_ref = sh_ref = sc_ref = wr_ref = None
    if emit_h:
        gnext_ref, sh_ref, sc_ref = refs[pos:pos + 3]
        pos += 3
    if n_experts:
        wr_ref = refs[pos]
        pos += 1
    xo_ref = refs[pos]
    pos += 1
    ho_ref = route_ref = None
    if emit_h:
        ho_ref = refs[pos]
        pos += 1
    if n_experts:
        route_ref = refs[pos]
        pos += 1
    n_split = EPILOGUE_SPLIT if tm % (EPILOGUE_SPLIT * 16) == 0 else 1
    rs = tm // n_split
    groups = [slice(s * rs, (s + 1) * rs) for s in range(n_split)]

    def body(b, cm):
        ys = [jnp.dot(m_ref[rows, :], w_ref[0], preferred_element_type=F32) for rows in groups]
        for rows, y in zip(groups, ys):
            h = _residual_epilogue(y, x_ref, gpost_ref, gate_ref, gnext_ref, sh_ref, sc_ref, xo_ref, ho_ref, b,
                                   n_batch, cm, rows)
            if n_experts:
                route_ref[rows, :] = _route_top2(_router_logits(h, wr_ref), n_experts)

    _by_tile_kind(pl.program_id(0), tm, p_rows, lc, body)


def _out_proj(m, w_stack, lw, xs, g_post, mods, l, j_gate, nxt, dims, router=None, name="out_proj"):
    t, d = xs.shape
    k = m.shape[1]
    p_rows, lc, n_batch = dims
    tm = _divisor_tile(p_rows, 544 if k <= d else 272)
    const = lambda i: (0, 0)
    rowtile = pl.BlockSpec((tm, d), lambda i: (i, 0))
    args = [m, w_stack, xs, g_post.reshape(1, d), mods]
    in_specs = [pl.BlockSpec((tm, k), lambda i: (i, 0)),
                pl.BlockSpec((1, k, d), lambda i: (lw, 0, 0), pipeline_mode=pl.Buffered(1)),
                rowtile, pl.BlockSpec((1, d), const), _mod_spec(l, j_gate, d)]
    out_shape = [jax.ShapeDtypeStruct((t, d), F32)]
    out_specs = [rowtile]
    n_experts = 0
    if nxt is not None:
        g_next, l_next, j_shift, j_scale = nxt
        args += [g_next.reshape(1, d), mods, mods]
        in_specs += [pl.BlockSpec((1, d), const), _mod_spec(l_next, j_shift, d), _mod_spec(l_next, j_scale, d)]
        out_shape.append(jax.ShapeDtypeStruct((t, d), BF if router is None else F32))
        out_specs.append(rowtile)
    if router is not None:
        wr, n_experts = router
        args.append(wr)
        in_specs.append(pl.BlockSpec(wr.shape, lambda i: (0, 0, 0)))
        out_shape.append(jax.ShapeDtypeStruct((t, 128), F32))
        out_specs.append(pl.BlockSpec((tm, 128), lambda i: (i, 0)))
    return pl.pallas_call(
        functools.partial(_out_proj_kernel, tm=tm, p_rows=p_rows, lc=lc, n_batch=n_batch,
                          emit_h=nxt is not None, n_experts=n_experts),
        grid=(t // tm,),
        in_specs=in_specs,
        out_specs=out_specs,
        out_shape=out_shape,
        compiler_params=_cparams(("parallel",)),
        name=name,
    )(*args)


def _swiglu_kernel(x_ref, w1_ref, w3_ref, o_ref, w1_s, w3_s):
    @pl.when(pl.program_id(1) == 0)
    def _():
        w1_s[...] = w1_ref[0].astype(BF)
        w3_s[...] = w3_ref[0].astype(BF)

    x = x_ref[...]
    a = jnp.dot(x, w1_s[...], preferred_element_type=F32)
    g = jnp.dot(x, w3_s[...], preferred_element_type=F32)
    o_ref[...] = (_silu(a) * g).astype(o_ref.dtype)


def _swiglu_up(h, w1, w3, jl):
    t, d = h.shape
    dff = w1.shape[2]
    tm = _divisor_tile(t, 1088)
    tn = _divisor_tile(dff, 512, 128)
    return pl.pallas_call(
        _swiglu_kernel,
        grid=(dff // tn, t // tm),
        in_specs=[
            pl.BlockSpec((tm, d), lambda j, i: (i, 0)),
            pl.BlockSpec((1, d, tn), lambda j, i: (jl, 0, j)),
            pl.BlockSpec((1, d, tn), lambda j, i: (jl, 0, j)),
        ],
        out_specs=pl.BlockSpec((tm, tn), lambda j, i: (i, j)),
        out_shape=jax.ShapeDtypeStruct((t, dff), BF),
        scratch_shapes=[pltpu.VMEM((d, tn), BF), pltpu.VMEM((d, tn), BF)],
        compiler_params=_cparams(("parallel", "arbitrary")),
        name="swiglu_up",
    )(h, w1, w3)


def _gather_rows_kernel(idx_ref, src_ref, o_ref, *scratch, tr):
    base = pl.program_id(0) * tr
    dst, sem = (o_ref, scratch[0]) if len(scratch) == 1 else scratch

    def copy(r):
        return pltpu.make_async_copy(src_ref.at[pl.ds(idx_ref[base + r], 1), :], dst.at[pl.ds(r, 1), :], sem)

    def issue(g, carry):
        for u in range(GATHER_UNROLL):
            copy(g * GATHER_UNROLL + u).start(priority=u % 2)
        return carry

    lax.fori_loop(0, tr // GATHER_UNROLL, issue, 0)
    pltpu.make_async_copy(src_ref.at[pl.ds(0, tr), :], dst.at[pl.ds(0, tr), :], sem).wait()
    if dst is not o_ref:
        o_ref[...] = dst[...].astype(o_ref.dtype)


def _gather_rows(src, idx, out_dtype):
    n, d = src.shape
    assert src.dtype == F32
    rows = idx.shape[0]
    tr = _divisor_tile(rows, MOE_GATHER_ROWS, 8)
    staging = [] if out_dtype == src.dtype else [pltpu.VMEM((tr, d), src.dtype)]
    return pl.pallas_call(
        functools.partial(_gather_rows_kernel, tr=tr),
        grid_spec=pltpu.PrefetchScalarGridSpec(
            num_scalar_prefetch=1,
            grid=(rows // tr,),
            in_specs=[pl.BlockSpec(memory_space=pl.ANY)],
            out_specs=pl.BlockSpec((tr, d), lambda i, idx_ref: (i, 0)),
            scratch_shapes=staging + [pltpu.SemaphoreType.DMA(())],
        ),
        out_shape=jax.ShapeDtypeStruct((rows, d), out_dtype),
        compiler_params=_cparams(("arbitrary",)),
        name="moe_gather",
    )(idx, src)


def _expert_rows_loop(st_ref, nt_ref, n_slots, load_copy, store_copy, compute):
    e, n = pl.program_id(0), pl.program_id(1)
    n_exp, n_col = pl.num_programs(0), pl.num_programs(1)
    start, n_tiles = st_ref[e], nt_ref[e]
    first_step = jnp.logical_and(e == 0, n == 0)
    last_step = jnp.logical_and(e == n_exp - 1, n == n_col - 1)
    e_next = jnp.minimum(jnp.where(n == n_col - 1, e + 1, e), n_exp - 1)
    ahead = n_slots - 1

    @pl.when(jnp.logical_and(first_step, n_tiles > 0))
    def _():
        load_copy(start, 0, 0).start(priority=1)

    for k in range(1, ahead):
        @pl.when(n_tiles > k)
        def _():
            load_copy(start, k, k).start(priority=1)

    def body(i, carry):
        load_copy(start, i, i % n_slots).wait()

        @pl.when(i + ahead < n_tiles)
        def _():
            load_copy(start, i + ahead, (i + ahead) % n_slots).start(priority=1)

        @pl.when(i >= 2)
        def _():
            store_copy(start, i - 2, i % 2).wait()

        compute(i % n_slots, i % 2)
        store_copy(start, i, i % 2).start(priority=1)
        return carry

    lax.fori_loop(0, n_tiles, body, 0)

    @pl.when(jnp.logical_and(jnp.logical_not(last_step), nt_ref[e_next] > 0))
    def _():
        load_copy(st_ref[e_next], 0, 0).start(priority=1)

    @pl.when(n_tiles >= 2)
    def _():
        store_copy(start, n_tiles - 2, n_tiles % 2).wait()

    @pl.when(n_tiles >= 1)
    def _():
        store_copy(start, n_tiles - 1, (n_tiles - 1) % 2).wait()


def _zero_tail_rows(out_hbm, zbuf, sem, first_row, col0, tn):
    zbuf[...] = jnp.zeros_like(zbuf)
    n_chunks = (out_hbm.shape[0] - first_row) // MOE_ROW_ALIGN

    def copy(c):
        r0 = pl.multiple_of(first_row + c * MOE_ROW_ALIGN, MOE_ROW_ALIGN)
        return pltpu.make_async_copy(zbuf, out_hbm.at[pl.ds(r0, MOE_ROW_ALIGN), pl.ds(col0, tn)], sem)

    def issue(c, carry):
        copy(c).start()
        return carry

    def drain(c, carry):
        copy(c).wait()
        return carry

    lax.fori_loop(0, n_chunks, issue, 0)
    lax.fori_loop(0, n_chunks, drain, 0)


def _moe_up_kernel(st_ref, nt_ref, x_hbm, w1_ref, w3_ref, a_hbm, w1_s, w3_s, xbuf, obuf, zbuf, sem_in, sem_out, sem_z,
                   *, tr, tn):
    e, n = pl.program_id(0), pl.program_id(1)
    w1_s[...] = w1_ref[0, 0].astype(BF)
    w3_s[...] = w3_ref[0, 0].astype(BF)
    col0 = pl.multiple_of(n * tn, tn)

    def rows(start, i):
        return pl.ds(pl.multiple_of(start + i * tr, MOE_ROW_ALIGN), tr)

    def load_copy(start, i, slot):
        return pltpu.make_async_copy(x_hbm.at[rows(start, i)], xbuf.at[slot], sem_in.at[slot])

    def store_copy(start, i, slot):
        return pltpu.make_async_copy(obuf.at[slot], a_hbm.at[rows(start, i), pl.ds(col0, tn)], sem_out.at[slot])

    def compute(in_slot, out_slot):
        x = xbuf[in_slot]
        a = jnp.dot(x, w1_s[...], preferred_element_type=F32)
        g = jnp.dot(x, w3_s[...], preferred_element_type=F32)
        obuf[out_slot] = (_silu(a) * g).astype(obuf.dtype)

    _expert_rows_loop(st_ref, nt_ref, xbuf.shape[0], load_copy, store_copy, compute)

    @pl.when(e == pl.num_programs(0) - 1)
    def _():
        _zero_tail_rows(a_hbm, zbuf, sem_z, st_ref[e] + nt_ref[e] * tr, col0, tn)


def _moe_up(xs, w1, w3, jl, starts, n_tiles, tr):
    rows, d = xs.shape
    n_exp, dff = w1.shape[1], w1.shape[3]
    tn = _divisor_tile(dff, 512, 128)
    return pl.pallas_call(
        functools.partial(_moe_up_kernel, tr=tr, tn=tn),
        grid_spec=pltpu.PrefetchScalarGridSpec(
            num_scalar_prefetch=2,
            grid=(n_exp, dff // tn),
            in_specs=[
                pl.BlockSpec(memory_space=pl.ANY),
                pl.BlockSpec((1, 1, d, tn), lambda e, n, st, nt: (jl, e, 0, n)),
                pl.BlockSpec((1, 1, d, tn), lambda e, n, st, nt: (jl, e, 0, n)),
            ],
            out_specs=pl.BlockSpec(memory_space=pl.ANY),
            scratch_shapes=[pltpu.VMEM((d, tn), BF), pltpu.VMEM((d, tn), BF),
                            pltpu.VMEM((MOE_UP_SLOTS, tr, d), BF), pltpu.VMEM((2, tr, tn), BF),
                            pltpu.VMEM((MOE_ROW_ALIGN, tn), BF),
                            pltpu.SemaphoreType.DMA((MOE_UP_SLOTS,)), pltpu.SemaphoreType.DMA((2,)),
                            pltpu.SemaphoreType.DMA(())],
        ),
        out_shape=jax.ShapeDtypeStruct((rows, dff), BF),
        compiler_params=_cparams(("arbitrary", "arbitrary")),
        name="moe_up",
    )(starts, n_tiles, xs, w1, w3)


def _moe_down_kernel(st_ref, nt_ref, a_hbm, w_ref, y_hbm, w_s, abuf, obuf, zbuf, sem_in, sem_out, sem_z, *, tr, tn):
    e, n = pl.program_id(0), pl.program_id(1)
    w_s[...] = w_ref[0, 0].astype(BF)
    col0 = pl.multiple_of(n * tn, tn)

    def rows(start, i):
        return pl.ds(pl.multiple_of(start + i * tr, MOE_ROW_ALIGN), tr)

    def load_copy(start, i, slot):
        return pltpu.make_async_copy(a_hbm.at[rows(start, i)], abuf.at[slot], sem_in.at[slot])

    def store_copy(start, i, slot):
        return pltpu.make_async_copy(obuf.at[slot], y_hbm.at[rows(start, i), pl.ds(col0, tn)], sem_out.at[slot])

    def compute(in_slot, out_slot):
        obuf[out_slot] = jnp.dot(abuf[in_slot], w_s[...], preferred_element_type=F32)

    _expert_rows_loop(st_ref, nt_ref, abuf.shape[0], load_copy, store_copy, compute)

    @pl.when(e == pl.num_programs(0) - 1)
    def _():
        _zero_tail_rows(y_hbm, zbuf, sem_z, st_ref[e] + nt_ref[e] * tr, col0, tn)


def _moe_down(act, w2, jl, starts, n_tiles, tr):
    rows, dff = act.shape
    n_exp, d = w2.shape[1], w2.shape[3]
    tn = _divisor_tile(d, 512, 128)
    return pl.pallas_call(
        functools.partial(_moe_down_kernel, tr=tr, tn=tn),
        grid_spec=pltpu.PrefetchScalarGridSpec(
            num_scalar_prefetch=2,
            grid=(n_exp, d // tn),
            in_specs=[
                pl.BlockSpec(memory_space=pl.ANY),
                pl.BlockSpec((1, 1, dff, tn), lambda e, n, st, nt: (jl, e, 0, n)),
            ],
            out_specs=pl.BlockSpec(memory_space=pl.ANY),
            scratch_shapes=[pltpu.VMEM((dff, tn), BF), pltpu.VMEM((MOE_DOWN_SLOTS, tr, dff), BF),
                            pltpu.VMEM((2, tr, tn), F32), pltpu.VMEM((MOE_ROW_ALIGN, tn), F32),
                            pltpu.SemaphoreType.DMA((MOE_DOWN_SLOTS,)), pltpu.SemaphoreType.DMA((2,)),
                            pltpu.SemaphoreType.DMA(())],
        ),
        out_shape=jax.ShapeDtypeStruct((rows, d), F32),
        compiler_params=_cparams(("arbitrary", "arbitrary")),
        name="moe_down",
    )(starts, n_tiles, act, w2)


def _combine_kernel(*refs, tm, p_rows, lc, n_batch, emit_h):
    ya_ref, yb_ref, rt_ref, x_ref, gpost_ref, gate_ref = refs[:6]
    gnext_ref = sh_ref = sc_ref = ho_ref = None
    if emit_h:
        gnext_ref, sh_ref, sc_ref, xo_ref, ho_ref = refs[6:]
    else:
        (xo_ref,) = refs[6:]
    def body(b, cm):
        rt = rt_ref[...]
        y = rt[:, 2:3] * ya_ref[...] + rt[:, 3:4] * yb_ref[...]
        _residual_epilogue(y, x_ref, gpost_ref, gate_ref, gnext_ref, sh_ref, sc_ref, xo_ref, ho_ref, b, n_batch, cm)

    _by_tile_kind(pl.program_id(0), tm, p_rows, lc, body)


def _moe_combine(y2, route, xs, g_post, mods, l, j_gate, nxt, dims):
    t, d = xs.shape
    p_rows, lc, n_batch = dims
    tm = _divisor_tile(p_rows, 544)
    nt = t // tm
    const = lambda i: (0, 0)
    rowtile = pl.BlockSpec((tm, d), lambda i: (i, 0))
    args = [y2, y2, route, xs, g_post.reshape(1, d), mods]
    in_specs = [rowtile, pl.BlockSpec((tm, d), lambda i: (i + nt, 0)), pl.BlockSpec((tm, 128), lambda i: (i, 0)),
                rowtile, pl.BlockSpec((1, d), const), _mod_spec(l, j_gate, d)]
    out_shape = [jax.ShapeDtypeStruct((t, d), F32)]
    out_specs = [rowtile]
    if nxt is not None:
        g_next, l_next, j_shift, j_scale = nxt
        args += [g_next.reshape(1, d), mods, mods]
        in_specs += [pl.BlockSpec((1, d), const), _mod_spec(l_next, j_shift, d), _mod_spec(l_next, j_scale, d)]
        out_shape.append(jax.ShapeDtypeStruct((t, d), BF))
        out_specs.append(rowtile)
    return pl.pallas_call(
        functools.partial(_combine_kernel, tm=tm, p_rows=p_rows, lc=lc, n_batch=n_batch, emit_h=nxt is not None),
        grid=(nt,),
        in_specs=in_specs,
        out_specs=out_specs,
        out_shape=out_shape,
        compiler_params=_cparams(("parallel",)),
        name="moe_combine",
    )(*args)


def _moe_ffn(h, route, w1, w3, w2, jl, xs, g_post, mods, l, nxt, dims):
    t, d = h.shape
    n_exp = w1.shape[1]
    expert = jnp.concatenate([route[:, 0], route[:, 1]]).astype(jnp.int32)
    token = jnp.concatenate([jnp.arange(t, dtype=jnp.int32)] * 2)
    onehot = (expert[:, None] == jnp.arange(n_exp, dtype=jnp.int32)[None, :]).astype(jnp.int32)
    rank = jnp.sum((jnp.cumsum(onehot, axis=0) - onehot) * onehot, axis=1)
    counts = jnp.sum(onehot, axis=0)
    padded = ((counts + MOE_ROW_ALIGN - 1) // MOE_ROW_ALIGN) * MOE_ROW_ALIGN
    ends = jnp.cumsum(padded)
    starts = (ends - padded).astype(jnp.int32)
    tiles_up = ((counts + MOE_UP_ROWS - 1) // MOE_UP_ROWS).astype(jnp.int32)
    tiles_down = ((counts + MOE_DOWN_ROWS - 1) // MOE_DOWN_ROWS).astype(jnp.int32)
    dest = starts[expert] + rank
    rows = 2 * t + n_exp * MOE_ROW_ALIGN + max(MOE_UP_ROWS, MOE_DOWN_ROWS)
    rows = ((rows + MOE_GATHER_ROWS - 1) // MOE_GATHER_ROWS) * MOE_GATHER_ROWS
    row_token = jnp.zeros((rows,), jnp.int32).at[dest].set(token)

    xs_sorted = _gather_rows(h, row_token, BF)
    act = _moe_up(xs_sorted, w1, w3, jl, starts, tiles_up, MOE_UP_ROWS)
    y_sorted = _moe_down(act, w2, jl, starts, tiles_down, MOE_DOWN_ROWS)
    y2 = _gather_rows(y_sorted, dest, F32)
    return _moe_combine(y2, route, xs, g_post, mods, l, 5, nxt, dims)


def _pad_heads(w, n_heads, width):
    r = w.shape[0]
    w3 = w.reshape(r, n_heads, width)
    return jnp.pad(w3, ((0, 0), (0, 0), (0, HEAD_PAD - width))).reshape(r, n_heads * HEAD_PAD)


def _rope_tables(l_lat, lc):
    tt = jnp.arange(l_lat)
    row = (tt // GRID_W).astype(F32)
    col = (tt % GRID_W).astype(F32)
    axis_dim = MLA_ROPE // 2
    inv_freq = 1.0 / (ROPE_THETA ** (jnp.arange(0, axis_dim, 2, dtype=F32) / axis_dim))
    ar, ac = row[:, None] * inv_freq, col[:, None] * inv_freq
    ang = jnp.concatenate([ar, ar, ac, ac], axis=-1)
    cos = jnp.concatenate([jnp.ones((lc, MLA_ROPE), F32), jnp.cos(ang)], axis=0)
    sin = jnp.concatenate([jnp.zeros((lc, MLA_ROPE), F32), jnp.sin(ang)], axis=0)
    padc = jnp.zeros((lc + l_lat, 128 - MLA_ROPE), F32)
    pm = np.zeros((128, 128), np.float32)
    q4 = MLA_ROPE // 4
    for base in (0, 2 * q4):
        for i in range(q4):
            pm[base + q4 + i, base + i] = -1.0
            pm[base + i, base + q4 + i] = 1.0
    return jnp.concatenate([cos, padc], axis=1), jnp.concatenate([sin, padc], axis=1), jnp.asarray(pm, BF)


def kernel(x, c, ctx, c_ctx, w_mod, b_mod, g_mix_pre, g_mix_post, g_ffn_pre, g_ffn_post, w_in, mla_g_q, mla_g_kv, mla_w_uq, mla_w_ukv, hg_lb_logits, hg_g_norm, conv_w, conv_b, conv_ln_g, conv_ln_b, pool_w, pool_scale, w_branch, w_out, ffn_w1, ffn_w3, ffn_w2, moe_router, moe_w1, moe_w3, moe_w2):
    n_batch, l_lat, d = x.shape
    lc = ctx.shape[1]
    depth = w_in.shape[0]
    p_rows = lc + l_lat
    t = n_batch * p_rows
    dims = (p_rows, lc, n_batch)
    assert n_batch + 1 <= 8 and lc % HG_CHUNK == 0 and l_lat % HG_CHUNK == 0 and l_lat % GRID_W == 0
    n_heads = mla_w_uq.shape[2] // MLA_QK

    xs = jnp.concatenate([ctx, x], axis=1).reshape(t, d)
    cvec = jnp.zeros((8, d), F32).at[:n_batch].set(c).at[n_batch].set(c_ctx)
    mods = _modvecs(cvec, w_mod, b_mod)

    lb = jnp.cumsum(jax.nn.softmax(hg_lb_logits.astype(F32), axis=1), axis=1)
    lb = lb - lb[:, :1]
    cos, sin, pm = _rope_tables(l_lat, lc)

    w_in_t = jnp.swapaxes(w_in, 1, 2)
    w_out_bf = w_out.astype(BF)
    ffn_w2_bf = ffn_w2.astype(BF)

    h = _modulate(xs, g_mix_pre[0], mods, 0, 0, 1, dims)
    for l in range(depth):
        last = l == depth - 1
        z = _in_proj(h, w_in_t, l, BF)
        q3, k3, v3 = _mla_proj(z, mla_g_q[l], mla_g_kv[l], _pad_heads(mla_w_uq[l], n_heads, MLA_QK),
                               mla_w_ukv[l], cos, sin, pm, dims)
        y_mla = _attention(q3, k3, v3, dims)
        y_hg = _hgrn_readout(*_hgrn(z, lb[:, l], dims), z, hg_g_norm[l])
        y_conv = _conv_mixer(z, conv_w[l], conv_b[l], conv_ln_g[l], conv_ln_b[l], dims)
        y_pool = _pool_mixer(z, pool_w[l], pool_scale[l], dims)
        m = _merge([y_mla, y_hg, y_conv, y_pool], z, w_branch, l)
        j = l // 2
        nxt_ffn = (g_ffn_pre[l], l, 3, 4)
        nxt_mix = None if last else (g_mix_pre[l + 1], l + 1, 0, 1)
        if l % 2 == 0:
            xs, h2 = _out_proj(m, w_out_bf, l, xs, g_mix_post[l], mods, l, 2, nxt_ffn, dims)
            act = _swiglu_up(h2, ffn_w1, ffn_w3, j)
            res = _out_proj(act, ffn_w2_bf, j, xs, g_ffn_post[l], mods, l, 5, nxt_mix, dims, name="ffn_down")
        else:
            n_exp = moe_router.shape[2]
            wr = jnp.pad(moe_router[j], ((0, 0), (0, 128 - n_exp)))
            wr_hi = wr.astype(BF)
            wr2 = jnp.stack([wr_hi, (wr - wr_hi.astype(F32)).astype(BF)])
            xs, h2, route = _out_proj(m, w_out_bf, l, xs, g_mix_post[l], mods, l, 2, nxt_ffn, dims,
                                      router=(wr2, n_exp))
            res = _moe_ffn(h2, route, moe_w1, moe_w3, moe_w2, j, xs, g_ffn_post[l], mods, l, nxt_mix, dims)
        if last:
            (xs,) = res
        else:
            xs, h = res
    return xs.reshape(n_batch, p_rows, d)[:, lc:, :]
```
